```python
import jax, jax.numpy as jnp
from jax import lax
import numpy as np

D_MODEL = 2048
BATCH = 8
SEQ = 2048
DEPTH = 1

D_MIX = D_MODEL
ROPE_THETA = 10000.0
RMS_EPS = 1e-6
NEG_INF = -1e30
TINY = 1e-30
Q_BLOCK = 128

WIDTH_A = D_MIX // 2
HEAD_DIM_A = 128
N_HEADS_A = WIDTH_A // HEAD_DIM_A
N_KV_A = 2
GROUP_A = N_HEADS_A // N_KV_A
KV_A = N_KV_A * HEAD_DIM_A
CMP_BLOCK = 32
CMP_STRIDE = 16
CMP_HIDDEN = 256
SEL_BLOCK = 64
SEL_TOPK = 16
SEL_Q_CHUNK = 32
WIN_A = 512
FORCE_SCORE = 1e4

WIDTH_B = D_MIX - WIDTH_A
HEAD_DIM_B = 64
N_HEADS_B = WIDTH_B // HEAD_DIM_B
N_KV_B = 2
GROUP_B = N_HEADS_B // N_KV_B
KV_B = N_KV_B * HEAD_DIM_B
WIN_B = 128

IN_SIZES = (WIDTH_A, KV_A, KV_A, KV_A, KV_A, KV_A, KV_A, WIDTH_A, 3 * N_HEADS_A,
            WIDTH_B, KV_B, KV_B, WIDTH_B)
D_IN = 2 * WIDTH_A + 6 * KV_A + 3 * N_HEADS_A + 2 * WIDTH_B + 2 * KV_B

kernel_name = 'nsa_swa_sink_hybrid_block'


def _rmsnorm(x, g):
    xf = x.astype(jnp.float32)
    y = xf * lax.rsqrt(jnp.mean(xf * xf, axis=-1, keepdims=True) + RMS_EPS)
    return (y * g.astype(jnp.float32)).astype(x.dtype)


def _heads(t, n, d):
    return t.reshape(t.shape[0], t.shape[1], n, d)


def _rope(x):
    S, d = x.shape[1], x.shape[-1]
    inv = ROPE_THETA ** (-jnp.arange(0, d, 2, dtype=jnp.float32) / d)
    ang = jnp.arange(S, dtype=jnp.float32)[:, None] * inv[None, :]
    cos = jnp.cos(ang)[None, :, None, :]
    sin = jnp.sin(ang)[None, :, None, :]
    xf = x.astype(jnp.float32)
    x1, x2 = xf[..., : d // 2], xf[..., d // 2:]
    return jnp.concatenate([x1 * cos - x2 * sin, x2 * cos + x1 * sin], axis=-1).astype(x.dtype)


def _masked_softmax(s, mask, sink=None):
    s = jnp.where(mask, s.astype(jnp.float32), NEG_INF)
    m = jnp.max(s, axis=-1, keepdims=True)
    if sink is not None:
        m = jnp.maximum(m, sink)
    e = jnp.where(mask, jnp.exp(s - m), 0.0)
    denom = jnp.sum(e, axis=-1, keepdims=True)
    if sink is not None:
        denom = denom + jnp.exp(sink - m)
    return e / jnp.maximum(denom, TINY)


def _compress(kv, pos_emb, w1, w2):
    B, S, Hkv, D = kv.shape
    n_c = (S - CMP_BLOCK) // CMP_STRIDE + 1
    idx = jnp.arange(n_c)[:, None] * CMP_STRIDE + jnp.arange(CMP_BLOCK)[None, :]
    blocks = kv[:, idx] + pos_emb[None, None, :, None, :].astype(kv.dtype)
    flat = blocks.transpose(0, 1, 3, 2, 4).reshape(B, n_c, Hkv, CMP_BLOCK * D)
    return jax.nn.silu(flat @ w1) @ w2


def _compressed_attention(q, kc, vc):
    S, D = q.shape[1], q.shape[-1]
    n_c = kc.shape[1]
    s = jnp.einsum('bskgd,bckd->bkgsc', q, kc) * (D ** -0.5)
    ends = jnp.arange(n_c) * CMP_STRIDE + CMP_BLOCK - 1
    mask = ends[None, :] <= jnp.arange(S)[:, None]
    p = _masked_softmax(s, mask)
    o = jnp.einsum('bkgsc,bckd->bskgd', p.astype(vc.dtype), vc)
    return o, p


def _select_blocks(p_cmp, S):
    n_c = p_cmp.shape[-1]
    n_sel = S // SEL_BLOCK
    c_start = np.arange(n_c) * CMP_STRIDE
    j_start = np.arange(n_sel) * SEL_BLOCK
    overlap = (c_start[:, None] < j_start[None, :] + SEL_BLOCK) & (c_start[:, None] + CMP_BLOCK > j_start[None, :])
    p_sel = jnp.einsum('bkgsc,cj->bksj', p_cmp, jnp.asarray(overlap, jnp.float32))
    t = jnp.arange(S)[:, None]
    j = jnp.arange(n_sel)[None, :]
    cur = t // SEL_BLOCK
    forced = (j == 0) | (j == cur) | (j == cur - 1)
    valid = j * SEL_BLOCK <= t
    score = jnp.where(forced, FORCE_SCORE, jnp.where(valid, p_sel, -1.0))
    _, idx = lax.top_k(score, min(SEL_TOPK, n_sel))
    return idx


def _selected_attention(q, k, v, idx):
    B, S, Hkv, G, D = q.shape
    K = idx.shape[-1]
    n_sel = S // SEL_BLOCK
    n_ch = S // SEL_Q_CHUNK
    kb = k.reshape(B, n_sel, SEL_BLOCK, Hkv, D).transpose(0, 3, 1, 2, 4)
    vb = v.reshape(B, n_sel, SEL_BLOCK, Hkv, D).transpose(0, 3, 1, 2, 4)
    gather = jax.vmap(jax.vmap(lambda blocks, i: blocks[i]))
    scale = D ** -0.5

    def chunk_fn(args):
        qc, ic, tc = args
        kg = gather(kb, ic)
        vg = gather(vb, ic)
        s = jnp.einsum('bckgd,bkcnld->bkgcnl', qc, kg) * scale
        tok = ic[..., None] * SEL_BLOCK + jnp.arange(SEL_BLOCK)
        mask = (tok <= tc[None, None, :, None, None])[:, :, None]
        C = qc.shape[1]
        p = _masked_softmax(s.reshape(B, Hkv, G, C, K * SEL_BLOCK), mask.reshape(B, Hkv, 1, C, K * SEL_BLOCK))
        p = p.reshape(B, Hkv, G, C, K, SEL_BLOCK)
        return jnp.einsum('bkgcnl,bkcnld->bckgd', p.astype(vg.dtype), vg)

    qs = jnp.moveaxis(q.reshape(B, n_ch, SEL_Q_CHUNK, Hkv, G, D), 1, 0)
    ids = jnp.moveaxis(idx.reshape(B, Hkv, n_ch, SEL_Q_CHUNK, K), 2, 0)
    ts = jnp.arange(S).reshape(n_ch, SEL_Q_CHUNK)
    o = lax.map(chunk_fn, (qs, ids, ts))
    return jnp.moveaxis(o, 0, 1).reshape(B, S, Hkv, G, D)


def _banded_attention(q, k, v, window, sink=None):
    B, S, Hkv, G, D = q.shape
    nq = S // Q_BLOCK
    kv_len = window + Q_BLOCK
    pad = ((0, 0), (window, 0), (0, 0), (0, 0))
    idx = jnp.arange(nq)[:, None] * Q_BLOCK + jnp.arange(kv_len)[None, :]
    kb = jnp.pad(k, pad)[:, idx]
    vb = jnp.pad(v, pad)[:, idx]
    qb = q.reshape(B, nq, Q_BLOCK, Hkv, G, D)
    s = jnp.einsum('bnqkgd,bnjkd->bnkgqj', qb, kb) * (D ** -0.5)
    key_pos = idx - window
    q_pos = jnp.arange(S).reshape(nq, Q_BLOCK)
    diff = q_pos[:, :, None] - key_pos[:, None, :]
    mask = ((diff >= 0) & (diff < window) & (key_pos[:, None, :] >= 0))[None, :, None, None]
    sink_b = None if sink is None else sink.astype(jnp.float32)[None, None, :, :, None, None]
    p = _masked_softmax(s, mask, sink_b)
    o = jnp.einsum('bnkgqj,bnjkd->bnqkgd', p.astype(vb.dtype), vb)
    return o.reshape(B, S, Hkv, G, D)


def setup_inputs(seed: int = 0) -> dict:
    key = jax.random.key(seed)
    ks = jax.random.split(key, 12)

    def nrm(k, shape, scale):
        return jax.random.normal(k, shape, jnp.float32) * scale

    flat_in = CMP_BLOCK * HEAD_DIM_A
    return {
        'x': nrm(ks[0], (BATCH, SEQ, D_MODEL), 1.0),
        'w_in': nrm(ks[1], (DEPTH, D_MODEL, D_IN), D_MODEL ** -0.5),
        'cmp_k_w1': nrm(ks[2], (DEPTH, flat_in, CMP_HIDDEN), flat_in ** -0.5),
        'cmp_k_w2': nrm(ks[3], (DEPTH, CMP_HIDDEN, HEAD_DIM_A), CMP_HIDDEN ** -0.5),
        'cmp_v_w1': nrm(ks[4], (DEPTH, flat_in, CMP_HIDDEN), flat_in ** -0.5),
        'cmp_v_w2': nrm(ks[5], (DEPTH, CMP_HIDDEN, HEAD_DIM_A), CMP_HIDDEN ** -0.5),
        'cmp_k_pos': nrm(ks[6], (DEPTH, CMP_BLOCK, HEAD_DIM_A), 0.1),
        'cmp_v_pos': nrm(ks[7], (DEPTH, CMP_BLOCK, HEAD_DIM_A), 0.1),
        'sinks': nrm(ks[8], (DEPTH, N_HEADS_B), 1.0),
        'w_out': nrm(ks[9], (DEPTH, D_MIX, D_MODEL), D_MIX ** -0.5),
        'norm_g': 1.0 + nrm(ks[10], (DEPTH, D_MODEL), 0.01),
        'final_g': 1.0 + nrm(ks[11], (D_MODEL,), 0.01),
    }


def reference(x, w_in, cmp_k_w1, cmp_k_w2, cmp_v_w1, cmp_v_w2, cmp_k_pos, cmp_v_pos, sinks, w_out, norm_g, final_g):
    B, S, _ = x.shape
    offsets = np.cumsum(IN_SIZES)[:-1].tolist()
    for l in range(DEPTH):
        h = _rmsnorm(x, norm_g[l])
        proj = h @ w_in[l]
        (qa, kca, vca, ksa, vsa, kwa, vwa, za, ga,
         qb, kb, vb, zb) = jnp.split(proj, offsets, axis=-1)

        qa = _rope(_heads(qa, N_HEADS_A, HEAD_DIM_A)).reshape(B, S, N_KV_A, GROUP_A, HEAD_DIM_A)
        kc = _compress(_rope(_heads(kca, N_KV_A, HEAD_DIM_A)), cmp_k_pos[l], cmp_k_w1[l], cmp_k_w2[l])
        vc = _compress(_heads(vca, N_KV_A, HEAD_DIM_A), cmp_v_pos[l], cmp_v_w1[l], cmp_v_w2[l])
        o_cmp, p_cmp = _compressed_attention(qa, kc, vc)
        sel_idx = _select_blocks(p_cmp, S)
        o_sel = _selected_attention(qa, _rope(_heads(ksa, N_KV_A, HEAD_DIM_A)), _heads(vsa, N_KV_A, HEAD_DIM_A), sel_idx)
        o_win = _banded_attention(qa, _rope(_heads(kwa, N_KV_A, HEAD_DIM_A)), _heads(vwa, N_KV_A, HEAD_DIM_A), WIN_A)
        gates = jax.nn.sigmoid(ga.astype(jnp.float32)).reshape(B, S, N_KV_A, GROUP_A, 3).astype(x.dtype)
        o_a = gates[..., 0:1] * o_cmp + gates[..., 1:2] * o_sel + gates[..., 2:3] * o_win
        y_a = o_a.reshape(B, S, WIDTH_A) * jax.nn.silu(za)

        qb = _rope(_heads(qb, N_HEADS_B, HEAD_DIM_B)).reshape(B, S, N_KV_B, GROUP_B, HEAD_DIM_B)
        o_b = _banded_attention(qb, _rope(_heads(kb, N_KV_B, HEAD_DIM_B)), _heads(vb, N_KV_B, HEAD_DIM_B),
                                WIN_B, sink=sinks[l].reshape(N_KV_B, GROUP_B))
        y_b = o_b.reshape(B, S, WIDTH_B) * jax.nn.silu(zb)

        x = x + jnp.concatenate([y_a, y_b], axis=-1) @ w_out[l]
    return _rmsnorm(x, final_g)
```

```python
import functools

import numpy as np
import jax
import jax.numpy as jnp
from jax import lax
from jax.experimental import pallas as pl
from jax.experimental.pallas import tpu as pltpu

F32 = jnp.float32
BF16 = jnp.bfloat16

D_MODEL = 2048
ROPE_THETA = 10000.0
RMS_EPS = 1e-6
NEG_INF = -1e30
TINY = 1e-30

WIDTH_A = 1024
HEAD_DIM_A = 128
N_KV_A = 2
GROUP_A = 4
KV_A = N_KV_A * HEAD_DIM_A
CMP_BLOCK = 32
CMP_STRIDE = 16
CMP_HIDDEN = 256
SEL_BLOCK = 64
SEL_TOPK = 16
WIN_A = 512
FORCE_SCORE = 1e4

WIDTH_B = 1024
HEAD_DIM_B = 64
N_HEADS_B = 16
N_KV_B = 2
GROUP_B = 8
KV_B = N_KV_B * HEAD_DIM_B
WIN_B = 128

IN_SIZES = (WIDTH_A, KV_A, KV_A, KV_A, KV_A, KV_A, KV_A, WIDTH_A, 3 * 8,
            WIDTH_B, KV_B, KV_B, WIDTH_B)

LANES = 128
CHUNK = 256

EP_NONE, EP_ROPE128, EP_ROPE64, EP_SILU, EP_SIGMOID = range(5)

_PROJ_LAYOUT = (
    ("qa", 4, EP_ROPE128), ("kca", 1, EP_ROPE128), ("ksa", 1, EP_ROPE128), ("kwa", 1, EP_ROPE128),
    ("vca", 1, EP_NONE), ("vsa", 1, EP_NONE), ("vwa", 1, EP_NONE),
    ("za", 4, EP_SILU), ("zb", 4, EP_SILU),
    ("qb", 4, EP_ROPE64), ("kb2", 1, EP_ROPE64), ("vb2", 1, EP_NONE), ("ga", 1, EP_SIGMOID),
)
_CHUNK_KINDS = tuple(k for _, n, k in _PROJ_LAYOUT for _ in range(n))
_CHUNK_START = {}
_c = 0
for _name, _n, _k in _PROJ_LAYOUT:
    _CHUNK_START[_name] = _c
    _c += _n
N_CHUNKS = _c
D_PROJ = N_CHUNKS * CHUNK


def _col(name):
    return _CHUNK_START[name] * CHUNK


def _dot(a, b):
    return jnp.dot(a, b, preferred_element_type=F32)


def _dot_nt(a, b):
    return lax.dot_general(a, b, (((1,), (1,)), ((), ())), preferred_element_type=F32)


IN_TM = 512
IN_TN = 1280
_IN_CPT = IN_TN // CHUNK


def _in_proj_kernel(x_ref, g_ref, w_ref, tab_ref, o_ref, h_ref, acc_ref):
    j = pl.program_id(1)

    @pl.when(j == 0)
    def _():
        x = x_ref[...]
        ms = jnp.mean(x * x, axis=-1, keepdims=True)
        h_ref[...] = ((x * lax.rsqrt(ms + RMS_EPS)) * g_ref[...]).astype(BF16)

    acc_ref[...] = _dot(h_ref[...], w_ref[...])

    def epilogue(kind, c0):
        a = acc_ref[:, c0:c0 + LANES]
        if kind == EP_NONE:
            r = a
        elif kind == EP_ROPE128:
            r = a * tab_ref[0] + pltpu.roll(a, 64, 1) * tab_ref[1]
        elif kind == EP_ROPE64:
            r = a * tab_ref[2] + pltpu.roll(a, 96, 1) * tab_ref[3] + pltpu.roll(a, 32, 1) * tab_ref[4]
        elif kind == EP_SILU:
            r = a * jax.nn.sigmoid(a)
        else:
            r = jax.nn.sigmoid(a)
        o_ref[:, c0:c0 + LANES] = r.astype(o_ref.dtype)

    n_tiles = N_CHUNKS // _IN_CPT
    for c in range(_IN_CPT):
        kinds = {}
        for jj in range(n_tiles):
            kinds.setdefault(_CHUNK_KINDS[jj * _IN_CPT + c], []).append(jj)
        for kind, tiles in kinds.items():
            cond = functools.reduce(jnp.logical_or, [j == jj for jj in tiles])

            @pl.when(cond)
            def _(kind=kind, c=c):
                for sub in range(CHUNK // LANES):
                    epilogue(kind, c * CHUNK + sub * LANES)


def _rope_tables(S):
    def cs(d):
        inv = ROPE_THETA ** (-jnp.arange(0, d, 2, dtype=F32) / d)
        ang = jnp.arange(S, dtype=F32)[:, None] * inv[None, :]
        return jnp.cos(ang), jnp.sin(ang)

    c128, s128 = cs(HEAD_DIM_A)
    c64, s64 = cs(HEAD_DIM_B)
    z32 = jnp.zeros_like(s64)
    return jnp.stack([
        jnp.concatenate([c128, c128], axis=1),
        jnp.concatenate([-s128, s128], axis=1),
        jnp.concatenate([c64, c64, c64, c64], axis=1),
        jnp.concatenate([-s64, z32, -s64, z32], axis=1),
        jnp.concatenate([z32, s64, z32, s64], axis=1),
    ])


def _prep_w_in(w):
    offs = np.cumsum((0,) + IN_SIZES)
    seg = [w[:, offs[i]:offs[i + 1]] for i in range(len(IN_SIZES))]
    qa, kca, vca, ksa, vsa, kwa, vwa, za, ga, qb, kb, vb, zb = seg
    kb0, kb1 = kb[:, :HEAD_DIM_B], kb[:, HEAD_DIM_B:]
    vb0, vb1 = vb[:, :HEAD_DIM_B], vb[:, HEAD_DIM_B:]
    n_g = 3 * GROUP_A
    pad = jnp.zeros((w.shape[0], LANES - n_g), w.dtype)
    cols = [qa, kca, ksa, kwa, vca, vsa, vwa, za, zb, qb,
            kb0, kb0, kb1, kb1, vb0, vb0, vb1, vb1,
            ga[:, :n_g], pad, ga[:, n_g:], pad]
    out = jnp.concatenate(cols, axis=1).astype(BF16)
    assert out.shape[1] == D_PROJ
    return out


def _in_proj(x2, g, w_all, tabs, S):
    M = x2.shape[0]
    spt = S // IN_TM
    return pl.pallas_call(
        _in_proj_kernel,
        grid=(M // IN_TM, D_PROJ // IN_TN),
        in_specs=[
            pl.BlockSpec((IN_TM, D_MODEL), lambda i, j: (i, 0)),
            pl.BlockSpec((1, D_MODEL), lambda i, j: (0, 0)),
            pl.BlockSpec((D_MODEL, IN_TN), lambda i, j: (0, j)),
            pl.BlockSpec((5, IN_TM, LANES), lambda i, j: (0, i % spt, 0)),
        ],
        out_specs=pl.BlockSpec((IN_TM, IN_TN), lambda i, j: (i, j)),
        out_shape=jax.ShapeDtypeStruct((M, D_PROJ), BF16),
        scratch_shapes=[pltpu.VMEM((IN_TM, D_MODEL), BF16), pltpu.VMEM((IN_TM, IN_TN), F32)],
        compiler_params=pltpu.CompilerParams(
            dimension_semantics=("arbitrary", "arbitrary"), vmem_limit_bytes=48 * 1024 * 1024),
        name="in_proj",
    )(x2, g, w_all, tabs)


CMP_TM = 256
_HALF = CMP_STRIDE * HEAD_DIM_A


def _compress_kernel(t_ref, w1_ref, w2_ref, pos_ref, o_ref):
    t = t_ref[0]
    u = _dot(t, w1_ref[0, 0:_HALF, :])
    v = _dot(t, w1_ref[0, _HALF:2 * _HALF, :])
    pb = _dot(pos_ref[0], w1_ref[0])
    pre = u + pltpu.roll(v, CMP_TM - 1, 0) + pb[0:1]
    hid = pre * jax.nn.sigmoid(pre)
    out = _dot(hid.astype(BF16), w2_ref[0])
    row = lax.broadcasted_iota(jnp.int32, out.shape, 0)
    n_cp = LANES
    out = jnp.where((row & (n_cp - 1)) == n_cp - 1, 0.0, out)
    o_ref[0] = out.astype(o_ref.dtype)


def _compress(t_kv, w1, w2, pos):
    R = t_kv.shape[1]
    return pl.pallas_call(
        _compress_kernel,
        grid=(2, R // CMP_TM),
        in_specs=[
            pl.BlockSpec((1, CMP_TM, _HALF), lambda s, i: (s, i, 0)),
            pl.BlockSpec((1, 2 * _HALF, CMP_HIDDEN), lambda s, i: (s, 0, 0)),
            pl.BlockSpec((1, CMP_HIDDEN, HEAD_DIM_A), lambda s, i: (s, 0, 0)),
            pl.BlockSpec((1, 8, 2 * _HALF), lambda s, i: (s, 0, 0)),
        ],
        out_specs=pl.BlockSpec((1, CMP_TM, HEAD_DIM_A), lambda s, i: (s, i, 0)),
        out_shape=jax.ShapeDtypeStruct((2, R, HEAD_DIM_A), BF16),
        compiler_params=pltpu.CompilerParams(dimension_semantics=("arbitrary", "arbitrary")),
        name="compress",
    )(t_kv, w1, w2, pos)


TQ = 128
SEL_TK = 512
N_SELP = 32


def _attn_a_kernel(q_ref, ksel_ref, vsel_ref, kwin_ref, vwin_ref, kc_ref, vc_ref, g_ref, z_ref, ov_ref,
                   y_ref, kext_ref, qext_ref, m_ref, l_ref, acc_ref):
    qi = pl.program_id(2)
    t0 = qi * TQ
    S = ksel_ref.shape[1]
    R = GROUP_A * TQ
    scale = HEAD_DIM_A ** -0.5

    @pl.when(qi == 0)
    def _():
        kext_ref[:, 0:LANES] = ksel_ref[0]
        row = lax.broadcasted_iota(jnp.int32, (S, LANES), 0)
        lane = lax.broadcasted_iota(jnp.int32, (S, LANES), 1)
        kext_ref[:, LANES:2 * LANES] = jnp.where((row // SEL_BLOCK) == lane, 1.0, 0.0).astype(BF16)

    q = q_ref[0]
    q4 = jnp.concatenate([q[:, h * LANES:(h + 1) * LANES] for h in range(GROUP_A)], axis=0)
    kc = kc_ref[0]
    vc = vc_ref[0]

    st = _dot_nt(kc, q4) * scale
    cidx = lax.broadcasted_iota(jnp.int32, st.shape, 0)
    tq_t = t0 + (lax.broadcasted_iota(jnp.int32, st.shape, 1) & (TQ - 1))
    valid_t = (cidx * CMP_STRIDE + (CMP_BLOCK - 1)) <= tq_t
    st = jnp.where(valid_t, st, NEG_INF)
    mt = jnp.max(st, axis=0, keepdims=True)
    et = jnp.where(valid_t, jnp.exp(st - mt), 0.0)
    pt = et / jnp.maximum(jnp.sum(et, axis=0, keepdims=True), TINY)
    psum = pt[:, 0:TQ]
    for h in range(1, GROUP_A):
        psum = psum + pt[:, h * TQ:(h + 1) * TQ]
    ov = ov_ref[...]
    p_hi = psum.astype(BF16)
    r1 = psum - p_hi.astype(F32)
    p_mid = r1.astype(BF16)
    p_lo = (r1 - p_mid.astype(F32)).astype(BF16)
    psel = _dot(ov, p_hi) + _dot(ov, p_mid) + _dot(ov, p_lo)

    jidx = lax.broadcasted_iota(jnp.int32, psel.shape, 0)
    tsel = t0 + lax.broadcasted_iota(jnp.int32, psel.shape, 1)
    cur = tsel // SEL_BLOCK
    forced = (jidx == 0) | (jidx == cur) | (jidx == cur - 1)
    score = jnp.where(forced, FORCE_SCORE, jnp.where(jidx * SEL_BLOCK <= tsel, psel, -1.0))
    rank = jnp.zeros(psel.shape, F32)
    for i in range(N_SELP):
        row_i = jnp.broadcast_to(score[i:i + 1, :], score.shape)
        rank = rank + jnp.where(jidx > i, jnp.where(row_i >= score, 1.0, 0.0), jnp.where(row_i > score, 1.0, 0.0))
    bias_t = jnp.where(rank < SEL_TOPK, 0.0, NEG_INF)
    bias_t = jnp.concatenate([bias_t, jnp.zeros((LANES - N_SELP, TQ), F32)], axis=0)
    bias = bias_t.T.astype(BF16)
    qext_ref[:, 0:LANES] = q4
    qext_ref[:, LANES:2 * LANES] = jnp.concatenate([bias] * GROUP_A, axis=0)

    qpos = t0 + (lax.broadcasted_iota(jnp.int32, (R, 1), 0) & (TQ - 1))

    sc = _dot_nt(q4, kc) * scale
    cl = lax.broadcasted_iota(jnp.int32, sc.shape, 1)
    valid_c = (cl * CMP_STRIDE + (CMP_BLOCK - 1)) <= qpos
    sc = jnp.where(valid_c, sc, NEG_INF)
    mc = jnp.max(sc, axis=1, keepdims=True)
    ec = jnp.where(valid_c, jnp.exp(sc - mc), 0.0)
    dc = jnp.maximum(jnp.sum(ec, axis=1, keepdims=True), TINY)
    o_cmp = _dot(ec.astype(BF16), vc) / dc

    m_ref[...] = jnp.full(m_ref.shape, NEG_INF, F32)
    l_ref[...] = jnp.zeros(l_ref.shape, F32)
    acc_ref[...] = jnp.zeros(acc_ref.shape, F32)

    def sel_step(i, carry):
        ks = pl.multiple_of(i * SEL_TK, SEL_TK)
        s = _dot_nt(qext_ref[...], kext_ref[pl.ds(ks, SEL_TK), :]) * scale
        kpos = ks + lax.broadcasted_iota(jnp.int32, s.shape, 1)
        s = jnp.where(kpos <= qpos, s, NEG_INF)
        m_prev = m_ref[...]
        m_new = jnp.maximum(m_prev, jnp.max(s, axis=1, keepdims=True))
        alpha = jnp.exp(m_prev - m_new)
        p = jnp.exp(s - m_new)
        l_ref[...] = alpha * l_ref[...] + jnp.sum(p, axis=1, keepdims=True)
        acc_ref[...] = alpha * acc_ref[...] + _dot(p.astype(BF16), vsel_ref[0, pl.ds(ks, SEL_TK), :])
        m_ref[...] = m_new
        return carry

    lax.fori_loop(0, qi // (SEL_TK // TQ) + 1, sel_step, 0)
    o_sel = acc_ref[...] / jnp.maximum(l_ref[...], TINY)

    n_win = WIN_A + TQ
    kst = pl.multiple_of(jnp.maximum(t0 - WIN_A, 0), TQ)
    sw = _dot_nt(q4, kwin_ref[0, pl.ds(kst, n_win), :]) * scale
    dpos = qpos - (kst + lax.broadcasted_iota(jnp.int32, sw.shape, 1))
    sw = jnp.where((dpos >= 0) & (dpos < WIN_A), sw, NEG_INF)
    mw = jnp.max(sw, axis=1, keepdims=True)
    ew = jnp.exp(sw - mw)
    dw = jnp.maximum(jnp.sum(ew, axis=1, keepdims=True), TINY)
    o_win = _dot(ew.astype(BF16), vwin_ref[0, pl.ds(kst, n_win), :]) / dw

    g = g_ref[0].astype(F32)
    z = z_ref[0]
    outs = []
    for h in range(GROUP_A):
        rows = slice(h * TQ, (h + 1) * TQ)
        o = (g[:, 3 * h:3 * h + 1] * o_cmp[rows] + g[:, 3 * h + 1:3 * h + 2] * o_sel[rows]
             + g[:, 3 * h + 2:3 * h + 3] * o_win[rows])
        outs.append(o * z[:, h * LANES:(h + 1) * LANES].astype(F32))
    y_ref[0] = jnp.concatenate(outs, axis=1).astype(y_ref.dtype)


def _attn_a(p3, kvc, ov, y_shape):
    B, S, _ = p3.shape
    n_cp = S // CMP_STRIDE
    assert n_cp == LANES and S // SEL_BLOCK == N_SELP
    wq = GROUP_A * HEAD_DIM_A
    cb = lambda name: _col(name) // LANES
    kv_spec = lambda name: pl.BlockSpec((1, S, LANES), lambda b, k, i, c=cb(name): (b, 0, c + k))
    return pl.pallas_call(
        _attn_a_kernel,
        grid=(B, N_KV_A, S // TQ),
        in_specs=[
            pl.BlockSpec((1, TQ, wq), lambda b, k, i: (b, i, _col("qa") // wq + k)),
            kv_spec("ksa"), kv_spec("vsa"), kv_spec("kwa"), kv_spec("vwa"),
            pl.BlockSpec((1, n_cp, LANES), lambda b, k, i: (b * N_KV_A + k, 0, 0)),
            pl.BlockSpec((1, n_cp, LANES), lambda b, k, i: (B * N_KV_A + b * N_KV_A + k, 0, 0)),
            pl.BlockSpec((1, TQ, LANES), lambda b, k, i, c=cb("ga"): (b, i, c + k)),
            pl.BlockSpec((1, TQ, wq), lambda b, k, i: (b, i, _col("za") // wq + k)),
            pl.BlockSpec((N_SELP, LANES), lambda b, k, i: (0, 0)),
        ],
        out_specs=pl.BlockSpec((1, TQ, wq), lambda b, k, i: (b, i, k)),
        out_shape=y_shape,
        scratch_shapes=[
            pltpu.VMEM((S, 2 * LANES), BF16),
            pltpu.VMEM((GROUP_A * TQ, 2 * LANES), BF16),
            pltpu.VMEM((GROUP_A * TQ, 1), F32),
            pltpu.VMEM((GROUP_A * TQ, 1), F32),
            pltpu.VMEM((GROUP_A * TQ, LANES), F32),
        ],
        compiler_params=pltpu.CompilerParams(
            dimension_semantics=("arbitrary", "arbitrary", "arbitrary"), vmem_limit_bytes=48 * 1024 * 1024),
        name="attn_a",
    )(p3, p3, p3, p3, p3, kvc, kvc, p3, p3, ov)


def _attn_b_kernel(sink_ref, q_ref, k_ref, v_ref, z_ref, y_ref):
    kv = pl.program_id(1)
    qi = pl.program_id(2)
    t0 = qi * TQ
    R = GROUP_B * TQ
    scale = HEAD_DIM_B ** -0.5
    n_keys = WIN_B + TQ

    q = q_ref[0]
    lane = lax.broadcasted_iota(jnp.int32, (TQ, LANES), 1)
    left = lane < HEAD_DIM_B
    zero = jnp.zeros((TQ, LANES), q.dtype)
    parts = []
    for h in range(GROUP_B):
        pair = q[:, (h // 2) * LANES:(h // 2 + 1) * LANES]
        parts.append(jnp.where(left if h % 2 == 0 else ~left, pair, zero))
    qs = jnp.concatenate(parts, axis=0)

    kst = pl.multiple_of(jnp.maximum(t0 - WIN_B, 0), TQ)
    s = _dot_nt(qs, k_ref[0, pl.ds(kst, n_keys), :]) * scale
    qpos = t0 + (lax.broadcasted_iota(jnp.int32, (R, 1), 0) & (TQ - 1))
    dpos = qpos - (kst + lax.broadcasted_iota(jnp.int32, s.shape, 1))
    s = jnp.where((dpos >= 0) & (dpos < WIN_B), s, NEG_INF)
    sink = jnp.concatenate(
        [jnp.full((TQ, 1), sink_ref[kv * GROUP_B + h], F32) for h in range(GROUP_B)], axis=0)
    m = jnp.maximum(jnp.max(s, axis=1, keepdims=True), sink)
    e = jnp.exp(s - m)
    den = jnp.maximum(jnp.sum(e, axis=1, keepdims=True) + jnp.exp(sink - m), TINY)
    o = _dot(e.astype(BF16), v_ref[0, pl.ds(kst, n_keys), :]) / den

    z = z_ref[0]
    outs = []
    for c in range(GROUP_B // 2):
        o_pair = jnp.where(left, o[(2 * c) * TQ:(2 * c + 1) * TQ], o[(2 * c + 1) * TQ:(2 * c + 2) * TQ])
        outs.append(o_pair * z[:, c * LANES:(c + 1) * LANES].astype(F32))
    y_ref[0] = jnp.concatenate(outs, axis=1).astype(y_ref.dtype)


def _attn_b(sinks, p3):
    B, S, _ = p3.shape
    wq = GROUP_B * HEAD_DIM_B
    cb = lambda name: _col(name) // LANES
    grid_spec = pltpu.PrefetchScalarGridSpec(
        num_scalar_prefetch=1,
        grid=(B, N_KV_B, S // TQ),
        in_specs=[
            pl.BlockSpec((1, TQ, wq), lambda b, k, i, s: (b, i, _col("qb") // wq + k)),
            pl.BlockSpec((1, S, LANES), lambda b, k, i, s, c=cb("kb2"): (b, 0, c + k)),
            pl.BlockSpec((1, S, LANES), lambda b, k, i, s, c=cb("vb2"): (b, 0, c + k)),
            pl.BlockSpec((1, TQ, wq), lambda b, k, i, s: (b, i, _col("zb") // wq + k)),
        ],
        out_specs=pl.BlockSpec((1, TQ, wq), lambda b, k, i, s: (b, i, k)),
    )
    return pl.pallas_call(
        _attn_b_kernel,
        grid_spec=grid_spec,
        out_shape=jax.ShapeDtypeStruct((B, S, WIDTH_B), BF16),
        compiler_params=pltpu.CompilerParams(
            dimension_semantics=("arbitrary", "arbitrary", "arbitrary")),
        name="attn_b",
    )(sinks, p3, p3, p3, p3)


OUT_TM = 512


def _out_proj_kernel(ya_ref, yb_ref, wa_ref, wb_ref, x_ref, g_ref, o_ref):
    r = x_ref[...] + _dot(ya_ref[...], wa_ref[...]) + _dot(yb_ref[...], wb_ref[...])
    ms = jnp.mean(r * r, axis=-1, keepdims=True)
    o_ref[...] = (r * lax.rsqrt(ms + RMS_EPS)) * g_ref[...]


def _out_proj(ya2, yb2, w_out, x2, g):
    M = x2.shape[0]
    return pl.pallas_call(
        _out_proj_kernel,
        grid=(M // OUT_TM,),
        in_specs=[
            pl.BlockSpec((OUT_TM, WIDTH_A), lambda i: (i, 0)),
            pl.BlockSpec((OUT_TM, WIDTH_B), lambda i: (i, 0)),
            pl.BlockSpec((WIDTH_A, D_MODEL), lambda i: (0, 0)),
            pl.BlockSpec((WIDTH_B, D_MODEL), lambda i: (WIDTH_A // WIDTH_B, 0)),
            pl.BlockSpec((OUT_TM, D_MODEL), lambda i: (i, 0)),
            pl.BlockSpec((1, D_MODEL), lambda i: (0, 0)),
        ],
        out_specs=pl.BlockSpec((OUT_TM, D_MODEL), lambda i: (i, 0)),
        out_shape=jax.ShapeDtypeStruct((M, D_MODEL), F32),
        compiler_params=pltpu.CompilerParams(
            dimension_semantics=("arbitrary",), vmem_limit_bytes=56 * 1024 * 1024),
        name="out_proj",
    )(ya2, yb2, w_out, w_out, x2, g)


def _overlap_matrix(n_cp):
    c_start = np.arange(n_cp) * CMP_STRIDE
    j_start = np.arange(N_SELP) * SEL_BLOCK
    ov = (c_start[None, :] < j_start[:, None] + SEL_BLOCK) & (c_start[None, :] + CMP_BLOCK > j_start[:, None])
    return jnp.asarray(ov, BF16)


def _stride_rows(p3, col0):
    B, S, _ = p3.shape
    t = p3[:, :, col0:col0 + KV_A].reshape(B, S // CMP_STRIDE, CMP_STRIDE, N_KV_A, HEAD_DIM_A)
    return t.transpose(0, 3, 1, 2, 4).reshape(B * N_KV_A * (S // CMP_STRIDE), _HALF)


def kernel(x, w_in, cmp_k_w1, cmp_k_w2, cmp_v_w1, cmp_v_w2, cmp_k_pos, cmp_v_pos, sinks, w_out, norm_g, final_g):
    B, S, D = x.shape
    assert D == D_MODEL and w_in.shape[0] == 1
    x2 = x.reshape(B * S, D)

    w_all = _prep_w_in(w_in[0])
    tabs = _rope_tables(S)
    p2 = _in_proj(x2, norm_g[0].reshape(1, D), w_all, tabs, S)
    p3 = p2.reshape(B, S, D_PROJ)

    t_kv = jnp.stack([_stride_rows(p3, _col("kca")), _stride_rows(p3, _col("vca"))])
    w1 = jnp.stack([cmp_k_w1[0], cmp_v_w1[0]]).astype(BF16)
    w2 = jnp.stack([cmp_k_w2[0], cmp_v_w2[0]]).astype(BF16)
    pos = jnp.stack([cmp_k_pos[0], cmp_v_pos[0]]).reshape(2, 1, CMP_BLOCK * HEAD_DIM_A)
    pos = jnp.broadcast_to(pos, (2, 8, CMP_BLOCK * HEAD_DIM_A)).astype(BF16)
    kvc = _compress(t_kv, w1, w2, pos)
    kvc = kvc.reshape(2 * B * N_KV_A, S // CMP_STRIDE, HEAD_DIM_A)

    y_a = _attn_a(p3, kvc, _overlap_matrix(S // CMP_STRIDE), jax.ShapeDtypeStruct((B, S, WIDTH_A), BF16))
    y_b = _attn_b(sinks[0], p3)

    out = _out_proj(y_a.reshape(B * S, WIDTH_A), y_b.reshape(B * S, WIDTH_B), w_out[0].astype(BF16), x2,
                    final_g.reshape(1, D))
    return out.reshape(B, S, D)
```

```python
import functools
import math

import numpy as np
import jax
import jax.numpy as jnp
from jax import lax
from jax.experimental import pallas as pl
from jax.experimental.pallas import tpu as pltpu

F32 = jnp.float32
BF16 = jnp.bfloat16

D_MODEL = 2048
ROPE_THETA = 10000.0
RMS_EPS = 1e-6
NEG_INF = -1e30
TINY = 1e-30
LOG2E = math.log2(math.e)

WIDTH_A = 1024
HEAD_DIM_A = 128
N_KV_A = 2
GROUP_A = 4
KV_A = N_KV_A * HEAD_DIM_A
CMP_BLOCK = 32
CMP_STRIDE = 16
CMP_HIDDEN = 256
SEL_BLOCK = 64
SEL_TOPK = 16
WIN_A = 512
FORCE_SCORE = 1e4

WIDTH_B = 1024
HEAD_DIM_B = 64
N_HEADS_B = 16
N_KV_B = 2
GROUP_B = 8
KV_B = N_KV_B * HEAD_DIM_B
WIN_B = 128

IN_SIZES = (WIDTH_A, KV_A, KV_A, KV_A, KV_A, KV_A, KV_A, WIDTH_A, 3 * 8,
            WIDTH_B, KV_B, KV_B, WIDTH_B)

LANES = 128
SUBLANES = 8
CHUNK = 256

EP_NONE, EP_ROPE128, EP_ROPE64, EP_SILU, EP_SIGMOID, EP_ROPE128_Q, EP_ROPE64_Q = range(7)

_PROJ_LAYOUT = (
    ("qa", 4, EP_ROPE128_Q), ("kca", 1, EP_ROPE128), ("ksa", 1, EP_ROPE128), ("kwa", 1, EP_ROPE128),
    ("vca", 1, EP_NONE), ("vsa", 1, EP_NONE), ("vwa", 1, EP_NONE),
    ("za", 4, EP_SILU), ("zb", 4, EP_SILU),
    ("qb", 4, EP_ROPE64_Q), ("kb2", 1, EP_ROPE64), ("vb2", 1, EP_NONE), ("ga", 1, EP_SIGMOID),
)
_CHUNK_KINDS = tuple(k for _, n, k in _PROJ_LAYOUT for _ in range(n))
_CHUNK_START = {}
_c = 0
for _name, _n, _k in _PROJ_LAYOUT:
    _CHUNK_START[_name] = _c
    _c += _n
N_CHUNKS = _c
D_PROJ = N_CHUNKS * CHUNK


def _col(name):
    return _CHUNK_START[name] * CHUNK


def _dot(a, b):
    return jnp.dot(a, b, preferred_element_type=F32)


def _dot_nt(a, b):
    return lax.dot_general(a, b, (((1,), (1,)), ((), ())), preferred_element_type=F32)


IN_TM = 512
IN_TN = 1280
_IN_CPT = IN_TN // CHUNK
_N_TABS = 10


def _in_proj_kernel(x_ref, g_ref, w_ref, tab_ref, o_ref, h_ref, acc_ref):
    j = pl.program_id(1)

    @pl.when(j == 0)
    def _():
        x = x_ref[...]
        ms = jnp.mean(x * x, axis=-1, keepdims=True)
        h_ref[...] = ((x * lax.rsqrt(ms + RMS_EPS)) * g_ref[...]).astype(BF16)

    acc_ref[...] = _dot(h_ref[...], w_ref[...])

    def epilogue(kind, c0):
        a = acc_ref[:, c0:c0 + LANES]
        if kind == EP_NONE:
            r = a
        elif kind in (EP_ROPE128, EP_ROPE128_Q):
            t = 0 if kind == EP_ROPE128 else 5
            r = a * tab_ref[t] + pltpu.roll(a, 64, 1) * tab_ref[t + 1]
        elif kind in (EP_ROPE64, EP_ROPE64_Q):
            t = 2 if kind == EP_ROPE64 else 7
            r = a * tab_ref[t] + pltpu.roll(a, 96, 1) * tab_ref[t + 1] + pltpu.roll(a, 32, 1) * tab_ref[t + 2]
        elif kind == EP_SILU:
            r = a * jax.nn.sigmoid(a)
        else:
            r = jax.nn.sigmoid(a)
        o_ref[:, c0:c0 + LANES] = r.astype(o_ref.dtype)

    n_tiles = N_CHUNKS // _IN_CPT
    for c in range(_IN_CPT):
        kinds = {}
        for jj in range(n_tiles):
            kinds.setdefault(_CHUNK_KINDS[jj * _IN_CPT + c], []).append(jj)
        for kind, tiles in kinds.items():
            cond = functools.reduce(jnp.logical_or, [j == jj for jj in tiles])

            @pl.when(cond)
            def _(kind=kind, c=c):
                for sub in range(CHUNK // LANES):
                    epilogue(kind, c * CHUNK + sub * LANES)


def _rope_tables(S):
    def cs(d):
        inv = ROPE_THETA ** (-jnp.arange(0, d, 2, dtype=F32) / d)
        ang = jnp.arange(S, dtype=F32)[:, None] * inv[None, :]
        return jnp.cos(ang), jnp.sin(ang)

    c128, s128 = cs(HEAD_DIM_A)
    c64, s64 = cs(HEAD_DIM_B)
    z32 = jnp.zeros_like(s64)
    t128 = [jnp.concatenate([c128, c128], axis=1), jnp.concatenate([-s128, s128], axis=1)]
    t64 = [jnp.concatenate([c64, c64, c64, c64], axis=1),
           jnp.concatenate([-s64, z32, -s64, z32], axis=1),
           jnp.concatenate([z32, s64, z32, s64], axis=1)]
    qa = HEAD_DIM_A ** -0.5 * LOG2E
    qb = HEAD_DIM_B ** -0.5 * LOG2E
    return jnp.stack(t128 + t64 + [t * qa for t in t128] + [t * qb for t in t64])


def _prep_w_in(w):
    offs = np.cumsum((0,) + IN_SIZES)
    seg = [w[:, offs[i]:offs[i + 1]] for i in range(len(IN_SIZES))]
    qa, kca, vca, ksa, vsa, kwa, vwa, za, ga, qb, kb, vb, zb = seg
    kb0, kb1 = kb[:, :HEAD_DIM_B], kb[:, HEAD_DIM_B:]
    vb0, vb1 = vb[:, :HEAD_DIM_B], vb[:, HEAD_DIM_B:]
    n_g = 3 * GROUP_A
    pad = jnp.zeros((w.shape[0], LANES - n_g), w.dtype)
    cols = [qa, kca, ksa, kwa, vca, vsa, vwa, za, zb, qb,
            kb0, kb0, kb1, kb1, vb0, vb0, vb1, vb1,
            ga[:, :n_g], pad, ga[:, n_g:], pad]
    out = jnp.concatenate(cols, axis=1).astype(BF16)
    assert out.shape[1] == D_PROJ
    return out


def _in_proj(x2, g, w_all, tabs, S):
    M = x2.shape[0]
    spt = S // IN_TM
    return pl.pallas_call(
        _in_proj_kernel,
        grid=(M // IN_TM, D_PROJ // IN_TN),
        in_specs=[
            pl.BlockSpec((IN_TM, D_MODEL), lambda i, j: (i, 0)),
            pl.BlockSpec((1, D_MODEL), lambda i, j: (0, 0)),
            pl.BlockSpec((D_MODEL, IN_TN), lambda i, j: (0, j)),
            pl.BlockSpec((_N_TABS, IN_TM, LANES), lambda i, j: (0, i % spt, 0)),
        ],
        out_specs=pl.BlockSpec((IN_TM, IN_TN), lambda i, j: (i, j)),
        out_shape=jax.ShapeDtypeStruct((M, D_PROJ), BF16),
        scratch_shapes=[pltpu.VMEM((IN_TM, D_MODEL), BF16), pltpu.VMEM((IN_TM, IN_TN), F32)],
        compiler_params=pltpu.CompilerParams(
            dimension_semantics=("arbitrary", "arbitrary"), vmem_limit_bytes=48 * 1024 * 1024),
        name="in_proj",
    )(x2, g, w_all, tabs)


CMP_TM = 256
_HALF = CMP_STRIDE * HEAD_DIM_A


def _compress_kernel(t_ref, w1_ref, w2_ref, pos_ref, o_ref):
    t = t_ref[0]
    u = _dot(t, w1_ref[0, 0:_HALF, :])
    v = _dot(t, w1_ref[0, _HALF:2 * _HALF, :])
    pb = _dot(pos_ref[0], w1_ref[0])
    pre = u + pltpu.roll(v, CMP_TM - 1, 0) + pb[0:1]
    hid = pre * jax.nn.sigmoid(pre)
    out = _dot(hid.astype(BF16), w2_ref[0])
    row = lax.broadcasted_iota(jnp.int32, out.shape, 0)
    n_cp = LANES
    out = jnp.where((row & (n_cp - 1)) == n_cp - 1, 0.0, out)
    o_ref[0] = out.astype(o_ref.dtype)


def _compress(t_kv, w1, w2, pos):
    R = t_kv.shape[1]
    return pl.pallas_call(
        _compress_kernel,
        grid=(2, R // CMP_TM),
        in_specs=[
            pl.BlockSpec((1, CMP_TM, _HALF), lambda s, i: (s, i, 0)),
            pl.BlockSpec((1, 2 * _HALF, CMP_HIDDEN), lambda s, i: (s, 0, 0)),
            pl.BlockSpec((1, CMP_HIDDEN, HEAD_DIM_A), lambda s, i: (s, 0, 0)),
            pl.BlockSpec((1, 8, 2 * _HALF), lambda s, i: (s, 0, 0)),
        ],
        out_specs=pl.BlockSpec((1, CMP_TM, HEAD_DIM_A), lambda s, i: (s, i, 0)),
        out_shape=jax.ShapeDtypeStruct((2, R, HEAD_DIM_A), BF16),
        compiler_params=pltpu.CompilerParams(dimension_semantics=("arbitrary", "arbitrary")),
        name="compress",
    )(t_kv, w1, w2, pos)


TQ = 128
ONES_ROWS = 16


def _transpose_values(v_ref, vt_ref, n_rows):
    n_blk = vt_ref.shape[0]
    for blk in range(n_blk):
        vt = v_ref[0, blk * LANES:(blk + 1) * LANES, :].astype(F32).T
        vt_ref[blk, 0:n_rows, :] = vt[0:n_rows].astype(vt_ref.dtype)
        vt_ref[blk, n_rows:n_rows + ONES_ROWS, :] = jnp.ones((ONES_ROWS, LANES), vt_ref.dtype)


def _run_skewed(jobs):
    pending = [jobs[0][0]()]
    for i, (_, finish) in enumerate(jobs):
        if i + 1 < len(jobs):
            pending.append(jobs[i + 1][0]())
        finish(pending[i])
        pending[i] = None


def _softmax_step(s, m, acc, vt):
    m_new = jnp.maximum(m, jnp.max(s, axis=0, keepdims=True))
    p = jnp.exp2(s - m_new).astype(BF16)
    return m_new, jnp.exp2(m - m_new) * acc + _dot(vt, p)


SEL_TK = 256
N_SELP = 32
PAIR = 2 * TQ
_VROWS_A = HEAD_DIM_A + ONES_ROWS


def _attn_a_kernel(q_ref, ksel_ref, vsel_ref, kwin_ref, vwin_ref, kc_ref, vc_ref, g_ref, z_ref, ov_ref,
                   y_ref, vselt_ref, vwint_ref, bias_ref, acc_ref, out_ref):
    qi = pl.program_id(2)
    t0 = qi * TQ

    @pl.when(qi == 0)
    def _():
        _transpose_values(vsel_ref, vselt_ref, HEAD_DIM_A)
        _transpose_values(vwin_ref, vwint_ref, HEAD_DIM_A)

    q = q_ref[0]
    qh = [q[:, h * LANES:(h + 1) * LANES] for h in range(GROUP_A)]
    qpair = [jnp.concatenate(qh[2 * p:2 * p + 2], axis=0) for p in range(GROUP_A // 2)]
    tq = t0 + lax.broadcasted_iota(jnp.int32, (1, TQ), 1)
    tq2 = jnp.concatenate([tq, tq], axis=1)

    kc = kc_ref[0]
    vct = vc_ref[0].astype(F32).T.astype(BF16)
    cidx = lax.broadcasted_iota(jnp.int32, (LANES, PAIR), 0)
    valid_c = (cidx * CMP_STRIDE + (CMP_BLOCK - 1)) <= tq2
    n_pairs = GROUP_A // 2
    gt = g_ref[0].astype(F32).T

    def gate(p, branch):
        return jnp.concatenate([gt[3 * h + branch:3 * h + branch + 1] for h in (2 * p, 2 * p + 1)], axis=1)

    psums = []

    def cmp_finish(p, s):
        m = jnp.max(s, axis=0, keepdims=True)
        e = jnp.where(valid_c, jnp.exp2(s - m), 0.0)
        pc = e / jnp.maximum(jnp.sum(e, axis=0, keepdims=True), TINY)
        psums.append(pc[:, 0:TQ] + pc[:, TQ:PAIR])
        out_ref[p] = gate(p, 0) * _dot(vct, pc.astype(BF16))

    _run_skewed([(lambda p=p: jnp.where(valid_c, _dot_nt(kc, qpair[p]), NEG_INF),
                  lambda s, p=p: cmp_finish(p, s)) for p in range(n_pairs)])
    psum = psums[0] + psums[1]

    ov = ov_ref[...]
    p_hi = psum.astype(BF16)
    r1 = psum - p_hi.astype(F32)
    p_mid = r1.astype(BF16)
    p_lo = (r1 - p_mid.astype(F32)).astype(BF16)
    psel = _dot(ov, p_hi) + _dot(ov, p_mid) + _dot(ov, p_lo)

    kst = pl.multiple_of(jnp.maximum(t0 - WIN_A, 0), TQ)
    kb0 = kst // LANES
    win_m = [jnp.full((1, PAIR), NEG_INF, F32) for _ in range(n_pairs)]
    win_acc = [jnp.zeros((_VROWS_A, PAIR), F32) for _ in range(n_pairs)]

    def win_scores(p, off, tk):
        kt = kwin_ref[0, pl.ds(pl.multiple_of(kst + off, LANES), tk), :]
        dpos = tq2 - (kst + off + lax.broadcasted_iota(jnp.int32, (tk, PAIR), 0))
        return jnp.where((dpos >= 0) & (dpos < WIN_A), _dot_nt(kt, qpair[p]), NEG_INF)

    def win_finish(p, off, tk, s):
        vt = jnp.concatenate([vwint_ref[kb0 + off // LANES + i] for i in range(tk // LANES)], axis=1)
        win_m[p], win_acc[p] = _softmax_step(s, win_m[p], win_acc[p], vt)

    win_jobs = []
    off = 0
    for tk in (256, 256, 128):
        for p in range(n_pairs):
            win_jobs.append((functools.partial(win_scores, p, off, tk), functools.partial(win_finish, p, off, tk)))
        off += tk
    _run_skewed(win_jobs)
    for p in range(n_pairs):
        a = win_acc[p]
        out_ref[p] = out_ref[p] + (gate(p, 2) / jnp.maximum(a[HEAD_DIM_A:HEAD_DIM_A + 1], TINY)) * a[0:HEAD_DIM_A]

    jidx = lax.broadcasted_iota(jnp.int32, psel.shape, 0)
    cur = tq // SEL_BLOCK
    forced = (jidx == 0) | (jidx == cur) | (jidx == cur - 1)
    score = jnp.where(forced, FORCE_SCORE, jnp.where(jidx * SEL_BLOCK <= tq, psel, -1.0))
    rank = jnp.zeros(psel.shape, F32)
    for i in range(N_SELP):
        row_i = jnp.broadcast_to(score[i:i + 1, :], score.shape)
        rank = rank + jnp.where(jidx > i, jnp.where(row_i >= score, 1.0, 0.0), jnp.where(row_i > score, 1.0, 0.0))
    bias_t = jnp.where(rank < SEL_TOPK, 0.0, NEG_INF)
    for j in range(N_SELP):
        bias_ref[j] = jnp.broadcast_to(bias_t[j:j + 1, :], (SUBLANES, TQ))

    blk_per_tile = SEL_TK // SEL_BLOCK
    rep = SEL_BLOCK // SUBLANES

    def sel_tile(j, ms, causal):
        ks = pl.multiple_of(j * SEL_TK, SEL_TK)
        kt = ksel_ref[0, pl.ds(ks, SEL_TK), :]
        vt = jnp.concatenate([vselt_ref[j * (SEL_TK // LANES) + i] for i in range(SEL_TK // LANES)], axis=1)
        bt = jnp.concatenate(
            [pltpu.repeat(bias_ref[j * blk_per_tile + b], rep, axis=0) for b in range(blk_per_tile)], axis=0)
        bt = jnp.concatenate([bt, bt], axis=1)
        if causal:
            kpos = ks + lax.broadcasted_iota(jnp.int32, (SEL_TK, PAIR), 0)
            bt = jnp.where(kpos <= tq2, bt, NEG_INF)
        out = list(ms)

        def finish(p, s):
            out[p], acc_new = _softmax_step(s, ms[p], acc_ref[p], vt)
            acc_ref[p] = acc_new

        _run_skewed([(lambda p=p: _dot_nt(kt, qpair[p]) + bt, functools.partial(finish, p))
                     for p in range(n_pairs)])
        return tuple(out)

    acc_ref[...] = jnp.zeros(acc_ref.shape, F32)
    m0 = tuple(jnp.full((1, PAIR), NEG_INF, F32) for _ in range(n_pairs))
    last = qi // (SEL_TK // TQ)
    ms = lax.fori_loop(0, last, lambda j, ms: sel_tile(j, ms, False), m0)
    sel_tile(last, ms, True)

    z = z_ref[0]
    for p in range(n_pairs):
        a = acc_ref[p]
        o = out_ref[p] + (gate(p, 1) / jnp.maximum(a[HEAD_DIM_A:HEAD_DIM_A + 1], TINY)) * a[0:HEAD_DIM_A]
        for h in (2 * p, 2 * p + 1):
            oh = o[:, (h % 2) * TQ:(h % 2 + 1) * TQ].T
            y_ref[0, :, h * LANES:(h + 1) * LANES] = (
                oh * z[:, h * LANES:(h + 1) * LANES].astype(F32)).astype(y_ref.dtype)


def _attn_a(p3, kvc, ov, y_shape):
    B, S, _ = p3.shape
    n_cp = S // CMP_STRIDE
    assert n_cp == LANES and S // SEL_BLOCK == N_SELP
    wq = GROUP_A * HEAD_DIM_A
    cb = lambda name: _col(name) // LANES
    kv_spec = lambda name: pl.BlockSpec((1, S, LANES), lambda b, k, i, c=cb(name): (b, 0, c + k))
    return pl.pallas_call(
        _attn_a_kernel,
        grid=(B, N_KV_A, S // TQ),
        in_specs=[
            pl.BlockSpec((1, TQ, wq), lambda b, k, i: (b, i, _col("qa") // wq + k)),
            kv_spec("ksa"), kv_spec("vsa"), kv_spec("kwa"), kv_spec("vwa"),
            pl.BlockSpec((1, n_cp, LANES), lambda b, k, i: (b * N_KV_A + k, 0, 0)),
            pl.BlockSpec((1, n_cp, LANES), lambda b, k, i: (B * N_KV_A + b * N_KV_A + k, 0, 0)),
            pl.BlockSpec((1, TQ, LANES), lambda b, k, i, c=cb("ga"): (b, i, c + k)),
            pl.BlockSpec((1, TQ, wq), lambda b, k, i: (b, i, _col("za") // wq + k)),
            pl.BlockSpec((N_SELP, LANES), lambda b, k, i: (0, 0)),
        ],
        out_specs=pl.BlockSpec((1, TQ, wq), lambda b, k, i: (b, i, k)),
        out_shape=y_shape,
        scratch_shapes=[
            pltpu.VMEM((S // LANES, _VROWS_A, LANES), BF16),
            pltpu.VMEM((S // LANES, _VROWS_A, LANES), BF16),
            pltpu.VMEM((N_SELP, SUBLANES, TQ), F32),
            pltpu.VMEM((GROUP_A // 2, _VROWS_A, PAIR), F32),
            pltpu.VMEM((GROUP_A // 2, HEAD_DIM_A, PAIR), F32),
        ],
        compiler_params=pltpu.CompilerParams(
            dimension_semantics=("arbitrary", "arbitrary", "arbitrary"), vmem_limit_bytes=48 * 1024 * 1024),
        name="attn_a",
    )(p3, p3, p3, p3, p3, kvc, kvc, p3, p3, ov)


_VROWS_B = HEAD_DIM_B + ONES_ROWS


def _attn_b_kernel(sink_ref, q_ref, k_ref, v_ref, z_ref, y_ref, vt_ref):
    kv = pl.program_id(1)
    qi = pl.program_id(2)
    t0 = qi * TQ
    n_keys = WIN_B + TQ

    @pl.when(qi == 0)
    def _():
        _transpose_values(v_ref, vt_ref, HEAD_DIM_B)

    q = q_ref[0]
    left = lax.broadcasted_iota(jnp.int32, (TQ, LANES), 1) < HEAD_DIM_B
    zero = jnp.zeros((TQ, LANES), q.dtype)

    kst = pl.multiple_of(jnp.maximum(t0 - WIN_B, 0), TQ)
    kt = k_ref[0, pl.ds(kst, n_keys), :]
    kb0 = kst // LANES
    vt = jnp.concatenate([vt_ref[kb0 + i] for i in range(n_keys // LANES)], axis=1)
    tq = t0 + lax.broadcasted_iota(jnp.int32, (1, TQ), 1)
    tq2 = jnp.concatenate([tq, tq], axis=1)
    dpos = tq2 - (kst + lax.broadcasted_iota(jnp.int32, (n_keys, PAIR), 0))
    bias = jnp.where((dpos >= 0) & (dpos < WIN_B), 0.0, NEG_INF)

    z = z_ref[0]

    def scores(c):
        pair = q[:, c * LANES:(c + 1) * LANES]
        qp = jnp.concatenate([jnp.where(left, pair, zero), jnp.where(left, zero, pair)], axis=0)
        return _dot_nt(kt, qp) + bias

    def finish(c, s):
        sink = jnp.concatenate(
            [jnp.full((1, TQ), sink_ref[kv * GROUP_B + 2 * c + i] * LOG2E, F32) for i in range(2)], axis=1)
        m = jnp.maximum(jnp.max(s, axis=0, keepdims=True), sink)
        p = jnp.exp2(s - m).astype(BF16)
        a = _dot(vt, p)
        den = jnp.maximum(a[HEAD_DIM_B:HEAD_DIM_B + 1] + jnp.exp2(sink - m), TINY)
        o = a[0:HEAD_DIM_B] / den
        o2 = jnp.concatenate([o[:, 0:TQ], o[:, TQ:PAIR]], axis=0)
        y_ref[0, :, c * LANES:(c + 1) * LANES] = (
            o2.T * z[:, c * LANES:(c + 1) * LANES].astype(F32)).astype(y_ref.dtype)

    _run_skewed([(functools.partial(scores, c), functools.partial(finish, c)) for c in range(GROUP_B // 2)])


def _attn_b(sinks, p3):
    B, S, _ = p3.shape
    wq = GROUP_B * HEAD_DIM_B
    cb = lambda name: _col(name) // LANES
    grid_spec = pltpu.PrefetchScalarGridSpec(
        num_scalar_prefetch=1,
        grid=(B, N_KV_B, S // TQ),
        in_specs=[
            pl.BlockSpec((1, TQ, wq), lambda b, k, i, s: (b, i, _col("qb") // wq + k)),
            pl.BlockSpec((1, S, LANES), lambda b, k, i, s, c=cb("kb2"): (b, 0, c + k)),
            pl.BlockSpec((1, S, LANES), lambda b, k, i, s, c=cb("vb2"): (b, 0, c + k)),
            pl.BlockSpec((1, TQ, wq), lambda b, k, i, s: (b, i, _col("zb") // wq + k)),
        ],
        out_specs=pl.BlockSpec((1, TQ, wq), lambda b, k, i, s: (b, i, k)),
        scratch_shapes=[pltpu.VMEM((S // LANES, _VROWS_B, LANES), BF16)],
    )
    return pl.pallas_call(
        _attn_b_kernel,
        grid_spec=grid_spec,
        out_shape=jax.ShapeDtypeStruct((B, S, WIDTH_B), BF16),
        compiler_params=pltpu.CompilerParams(
            dimension_semantics=("arbitrary", "arbitrary", "arbitrary")),
        name="attn_b",
    )(sinks, p3, p3, p3, p3)


OUT_TM = 512


def _out_proj_kernel(ya_ref, yb_ref, wa_ref, wb_ref, x_ref, g_ref, o_ref):
    r = x_ref[...] + _dot(ya_ref[...], wa_ref[...]) + _dot(yb_ref[...], wb_ref[...])
    ms = jnp.mean(r * r, axis=-1, keepdims=True)
    o_ref[...] = (r * lax.rsqrt(ms + RMS_EPS)) * g_ref[...]


def _out_proj(ya2, yb2, w_out, x2, g):
    M = x2.shape[0]
    return pl.pallas_call(
        _out_proj_kernel,
        grid=(M // OUT_TM,),
        in_specs=[
            pl.BlockSpec((OUT_TM, WIDTH_A), lambda i: (i, 0)),
            pl.BlockSpec((OUT_TM, WIDTH_B), lambda i: (i, 0)),
            pl.BlockSpec((WIDTH_A, D_MODEL), lambda i: (0, 0)),
            pl.BlockSpec((WIDTH_B, D_MODEL), lambda i: (WIDTH_A // WIDTH_B, 0)),
            pl.BlockSpec((OUT_TM, D_MODEL), lambda i: (i, 0)),
            pl.BlockSpec((1, D_MODEL), lambda i: (0, 0)),
        ],
        out_specs=pl.BlockSpec((OUT_TM, D_MODEL), lambda i: (i, 0)),
        out_shape=jax.ShapeDtypeStruct((M, D_MODEL), F32),
        compiler_params=pltpu.CompilerParams(
            dimension_semantics=("arbitrary",), vmem_limit_bytes=56 * 1024 * 1024),
        name="out_proj",
    )(ya2, yb2, w_out, w_out, x2, g)


def _overlap_matrix(n_cp):
    c_start = np.arange(n_cp) * CMP_STRIDE
    j_start = np.arange(N_SELP) * SEL_BLOCK
    ov = (c_start[None, :] < j_start[:, None] + SEL_BLOCK) & (c_start[None, :] + CMP_BLOCK > j_start[:, None])
    return jnp.asarray(ov, BF16)


def _stride_rows(p3, col0):
    B, S, _ = p3.shape
    t = p3[:, :, col0:col0 + KV_A].reshape(B, S // CMP_STRIDE, CMP_STRIDE, N_KV_A, HEAD_DIM_A)
    return t.transpose(0, 3, 1, 2, 4).reshape(B * N_KV_A * (S // CMP_STRIDE), _HALF)


def kernel(x, w_in, cmp_k_w1, cmp_k_w2, cmp_v_w1, cmp_v_w2, cmp_k_pos, cmp_v_pos, sinks, w_out, norm_g, final_g):
    B, S, D = x.shape
    assert D == D_MODEL and w_in.shape[0] == 1
    x2 = x.reshape(B * S, D)

    w_all = _prep_w_in(w_in[0])
    tabs = _rope_tables(S)
    p2 = _in_proj(x2, norm_g[0].reshape(1, D), w_all, tabs, S)
    p3 = p2.reshape(B, S, D_PROJ)

    t_kv = jnp.stack([_stride_rows(p3, _col("kca")), _stride_rows(p3, _col("vca"))])
    w1 = jnp.stack([cmp_k_w1[0], cmp_v_w1[0]]).astype(BF16)
    w2 = jnp.stack([cmp_k_w2[0], cmp_v_w2[0]]).astype(BF16)
    pos = jnp.stack([cmp_k_pos[0], cmp_v_pos[0]]).reshape(2, 1, CMP_BLOCK * HEAD_DIM_A)
    pos = jnp.broadcast_to(pos, (2, 8, CMP_BLOCK * HEAD_DIM_A)).astype(BF16)
    kvc = _compress(t_kv, w1, w2, pos)
    kvc = kvc.reshape(2 * B * N_KV_A, S // CMP_STRIDE, HEAD_DIM_A)

    y_a = _attn_a(p3, kvc, _overlap_matrix(S // CMP_STRIDE), jax.ShapeDtypeStruct((B, S, WIDTH_A), BF16))
    y_b = _attn_b(sinks[0], p3)

    out = _out_proj(y_a.reshape(B * S, WIDTH_A), y_b.reshape(B * S, WIDTH_B), w_out[0].astype(BF16), x2,
                    final_g.reshape(1, D))
    return out.reshape(B, S, D)
```

```python
import functools
import math

import numpy as np
import jax
import jax.numpy as jnp
from jax import lax
from jax.experimental import pallas as pl
from jax.experimental.pallas import tpu as pltpu

F32 = jnp.float32
BF16 = jnp.bfloat16

D_MODEL = 2048
ROPE_THETA = 10000.0
RMS_EPS = 1e-6
NEG_INF = -1e30
TINY = 1e-30
LOG2E = math.log2(math.e)

WIDTH_A = 1024
HEAD_DIM_A = 128
N_KV_A = 2
GROUP_A = 4
KV_A = N_KV_A * HEAD_DIM_A
CMP_BLOCK = 32
CMP_STRIDE = 16
CMP_HIDDEN = 256
SEL_BLOCK = 64
SEL_TOPK = 16
WIN_A = 512
FORCE_SCORE = 1e4

WIDTH_B = 1024
HEAD_DIM_B = 64
N_HEADS_B = 16
N_KV_B = 2
GROUP_B = 8
KV_B = N_KV_B * HEAD_DIM_B
WIN_B = 128

IN_SIZES = (WIDTH_A, KV_A, KV_A, KV_A, KV_A, KV_A, KV_A, WIDTH_A, 3 * 8,
            WIDTH_B, KV_B, KV_B, WIDTH_B)

LANES = 128
SUBLANES = 8
CHUNK = 256

EP_NONE, EP_ROPE128, EP_ROPE64, EP_SILU, EP_SIGMOID, EP_ROPE128_Q, EP_ROPE64_Q = range(7)

_PROJ_LAYOUT = (
    ("qa", 4, EP_ROPE128_Q), ("kca", 1, EP_ROPE128), ("ksa", 1, EP_ROPE128), ("kwa", 1, EP_ROPE128),
    ("vca", 1, EP_NONE), ("vsa", 1, EP_NONE), ("vwa", 1, EP_NONE),
    ("za", 4, EP_SILU), ("zb", 4, EP_SILU),
    ("qb", 4, EP_ROPE64_Q), ("kb2", 1, EP_ROPE64), ("vb2", 1, EP_NONE), ("ga", 1, EP_SIGMOID),
)
_CHUNK_KINDS = tuple(k for _, n, k in _PROJ_LAYOUT for _ in range(n))
_CHUNK_START = {}
_c = 0
for _name, _n, _k in _PROJ_LAYOUT:
    _CHUNK_START[_name] = _c
    _c += _n
N_CHUNKS = _c
D_PROJ = N_CHUNKS * CHUNK


def _col(name):
    return _CHUNK_START[name] * CHUNK


def _dot(a, b):
    return jnp.dot(a, b, preferred_element_type=F32)


def _dot_nt(a, b):
    return lax.dot_general(a, b, (((1,), (1,)), ((), ())), preferred_element_type=F32)


IN_TN = 1280
IN_TM_NORM = 512
IN_TM = 1024
IN_SUB = 512
_IN_CPT = IN_TN // CHUNK
_N_TABS = 10


def _in_proj_kernel(kinds, with_norm, *refs):
    if with_norm:
        x_ref, g_ref, w_ref, tab_ref, o_ref, hout_ref = refs
        x = x_ref[...]
        ms = jnp.mean(x * x, axis=-1, keepdims=True)
        hout_ref[...] = ((x * lax.rsqrt(ms + RMS_EPS)) * g_ref[...]).astype(BF16)
        h_ref = hout_ref
    else:
        h_ref, w_ref, tab_ref, _, o_ref = refs
    tm = h_ref.shape[0]

    def epilogue(kind, rows, a):
        if kind == EP_NONE:
            return a
        if kind in (EP_ROPE128, EP_ROPE128_Q):
            t = 0 if kind == EP_ROPE128 else 5
            return a * tab_ref[t, rows, :] + pltpu.roll(a, 64, 1) * tab_ref[t + 1, rows, :]
        if kind in (EP_ROPE64, EP_ROPE64_Q):
            t = 2 if kind == EP_ROPE64 else 7
            return (a * tab_ref[t, rows, :] + pltpu.roll(a, 96, 1) * tab_ref[t + 1, rows, :]
                    + pltpu.roll(a, 32, 1) * tab_ref[t + 2, rows, :])
        if kind == EP_SILU:
            return a * jax.nn.sigmoid(a)
        return jax.nn.sigmoid(a)

    def finish(kind, rows, c, acc):
        for sub in range(CHUNK // LANES):
            cols = slice(c * CHUNK + sub * LANES, c * CHUNK + (sub + 1) * LANES)
            o_ref[rows, cols] = epilogue(kind, rows, acc[:, sub * LANES:(sub + 1) * LANES]).astype(o_ref.dtype)

    jobs = []
    for r in range(tm // IN_SUB):
        rows = slice(r * IN_SUB, (r + 1) * IN_SUB)
        for c, kind in enumerate(kinds):
            jobs.append((lambda rows=rows, c=c: _dot(h_ref[rows, :], w_ref[:, c * CHUNK:(c + 1) * CHUNK]),
                         functools.partial(finish, kind, rows, c)))
    _run_skewed(jobs)


def _rope_tables(S):
    def cs(d):
        inv = ROPE_THETA ** (-jnp.arange(0, d, 2, dtype=F32) / d)
        ang = jnp.arange(S, dtype=F32)[:, None] * inv[None, :]
        return jnp.cos(ang), jnp.sin(ang)

    c128, s128 = cs(HEAD_DIM_A)
    c64, s64 = cs(HEAD_DIM_B)
    z32 = jnp.zeros_like(s64)
    t128 = [jnp.concatenate([c128, c128], axis=1), jnp.concatenate([-s128, s128], axis=1)]
    t64 = [jnp.concatenate([c64, c64, c64, c64], axis=1),
           jnp.concatenate([-s64, z32, -s64, z32], axis=1),
           jnp.concatenate([z32, s64, z32, s64], axis=1)]
    qa = HEAD_DIM_A ** -0.5 * LOG2E
    qb = HEAD_DIM_B ** -0.5 * LOG2E
    return jnp.stack(t128 + t64 + [t * qa for t in t128] + [t * qb for t in t64])


def _prep_w_in(w):
    offs = np.cumsum((0,) + IN_SIZES)
    seg = [w[:, offs[i]:offs[i + 1]] for i in range(len(IN_SIZES))]
    qa, kca, vca, ksa, vsa, kwa, vwa, za, ga, qb, kb, vb, zb = seg
    kb0, kb1 = kb[:, :HEAD_DIM_B], kb[:, HEAD_DIM_B:]
    vb0, vb1 = vb[:, :HEAD_DIM_B], vb[:, HEAD_DIM_B:]
    n_g = 3 * GROUP_A
    pad = jnp.zeros((w.shape[0], LANES - n_g), w.dtype)
    cols = [qa, kca, ksa, kwa, vca, vsa, vwa, za, zb, qb,
            kb0, kb0, kb1, kb1, vb0, vb0, vb1, vb1,
            ga[:, :n_g], pad, ga[:, n_g:], pad]
    out = jnp.concatenate(cols, axis=1).astype(BF16)
    assert out.shape[1] == D_PROJ
    return out


def _in_proj(x2, g, w_all, tabs, S):
    M = x2.shape[0]
    params = pltpu.CompilerParams(dimension_semantics=("arbitrary",), vmem_limit_bytes=48 * 1024 * 1024)
    out_sds = jax.ShapeDtypeStruct((M, D_PROJ), BF16)

    def specs(tm, j):
        spt = S // tm
        return (pl.BlockSpec((D_MODEL, IN_TN), lambda i: (0, j)),
                pl.BlockSpec((_N_TABS, tm, LANES), lambda i: (0, i % spt, 0)),
                pl.BlockSpec((tm, IN_TN), lambda i: (i, j)))

    w_spec, tab_spec, o_spec = specs(IN_TM_NORM, 0)
    p2, h = pl.pallas_call(
        functools.partial(_in_proj_kernel, _CHUNK_KINDS[0:_IN_CPT], True),
        grid=(M // IN_TM_NORM,),
        in_specs=[pl.BlockSpec((IN_TM_NORM, D_MODEL), lambda i: (i, 0)),
                  pl.BlockSpec((1, D_MODEL), lambda i: (0, 0)), w_spec, tab_spec],
        out_specs=(o_spec, pl.BlockSpec((IN_TM_NORM, D_MODEL), lambda i: (i, 0))),
        out_shape=(out_sds, jax.ShapeDtypeStruct((M, D_MODEL), BF16)),
        compiler_params=params,
        name="in_proj_0",
    )(x2, g, w_all, tabs)
    for j in range(1, D_PROJ // IN_TN):
        w_spec, tab_spec, o_spec = specs(IN_TM, j)
        p2 = pl.pallas_call(
            functools.partial(_in_proj_kernel, _CHUNK_KINDS[j * _IN_CPT:(j + 1) * _IN_CPT], False),
            grid=(M // IN_TM,),
            in_specs=[pl.BlockSpec((IN_TM, D_MODEL), lambda i: (i, 0)), w_spec, tab_spec,
                      pl.BlockSpec(memory_space=pl.ANY)],
            out_specs=o_spec,
            out_shape=out_sds,
            input_output_aliases={3: 0},
            compiler_params=params,
            name=f"in_proj_{j}",
        )(h, w_all, tabs, p2)
    return p2


CMP_TM = 256
_HALF = CMP_STRIDE * HEAD_DIM_A


def _compress_kernel(t_ref, w1_ref, w2_ref, pos_ref, o_ref):
    t = t_ref[0]
    u = _dot(t, w1_ref[0, 0:_HALF, :])
    v = _dot(t, w1_ref[0, _HALF:2 * _HALF, :])
    pb = _dot(pos_ref[0], w1_ref[0])
    pre = u + pltpu.roll(v, CMP_TM - 1, 0) + pb[0:1]
    hid = pre * jax.nn.sigmoid(pre)
    out = _dot(hid.astype(BF16), w2_ref[0])
    row = lax.broadcasted_iota(jnp.int32, out.shape, 0)
    n_cp = LANES
    out = jnp.where((row & (n_cp - 1)) == n_cp - 1, 0.0, out)
    o_ref[0] = out.astype(o_ref.dtype)


def _compress(t_kv, w1, w2, pos):
    R = t_kv.shape[1]
    return pl.pallas_call(
        _compress_kernel,
        grid=(2, R // CMP_TM),
        in_specs=[
            pl.BlockSpec((1, CMP_TM, _HALF), lambda s, i: (s, i, 0)),
            pl.BlockSpec((1, 2 * _HALF, CMP_HIDDEN), lambda s, i: (s, 0, 0)),
            pl.BlockSpec((1, CMP_HIDDEN, HEAD_DIM_A), lambda s, i: (s, 0, 0)),
            pl.BlockSpec((1, 8, 2 * _HALF), lambda s, i: (s, 0, 0)),
        ],
        out_specs=pl.BlockSpec((1, CMP_TM, HEAD_DIM_A), lambda s, i: (s, i, 0)),
        out_shape=jax.ShapeDtypeStruct((2, R, HEAD_DIM_A), BF16),
        compiler_params=pltpu.CompilerParams(dimension_semantics=("arbitrary", "arbitrary")),
        name="compress",
    )(t_kv, w1, w2, pos)


TQ = 128
ONES_ROWS = 16


def _transpose_values(v_ref, vt_ref, n_rows):
    n_blk = vt_ref.shape[0]
    for blk in range(n_blk):
        vt = v_ref[0, blk * LANES:(blk + 1) * LANES, :].astype(F32).T
        vt_ref[blk, 0:n_rows, :] = vt[0:n_rows].astype(vt_ref.dtype)
        vt_ref[blk, n_rows:n_rows + ONES_ROWS, :] = jnp.ones((ONES_ROWS, LANES), vt_ref.dtype)


def _run_skewed(jobs):
    pending = [jobs[0][0]()]
    for i, (_, finish) in enumerate(jobs):
        if i + 1 < len(jobs):
            pending.append(jobs[i + 1][0]())
        finish(pending[i])
        pending[i] = None


def _softmax_step(s, m, acc, vt):
    m_new = jnp.maximum(m, jnp.max(s, axis=0, keepdims=True))
    p = jnp.exp2(s - m_new).astype(BF16)
    return m_new, jnp.exp2(m - m_new) * acc + _dot(vt, p)


SEL_TK = 256
N_SELP = 32
PAIR = 2 * TQ
_VROWS_A = HEAD_DIM_A + ONES_ROWS


def _attn_a_kernel(q_ref, ksel_ref, vsel_ref, kwin_ref, vwin_ref, kc_ref, vc_ref, g_ref, z_ref, ov_ref,
                   y_ref, vselt_ref, vwint_ref, bias_ref, acc_ref, out_ref):
    qi = pl.program_id(2)
    t0 = qi * TQ

    @pl.when(qi == 0)
    def _():
        _transpose_values(vsel_ref, vselt_ref, HEAD_DIM_A)
        _transpose_values(vwin_ref, vwint_ref, HEAD_DIM_A)

    q = q_ref[0]
    qh = [q[:, h * LANES:(h + 1) * LANES] for h in range(GROUP_A)]
    qpair = [jnp.concatenate(qh[2 * p:2 * p + 2], axis=0) for p in range(GROUP_A // 2)]
    tq = t0 + lax.broadcasted_iota(jnp.int32, (1, TQ), 1)
    tq2 = jnp.concatenate([tq, tq], axis=1)

    kc = kc_ref[0]
    vct = vc_ref[0].astype(F32).T.astype(BF16)
    cidx = lax.broadcasted_iota(jnp.int32, (LANES, PAIR), 0)
    valid_c = (cidx * CMP_STRIDE + (CMP_BLOCK - 1)) <= tq2
    n_pairs = GROUP_A // 2
    gt = g_ref[0].astype(F32).T

    def gate(p, branch):
        return jnp.concatenate([gt[3 * h + branch:3 * h + branch + 1] for h in (2 * p, 2 * p + 1)], axis=1)

    psums = []

    def cmp_finish(p, s):
        m = jnp.max(s, axis=0, keepdims=True)
        e = jnp.where(valid_c, jnp.exp2(s - m), 0.0)
        pc = e / jnp.maximum(jnp.sum(e, axis=0, keepdims=True), TINY)
        psums.append(pc[:, 0:TQ] + pc[:, TQ:PAIR])
        out_ref[p] = gate(p, 0) * _dot(vct, pc.astype(BF16))

    _run_skewed([(lambda p=p: jnp.where(valid_c, _dot_nt(kc, qpair[p]), NEG_INF),
                  lambda s, p=p: cmp_finish(p, s)) for p in range(n_pairs)])
    psum = psums[0] + psums[1]

    ov = ov_ref[...]
    p_hi = psum.astype(BF16)
    r1 = psum - p_hi.astype(F32)
    p_mid = r1.astype(BF16)
    p_lo = (r1 - p_mid.astype(F32)).astype(BF16)
    psel = _dot(ov, p_hi) + _dot(ov, p_mid) + _dot(ov, p_lo)

    kst = pl.multiple_of(jnp.maximum(t0 - WIN_A, 0), TQ)
    kb0 = kst // LANES
    win_m = [jnp.full((1, PAIR), NEG_INF, F32) for _ in range(n_pairs)]
    win_acc = [jnp.zeros((_VROWS_A, PAIR), F32) for _ in range(n_pairs)]

    def win_scores(p, off, tk):
        kt = kwin_ref[0, pl.ds(pl.multiple_of(kst + off, LANES), tk), :]
        dpos = tq2 - (kst + off + lax.broadcasted_iota(jnp.int32, (tk, PAIR), 0))
        return jnp.where((dpos >= 0) & (dpos < WIN_A), _dot_nt(kt, qpair[p]), NEG_INF)

    def win_finish(p, off, tk, s):
        vt = jnp.concatenate([vwint_ref[kb0 + off // LANES + i] for i in range(tk // LANES)], axis=1)
        win_m[p], win_acc[p] = _softmax_step(s, win_m[p], win_acc[p], vt)

    win_jobs = []
    off = 0
    for tk in (256, 256, 128):
        for p in range(n_pairs):
            win_jobs.append((functools.partial(win_scores, p, off, tk), functools.partial(win_finish, p, off, tk)))
        off += tk
    _run_skewed(win_jobs)
    for p in range(n_pairs):
        a = win_acc[p]
        out_ref[p] = out_ref[p] + (gate(p, 2) / jnp.maximum(a[HEAD_DIM_A:HEAD_DIM_A + 1], TINY)) * a[0:HEAD_DIM_A]

    jidx = lax.broadcasted_iota(jnp.int32, psel.shape, 0)
    cur = tq // SEL_BLOCK
    forced = (jidx == 0) | (jidx == cur) | (jidx == cur - 1)
    score = jnp.where(forced, FORCE_SCORE, jnp.where(jidx * SEL_BLOCK <= tq, psel, -1.0))
    rank = jnp.zeros(psel.shape, F32)
    for i in range(N_SELP):
        row_i = jnp.broadcast_to(score[i:i + 1, :], score.shape)
        rank = rank + jnp.where(jidx > i, jnp.where(row_i >= score, 1.0, 0.0), jnp.where(row_i > score, 1.0, 0.0))
    bias_t = jnp.where(rank < SEL_TOPK, 0.0, NEG_INF)
    for j in range(N_SELP):
        bias_ref[j] = jnp.broadcast_to(bias_t[j:j + 1, :], (SUBLANES, TQ))

    blk_per_tile = SEL_TK // SEL_BLOCK
    rep = SEL_BLOCK // SUBLANES

    def sel_tile(j, ms, causal):
        ks = pl.multiple_of(j * SEL_TK, SEL_TK)
        kt = ksel_ref[0, pl.ds(ks, SEL_TK), :]
        vt = jnp.concatenate([vselt_ref[j * (SEL_TK // LANES) + i] for i in range(SEL_TK // LANES)], axis=1)
        bt = jnp.concatenate(
            [bias_ref[j * blk_per_tile + b] for b in range(blk_per_tile) for _ in range(rep)], axis=0)
        bt = jnp.concatenate([bt, bt], axis=1)
        if causal:
            kpos = ks + lax.broadcasted_iota(jnp.int32, (SEL_TK, PAIR), 0)
            bt = jnp.where(kpos <= tq2, bt, NEG_INF)
        out = list(ms)

        def finish(p, s):
            out[p], acc_new = _softmax_step(s, ms[p], acc_ref[p], vt)
            acc_ref[p] = acc_new

        _run_skewed([(lambda p=p: _dot_nt(kt, qpair[p]) + bt, functools.partial(finish, p))
                     for p in range(n_pairs)])
        return tuple(out)

    acc_ref[...] = jnp.zeros(acc_ref.shape, F32)
    m0 = tuple(jnp.full((1, PAIR), NEG_INF, F32) for _ in range(n_pairs))
    last = qi // (SEL_TK // TQ)
    ms = lax.fori_loop(0, last, lambda j, ms: sel_tile(j, ms, False), m0)
    sel_tile(last, ms, True)

    z = z_ref[0]
    for p in range(n_pairs):
        a = acc_ref[p]
        o = out_ref[p] + (gate(p, 1) / jnp.maximum(a[HEAD_DIM_A:HEAD_DIM_A + 1], TINY)) * a[0:HEAD_DIM_A]
        for h in (2 * p, 2 * p + 1):
            oh = o[:, (h % 2) * TQ:(h % 2 + 1) * TQ].T
            y_ref[0, :, h * LANES:(h + 1) * LANES] = (
                oh * z[:, h * LANES:(h + 1) * LANES].astype(F32)).astype(y_ref.dtype)


def _attn_a(p3, kvc, ov, y_shape):
    B, S, _ = p3.shape
    n_cp = S // CMP_STRIDE
    assert n_cp == LANES and S // SEL_BLOCK == N_SELP
    wq = GROUP_A * HEAD_DIM_A
    cb = lambda name: _col(name) // LANES
    kv_spec = lambda name: pl.BlockSpec((1, S, LANES), lambda b, k, i, c=cb(name): (b, 0, c + k))
    return pl.pallas_call(
        _attn_a_kernel,
        grid=(B, N_KV_A, S // TQ),
        in_specs=[
            pl.BlockSpec((1, TQ, wq), lambda b, k, i: (b, i, _col("qa") // wq + k)),
            kv_spec("ksa"), kv_spec("vsa"), kv_spec("kwa"), kv_spec("vwa"),
            pl.BlockSpec((1, n_cp, LANES), lambda b, k, i: (b * N_KV_A + k, 0, 0)),
            pl.BlockSpec((1, n_cp, LANES), lambda b, k, i: (B * N_KV_A + b * N_KV_A + k, 0, 0)),
            pl.BlockSpec((1, TQ, LANES), lambda b, k, i, c=cb("ga"): (b, i, c + k)),
            pl.BlockSpec((1, TQ, wq), lambda b, k, i: (b, i, _col("za") // wq + k)),
            pl.BlockSpec((N_SELP, LANES), lambda b, k, i: (0, 0)),
        ],
        out_specs=pl.BlockSpec((1, TQ, wq), lambda b, k, i: (b, i, k)),
        out_shape=y_shape,
        scratch_shapes=[
            pltpu.VMEM((S // LANES, _VROWS_A, LANES), BF16),
            pltpu.VMEM((S // LANES, _VROWS_A, LANES), BF16),
            pltpu.VMEM((N_SELP, SUBLANES, TQ), F32),
            pltpu.VMEM((GROUP_A // 2, _VROWS_A, PAIR), F32),
            pltpu.VMEM((GROUP_A // 2, HEAD_DIM_A, PAIR), F32),
        ],
        compiler_params=pltpu.CompilerParams(
            dimension_semantics=("arbitrary", "arbitrary", "arbitrary"), vmem_limit_bytes=48 * 1024 * 1024),
        name="attn_a",
    )(p3, p3, p3, p3, p3, kvc, kvc, p3, p3, ov)


_VROWS_B = HEAD_DIM_B + ONES_ROWS


def _attn_b_kernel(sink_ref, q_ref, k_ref, v_ref, z_ref, y_ref, vt_ref):
    kv = pl.program_id(1)
    qi = pl.program_id(2)
    t0 = qi * TQ
    n_keys = WIN_B + TQ

    @pl.when(qi == 0)
    def _():
        _transpose_values(v_ref, vt_ref, HEAD_DIM_B)

    q = q_ref[0]
    left = lax.broadcasted_iota(jnp.int32, (TQ, LANES), 1) < HEAD_DIM_B
    zero = jnp.zeros((TQ, LANES), q.dtype)

    kst = pl.multiple_of(jnp.maximum(t0 - WIN_B, 0), TQ)
    kt = k_ref[0, pl.ds(kst, n_keys), :]
    kb0 = kst // LANES
    vt = jnp.concatenate([vt_ref[kb0 + i] for i in range(n_keys // LANES)], axis=1)
    tq = t0 + lax.broadcasted_iota(jnp.int32, (1, TQ), 1)
    tq2 = jnp.concatenate([tq, tq], axis=1)
    dpos = tq2 - (kst + lax.broadcasted_iota(jnp.int32, (n_keys, PAIR), 0))
    bias = jnp.where((dpos >= 0) & (dpos < WIN_B), 0.0, NEG_INF)

    z = z_ref[0]

    def scores(c):
        pair = q[:, c * LANES:(c + 1) * LANES]
        qp = jnp.concatenate([jnp.where(left, pair, zero), jnp.where(left, zero, pair)], axis=0)
        return _dot_nt(kt, qp) + bias

    def finish(c, s):
        sink = jnp.concatenate(
            [jnp.full((1, TQ), sink_ref[kv * GROUP_B + 2 * c + i] * LOG2E, F32) for i in range(2)], axis=1)
        m = jnp.maximum(jnp.max(s, axis=0, keepdims=True), sink)
        p = jnp.exp2(s - m).astype(BF16)
        a = _dot(vt, p)
        den = jnp.maximum(a[HEAD_DIM_B:HEAD_DIM_B + 1] + jnp.exp2(sink - m), TINY)
        o = a[0:HEAD_DIM_B] / den
        o2 = jnp.concatenate([o[:, 0:TQ], o[:, TQ:PAIR]], axis=0)
        y_ref[0, :, c * LANES:(c + 1) * LANES] = (
            o2.T * z[:, c * LANES:(c + 1) * LANES].astype(F32)).astype(y_ref.dtype)

    _run_skewed([(functools.partial(scores, c), functools.partial(finish, c)) for c in range(GROUP_B // 2)])


def _attn_b(sinks, p3):
    B, S, _ = p3.shape
    wq = GROUP_B * HEAD_DIM_B
    cb = lambda name: _col(name) // LANES
    grid_spec = pltpu.PrefetchScalarGridSpec(
        num_scalar_prefetch=1,
        grid=(B, N_KV_B, S // TQ),
        in_specs=[
            pl.BlockSpec((1, TQ, wq), lambda b, k, i, s: (b, i, _col("qb") // wq + k)),
            pl.BlockSpec((1, S, LANES), lambda b, k, i, s, c=cb("kb2"): (b, 0, c + k)),
            pl.BlockSpec((1, S, LANES), lambda b, k, i, s, c=cb("vb2"): (b, 0, c + k)),
            pl.BlockSpec((1, TQ, wq), lambda b, k, i, s: (b, i, _col("zb") // wq + k)),
        ],
        out_specs=pl.BlockSpec((1, TQ, wq), lambda b, k, i, s: (b, i, k)),
        scratch_shapes=[pltpu.VMEM((S // LANES, _VROWS_B, LANES), BF16)],
    )
    return pl.pallas_call(
        _attn_b_kernel,
        grid_spec=grid_spec,
        out_shape=jax.ShapeDtypeStruct((B, S, WIDTH_B), BF16),
        compiler_params=pltpu.CompilerParams(
            dimension_semantics=("arbitrary", "arbitrary", "arbitrary")),
        name="attn_b",
    )(sinks, p3, p3, p3, p3)


OUT_TM = 512


def _out_proj_kernel(ya_ref, yb_ref, wa_ref, wb_ref, x_ref, g_ref, o_ref):
    r = x_ref[...] + _dot(ya_ref[...], wa_ref[...]) + _dot(yb_ref[...], wb_ref[...])
    ms = jnp.mean(r * r, axis=-1, keepdims=True)
    o_ref[...] = (r * lax.rsqrt(ms + RMS_EPS)) * g_ref[...]


def _out_proj(ya2, yb2, w_out, x2, g):
    M = x2.shape[0]
    return pl.pallas_call(
        _out_proj_kernel,
        grid=(M // OUT_TM,),
        in_specs=[
            pl.BlockSpec((OUT_TM, WIDTH_A), lambda i: (i, 0)),
            pl.BlockSpec((OUT_TM, WIDTH_B), lambda i: (i, 0)),
            pl.BlockSpec((WIDTH_A, D_MODEL), lambda i: (0, 0)),
            pl.BlockSpec((WIDTH_B, D_MODEL), lambda i: (WIDTH_A // WIDTH_B, 0)),
            pl.BlockSpec((OUT_TM, D_MODEL), lambda i: (i, 0)),
            pl.BlockSpec((1, D_MODEL), lambda i: (0, 0)),
        ],
        out_specs=pl.BlockSpec((OUT_TM, D_MODEL), lambda i: (i, 0)),
        out_shape=jax.ShapeDtypeStruct((M, D_MODEL), F32),
        compiler_params=pltpu.CompilerParams(
            dimension_semantics=("arbitrary",), vmem_limit_bytes=56 * 1024 * 1024),
        name="out_proj",
    )(ya2, yb2, w_out, w_out, x2, g)


def _overlap_matrix(n_cp):
    c_start = np.arange(n_cp) * CMP_STRIDE
    j_start = np.arange(N_SELP) * SEL_BLOCK
    ov = (c_start[None, :] < j_start[:, None] + SEL_BLOCK) & (c_start[None, :] + CMP_BLOCK > j_start[:, None])
    return jnp.asarray(ov, BF16)


def _stride_rows(p3, col0):
    B, S, _ = p3.shape
    t = p3[:, :, col0:col0 + KV_A].reshape(B, S // CMP_STRIDE, CMP_STRIDE, N_KV_A, HEAD_DIM_A)
    return t.transpose(0, 3, 1, 2, 4).reshape(B * N_KV_A * (S // CMP_STRIDE), _HALF)


def kernel(x, w_in, cmp_k_w1, cmp_k_w2, cmp_v_w1, cmp_v_w2, cmp_k_pos, cmp_v_pos, sinks, w_out, norm_g, final_g):
    B, S, D = x.shape
    assert D == D_MODEL and w_in.shape[0] == 1
    x2 = x.reshape(B * S, D)

    w_all = _prep_w_in(w_in[0])
    tabs = _rope_tables(S)
    p2 = _in_proj(x2, norm_g[0].reshape(1, D), w_all, tabs, S)
    p3 = p2.reshape(B, S, D_PROJ)

    t_kv = jnp.stack([_stride_rows(p3, _col("kca")), _stride_rows(p3, _col("vca"))])
    w1 = jnp.stack([cmp_k_w1[0], cmp_v_w1[0]]).astype(BF16)
    w2 = jnp.stack([cmp_k_w2[0], cmp_v_w2[0]]).astype(BF16)
    pos = jnp.stack([cmp_k_pos[0], cmp_v_pos[0]]).reshape(2, 1, CMP_BLOCK * HEAD_DIM_A)
    pos = jnp.broadcast_to(pos, (2, 8, CMP_BLOCK * HEAD_DIM_A)).astype(BF16)
    kvc = _compress(t_kv, w1, w2, pos)
    kvc = kvc.reshape(2 * B * N_KV_A, S // CMP_STRIDE, HEAD_DIM_A)

    y_a = _attn_a(p3, kvc, _overlap_matrix(S // CMP_STRIDE), jax.ShapeDtypeStruct((B, S, WIDTH_A), BF16))
    y_b = _attn_b(sinks[0], p3)

    out = _out_proj(y_a.reshape(B * S, WIDTH_A), y_b.reshape(B * S, WIDTH_B), w_out[0].astype(BF16), x2,
                    final_g.reshape(1, D))
    return out.reshape(B, S, D)
```

```python
import functools
import math

import numpy as np
import jax
import jax.numpy as jnp
from jax import lax
from jax.experimental import pallas as pl
from jax.experimental.pallas import tpu as pltpu

F32 = jnp.float32
BF16 = jnp.bfloat16

D_MODEL = 2048
ROPE_THETA = 10000.0
RMS_EPS = 1e-6
NEG_INF = -1e30
TINY = 1e-30
LOG2E = math.log2(math.e)

WIDTH_A = 1024
HEAD_DIM_A = 128
N_KV_A = 2
GROUP_A = 4
KV_A = N_KV_A * HEAD_DIM_A
CMP_BLOCK = 32
CMP_STRIDE = 16
CMP_HIDDEN = 256
SEL_BLOCK = 64
SEL_TOPK = 16
WIN_A = 512
FORCE_SCORE = 1e4

WIDTH_B = 1024
HEAD_DIM_B = 64
N_HEADS_B = 16
N_KV_B = 2
GROUP_B = 8
KV_B = N_KV_B * HEAD_DIM_B
WIN_B = 128

IN_SIZES = (WIDTH_A, KV_A, KV_A, KV_A, KV_A, KV_A, KV_A, WIDTH_A, 3 * 8,
            WIDTH_B, KV_B, KV_B, WIDTH_B)

LANES = 128
SUBLANES = 8
CHUNK = 256

EP_NONE, EP_ROPE128, EP_ROPE64, EP_SILU, EP_SIGMOID, EP_ROPE128_Q, EP_ROPE64_Q = range(7)

_PROJ_LAYOUT = (
    ("qa", 4, EP_ROPE128_Q), ("kca", 1, EP_ROPE128), ("ksa", 1, EP_ROPE128), ("kwa", 1, EP_ROPE128),
    ("vca", 1, EP_NONE), ("vsa", 1, EP_NONE), ("vwa", 1, EP_NONE),
    ("za", 4, EP_SILU), ("zb", 4, EP_SILU),
    ("qb", 4, EP_ROPE64_Q), ("kb2", 1, EP_ROPE64), ("vb2", 1, EP_NONE), ("ga", 1, EP_SIGMOID),
)
_CHUNK_KINDS = tuple(k for _, n, k in _PROJ_LAYOUT for _ in range(n))
_CHUNK_START = {}
_c = 0
for _name, _n, _k in _PROJ_LAYOUT:
    _CHUNK_START[_name] = _c
    _c += _n
N_CHUNKS = _c
D_PROJ = N_CHUNKS * CHUNK


def _col(name):
    return _CHUNK_START[name] * CHUNK


def _dot(a, b):
    return jnp.dot(a, b, preferred_element_type=F32)


def _dot_nt(a, b):
    return lax.dot_general(a, b, (((1,), (1,)), ((), ())), preferred_element_type=F32)


IN_TN = 1280
IN_TM_NORM = 512
IN_TM = 1024
IN_SUB = 512
_IN_CPT = IN_TN // CHUNK
_N_TABS = 10


def _in_proj_kernel(kinds, with_norm, *refs):
    if with_norm:
        x_ref, g_ref, w_ref, tab_ref, o_ref, hout_ref = refs
        x = x_ref[...]
        ms = jnp.mean(x * x, axis=-1, keepdims=True)
        hout_ref[...] = ((x * lax.rsqrt(ms + RMS_EPS)) * g_ref[...]).astype(BF16)
        h_ref = hout_ref
    else:
        h_ref, w_ref, tab_ref, _, o_ref = refs
    tm = h_ref.shape[0]

    def epilogue(kind, rows, a):
        if kind == EP_NONE:
            return a
        if kind in (EP_ROPE128, EP_ROPE128_Q):
            t = 0 if kind == EP_ROPE128 else 5
            return a * tab_ref[t, rows, :] + pltpu.roll(a, 64, 1) * tab_ref[t + 1, rows, :]
        if kind in (EP_ROPE64, EP_ROPE64_Q):
            t = 2 if kind == EP_ROPE64 else 7
            return (a * tab_ref[t, rows, :] + pltpu.roll(a, 96, 1) * tab_ref[t + 1, rows, :]
                    + pltpu.roll(a, 32, 1) * tab_ref[t + 2, rows, :])
        if kind == EP_SILU:
            return a * jax.nn.sigmoid(a)
        return jax.nn.sigmoid(a)

    def finish(kind, rows, c, acc):
        for sub in range(CHUNK // LANES):
            cols = slice(c * CHUNK + sub * LANES, c * CHUNK + (sub + 1) * LANES)
            o_ref[rows, cols] = epilogue(kind, rows, acc[:, sub * LANES:(sub + 1) * LANES]).astype(o_ref.dtype)

    jobs = []
    for r in range(tm // IN_SUB):
        rows = slice(r * IN_SUB, (r + 1) * IN_SUB)
        for c, kind in enumerate(kinds):
            jobs.append((lambda rows=rows, c=c: _dot(h_ref[rows, :], w_ref[:, c * CHUNK:(c + 1) * CHUNK]),
                         functools.partial(finish, kind, rows, c)))
    _run_skewed(jobs)


def _rope_tables(S):
    def cs(d):
        inv = ROPE_THETA ** (-jnp.arange(0, d, 2, dtype=F32) / d)
        ang = jnp.arange(S, dtype=F32)[:, None] * inv[None, :]
        return jnp.cos(ang), jnp.sin(ang)

    c128, s128 = cs(HEAD_DIM_A)
    c64, s64 = cs(HEAD_DIM_B)
    z32 = jnp.zeros_like(s64)
    t128 = [jnp.concatenate([c128, c128], axis=1), jnp.concatenate([-s128, s128], axis=1)]
    t64 = [jnp.concatenate([c64, c64, c64, c64], axis=1),
           jnp.concatenate([-s64, z32, -s64, z32], axis=1),
           jnp.concatenate([z32, s64, z32, s64], axis=1)]
    qa = HEAD_DIM_A ** -0.5 * LOG2E
    qb = HEAD_DIM_B ** -0.5 * LOG2E
    return jnp.stack(t128 + t64 + [t * qa for t in t128] + [t * qb for t in t64])


def _prep_w_in(w):
    offs = np.cumsum((0,) + IN_SIZES)
    seg = [w[:, offs[i]:offs[i + 1]] for i in range(len(IN_SIZES))]
    qa, kca, vca, ksa, vsa, kwa, vwa, za, ga, qb, kb, vb, zb = seg
    kb0, kb1 = kb[:, :HEAD_DIM_B], kb[:, HEAD_DIM_B:]
    vb0, vb1 = vb[:, :HEAD_DIM_B], vb[:, HEAD_DIM_B:]
    n_g = 3 * GROUP_A
    pad = jnp.zeros((w.shape[0], LANES - n_g), w.dtype)
    cols = [qa, kca, ksa, kwa, vca, vsa, vwa, za, zb, qb,
            kb0, kb0, kb1, kb1, vb0, vb0, vb1, vb1,
            ga[:, :n_g], pad, ga[:, n_g:], pad]
    out = jnp.concatenate(cols, axis=1).astype(BF16)
    assert out.shape[1] == D_PROJ
    return out


def _in_proj(x2, g, w_all, tabs, S):
    M = x2.shape[0]
    params = pltpu.CompilerParams(dimension_semantics=("arbitrary",), vmem_limit_bytes=48 * 1024 * 1024)
    out_sds = jax.ShapeDtypeStruct((M, D_PROJ), BF16)

    def specs(tm, j):
        spt = S // tm
        return (pl.BlockSpec((D_MODEL, IN_TN), lambda i: (0, j)),
                pl.BlockSpec((_N_TABS, tm, LANES), lambda i: (0, i % spt, 0)),
                pl.BlockSpec((tm, IN_TN), lambda i: (i, j)))

    w_spec, tab_spec, o_spec = specs(IN_TM_NORM, 0)
    p2, h = pl.pallas_call(
        functools.partial(_in_proj_kernel, _CHUNK_KINDS[0:_IN_CPT], True),
        grid=(M // IN_TM_NORM,),
        in_specs=[pl.BlockSpec((IN_TM_NORM, D_MODEL), lambda i: (i, 0)),
                  pl.BlockSpec((1, D_MODEL), lambda i: (0, 0)), w_spec, tab_spec],
        out_specs=(o_spec, pl.BlockSpec((IN_TM_NORM, D_MODEL), lambda i: (i, 0))),
        out_shape=(out_sds, jax.ShapeDtypeStruct((M, D_MODEL), BF16)),
        compiler_params=params,
        name="in_proj_0",
    )(x2, g, w_all, tabs)
    for j in range(1, D_PROJ // IN_TN):
        w_spec, tab_spec, o_spec = specs(IN_TM, j)
        p2 = pl.pallas_call(
            functools.partial(_in_proj_kernel, _CHUNK_KINDS[j * _IN_CPT:(j + 1) * _IN_CPT], False),
            grid=(M // IN_TM,),
            in_specs=[pl.BlockSpec((IN_TM, D_MODEL), lambda i: (i, 0)), w_spec, tab_spec,
                      pl.BlockSpec(memory_space=pl.ANY)],
            out_specs=o_spec,
            out_shape=out_sds,
            input_output_aliases={3: 0},
            compiler_params=params,
            name=f"in_proj_{j}",
        )(h, w_all, tabs, p2)
    return p2


CMP_TM = 256
_HALF = CMP_STRIDE * HEAD_DIM_A


def _compress_kernel(t_ref, w1_ref, w2_ref, pos_ref, o_ref):
    t = t_ref[0]
    u = _dot(t, w1_ref[0, 0:_HALF, :])
    v = _dot(t, w1_ref[0, _HALF:2 * _HALF, :])
    pb = _dot(pos_ref[0], w1_ref[0])
    pre = u + pltpu.roll(v, CMP_TM - 1, 0) + pb[0:1]
    hid = pre * jax.nn.sigmoid(pre)
    out = _dot(hid.astype(BF16), w2_ref[0])
    row = lax.broadcasted_iota(jnp.int32, out.shape, 0)
    n_cp = LANES
    out = jnp.where((row & (n_cp - 1)) == n_cp - 1, 0.0, out)
    o_ref[0] = out.astype(o_ref.dtype)


def _compress(t_kv, w1, w2, pos):
    R = t_kv.shape[1]
    return pl.pallas_call(
        _compress_kernel,
        grid=(2, R // CMP_TM),
        in_specs=[
            pl.BlockSpec((1, CMP_TM, _HALF), lambda s, i: (s, i, 0)),
            pl.BlockSpec((1, 2 * _HALF, CMP_HIDDEN), lambda s, i: (s, 0, 0)),
            pl.BlockSpec((1, CMP_HIDDEN, HEAD_DIM_A), lambda s, i: (s, 0, 0)),
            pl.BlockSpec((1, 8, 2 * _HALF), lambda s, i: (s, 0, 0)),
        ],
        out_specs=pl.BlockSpec((1, CMP_TM, HEAD_DIM_A), lambda s, i: (s, i, 0)),
        out_shape=jax.ShapeDtypeStruct((2, R, HEAD_DIM_A), BF16),
        compiler_params=pltpu.CompilerParams(dimension_semantics=("arbitrary", "arbitrary")),
        name="compress",
    )(t_kv, w1, w2, pos)


TQ = 128
ONES_ROWS = 16


def _transpose_values(v_ref, vt_ref, n_rows):
    n_blk = vt_ref.shape[0]
    for blk in range(n_blk):
        vt = v_ref[0, blk * LANES:(blk + 1) * LANES, :].astype(F32).T
        vt_ref[blk, 0:n_rows, :] = vt[0:n_rows].astype(vt_ref.dtype)
        vt_ref[blk, n_rows:n_rows + ONES_ROWS, :] = jnp.ones((ONES_ROWS, LANES), vt_ref.dtype)


def _run_skewed(jobs):
    pending = [jobs[0][0]()]
    for i, (_, finish) in enumerate(jobs):
        if i + 1 < len(jobs):
            pending.append(jobs[i + 1][0]())
        finish(pending[i])
        pending[i] = None


def _softmax_step(s, m, acc, vt):
    m_new = jnp.maximum(m, jnp.max(s, axis=0, keepdims=True))
    p = jnp.exp2(s - m_new).astype(BF16)
    return m_new, jnp.exp2(m - m_new) * acc + _dot(vt, p)


SEL_TK = 256
N_SELP = 32
PAIR = 2 * TQ
_VROWS_A = HEAD_DIM_A + ONES_ROWS
_WIN_TILES = (256, 256, 128)
_N_SEL_JOBS = 9


def _attn_a_kernel(qlo_ref, qhi_ref, ksel_ref, vsel_ref, kwin_ref, vwin_ref, kc_ref, vc_ref,
                   glo_ref, ghi_ref, zlo_ref, zhi_ref, ov_ref, ylo_ref, yhi_ref,
                   vselt_ref, vwint_ref, qs_ref, bias_ref, acc_ref, out_ref):
    i = pl.program_id(2)
    n_qt = 2 * pl.num_programs(2)
    n_pairs = GROUP_A // 2
    t0s = (i * TQ, (n_qt - 1 - i) * TQ)
    q_refs, g_refs, z_refs, y_refs = (qlo_ref, qhi_ref), (glo_ref, ghi_ref), (zlo_ref, zhi_ref), (ylo_ref, yhi_ref)

    @pl.when(i == 0)
    def _():
        _transpose_values(vsel_ref, vselt_ref, HEAD_DIM_A)
        _transpose_values(vwin_ref, vwint_ref, HEAD_DIM_A)

    lane = lax.broadcasted_iota(jnp.int32, (1, TQ), 1)
    lane2 = jnp.concatenate([lane, lane], axis=1)
    qpair, gts = [], []
    for t in range(2):
        q = q_refs[t][0]
        pairs = [jnp.concatenate([q[:, h * LANES:(h + 1) * LANES] for h in (2 * p, 2 * p + 1)], axis=0)
                 for p in range(n_pairs)]
        for p in range(n_pairs):
            qs_ref[t, p] = pairs[p]
        qpair.append(pairs)
        gts.append(g_refs[t][0].astype(F32).T)

    def gate(t, p, branch):
        return jnp.concatenate(
            [gts[t][3 * h + branch:3 * h + branch + 1] for h in (2 * p, 2 * p + 1)], axis=1)

    kc = kc_ref[0]
    vct = vc_ref[0].astype(F32).T.astype(BF16)
    cend = lax.broadcasted_iota(jnp.int32, (LANES, PAIR), 0) * CMP_STRIDE + (CMP_BLOCK - 1)
    psums = [[], []]

    def cmp_scores(t, p):
        return jnp.where(cend <= t0s[t] + lane2, _dot_nt(kc, qpair[t][p]), NEG_INF)

    def cmp_finish(t, p, s):
        m = jnp.max(s, axis=0, keepdims=True)
        e = jnp.where(cend <= t0s[t] + lane2, jnp.exp2(s - m), 0.0)
        pc = e / jnp.maximum(jnp.sum(e, axis=0, keepdims=True), TINY)
        psums[t].append(pc[:, 0:TQ] + pc[:, TQ:PAIR])
        out_ref[t, p] = gate(t, p, 0) * _dot(vct, pc.astype(BF16))

    ksts = [pl.multiple_of(jnp.maximum(t0 - WIN_A, 0), TQ) for t0 in t0s]
    win_m = [[jnp.full((1, PAIR), NEG_INF, F32) for _ in range(n_pairs)] for _ in range(2)]
    win_acc = [[jnp.zeros((_VROWS_A, PAIR), F32) for _ in range(n_pairs)] for _ in range(2)]

    def win_scores(t, p, off, tk):
        kt = kwin_ref[0, pl.ds(pl.multiple_of(ksts[t] + off, LANES), tk), :]
        dpos = (t0s[t] + lane2) - (ksts[t] + off + lax.broadcasted_iota(jnp.int32, (tk, PAIR), 0))
        return jnp.where((dpos >= 0) & (dpos < WIN_A), _dot_nt(kt, qpair[t][p]), NEG_INF)

    def win_finish(t, p, off, tk, s):
        vt = jnp.concatenate(
            [vwint_ref[ksts[t] // LANES + off // LANES + b] for b in range(tk // LANES)], axis=1)
        win_m[t][p], win_acc[t][p] = _softmax_step(s, win_m[t][p], win_acc[t][p], vt)

    jobs = [(functools.partial(cmp_scores, t, p), functools.partial(cmp_finish, t, p))
            for t in range(2) for p in range(n_pairs)]
    off = 0
    for tk in _WIN_TILES:
        jobs += [(functools.partial(win_scores, t, p, off, tk), functools.partial(win_finish, t, p, off, tk))
                 for t in range(2) for p in range(n_pairs)]
        off += tk
    _run_skewed(jobs)
    for t in range(2):
        for p in range(n_pairs):
            a = win_acc[t][p]
            out_ref[t, p] = out_ref[t, p] + (
                gate(t, p, 2) / jnp.maximum(a[HEAD_DIM_A:HEAD_DIM_A + 1], TINY)) * a[0:HEAD_DIM_A]

    ov = ov_ref[...]
    jidx = lax.broadcasted_iota(jnp.int32, (N_SELP, TQ), 0)
    for t in range(2):
        psum = psums[t][0] + psums[t][1]
        p_hi = psum.astype(BF16)
        r1 = psum - p_hi.astype(F32)
        p_mid = r1.astype(BF16)
        p_lo = (r1 - p_mid.astype(F32)).astype(BF16)
        psel = _dot(ov, p_hi) + _dot(ov, p_mid) + _dot(ov, p_lo)
        tq = t0s[t] + lane
        cur = tq // SEL_BLOCK
        forced = (jidx == 0) | (jidx == cur) | (jidx == cur - 1)
        score = jnp.where(forced, FORCE_SCORE, jnp.where(jidx * SEL_BLOCK <= tq, psel, -1.0))
        rank = jnp.zeros(psel.shape, F32)
        for r in range(N_SELP):
            row = jnp.broadcast_to(score[r:r + 1, :], score.shape)
            rank = rank + jnp.where(jidx > r, jnp.where(row >= score, 1.0, 0.0), jnp.where(row > score, 1.0, 0.0))
        bias_t = jnp.where(rank < SEL_TOPK, 0.0, NEG_INF)
        for j in range(N_SELP):
            bias_ref[t, j] = jnp.broadcast_to(bias_t[j:j + 1, :], (SUBLANES, TQ))

    blk_per_tile = SEL_TK // SEL_BLOCK
    rep = SEL_BLOCK // SUBLANES
    n_lo = i // (SEL_TK // TQ) + 1
    acc_ref[...] = jnp.zeros(acc_ref.shape, F32)
    ms = [[jnp.full((1, PAIR), NEG_INF, F32) for _ in range(n_pairs)] for _ in range(2)]
    ctx = {}

    def sel_scores(k, p):
        if p == 0:
            hi = k >= n_lo
            t = hi.astype(jnp.int32)
            j = jnp.where(hi, k - n_lo, k)
            ks = pl.multiple_of(j * SEL_TK, SEL_TK)
            bt = jnp.concatenate(
                [bias_ref[t, j * blk_per_tile + b] for b in range(blk_per_tile) for _ in range(rep)], axis=0)
            bt = jnp.concatenate([bt, bt], axis=1)
            kpos = ks + lax.broadcasted_iota(jnp.int32, (SEL_TK, PAIR), 0)
            bt = jnp.where(kpos <= jnp.where(hi, t0s[1], t0s[0]) + lane2, bt, NEG_INF)
            ctx[k] = (hi, t, j, ks, bt)
        hi, t, j, ks, bt = ctx[k]
        return _dot_nt(ksel_ref[0, pl.ds(ks, SEL_TK), :], qs_ref[t, p]) + bt

    def sel_finish(k, p, s):
        hi, t, j, ks, bt = ctx[k]
        vt = jnp.concatenate([vselt_ref[j * (SEL_TK // LANES) + b] for b in range(SEL_TK // LANES)], axis=1)
        m_new, acc_new = _softmax_step(s, jnp.where(hi, ms[1][p], ms[0][p]), acc_ref[t, p], vt)
        acc_ref[t, p] = acc_new
        ms[0][p] = jnp.where(hi, ms[0][p], m_new)
        ms[1][p] = jnp.where(hi, m_new, ms[1][p])

    sel_jobs = [(functools.partial(sel_scores, k, p), functools.partial(sel_finish, k, p))
                for k in range(_N_SEL_JOBS) for p in range(n_pairs)]
    _run_skewed(sel_jobs)

    for t in range(2):
        z = z_refs[t][0]
        for p in range(n_pairs):
            a = acc_ref[t, p]
            o = out_ref[t, p] + (gate(t, p, 1) / jnp.maximum(a[HEAD_DIM_A:HEAD_DIM_A + 1], TINY)) * a[0:HEAD_DIM_A]
            for h in (2 * p, 2 * p + 1):
                oh = o[:, (h % 2) * TQ:(h % 2 + 1) * TQ].T
                y_refs[t][0, :, h * LANES:(h + 1) * LANES] = (
                    oh * z[:, h * LANES:(h + 1) * LANES].astype(F32)).astype(y_refs[t].dtype)


def _attn_a(p3, kvc, ov):
    B, S, _ = p3.shape
    n_cp = S // CMP_STRIDE
    n_qt = S // TQ
    assert n_cp == LANES and S // SEL_BLOCK == N_SELP and (n_qt // 2) // (SEL_TK // TQ) * 2 + 1 == _N_SEL_JOBS
    wq = GROUP_A * HEAD_DIM_A
    cb = lambda name: _col(name) // LANES
    kv_spec = lambda name: pl.BlockSpec((1, S, LANES), lambda b, k, i, c=cb(name): (b, 0, c + k))
    lo = lambda width, c: pl.BlockSpec((1, TQ, width), lambda b, k, i: (b, i, c + k))
    hi = lambda width, c: pl.BlockSpec((1, TQ, width), lambda b, k, i: (b, n_qt - 1 - i, c + k))
    y_sds = jax.ShapeDtypeStruct((B, S // 2, WIDTH_A), BF16)
    return pl.pallas_call(
        _attn_a_kernel,
        grid=(B, N_KV_A, n_qt // 2),
        in_specs=[
            lo(wq, _col("qa") // wq), hi(wq, _col("qa") // wq),
            kv_spec("ksa"), kv_spec("vsa"), kv_spec("kwa"), kv_spec("vwa"),
            pl.BlockSpec((1, n_cp, LANES), lambda b, k, i: (b * N_KV_A + k, 0, 0)),
            pl.BlockSpec((1, n_cp, LANES), lambda b, k, i: (B * N_KV_A + b * N_KV_A + k, 0, 0)),
            lo(LANES, cb("ga")), hi(LANES, cb("ga")),
            lo(wq, _col("za") // wq), hi(wq, _col("za") // wq),
            pl.BlockSpec((N_SELP, LANES), lambda b, k, i: (0, 0)),
        ],
        out_specs=(pl.BlockSpec((1, TQ, wq), lambda b, k, i: (b, i, k)),
                   pl.BlockSpec((1, TQ, wq), lambda b, k, i: (b, n_qt // 2 - 1 - i, k))),
        out_shape=(y_sds, y_sds),
        scratch_shapes=[
            pltpu.VMEM((S // LANES, _VROWS_A, LANES), BF16),
            pltpu.VMEM((S // LANES, _VROWS_A, LANES), BF16),
            pltpu.VMEM((2, GROUP_A // 2, PAIR, HEAD_DIM_A), BF16),
            pltpu.VMEM((2, N_SELP, SUBLANES, TQ), F32),
            pltpu.VMEM((2, GROUP_A // 2, _VROWS_A, PAIR), F32),
            pltpu.VMEM((2, GROUP_A // 2, HEAD_DIM_A, PAIR), F32),
        ],
        compiler_params=pltpu.CompilerParams(
            dimension_semantics=("arbitrary", "arbitrary", "arbitrary"), vmem_limit_bytes=48 * 1024 * 1024),
        name="attn_a",
    )(p3, p3, p3, p3, p3, p3, kvc, kvc, p3, p3, p3, p3, ov)


_VROWS_B = HEAD_DIM_B + ONES_ROWS


def _attn_b_kernel(sink_ref, q_ref, k_ref, v_ref, z_ref, y_ref, vt_ref):
    kv = pl.program_id(1)
    qi = pl.program_id(2)
    t0 = qi * TQ
    n_keys = WIN_B + TQ

    @pl.when(qi == 0)
    def _():
        _transpose_values(v_ref, vt_ref, HEAD_DIM_B)

    q = q_ref[0]
    left = lax.broadcasted_iota(jnp.int32, (TQ, LANES), 1) < HEAD_DIM_B
    zero = jnp.zeros((TQ, LANES), q.dtype)

    kst = pl.multiple_of(jnp.maximum(t0 - WIN_B, 0), TQ)
    kt = k_ref[0, pl.ds(kst, n_keys), :]
    kb0 = kst // LANES
    vt = jnp.concatenate([vt_ref[kb0 + i] for i in range(n_keys // LANES)], axis=1)
    tq = t0 + lax.broadcasted_iota(jnp.int32, (1, TQ), 1)
    tq2 = jnp.concatenate([tq, tq], axis=1)
    dpos = tq2 - (kst + lax.broadcasted_iota(jnp.int32, (n_keys, PAIR), 0))
    bias = jnp.where((dpos >= 0) & (dpos < WIN_B), 0.0, NEG_INF)

    z = z_ref[0]

    def scores(c):
        pair = q[:, c * LANES:(c + 1) * LANES]
        qp = jnp.concatenate([jnp.where(left, pair, zero), jnp.where(left, zero, pair)], axis=0)
        return _dot_nt(kt, qp) + bias

    def finish(c, s):
        sink = jnp.concatenate(
            [jnp.full((1, TQ), sink_ref[kv * GROUP_B + 2 * c + i] * LOG2E, F32) for i in range(2)], axis=1)
        m = jnp.maximum(jnp.max(s, axis=0, keepdims=True), sink)
        p = jnp.exp2(s - m).astype(BF16)
        a = _dot(vt, p)
        den = jnp.maximum(a[HEAD_DIM_B:HEAD_DIM_B + 1] + jnp.exp2(sink - m), TINY)
        o = a[0:HEAD_DIM_B] / den
        o2 = jnp.concatenate([o[:, 0:TQ], o[:, TQ:PAIR]], axis=0)
        y_ref[0, :, c * LANES:(c + 1) * LANES] = (
            o2.T * z[:, c * LANES:(c + 1) * LANES].astype(F32)).astype(y_ref.dtype)

    _run_skewed([(functools.partial(scores, c), functools.partial(finish, c)) for c in range(GROUP_B // 2)])


def _attn_b(sinks, p3):
    B, S, _ = p3.shape
    wq = GROUP_B * HEAD_DIM_B
    cb = lambda name: _col(name) // LANES
    grid_spec = pltpu.PrefetchScalarGridSpec(
        num_scalar_prefetch=1,
        grid=(B, N_KV_B, S // TQ),
        in_specs=[
            pl.BlockSpec((1, TQ, wq), lambda b, k, i, s: (b, i, _col("qb") // wq + k)),
            pl.BlockSpec((1, S, LANES), lambda b, k, i, s, c=cb("kb2"): (b, 0, c + k)),
            pl.BlockSpec((1, S, LANES), lambda b, k, i, s, c=cb("vb2"): (b, 0, c + k)),
            pl.BlockSpec((1, TQ, wq), lambda b, k, i, s: (b, i, _col("zb") // wq + k)),
        ],
        out_specs=pl.BlockSpec((1, TQ, wq), lambda b, k, i, s: (b, i, k)),
        scratch_shapes=[pltpu.VMEM((S // LANES, _VROWS_B, LANES), BF16)],
    )
    return pl.pallas_call(
        _attn_b_kernel,
        grid_spec=grid_spec,
        out_shape=jax.ShapeDtypeStruct((B, S, WIDTH_B), BF16),
        compiler_params=pltpu.CompilerParams(
            dimension_semantics=("arbitrary", "arbitrary", "arbitrary")),
        name="attn_b",
    )(sinks, p3, p3, p3, p3)


OUT_TM = 512


def _out_proj_kernel(ya_ref, yb_ref, wa_ref, wb_ref, x_ref, g_ref, o_ref):
    r = x_ref[...] + _dot(ya_ref[...], wa_ref[...]) + _dot(yb_ref[...], wb_ref[...])
    ms = jnp.mean(r * r, axis=-1, keepdims=True)
    o_ref[...] = (r * lax.rsqrt(ms + RMS_EPS)) * g_ref[...]


def _out_proj(ya2, yb2, w_out, x2, g):
    M = x2.shape[0]
    return pl.pallas_call(
        _out_proj_kernel,
        grid=(M // OUT_TM,),
        in_specs=[
            pl.BlockSpec((OUT_TM, WIDTH_A), lambda i: (i, 0)),
            pl.BlockSpec((OUT_TM, WIDTH_B), lambda i: (i, 0)),
            pl.BlockSpec((WIDTH_A, D_MODEL), lambda i: (0, 0)),
            pl.BlockSpec((WIDTH_B, D_MODEL), lambda i: (WIDTH_A // WIDTH_B, 0)),
            pl.BlockSpec((OUT_TM, D_MODEL), lambda i: (i, 0)),
            pl.BlockSpec((1, D_MODEL), lambda i: (0, 0)),
        ],
        out_specs=pl.BlockSpec((OUT_TM, D_MODEL), lambda i: (i, 0)),
        out_shape=jax.ShapeDtypeStruct((M, D_MODEL), F32),
        compiler_params=pltpu.CompilerParams(
            dimension_semantics=("arbitrary",), vmem_limit_bytes=56 * 1024 * 1024),
        name="out_proj",
    )(ya2, yb2, w_out, w_out, x2, g)


def _overlap_matrix(n_cp):
    c_start = np.arange(n_cp) * CMP_STRIDE
    j_start = np.arange(N_SELP) * SEL_BLOCK
    ov = (c_start[None, :] < j_start[:, None] + SEL_BLOCK) & (c_start[None, :] + CMP_BLOCK > j_start[:, None])
    return jnp.asarray(ov, BF16)


def _stride_rows(p3, col0):
    B, S, _ = p3.shape
    t = p3[:, :, col0:col0 + KV_A].reshape(B, S // CMP_STRIDE, CMP_STRIDE, N_KV_A, HEAD_DIM_A)
    return t.transpose(0, 3, 1, 2, 4).reshape(B * N_KV_A * (S // CMP_STRIDE), _HALF)


def kernel(x, w_in, cmp_k_w1, cmp_k_w2, cmp_v_w1, cmp_v_w2, cmp_k_pos, cmp_v_pos, sinks, w_out, norm_g, final_g):
    B, S, D = x.shape
    assert D == D_MODEL and w_in.shape[0] == 1
    x2 = x.reshape(B * S, D)

    w_all = _prep_w_in(w_in[0])
    tabs = _rope_tables(S)
    p2 = _in_proj(x2, norm_g[0].reshape(1, D), w_all, tabs, S)
    p3 = p2.reshape(B, S, D_PROJ)

    t_kv = jnp.stack([_stride_rows(p3, _col("kca")), _stride_rows(p3, _col("vca"))])
    w1 = jnp.stack([cmp_k_w1[0], cmp_v_w1[0]]).astype(BF16)
    w2 = jnp.stack([cmp_k_w2[0], cmp_v_w2[0]]).astype(BF16)
    pos = jnp.stack([cmp_k_pos[0], cmp_v_pos[0]]).reshape(2, 1, CMP_BLOCK * HEAD_DIM_A)
    pos = jnp.broadcast_to(pos, (2, 8, CMP_BLOCK * HEAD_DIM_A)).astype(BF16)
    kvc = _compress(t_kv, w1, w2, pos)
    kvc = kvc.reshape(2 * B * N_KV_A, S // CMP_STRIDE, HEAD_DIM_A)

    y_a = jnp.concatenate(_attn_a(p3, kvc, _overlap_matrix(S // CMP_STRIDE)), axis=1)
    y_b = _attn_b(sinks[0], p3)

    out = _out_proj(y_a.reshape(B * S, WIDTH_A), y_b.reshape(B * S, WIDTH_B), w_out[0].astype(BF16), x2,
                    final_g.reshape(1, D))
    return out.reshape(B, S, D)
```

```python
import functools
import math

import numpy as np
import jax
import jax.numpy as jnp
from jax import lax
from jax.experimental import pallas as pl
from jax.experimental.pallas import tpu as pltpu

F32 = jnp.float32
BF16 = jnp.bfloat16

D_MODEL = 2048
ROPE_THETA = 10000.0
RMS_EPS = 1e-6
NEG_INF = -1e30
TINY = 1e-30
LOG2E = math.log2(math.e)

WIDTH_A = 1024
HEAD_DIM_A = 128
N_KV_A = 2
GROUP_A = 4
KV_A = N_KV_A * HEAD_DIM_A
CMP_BLOCK = 32
CMP_STRIDE = 16
CMP_HIDDEN = 256
SEL_BLOCK = 64
SEL_TOPK = 16
WIN_A = 512
FORCE_SCORE = 1e4

WIDTH_B = 1024
HEAD_DIM_B = 64
N_HEADS_B = 16
N_KV_B = 2
GROUP_B = 8
KV_B = N_KV_B * HEAD_DIM_B
WIN_B = 128

IN_SIZES = (WIDTH_A, KV_A, KV_A, KV_A, KV_A, KV_A, KV_A, WIDTH_A, 3 * 8,
            WIDTH_B, KV_B, KV_B, WIDTH_B)

LANES = 128
SUBLANES = 8
CHUNK = 256

EP_NONE, EP_ROPE128, EP_ROPE64, EP_SILU, EP_SIGMOID, EP_ROPE128_Q, EP_ROPE64_Q = range(7)

_PROJ_LAYOUT = (
    ("qa", 4, EP_ROPE128_Q), ("kca", 1, EP_ROPE128),
    ("ksa", 1, EP_ROPE128), ("kwa", 1, EP_ROPE128), ("vca", 1, EP_NONE), ("vsa", 1, EP_NONE), ("vwa", 1, EP_NONE),
    ("za", 4, EP_SILU), ("ga", 1, EP_SIGMOID),
    ("zb", 4, EP_SILU), ("kb2", 1, EP_ROPE64),
    ("qb", 4, EP_ROPE64_Q), ("vb2", 1, EP_NONE),
)
_CHUNK_KINDS = tuple(k for _, n, k in _PROJ_LAYOUT for _ in range(n))
_CHUNK_START = {}
_c = 0
for _name, _n, _k in _PROJ_LAYOUT:
    _CHUNK_START[_name] = _c
    _c += _n
N_CHUNKS = _c
D_PROJ = N_CHUNKS * CHUNK
IN_TN = 1280
_IN_CPT = IN_TN // CHUNK
N_SLABS = D_PROJ // IN_TN


def _slab(name):
    return _CHUNK_START[name] // _IN_CPT


def _col(name):
    return (_CHUNK_START[name] % _IN_CPT) * CHUNK


def _dot(a, b):
    return jnp.dot(a, b, preferred_element_type=F32)


def _dot_nt(a, b):
    return lax.dot_general(a, b, (((1,), (1,)), ((), ())), preferred_element_type=F32)


IN_TM_NORM = 512
IN_TM = 1024
IN_SUB = 512

_TAB_GROUP = {EP_ROPE128: "r128", EP_ROPE128_Q: "r128q", EP_ROPE64: "r64", EP_ROPE64_Q: "r64q"}
_TAB_SIZE = {"r128": 2, "r128q": 2, "r64": 3, "r64q": 3}


def _slab_tab_groups(kinds):
    return tuple(dict.fromkeys(_TAB_GROUP[k] for k in kinds if k in _TAB_GROUP))


def _in_proj_kernel(kinds, with_norm, seq_len, *refs):
    groups = _slab_tab_groups(kinds)
    refs = list(refs)
    if with_norm:
        x_ref, g_ref = refs.pop(0), refs.pop(0)
    else:
        h_ref = refs.pop(0)
    w_ref = refs.pop(0)
    tab_ref = refs.pop(0) if groups else None
    o_ref = refs.pop(0)
    if with_norm:
        h_ref = refs.pop(0)
        x = x_ref[...]
        ms = jnp.mean(x * x, axis=-1, keepdims=True)
        h_ref[...] = ((x * lax.rsqrt(ms + RMS_EPS)) * g_ref[...]).astype(BF16)
    tm = h_ref.shape[0]
    pos0 = (pl.program_id(0) % (seq_len // tm)) * tm
    base, n = {}, 0
    for grp in groups:
        base[grp] = n
        n += _TAB_SIZE[grp]

    def epilogue(kind, r, a):
        if kind == EP_NONE:
            return a
        if kind == EP_SILU:
            return a * jax.nn.sigmoid(a)
        if kind == EP_SIGMOID:
            return jax.nn.sigmoid(a)
        t = base[_TAB_GROUP[kind]]
        tab = lambda k: tab_ref[t + k, pl.ds(pl.multiple_of(pos0 + r * IN_SUB, IN_SUB), IN_SUB), :]
        if kind in (EP_ROPE128, EP_ROPE128_Q):
            return a * tab(0) + pltpu.roll(a, 64, 1) * tab(1)
        return a * tab(0) + pltpu.roll(a, 96, 1) * tab(1) + pltpu.roll(a, 32, 1) * tab(2)

    def finish(kind, r, c, acc):
        rows = slice(r * IN_SUB, (r + 1) * IN_SUB)
        for sub in range(CHUNK // LANES):
            cols = slice(c * CHUNK + sub * LANES, c * CHUNK + (sub + 1) * LANES)
            o_ref[rows, cols] = epilogue(kind, r, acc[:, sub * LANES:(sub + 1) * LANES]).astype(o_ref.dtype)

    jobs = []
    for r in range(tm // IN_SUB):
        for c, kind in enumerate(kinds):
            jobs.append((lambda r=r, c=c: _dot(h_ref[r * IN_SUB:(r + 1) * IN_SUB, :],
                                               w_ref[:, c * CHUNK:(c + 1) * CHUNK]),
                         functools.partial(finish, kind, r, c)))
    _run_skewed(jobs)


def _rope_tables(S):
    def cs(d):
        inv = ROPE_THETA ** (-jnp.arange(0, d, 2, dtype=F32) / d)
        ang = jnp.arange(S, dtype=F32)[:, None] * inv[None, :]
        return jnp.cos(ang), jnp.sin(ang)

    c128, s128 = cs(HEAD_DIM_A)
    c64, s64 = cs(HEAD_DIM_B)
    z32 = jnp.zeros_like(s64)
    t128 = [jnp.concatenate([c128, c128], axis=1), jnp.concatenate([-s128, s128], axis=1)]
    t64 = [jnp.concatenate([c64, c64, c64, c64], axis=1),
           jnp.concatenate([-s64, z32, -s64, z32], axis=1),
           jnp.concatenate([z32, s64, z32, s64], axis=1)]
    qa = HEAD_DIM_A ** -0.5 * LOG2E
    qb = HEAD_DIM_B ** -0.5 * LOG2E
    return {"r128": t128, "r64": t64, "r128q": [t * qa for t in t128], "r64q": [t * qb for t in t64]}


def _prep_w_in(w):
    offs = np.cumsum((0,) + IN_SIZES)
    seg = [w[:, offs[i]:offs[i + 1]] for i in range(len(IN_SIZES))]
    qa, kca, vca, ksa, vsa, kwa, vwa, za, ga, qb, kb, vb, zb = seg
    kb0, kb1 = kb[:, :HEAD_DIM_B], kb[:, HEAD_DIM_B:]
    vb0, vb1 = vb[:, :HEAD_DIM_B], vb[:, HEAD_DIM_B:]
    n_g = 3 * GROUP_A
    pad = jnp.zeros((w.shape[0], LANES - n_g), w.dtype)
    cols = {"qa": [qa], "kca": [kca], "ksa": [ksa], "kwa": [kwa], "vca": [vca], "vsa": [vsa], "vwa": [vwa],
            "za": [za], "ga": [ga[:, :n_g], pad, ga[:, n_g:], pad], "zb": [zb],
            "kb2": [kb0, kb0, kb1, kb1], "qb": [qb], "vb2": [vb0, vb0, vb1, vb1]}
    out = jnp.concatenate([c for name, _, _ in _PROJ_LAYOUT for c in cols[name]], axis=1).astype(BF16)
    assert out.shape[1] == D_PROJ
    return out


def _in_proj(x2, g, w_all, tabs, S):
    M = x2.shape[0]
    params = pltpu.CompilerParams(dimension_semantics=("arbitrary",), vmem_limit_bytes=48 * 1024 * 1024)
    out_sds = jax.ShapeDtypeStruct((M, IN_TN), BF16)
    slabs, h = [], None
    for j in range(N_SLABS):
        kinds = _CHUNK_KINDS[j * _IN_CPT:(j + 1) * _IN_CPT]
        with_norm = j == 0
        tm = IN_TM_NORM if with_norm else IN_TM
        groups = _slab_tab_groups(kinds)
        act_spec = pl.BlockSpec((tm, D_MODEL), lambda i: (i, 0))
        in_specs = [act_spec, pl.BlockSpec((1, D_MODEL), lambda i: (0, 0))] if with_norm else [act_spec]
        args = [x2, g] if with_norm else [h]
        in_specs.append(pl.BlockSpec((D_MODEL, IN_TN), lambda i, j=j: (0, j)))
        args.append(w_all)
        if groups:
            tab = jnp.stack([t for grp in groups for t in tabs[grp]])
            in_specs.append(pl.BlockSpec(tab.shape, lambda i: (0, 0, 0)))
            args.append(tab)
        o_spec = pl.BlockSpec((tm, IN_TN), lambda i: (i, 0))
        res = pl.pallas_call(
            functools.partial(_in_proj_kernel, kinds, with_norm, S),
            grid=(M // tm,),
            in_specs=in_specs,
            out_specs=(o_spec, act_spec) if with_norm else o_spec,
            out_shape=(out_sds, jax.ShapeDtypeStruct((M, D_MODEL), BF16)) if with_norm else out_sds,
            compiler_params=params,
            name=f"in_proj_{j}",
        )(*args)
        if with_norm:
            res, h = res
        slabs.append(res)
    return slabs


CMP_TM = 256
_HALF = CMP_STRIDE * HEAD_DIM_A


def _compress_kernel(t_ref, w1_ref, w2_ref, pos_ref, o_ref):
    t = t_ref[0]
    u = _dot(t, w1_ref[0, 0:_HALF, :])
    v = _dot(t, w1_ref[0, _HALF:2 * _HALF, :])
    pb = _dot(pos_ref[0], w1_ref[0])
    pre = u + pltpu.roll(v, CMP_TM - 1, 0) + pb[0:1]
    hid = pre * jax.nn.sigmoid(pre)
    out = _dot(hid.astype(BF16), w2_ref[0])
    row = lax.broadcasted_iota(jnp.int32, out.shape, 0)
    n_cp = LANES
    out = jnp.where((row & (n_cp - 1)) == n_cp - 1, 0.0, out)
    o_ref[0] = out.astype(o_ref.dtype)


def _compress(t_kv, w1, w2, pos):
    R = t_kv.shape[1]
    return pl.pallas_call(
        _compress_kernel,
        grid=(2, R // CMP_TM),
        in_specs=[
            pl.BlockSpec((1, CMP_TM, _HALF), lambda s, i: (s, i, 0)),
            pl.BlockSpec((1, 2 * _HALF, CMP_HIDDEN), lambda s, i: (s, 0, 0)),
            pl.BlockSpec((1, CMP_HIDDEN, HEAD_DIM_A), lambda s, i: (s, 0, 0)),
            pl.BlockSpec((1, 8, 2 * _HALF), lambda s, i: (s, 0, 0)),
        ],
        out_specs=pl.BlockSpec((1, CMP_TM, HEAD_DIM_A), lambda s, i: (s, i, 0)),
        out_shape=jax.ShapeDtypeStruct((2, R, HEAD_DIM_A), BF16),
        compiler_params=pltpu.CompilerParams(dimension_semantics=("arbitrary", "arbitrary")),
        name="compress",
    )(t_kv, w1, w2, pos)


TQ = 128
ONES_ROWS = 16


def _transpose_values(v_ref, vt_ref, n_rows):
    n_blk = vt_ref.shape[0]
    for blk in range(n_blk):
        vt = v_ref[0, blk * LANES:(blk + 1) * LANES, :].astype(F32).T
        vt_ref[blk, 0:n_rows, :] = vt[0:n_rows].astype(vt_ref.dtype)
        vt_ref[blk, n_rows:n_rows + ONES_ROWS, :] = jnp.ones((ONES_ROWS, LANES), vt_ref.dtype)


def _run_skewed(jobs):
    pending = [jobs[0][0]()]
    for i, (_, finish) in enumerate(jobs):
        if i + 1 < len(jobs):
            pending.append(jobs[i + 1][0]())
        finish(pending[i])
        pending[i] = None


def _softmax_step(s, m, acc, vt):
    m_new = jnp.maximum(m, jnp.max(s, axis=0, keepdims=True))
    p = jnp.exp2(s - m_new).astype(BF16)
    return m_new, jnp.exp2(m - m_new) * acc + _dot(vt, p)


SEL_TK = 256
N_SELP = 32
PAIR = 2 * TQ
_VROWS_A = HEAD_DIM_A + ONES_ROWS
_WIN_TILES = (256, 256, 128)
_N_SEL_JOBS = 9


def _attn_a_kernel(qlo_ref, qhi_ref, ksel_ref, vsel_ref, kwin_ref, vwin_ref, kc_ref, vc_ref,
                   glo_ref, ghi_ref, zlo_ref, zhi_ref, ov_ref, ylo_ref, yhi_ref,
                   vselt_ref, vwint_ref, qs_ref, bias_ref, acc_ref, out_ref):
    i = pl.program_id(2)
    n_qt = 2 * pl.num_programs(2)
    n_pairs = GROUP_A // 2
    t0s = (i * TQ, (n_qt - 1 - i) * TQ)
    q_refs, g_refs, z_refs, y_refs = (qlo_ref, qhi_ref), (glo_ref, ghi_ref), (zlo_ref, zhi_ref), (ylo_ref, yhi_ref)

    @pl.when(i == 0)
    def _():
        _transpose_values(vsel_ref, vselt_ref, HEAD_DIM_A)
        _transpose_values(vwin_ref, vwint_ref, HEAD_DIM_A)

    lane = lax.broadcasted_iota(jnp.int32, (1, TQ), 1)
    lane2 = jnp.concatenate([lane, lane], axis=1)
    qpair, gts = [], []
    for t in range(2):
        q = q_refs[t][0]
        pairs = [jnp.concatenate([q[:, h * LANES:(h + 1) * LANES] for h in (2 * p, 2 * p + 1)], axis=0)
                 for p in range(n_pairs)]
        for p in range(n_pairs):
            qs_ref[t, p] = pairs[p]
        qpair.append(pairs)
        gts.append(g_refs[t][0].astype(F32).T)

    def gate(t, p, branch):
        return jnp.concatenate(
            [gts[t][3 * h + branch:3 * h + branch + 1] for h in (2 * p, 2 * p + 1)], axis=1)

    kc = kc_ref[0]
    vct = vc_ref[0].astype(F32).T.astype(BF16)
    cend = lax.broadcasted_iota(jnp.int32, (LANES, PAIR), 0) * CMP_STRIDE + (CMP_BLOCK - 1)
    psums = [[], []]

    def cmp_scores(t, p):
        return jnp.where(cend <= t0s[t] + lane2, _dot_nt(kc, qpair[t][p]), NEG_INF)

    def cmp_finish(t, p, s):
        m = jnp.max(s, axis=0, keepdims=True)
        e = jnp.where(cend <= t0s[t] + lane2, jnp.exp2(s - m), 0.0)
        pc = e / jnp.maximum(jnp.sum(e, axis=0, keepdims=True), TINY)
        psums[t].append(pc[:, 0:TQ] + pc[:, TQ:PAIR])
        out_ref[t, p] = gate(t, p, 0) * _dot(vct, pc.astype(BF16))

    ksts = [pl.multiple_of(jnp.maximum(t0 - WIN_A, 0), TQ) for t0 in t0s]
    win_m = [[jnp.full((1, PAIR), NEG_INF, F32) for _ in range(n_pairs)] for _ in range(2)]
    win_acc = [[jnp.zeros((_VROWS_A, PAIR), F32) for _ in range(n_pairs)] for _ in range(2)]

    def win_scores(t, p, off, tk):
        kt = kwin_ref[0, pl.ds(pl.multiple_of(ksts[t] + off, LANES), tk), :]
        dpos = (t0s[t] + lane2) - (ksts[t] + off + lax.broadcasted_iota(jnp.int32, (tk, PAIR), 0))
        return jnp.where((dpos >= 0) & (dpos < WIN_A), _dot_nt(kt, qpair[t][p]), NEG_INF)

    def win_finish(t, p, off, tk, s):
        vt = jnp.concatenate(
            [vwint_ref[ksts[t] // LANES + off // LANES + b] for b in range(tk // LANES)], axis=1)
        win_m[t][p], win_acc[t][p] = _softmax_step(s, win_m[t][p], win_acc[t][p], vt)

    jobs = [(functools.partial(cmp_scores, t, p), functools.partial(cmp_finish, t, p))
            for t in range(2) for p in range(n_pairs)]
    off = 0
    for tk in _WIN_TILES:
        jobs += [(functools.partial(win_scores, t, p, off, tk), functools.partial(win_finish, t, p, off, tk))
                 for t in range(2) for p in range(n_pairs)]
        off += tk
    _run_skewed(jobs)
    for t in range(2):
        for p in range(n_pairs):
            a = win_acc[t][p]
            out_ref[t, p] = out_ref[t, p] + (
                gate(t, p, 2) / jnp.maximum(a[HEAD_DIM_A:HEAD_DIM_A + 1], TINY)) * a[0:HEAD_DIM_A]

    ov = ov_ref[...]
    jidx = lax.broadcasted_iota(jnp.int32, (N_SELP, TQ), 0)
    for t in range(2):
        psum = psums[t][0] + psums[t][1]
        p_hi = psum.astype(BF16)
        r1 = psum - p_hi.astype(F32)
        p_mid = r1.astype(BF16)
        p_lo = (r1 - p_mid.astype(F32)).astype(BF16)
        psel = _dot(ov, p_hi) + _dot(ov, p_mid) + _dot(ov, p_lo)
        tq = t0s[t] + lane
        cur = tq // SEL_BLOCK
        forced = (jidx == 0) | (jidx == cur) | (jidx == cur - 1)
        score = jnp.where(forced, FORCE_SCORE, jnp.where(jidx * SEL_BLOCK <= tq, psel, -1.0))
        rank = jnp.zeros(psel.shape, F32)
        for r in range(N_SELP):
            row = jnp.broadcast_to(score[r:r + 1, :], score.shape)
            rank = rank + jnp.where(jidx > r, jnp.where(row >= score, 1.0, 0.0), jnp.where(row > score, 1.0, 0.0))
        bias_t = jnp.where(rank < SEL_TOPK, 0.0, NEG_INF)
        for j in range(N_SELP):
            bias_ref[t, j] = jnp.broadcast_to(bias_t[j:j + 1, :], (SUBLANES, TQ))

    blk_per_tile = SEL_TK // SEL_BLOCK
    rep = SEL_BLOCK // SUBLANES
    n_lo = i // (SEL_TK // TQ) + 1
    acc_ref[...] = jnp.zeros(acc_ref.shape, F32)
    ms = [[jnp.full((1, PAIR), NEG_INF, F32) for _ in range(n_pairs)] for _ in range(2)]
    ctx = {}

    def sel_scores(k, p):
        if p == 0:
            hi = k >= n_lo
            t = hi.astype(jnp.int32)
            j = jnp.where(hi, k - n_lo, k)
            ks = pl.multiple_of(j * SEL_TK, SEL_TK)
            bt = jnp.concatenate(
                [bias_ref[t, j * blk_per_tile + b] for b in range(blk_per_tile) for _ in range(rep)], axis=0)
            bt = jnp.concatenate([bt, bt], axis=1)
            kpos = ks + lax.broadcasted_iota(jnp.int32, (SEL_TK, PAIR), 0)
            bt = jnp.where(kpos <= jnp.where(hi, t0s[1], t0s[0]) + lane2, bt, NEG_INF)
            ctx[k] = (hi, t, j, ks, bt)
        hi, t, j, ks, bt = ctx[k]
        return _dot_nt(ksel_ref[0, pl.ds(ks, SEL_TK), :], qs_ref[t, p]) + bt

    def sel_finish(k, p, s):
        hi, t, j, ks, bt = ctx[k]
        vt = jnp.concatenate([vselt_ref[j * (SEL_TK // LANES) + b] for b in range(SEL_TK // LANES)], axis=1)
        m_new, acc_new = _softmax_step(s, jnp.where(hi, ms[1][p], ms[0][p]), acc_ref[t, p], vt)
        acc_ref[t, p] = acc_new
        ms[0][p] = jnp.where(hi, ms[0][p], m_new)
        ms[1][p] = jnp.where(hi, m_new, ms[1][p])

    sel_jobs = [(functools.partial(sel_scores, k, p), functools.partial(sel_finish, k, p))
                for k in range(_N_SEL_JOBS) for p in range(n_pairs)]
    _run_skewed(sel_jobs)

    for t in range(2):
        z = z_refs[t][0]
        for p in range(n_pairs):
            a = acc_ref[t, p]
            o = out_ref[t, p] + (gate(t, p, 1) / jnp.maximum(a[HEAD_DIM_A:HEAD_DIM_A + 1], TINY)) * a[0:HEAD_DIM_A]
            for h in (2 * p, 2 * p + 1):
                oh = o[:, (h % 2) * TQ:(h % 2 + 1) * TQ].T
                y_refs[t][0, :, h * LANES:(h + 1) * LANES] = (
                    oh * z[:, h * LANES:(h + 1) * LANES].astype(F32)).astype(y_refs[t].dtype)


def _attn_a(slabs, kvc, ov):
    B, S, _ = slabs[0].shape
    sl = lambda name: slabs[_slab(name)]
    n_cp = S // CMP_STRIDE
    n_qt = S // TQ
    assert n_cp == LANES and S // SEL_BLOCK == N_SELP and (n_qt // 2) // (SEL_TK // TQ) * 2 + 1 == _N_SEL_JOBS
    wq = GROUP_A * HEAD_DIM_A
    cb = lambda name: _col(name) // LANES
    kv_spec = lambda name: pl.BlockSpec((1, S, LANES), lambda b, k, i, c=cb(name): (b, 0, c + k))
    lo = lambda width, c: pl.BlockSpec((1, TQ, width), lambda b, k, i: (b, i, c + k))
    hi = lambda width, c: pl.BlockSpec((1, TQ, width), lambda b, k, i: (b, n_qt - 1 - i, c + k))
    y_sds = jax.ShapeDtypeStruct((B, S // 2, WIDTH_A), BF16)
    return pl.pallas_call(
        _attn_a_kernel,
        grid=(B, N_KV_A, n_qt // 2),
        in_specs=[
            lo(wq, _col("qa") // wq), hi(wq, _col("qa") // wq),
            kv_spec("ksa"), kv_spec("vsa"), kv_spec("kwa"), kv_spec("vwa"),
            pl.BlockSpec((1, n_cp, LANES), lambda b, k, i: (b * N_KV_A + k, 0, 0)),
            pl.BlockSpec((1, n_cp, LANES), lambda b, k, i: (B * N_KV_A + b * N_KV_A + k, 0, 0)),
            lo(LANES, cb("ga")), hi(LANES, cb("ga")),
            lo(wq, _col("za") // wq), hi(wq, _col("za") // wq),
            pl.BlockSpec((N_SELP, LANES), lambda b, k, i: (0, 0)),
        ],
        out_specs=(pl.BlockSpec((1, TQ, wq), lambda b, k, i: (b, i, k)),
                   pl.BlockSpec((1, TQ, wq), lambda b, k, i: (b, n_qt // 2 - 1 - i, k))),
        out_shape=(y_sds, y_sds),
        scratch_shapes=[
            pltpu.VMEM((S // LANES, _VROWS_A, LANES), BF16),
            pltpu.VMEM((S // LANES, _VROWS_A, LANES), BF16),
            pltpu.VMEM((2, GROUP_A // 2, PAIR, HEAD_DIM_A), BF16),
            pltpu.VMEM((2, N_SELP, SUBLANES, TQ), F32),
            pltpu.VMEM((2, GROUP_A // 2, _VROWS_A, PAIR), F32),
            pltpu.VMEM((2, GROUP_A // 2, HEAD_DIM_A, PAIR), F32),
        ],
        compiler_params=pltpu.CompilerParams(
            dimension_semantics=("arbitrary", "arbitrary", "arbitrary"), vmem_limit_bytes=48 * 1024 * 1024),
        name="attn_a",
    )(sl("qa"), sl("qa"), sl("ksa"), sl("vsa"), sl("kwa"), sl("vwa"), kvc, kvc,
      sl("ga"), sl("ga"), sl("za"), sl("za"), ov)


_VROWS_B = HEAD_DIM_B + ONES_ROWS


B_QT = 4


def _attn_b_kernel(sink_ref, q_ref, k_ref, v_ref, z_ref, y_ref, vt_ref):
    kv = pl.program_id(1)
    qi = pl.program_id(2)
    n_keys = WIN_B + TQ

    @pl.when(qi == 0)
    def _():
        _transpose_values(v_ref, vt_ref, HEAD_DIM_B)

    left = lax.broadcasted_iota(jnp.int32, (TQ, LANES), 1) < HEAD_DIM_B
    zero = jnp.zeros((TQ, LANES), q_ref.dtype)
    lane = lax.broadcasted_iota(jnp.int32, (1, TQ), 1)
    lane2 = jnp.concatenate([lane, lane], axis=1)
    krow = lax.broadcasted_iota(jnp.int32, (n_keys, PAIR), 0)
    sinks = [jnp.concatenate(
        [jnp.full((1, TQ), sink_ref[kv * GROUP_B + 2 * c + i] * LOG2E, F32) for i in range(2)], axis=1)
        for c in range(GROUP_B // 2)]
    ctx = {}

    def tile_ctx(u):
        if u not in ctx:
            t0 = (qi * B_QT + u) * TQ
            kst = pl.multiple_of(jnp.maximum(t0 - WIN_B, 0), TQ)
            dpos = (t0 + lane2) - (kst + krow)
            bias = jnp.where((dpos >= 0) & (dpos < WIN_B), 0.0, NEG_INF)
            ctx[u] = (kst, bias)
        return ctx[u]

    def scores(u, c):
        kst, bias = tile_ctx(u)
        pair = q_ref[0, u * TQ:(u + 1) * TQ, c * LANES:(c + 1) * LANES]
        qp = jnp.concatenate([jnp.where(left, pair, zero), jnp.where(left, zero, pair)], axis=0)
        kt = k_ref[0, pl.ds(kst, n_keys), :]
        return _dot_nt(kt, qp) + bias

    def finish(u, c, s):
        kst, _ = tile_ctx(u)
        vt = jnp.concatenate([vt_ref[kst // LANES + b] for b in range(n_keys // LANES)], axis=1)
        m = jnp.maximum(jnp.max(s, axis=0, keepdims=True), sinks[c])
        p = jnp.exp2(s - m).astype(BF16)
        a = _dot(vt, p)
        den = jnp.maximum(a[HEAD_DIM_B:HEAD_DIM_B + 1] + jnp.exp2(sinks[c] - m), TINY)
        o = a[0:HEAD_DIM_B] / den
        o2 = jnp.concatenate([o[:, 0:TQ], o[:, TQ:PAIR]], axis=0)
        z = z_ref[0, u * TQ:(u + 1) * TQ, c * LANES:(c + 1) * LANES]
        y_ref[0, u * TQ:(u + 1) * TQ, c * LANES:(c + 1) * LANES] = (o2.T * z.astype(F32)).astype(y_ref.dtype)

    _run_skewed([(functools.partial(scores, u, c), functools.partial(finish, u, c))
                 for u in range(B_QT) for c in range(GROUP_B // 2)])


def _attn_b(sinks, slabs):
    B, S, _ = slabs[0].shape
    sl = lambda name: slabs[_slab(name)]
    wq = GROUP_B * HEAD_DIM_B
    cb = lambda name: _col(name) // LANES
    grid_spec = pltpu.PrefetchScalarGridSpec(
        num_scalar_prefetch=1,
        grid=(B, N_KV_B, S // (B_QT * TQ)),
        in_specs=[
            pl.BlockSpec((1, B_QT * TQ, wq), lambda b, k, i, s: (b, i, _col("qb") // wq + k)),
            pl.BlockSpec((1, S, LANES), lambda b, k, i, s, c=cb("kb2"): (b, 0, c + k)),
            pl.BlockSpec((1, S, LANES), lambda b, k, i, s, c=cb("vb2"): (b, 0, c + k)),
            pl.BlockSpec((1, B_QT * TQ, wq), lambda b, k, i, s: (b, i, _col("zb") // wq + k)),
        ],
        out_specs=pl.BlockSpec((1, B_QT * TQ, wq), lambda b, k, i, s: (b, i, k)),
        scratch_shapes=[pltpu.VMEM((S // LANES, _VROWS_B, LANES), BF16)],
    )
    return pl.pallas_call(
        _attn_b_kernel,
        grid_spec=grid_spec,
        out_shape=jax.ShapeDtypeStruct((B, S, WIDTH_B), BF16),
        compiler_params=pltpu.CompilerParams(
            dimension_semantics=("arbitrary", "arbitrary", "arbitrary")),
        name="attn_b",
    )(sinks, sl("qb"), sl("kb2"), sl("vb2"), sl("zb"))


OUT_TM = 512


def _out_proj_kernel(ya_ref, yb_ref, wa_ref, wb_ref, x_ref, g_ref, o_ref):
    r = x_ref[...] + _dot(ya_ref[...], wa_ref[...]) + _dot(yb_ref[...], wb_ref[...])
    ms = jnp.mean(r * r, axis=-1, keepdims=True)
    o_ref[...] = (r * lax.rsqrt(ms + RMS_EPS)) * g_ref[...]


def _out_proj(ya2, yb2, w_out, x2, g):
    M = x2.shape[0]
    return pl.pallas_call(
        _out_proj_kernel,
        grid=(M // OUT_TM,),
        in_specs=[
            pl.BlockSpec((OUT_TM, WIDTH_A), lambda i: (i, 0)),
            pl.BlockSpec((OUT_TM, WIDTH_B), lambda i: (i, 0)),
            pl.BlockSpec((WIDTH_A, D_MODEL), lambda i: (0, 0)),
            pl.BlockSpec((WIDTH_B, D_MODEL), lambda i: (WIDTH_A // WIDTH_B, 0)),
            pl.BlockSpec((OUT_TM, D_MODEL), lambda i: (i, 0)),
            pl.BlockSpec((1, D_MODEL), lambda i: (0, 0)),
        ],
        out_specs=pl.BlockSpec((OUT_TM, D_MODEL), lambda i: (i, 0)),
        out_shape=jax.ShapeDtypeStruct((M, D_MODEL), F32),
        compiler_params=pltpu.CompilerParams(
            dimension_semantics=("arbitrary",), vmem_limit_bytes=56 * 1024 * 1024),
        name="out_proj",
    )(ya2, yb2, w_out, w_out, x2, g)


def _overlap_matrix(n_cp):
    c_start = np.arange(n_cp) * CMP_STRIDE
    j_start = np.arange(N_SELP) * SEL_BLOCK
    ov = (c_start[None, :] < j_start[:, None] + SEL_BLOCK) & (c_start[None, :] + CMP_BLOCK > j_start[:, None])
    return jnp.asarray(ov, BF16)


def _stride_rows(p3, col0):
    B, S, _ = p3.shape
    t = p3[:, :, col0:col0 + KV_A].reshape(B, S // CMP_STRIDE, CMP_STRIDE, N_KV_A, HEAD_DIM_A)
    return t.transpose(0, 3, 1, 2, 4).reshape(B * N_KV_A * (S // CMP_STRIDE), _HALF)


def kernel(x, w_in, cmp_k_w1, cmp_k_w2, cmp_v_w1, cmp_v_w2, cmp_k_pos, cmp_v_pos, sinks, w_out, norm_g, final_g):
    B, S, D = x.shape
    assert D == D_MODEL and w_in.shape[0] == 1
    x2 = x.reshape(B * S, D)

    w_all = _prep_w_in(w_in[0])
    tabs = _rope_tables(S)
    slabs = [p.reshape(B, S, IN_TN) for p in _in_proj(x2, norm_g[0].reshape(1, D), w_all, tabs, S)]

    t_kv = jnp.stack([_stride_rows(slabs[_slab(n)], _col(n)) for n in ("kca", "vca")])
    w1 = jnp.stack([cmp_k_w1[0], cmp_v_w1[0]]).astype(BF16)
    w2 = jnp.stack([cmp_k_w2[0], cmp_v_w2[0]]).astype(BF16)
    pos = jnp.stack([cmp_k_pos[0], cmp_v_pos[0]]).reshape(2, 1, CMP_BLOCK * HEAD_DIM_A)
    pos = jnp.broadcast_to(pos, (2, 8, CMP_BLOCK * HEAD_DIM_A)).astype(BF16)
    kvc = _compress(t_kv, w1, w2, pos)
    kvc = kvc.reshape(2 * B * N_KV_A, S // CMP_STRIDE, HEAD_DIM_A)

    y_a = jnp.concatenate(_attn_a(slabs, kvc, _overlap_matrix(S // CMP_STRIDE)), axis=1)
    y_b = _attn_b(sinks[0], slabs)

    out = _out_proj(y_a.reshape(B * S, WIDTH_A), y_b.reshape(B * S, WIDTH_B), w_out[0].astype(BF16), x2,
                    final_g.reshape(1, D))
    return out.reshape(B, S, D)
```

```python
import functools
import math

import numpy as np
import jax
import jax.numpy as jnp
from jax import lax
from jax.experimental import pallas as pl
from jax.experimental.pallas import tpu as pltpu

F32 = jnp.float32
BF16 = jnp.bfloat16

D_MODEL = 2048
ROPE_THETA = 10000.0
RMS_EPS = 1e-6
NEG_INF = -1e30
TINY = 1e-30
LOG2E = math.log2(math.e)

WIDTH_A = 1024
HEAD_DIM_A = 128
N_KV_A = 2
GROUP_A = 4
KV_A = N_KV_A * HEAD_DIM_A
CMP_BLOCK = 32
CMP_STRIDE = 16
CMP_HIDDEN = 256
SEL_BLOCK = 64
SEL_TOPK = 16
WIN_A = 512
FORCE_SCORE = 1e4

WIDTH_B = 1024
HEAD_DIM_B = 64
N_HEADS_B = 16
N_KV_B = 2
GROUP_B = 8
KV_B = N_KV_B * HEAD_DIM_B
WIN_B = 128

IN_SIZES = (WIDTH_A, KV_A, KV_A, KV_A, KV_A, KV_A, KV_A, WIDTH_A, 3 * 8,
            WIDTH_B, KV_B, KV_B, WIDTH_B)

LANES = 128
SUBLANES = 8
CHUNK = 256

EP_NONE, EP_ROPE128, EP_ROPE64, EP_SILU, EP_SIGMOID, EP_ROPE128_Q, EP_ROPE64_Q = range(7)

_PROJ_LAYOUT = (
    ("qa", 4, EP_ROPE128_Q), ("kca", 1, EP_ROPE128),
    ("ksa", 1, EP_ROPE128), ("kwa", 1, EP_ROPE128), ("vca", 1, EP_NONE), ("vsa", 1, EP_NONE), ("vwa", 1, EP_NONE),
    ("za", 4, EP_SILU), ("ga", 1, EP_SIGMOID),
    ("zb", 4, EP_SILU), ("kb2", 1, EP_ROPE64),
    ("qb", 4, EP_ROPE64_Q), ("vb2", 1, EP_NONE),
)
_CHUNK_KINDS = tuple(k for _, n, k in _PROJ_LAYOUT for _ in range(n))
_CHUNK_START = {}
_c = 0
for _name, _n, _k in _PROJ_LAYOUT:
    _CHUNK_START[_name] = _c
    _c += _n
N_CHUNKS = _c
D_PROJ = N_CHUNKS * CHUNK
IN_TN = 1280
_IN_CPT = IN_TN // CHUNK
N_SLABS = D_PROJ // IN_TN


def _slab(name):
    return _CHUNK_START[name] // _IN_CPT


def _col(name):
    return (_CHUNK_START[name] % _IN_CPT) * CHUNK


def _dot(a, b):
    return jnp.dot(a, b, preferred_element_type=F32)


def _dot_nt(a, b):
    return lax.dot_general(a, b, (((1,), (1,)), ((), ())), preferred_element_type=F32)


IN_TM_NORM = 512
IN_TM = 1024
IN_SUB = 512

_TAB_GROUP = {EP_ROPE128: "r128", EP_ROPE128_Q: "r128q", EP_ROPE64: "r64", EP_ROPE64_Q: "r64q"}
_TAB_SIZE = {"r128": 2, "r128q": 2, "r64": 3, "r64q": 3}


def _slab_tab_groups(kinds):
    return tuple(dict.fromkeys(_TAB_GROUP[k] for k in kinds if k in _TAB_GROUP))


def _in_proj_kernel(kinds, with_norm, seq_len, *refs):
    groups = _slab_tab_groups(kinds)
    refs = list(refs)
    if with_norm:
        x_ref, g_ref = refs.pop(0), refs.pop(0)
    else:
        h_ref = refs.pop(0)
    w_ref = refs.pop(0)
    tab_ref = refs.pop(0) if groups else None
    o_ref = refs.pop(0)
    if with_norm:
        h_ref = refs.pop(0)
        x = x_ref[...]
        ms = jnp.mean(x * x, axis=-1, keepdims=True)
        h_ref[...] = ((x * lax.rsqrt(ms + RMS_EPS)) * g_ref[...]).astype(BF16)
    tm = h_ref.shape[0]
    pos0 = (pl.program_id(0) % (seq_len // tm)) * tm
    base, n = {}, 0
    for grp in groups:
        base[grp] = n
        n += _TAB_SIZE[grp]

    def epilogue(kind, r, a):
        if kind == EP_NONE:
            return a
        if kind == EP_SILU:
            return a * jax.nn.sigmoid(a)
        if kind == EP_SIGMOID:
            return jax.nn.sigmoid(a)
        t = base[_TAB_GROUP[kind]]
        tab = lambda k: tab_ref[t + k, pl.ds(pl.multiple_of(pos0 + r * IN_SUB, IN_SUB), IN_SUB), :]
        if kind in (EP_ROPE128, EP_ROPE128_Q):
            return a * tab(0) + pltpu.roll(a, 64, 1) * tab(1)
        return a * tab(0) + pltpu.roll(a, 96, 1) * tab(1) + pltpu.roll(a, 32, 1) * tab(2)

    def finish(kind, r, c, acc):
        rows = slice(r * IN_SUB, (r + 1) * IN_SUB)
        for sub in range(CHUNK // LANES):
            cols = slice(c * CHUNK + sub * LANES, c * CHUNK + (sub + 1) * LANES)
            o_ref[rows, cols] = epilogue(kind, r, acc[:, sub * LANES:(sub + 1) * LANES]).astype(o_ref.dtype)

    jobs = []
    for r in range(tm // IN_SUB):
        for c, kind in enumerate(kinds):
            jobs.append((lambda r=r, c=c: _dot(h_ref[r * IN_SUB:(r + 1) * IN_SUB, :],
                                               w_ref[:, c * CHUNK:(c + 1) * CHUNK]),
                         functools.partial(finish, kind, r, c)))
    _run_skewed(jobs)


def _rope_tables(S):
    def cs(d):
        inv = ROPE_THETA ** (-jnp.arange(0, d, 2, dtype=F32) / d)
        ang = jnp.arange(S, dtype=F32)[:, None] * inv[None, :]
        return jnp.cos(ang), jnp.sin(ang)

    c128, s128 = cs(HEAD_DIM_A)
    c64, s64 = cs(HEAD_DIM_B)
    z32 = jnp.zeros_like(s64)
    t128 = [jnp.concatenate([c128, c128], axis=1), jnp.concatenate([-s128, s128], axis=1)]
    t64 = [jnp.concatenate([c64, c64, c64, c64], axis=1),
           jnp.concatenate([-s64, z32, -s64, z32], axis=1),
           jnp.concatenate([z32, s64, z32, s64], axis=1)]
    qa = HEAD_DIM_A ** -0.5 * LOG2E
    qb = HEAD_DIM_B ** -0.5 * LOG2E
    return {"r128": t128, "r64": t64, "r128q": [t * qa for t in t128], "r64q": [t * qb for t in t64]}


W_PREP_TR = 256


def _w_in_pieces():
    offs = np.cumsum((0,) + IN_SIZES)
    names = ("qa", "kca", "vca", "ksa", "vsa", "kwa", "vwa", "za", "ga", "qb", "kb", "vb", "zb")
    start = {n: int(offs[i]) for i, n in enumerate(names)}
    width = dict(zip(names, IN_SIZES))
    n_g = 3 * GROUP_A
    halves = lambda n: [(start[n] + i * HEAD_DIM_B, HEAD_DIM_B) for i in (0, 0, 1, 1)]
    pieces = {n: [(start[n], width[n])] for n in ("qa", "kca", "ksa", "kwa", "vca", "vsa", "vwa", "za", "zb", "qb")}
    pieces["ga"] = [(start["ga"], n_g), (None, LANES - n_g), (start["ga"] + n_g, n_g), (None, LANES - n_g)]
    pieces["kb2"] = halves("kb")
    pieces["vb2"] = halves("vb")
    return [p for name, _, _ in _PROJ_LAYOUT for p in pieces[name]]


def _w_prep_kernel(w_ref, o_ref):
    col = 0
    parts, filled = [], 0
    for src, wd in _w_in_pieces():
        done = 0
        while done < wd:
            take = min(wd - done, LANES - filled)
            if src is None:
                parts.append(jnp.zeros((W_PREP_TR, take), F32))
            else:
                parts.append(w_ref[0, :, src + done:src + done + take])
            done += take
            filled += take
            if filled == LANES:
                blk = parts[0] if len(parts) == 1 else jnp.concatenate(parts, axis=1)
                o_ref[:, col:col + LANES] = blk.astype(o_ref.dtype)
                col += LANES
                parts, filled = [], 0
    assert col == D_PROJ and not parts


def _prep_w_in(w_in):
    d_in = w_in.shape[2]
    return pl.pallas_call(
        _w_prep_kernel,
        grid=(D_MODEL // W_PREP_TR,),
        in_specs=[pl.BlockSpec((1, W_PREP_TR, d_in), lambda i: (0, i, 0))],
        out_specs=pl.BlockSpec((W_PREP_TR, D_PROJ), lambda i: (i, 0)),
        out_shape=jax.ShapeDtypeStruct((D_MODEL, D_PROJ), BF16),
        compiler_params=pltpu.CompilerParams(dimension_semantics=("arbitrary",)),
        name="w_prep",
    )(w_in)


def _in_proj(x2, g, w_all, tabs, S):
    M = x2.shape[0]
    params = pltpu.CompilerParams(dimension_semantics=("arbitrary",), vmem_limit_bytes=48 * 1024 * 1024)
    out_sds = jax.ShapeDtypeStruct((M, IN_TN), BF16)
    slabs, h = [], None
    for j in range(N_SLABS):
        kinds = _CHUNK_KINDS[j * _IN_CPT:(j + 1) * _IN_CPT]
        with_norm = j == 0
        tm = IN_TM_NORM if with_norm else IN_TM
        groups = _slab_tab_groups(kinds)
        act_spec = pl.BlockSpec((tm, D_MODEL), lambda i: (i, 0))
        in_specs = [act_spec, pl.BlockSpec((1, D_MODEL), lambda i: (0, 0))] if with_norm else [act_spec]
        args = [x2, g] if with_norm else [h]
        in_specs.append(pl.BlockSpec((D_MODEL, IN_TN), lambda i, j=j: (0, j)))
        args.append(w_all)
        if groups:
            tab = jnp.stack([t for grp in groups for t in tabs[grp]])
            in_specs.append(pl.BlockSpec(tab.shape, lambda i: (0, 0, 0)))
            args.append(tab)
        o_spec = pl.BlockSpec((tm, IN_TN), lambda i: (i, 0))
        res = pl.pallas_call(
            functools.partial(_in_proj_kernel, kinds, with_norm, S),
            grid=(M // tm,),
            in_specs=in_specs,
            out_specs=(o_spec, act_spec) if with_norm else o_spec,
            out_shape=(out_sds, jax.ShapeDtypeStruct((M, D_MODEL), BF16)) if with_norm else out_sds,
            compiler_params=params,
            name=f"in_proj_{j}",
        )(*args)
        if with_norm:
            res, h = res
        slabs.append(res)
    return slabs


CMP_TM = 256
_HALF = CMP_STRIDE * HEAD_DIM_A


def _compress_kernel(t_ref, w1_ref, w2_ref, pos_ref, o_ref):
    t = t_ref[0]
    u = _dot(t, w1_ref[0, 0:_HALF, :])
    v = _dot(t, w1_ref[0, _HALF:2 * _HALF, :])
    pb = _dot(pos_ref[0], w1_ref[0])
    pre = u + pltpu.roll(v, CMP_TM - 1, 0) + pb[0:1]
    hid = pre * jax.nn.sigmoid(pre)
    out = _dot(hid.astype(BF16), w2_ref[0])
    row = lax.broadcasted_iota(jnp.int32, out.shape, 0)
    n_cp = LANES
    out = jnp.where((row & (n_cp - 1)) == n_cp - 1, 0.0, out)
    o_ref[0] = out.astype(o_ref.dtype)


def _compress(t_kv, w1, w2, pos):
    R = t_kv.shape[1]
    return pl.pallas_call(
        _compress_kernel,
        grid=(2, R // CMP_TM),
        in_specs=[
            pl.BlockSpec((1, CMP_TM, _HALF), lambda s, i: (s, i, 0)),
            pl.BlockSpec((1, 2 * _HALF, CMP_HIDDEN), lambda s, i: (s, 0, 0)),
            pl.BlockSpec((1, CMP_HIDDEN, HEAD_DIM_A), lambda s, i: (s, 0, 0)),
            pl.BlockSpec((1, 8, 2 * _HALF), lambda s, i: (s, 0, 0)),
        ],
        out_specs=pl.BlockSpec((1, CMP_TM, HEAD_DIM_A), lambda s, i: (s, i, 0)),
        out_shape=jax.ShapeDtypeStruct((2, R, HEAD_DIM_A), BF16),
        compiler_params=pltpu.CompilerParams(dimension_semantics=("arbitrary", "arbitrary")),
        name="compress",
    )(t_kv, w1, w2, pos)


TQ = 128
ONES_ROWS = 16


def _transpose_values(v_ref, vt_ref, n_rows):
    n_blk = vt_ref.shape[0]
    for blk in range(n_blk):
        vt = v_ref[0, blk * LANES:(blk + 1) * LANES, :].astype(F32).T
        vt_ref[blk, 0:n_rows, :] = vt[0:n_rows].astype(vt_ref.dtype)
        vt_ref[blk, n_rows:n_rows + ONES_ROWS, :] = jnp.ones((ONES_ROWS, LANES), vt_ref.dtype)


def _run_skewed(jobs):
    pending = [jobs[0][0]()]
    for i, (_, finish) in enumerate(jobs):
        if i + 1 < len(jobs):
            pending.append(jobs[i + 1][0]())
        finish(pending[i])
        pending[i] = None


def _softmax_step(s, m, acc, vt):
    m_new = jnp.maximum(m, jnp.max(s, axis=0, keepdims=True))
    p = jnp.exp2(s - m_new).astype(BF16)
    return m_new, jnp.exp2(m - m_new) * acc + _dot(vt, p)


SEL_TK = 256
N_SELP = 32
PAIR = 2 * TQ
_VROWS_A = HEAD_DIM_A + ONES_ROWS
_WIN_TILES = (256, 256, 128)
_N_SEL_JOBS = 9


def _attn_a_kernel(qlo_ref, qhi_ref, ksel_ref, vsel_ref, kwin_ref, vwin_ref, kc_ref, vc_ref,
                   glo_ref, ghi_ref, zlo_ref, zhi_ref, ov_ref, ylo_ref, yhi_ref,
                   vselt_ref, vwint_ref, qs_ref, bias_ref, acc_ref, out_ref):
    i = pl.program_id(2)
    n_qt = 2 * pl.num_programs(2)
    n_pairs = GROUP_A // 2
    t0s = (i * TQ, (n_qt - 1 - i) * TQ)
    q_refs, g_refs, z_refs, y_refs = (qlo_ref, qhi_ref), (glo_ref, ghi_ref), (zlo_ref, zhi_ref), (ylo_ref, yhi_ref)

    @pl.when(i == 0)
    def _():
        _transpose_values(vsel_ref, vselt_ref, HEAD_DIM_A)
        _transpose_values(vwin_ref, vwint_ref, HEAD_DIM_A)

    lane = lax.broadcasted_iota(jnp.int32, (1, TQ), 1)
    lane2 = jnp.concatenate([lane, lane], axis=1)
    qpair, gts = [], []
    for t in range(2):
        q = q_refs[t][0]
        pairs = [jnp.concatenate([q[:, h * LANES:(h + 1) * LANES] for h in (2 * p, 2 * p + 1)], axis=0)
                 for p in range(n_pairs)]
        for p in range(n_pairs):
            qs_ref[t, p] = pairs[p]
        qpair.append(pairs)
        gts.append(g_refs[t][0].astype(F32).T)

    def gate(t, p, branch):
        return jnp.concatenate(
            [gts[t][3 * h + branch:3 * h + branch + 1] for h in (2 * p, 2 * p + 1)], axis=1)

    kc = kc_ref[0]
    vct = vc_ref[0].astype(F32).T.astype(BF16)
    cend = lax.broadcasted_iota(jnp.int32, (LANES, PAIR), 0) * CMP_STRIDE + (CMP_BLOCK - 1)
    psums = [[], []]

    def cmp_scores(t, p):
        return jnp.where(cend <= t0s[t] + lane2, _dot_nt(kc, qpair[t][p]), NEG_INF)

    def cmp_finish(t, p, s):
        m = jnp.max(s, axis=0, keepdims=True)
        e = jnp.where(cend <= t0s[t] + lane2, jnp.exp2(s - m), 0.0)
        pc = e / jnp.maximum(jnp.sum(e, axis=0, keepdims=True), TINY)
        psums[t].append(pc[:, 0:TQ] + pc[:, TQ:PAIR])
        out_ref[t, p] = gate(t, p, 0) * _dot(vct, pc.astype(BF16))

    ksts = [pl.multiple_of(jnp.maximum(t0 - WIN_A, 0), TQ) for t0 in t0s]
    win_m = [[jnp.full((1, PAIR), NEG_INF, F32) for _ in range(n_pairs)] for _ in range(2)]
    win_acc = [[jnp.zeros((_VROWS_A, PAIR), F32) for _ in range(n_pairs)] for _ in range(2)]

    def win_scores(t, p, off, tk):
        kt = kwin_ref[0, pl.ds(pl.multiple_of(ksts[t] + off, LANES), tk), :]
        dpos = (t0s[t] + lane2) - (ksts[t] + off + lax.broadcasted_iota(jnp.int32, (tk, PAIR), 0))
        return jnp.where((dpos >= 0) & (dpos < WIN_A), _dot_nt(kt, qpair[t][p]), NEG_INF)

    def win_finish(t, p, off, tk, s):
        vt = jnp.concatenate(
            [vwint_ref[ksts[t] // LANES + off // LANES + b] for b in range(tk // LANES)], axis=1)
        win_m[t][p], win_acc[t][p] = _softmax_step(s, win_m[t][p], win_acc[t][p], vt)

    jobs = [(functools.partial(cmp_scores, t, p), functools.partial(cmp_finish, t, p))
            for t in range(2) for p in range(n_pairs)]
    off = 0
    for tk in _WIN_TILES:
        jobs += [(functools.partial(win_scores, t, p, off, tk), functools.partial(win_finish, t, p, off, tk))
                 for t in range(2) for p in range(n_pairs)]
        off += tk
    _run_skewed(jobs)
    for t in range(2):
        for p in range(n_pairs):
            a = win_acc[t][p]
            out_ref[t, p] = out_ref[t, p] + (
                gate(t, p, 2) / jnp.maximum(a[HEAD_DIM_A:HEAD_DIM_A + 1], TINY)) * a[0:HEAD_DIM_A]

    ov = ov_ref[...]
    jidx = lax.broadcasted_iota(jnp.int32, (N_SELP, TQ), 0)
    for t in range(2):
        psum = psums[t][0] + psums[t][1]
        p_hi = psum.astype(BF16)
        r1 = psum - p_hi.astype(F32)
        p_mid = r1.astype(BF16)
        p_lo = (r1 - p_mid.astype(F32)).astype(BF16)
        psel = _dot(ov, p_hi) + _dot(ov, p_mid) + _dot(ov, p_lo)
        tq = t0s[t] + lane
        cur = tq // SEL_BLOCK
        forced = (jidx == 0) | (jidx == cur) | (jidx == cur - 1)
        score = jnp.where(forced, FORCE_SCORE, jnp.where(jidx * SEL_BLOCK <= tq, psel, -1.0))
        rank = jnp.zeros(psel.shape, F32)
        for r in range(N_SELP):
            row = jnp.broadcast_to(score[r:r + 1, :], score.shape)
            rank = rank + jnp.where(jidx > r, jnp.where(row >= score, 1.0, 0.0), jnp.where(row > score, 1.0, 0.0))
        bias_t = jnp.where(rank < SEL_TOPK, 0.0, NEG_INF)
        for j in range(N_SELP):
            bias_ref[t, j] = jnp.broadcast_to(bias_t[j:j + 1, :], (SUBLANES, TQ))

    blk_per_tile = SEL_TK // SEL_BLOCK
    rep = SEL_BLOCK // SUBLANES
    n_lo = i // (SEL_TK // TQ) + 1
    acc_ref[...] = jnp.zeros(acc_ref.shape, F32)
    ms = [[jnp.full((1, PAIR), NEG_INF, F32) for _ in range(n_pairs)] for _ in range(2)]
    ctx = {}

    def sel_scores(k, p):
        if p == 0:
            hi = k >= n_lo
            t = hi.astype(jnp.int32)
            j = jnp.where(hi, k - n_lo, k)
            ks = pl.multiple_of(j * SEL_TK, SEL_TK)
            bt = jnp.concatenate(
                [bias_ref[t, j * blk_per_tile + b] for b in range(blk_per_tile) for _ in range(rep)], axis=0)
            bt = jnp.concatenate([bt, bt], axis=1)
            kpos = ks + lax.broadcasted_iota(jnp.int32, (SEL_TK, PAIR), 0)
            bt = jnp.where(kpos <= jnp.where(hi, t0s[1], t0s[0]) + lane2, bt, NEG_INF)
            ctx[k] = (hi, t, j, ks, bt)
        hi, t, j, ks, bt = ctx[k]
        return _dot_nt(ksel_ref[0, pl.ds(ks, SEL_TK), :], qs_ref[t, p]) + bt

    def sel_finish(k, p, s):
        hi, t, j, ks, bt = ctx[k]
        vt = jnp.concatenate([vselt_ref[j * (SEL_TK // LANES) + b] for b in range(SEL_TK // LANES)], axis=1)
        m_new, acc_new = _softmax_step(s, jnp.where(hi, ms[1][p], ms[0][p]), acc_ref[t, p], vt)
        acc_ref[t, p] = acc_new
        ms[0][p] = jnp.where(hi, ms[0][p], m_new)
        ms[1][p] = jnp.where(hi, m_new, ms[1][p])

    sel_jobs = [(functools.partial(sel_scores, k, p), functools.partial(sel_finish, k, p))
                for k in range(_N_SEL_JOBS) for p in range(n_pairs)]
    _run_skewed(sel_jobs)

    for t in range(2):
        z = z_refs[t][0]
        for p in range(n_pairs):
            a = acc_ref[t, p]
            o = out_ref[t, p] + (gate(t, p, 1) / jnp.maximum(a[HEAD_DIM_A:HEAD_DIM_A + 1], TINY)) * a[0:HEAD_DIM_A]
            for h in (2 * p, 2 * p + 1):
                oh = o[:, (h % 2) * TQ:(h % 2 + 1) * TQ].T
                y_refs[t][0, :, h * LANES:(h + 1) * LANES] = (
                    oh * z[:, h * LANES:(h + 1) * LANES].astype(F32)).astype(y_refs[t].dtype)


def _attn_a(slabs, kvc, ov):
    B, S, _ = slabs[0].shape
    sl = lambda name: slabs[_slab(name)]
    n_cp = S // CMP_STRIDE
    n_qt = S // TQ
    assert n_cp == LANES and S // SEL_BLOCK == N_SELP and (n_qt // 2) // (SEL_TK // TQ) * 2 + 1 == _N_SEL_JOBS
    wq = GROUP_A * HEAD_DIM_A
    cb = lambda name: _col(name) // LANES
    kv_spec = lambda name: pl.BlockSpec((1, S, LANES), lambda b, k, i, c=cb(name): (b, 0, c + k))
    lo = lambda width, c: pl.BlockSpec((1, TQ, width), lambda b, k, i: (b, i, c + k))
    hi = lambda width, c: pl.BlockSpec((1, TQ, width), lambda b, k, i: (b, n_qt - 1 - i, c + k))
    y_sds = jax.ShapeDtypeStruct((B, S // 2, WIDTH_A), BF16)
    return pl.pallas_call(
        _attn_a_kernel,
        grid=(B, N_KV_A, n_qt // 2),
        in_specs=[
            lo(wq, _col("qa") // wq), hi(wq, _col("qa") // wq),
            kv_spec("ksa"), kv_spec("vsa"), kv_spec("kwa"), kv_spec("vwa"),
            pl.BlockSpec((1, n_cp, LANES), lambda b, k, i: (b * N_KV_A + k, 0, 0)),
            pl.BlockSpec((1, n_cp, LANES), lambda b, k, i: (B * N_KV_A + b * N_KV_A + k, 0, 0)),
            lo(LANES, cb("ga")), hi(LANES, cb("ga")),
            lo(wq, _col("za") // wq), hi(wq, _col("za") // wq),
            pl.BlockSpec((N_SELP, LANES), lambda b, k, i: (0, 0)),
        ],
        out_specs=(pl.BlockSpec((1, TQ, wq), lambda b, k, i: (b, i, k)),
                   pl.BlockSpec((1, TQ, wq), lambda b, k, i: (b, n_qt // 2 - 1 - i, k))),
        out_shape=(y_sds, y_sds),
        scratch_shapes=[
            pltpu.VMEM((S // LANES, _VROWS_A, LANES), BF16),
            pltpu.VMEM((S // LANES, _VROWS_A, LANES), BF16),
            pltpu.VMEM((2, GROUP_A // 2, PAIR, HEAD_DIM_A), BF16),
            pltpu.VMEM((2, N_SELP, SUBLANES, TQ), F32),
            pltpu.VMEM((2, GROUP_A // 2, _VROWS_A, PAIR), F32),
            pltpu.VMEM((2, GROUP_A // 2, HEAD_DIM_A, PAIR), F32),
        ],
        compiler_params=pltpu.CompilerParams(
            dimension_semantics=("arbitrary", "arbitrary", "arbitrary"), vmem_limit_bytes=48 * 1024 * 1024),
        name="attn_a",
    )(sl("qa"), sl("qa"), sl("ksa"), sl("vsa"), sl("kwa"), sl("vwa"), kvc, kvc,
      sl("ga"), sl("ga"), sl("za"), sl("za"), ov)


_VROWS_B = HEAD_DIM_B + ONES_ROWS


B_QT = 4


def _attn_b_kernel(sink_ref, q_ref, k_ref, v_ref, z_ref, y_ref, vt_ref):
    kv = pl.program_id(1)
    qi = pl.program_id(2)
    n_keys = WIN_B + TQ

    @pl.when(qi == 0)
    def _():
        _transpose_values(v_ref, vt_ref, HEAD_DIM_B)

    left = lax.broadcasted_iota(jnp.int32, (TQ, LANES), 1) < HEAD_DIM_B
    zero = jnp.zeros((TQ, LANES), q_ref.dtype)
    lane = lax.broadcasted_iota(jnp.int32, (1, TQ), 1)
    lane2 = jnp.concatenate([lane, lane], axis=1)
    krow = lax.broadcasted_iota(jnp.int32, (n_keys, PAIR), 0)
    sinks = [jnp.concatenate(
        [jnp.full((1, TQ), sink_ref[kv * GROUP_B + 2 * c + i] * LOG2E, F32) for i in range(2)], axis=1)
        for c in range(GROUP_B // 2)]
    ctx = {}

    def tile_ctx(u):
        if u not in ctx:
            t0 = (qi * B_QT + u) * TQ
            kst = pl.multiple_of(jnp.maximum(t0 - WIN_B, 0), TQ)
            dpos = (t0 + lane2) - (kst + krow)
            bias = jnp.where((dpos >= 0) & (dpos < WIN_B), 0.0, NEG_INF)
            ctx[u] = (kst, bias)
        return ctx[u]

    def scores(u, c):
        kst, bias = tile_ctx(u)
        pair = q_ref[0, u * TQ:(u + 1) * TQ, c * LANES:(c + 1) * LANES]
        qp = jnp.concatenate([jnp.where(left, pair, zero), jnp.where(left, zero, pair)], axis=0)
        kt = k_ref[0, pl.ds(kst, n_keys), :]
        return _dot_nt(kt, qp) + bias

    def finish(u, c, s):
        kst, _ = tile_ctx(u)
        vt = jnp.concatenate([vt_ref[kst // LANES + b] for b in range(n_keys // LANES)], axis=1)
        m = jnp.maximum(jnp.max(s, axis=0, keepdims=True), sinks[c])
        p = jnp.exp2(s - m).astype(BF16)
        a = _dot(vt, p)
        den = jnp.maximum(a[HEAD_DIM_B:HEAD_DIM_B + 1] + jnp.exp2(sinks[c] - m), TINY)
        o = a[0:HEAD_DIM_B] / den
        o2 = jnp.concatenate([o[:, 0:TQ], o[:, TQ:PAIR]], axis=0)
        z = z_ref[0, u * TQ:(u + 1) * TQ, c * LANES:(c + 1) * LANES]
        y_ref[0, u * TQ:(u + 1) * TQ, c * LANES:(c + 1) * LANES] = (o2.T * z.astype(F32)).astype(y_ref.dtype)

    _run_skewed([(functools.partial(scores, u, c), functools.partial(finish, u, c))
                 for u in range(B_QT) for c in range(GROUP_B // 2)])


def _attn_b(sinks, slabs):
    B, S, _ = slabs[0].shape
    sl = lambda name: slabs[_slab(name)]
    wq = GROUP_B * HEAD_DIM_B
    cb = lambda name: _col(name) // LANES
    grid_spec = pltpu.PrefetchScalarGridSpec(
        num_scalar_prefetch=1,
        grid=(B, N_KV_B, S // (B_QT * TQ)),
        in_specs=[
            pl.BlockSpec((1, B_QT * TQ, wq), lambda b, k, i, s: (b, i, _col("qb") // wq + k)),
            pl.BlockSpec((1, S, LANES), lambda b, k, i, s, c=cb("kb2"): (b, 0, c + k)),
            pl.BlockSpec((1, S, LANES), lambda b, k, i, s, c=cb("vb2"): (b, 0, c + k)),
            pl.BlockSpec((1, B_QT * TQ, wq), lambda b, k, i, s: (b, i, _col("zb") // wq + k)),
        ],
        out_specs=pl.BlockSpec((1, B_QT * TQ, wq), lambda b, k, i, s: (b, i, k)),
        scratch_shapes=[pltpu.VMEM((S // LANES, _VROWS_B, LANES), BF16)],
    )
    return pl.pallas_call(
        _attn_b_kernel,
        grid_spec=grid_spec,
        out_shape=jax.ShapeDtypeStruct((B, S, WIDTH_B), BF16),
        compiler_params=pltpu.CompilerParams(
            dimension_semantics=("arbitrary", "arbitrary", "arbitrary")),
        name="attn_b",
    )(sinks, sl("qb"), sl("kb2"), sl("vb2"), sl("zb"))


OUT_TM = 512


def _out_proj_kernel(tiles_per_half, ylo_ref, yhi_ref, yb_ref, wa_ref, wb_ref, x_ref, g_ref, o_ref):
    in_lo = (pl.program_id(0) % (2 * tiles_per_half)) < tiles_per_half
    ya = jnp.where(in_lo, ylo_ref[...], yhi_ref[...])
    r = x_ref[...] + _dot(ya, wa_ref[...]) + _dot(yb_ref[...], wb_ref[...])
    ms = jnp.mean(r * r, axis=-1, keepdims=True)
    o_ref[...] = (r * lax.rsqrt(ms + RMS_EPS)) * g_ref[...]


def _out_proj(ya_lo, ya_hi, yb2, w_out, x2, g, S):
    M = x2.shape[0]
    tph = S // 2 // OUT_TM
    half = lambda i: (i // (2 * tph)) * tph
    t = lambda i: i % (2 * tph)
    return pl.pallas_call(
        functools.partial(_out_proj_kernel, tph),
        grid=(M // OUT_TM,),
        in_specs=[
            pl.BlockSpec((OUT_TM, WIDTH_A), lambda i: (half(i) + jnp.minimum(t(i), tph - 1), 0)),
            pl.BlockSpec((OUT_TM, WIDTH_A), lambda i: (half(i) + jnp.maximum(t(i) - tph, 0), 0)),
            pl.BlockSpec((OUT_TM, WIDTH_B), lambda i: (i, 0)),
            pl.BlockSpec((WIDTH_A, D_MODEL), lambda i: (0, 0)),
            pl.BlockSpec((WIDTH_B, D_MODEL), lambda i: (WIDTH_A // WIDTH_B, 0)),
            pl.BlockSpec((OUT_TM, D_MODEL), lambda i: (i, 0)),
            pl.BlockSpec((1, D_MODEL), lambda i: (0, 0)),
        ],
        out_specs=pl.BlockSpec((OUT_TM, D_MODEL), lambda i: (i, 0)),
        out_shape=jax.ShapeDtypeStruct((M, D_MODEL), F32),
        compiler_params=pltpu.CompilerParams(
            dimension_semantics=("arbitrary",), vmem_limit_bytes=56 * 1024 * 1024),
        name="out_proj",
    )(ya_lo, ya_hi, yb2, w_out, w_out, x2, g)


def _overlap_matrix(n_cp):
    c_start = np.arange(n_cp) * CMP_STRIDE
    j_start = np.arange(N_SELP) * SEL_BLOCK
    ov = (c_start[None, :] < j_start[:, None] + SEL_BLOCK) & (c_start[None, :] + CMP_BLOCK > j_start[:, None])
    return jnp.asarray(ov, BF16)


def _stride_rows(p3, col0):
    B, S, _ = p3.shape
    t = p3[:, :, col0:col0 + KV_A].reshape(B, S // CMP_STRIDE, CMP_STRIDE, N_KV_A, HEAD_DIM_A)
    return t.transpose(0, 3, 1, 2, 4).reshape(B * N_KV_A * (S // CMP_STRIDE), _HALF)


def kernel(x, w_in, cmp_k_w1, cmp_k_w2, cmp_v_w1, cmp_v_w2, cmp_k_pos, cmp_v_pos, sinks, w_out, norm_g, final_g):
    B, S, D = x.shape
    assert D == D_MODEL and w_in.shape[0] == 1
    x2 = x.reshape(B * S, D)

    w_all = _prep_w_in(w_in)
    tabs = _rope_tables(S)
    slabs = [p.reshape(B, S, IN_TN) for p in _in_proj(x2, norm_g[0].reshape(1, D), w_all, tabs, S)]

    t_kv = jnp.stack([_stride_rows(slabs[_slab(n)], _col(n)) for n in ("kca", "vca")])
    w1 = jnp.stack([cmp_k_w1[0], cmp_v_w1[0]]).astype(BF16)
    w2 = jnp.stack([cmp_k_w2[0], cmp_v_w2[0]]).astype(BF16)
    pos = jnp.stack([cmp_k_pos[0], cmp_v_pos[0]]).reshape(2, 1, CMP_BLOCK * HEAD_DIM_A)
    pos = jnp.broadcast_to(pos, (2, 8, CMP_BLOCK * HEAD_DIM_A)).astype(BF16)
    kvc = _compress(t_kv, w1, w2, pos)
    kvc = kvc.reshape(2 * B * N_KV_A, S // CMP_STRIDE, HEAD_DIM_A)

    ya_lo, ya_hi = _attn_a(slabs, kvc, _overlap_matrix(S // CMP_STRIDE))
    y_b = _attn_b(sinks[0], slabs)

    out = _out_proj(ya_lo.reshape(B * S // 2, WIDTH_A), ya_hi.reshape(B * S // 2, WIDTH_A),
                    y_b.reshape(B * S, WIDTH_B), w_out[0].astype(BF16), x2, final_g.reshape(1, D), S)
    return out.reshape(B, S, D)
```

```python
import functools
import math

import numpy as np
import jax
import jax.numpy as jnp
from jax import lax
from jax.experimental import pallas as pl
from jax.experimental.pallas import tpu as pltpu

F32 = jnp.float32
BF16 = jnp.bfloat16

D_MODEL = 2048
ROPE_THETA = 10000.0
RMS_EPS = 1e-6
NEG_INF = -1e30
TINY = 1e-30
LOG2E = math.log2(math.e)

WIDTH_A = 1024
HEAD_DIM_A = 128
N_KV_A = 2
GROUP_A = 4
KV_A = N_KV_A * HEAD_DIM_A
CMP_BLOCK = 32
CMP_STRIDE = 16
CMP_HIDDEN = 256
SEL_BLOCK = 64
SEL_TOPK = 16
WIN_A = 512
FORCE_SCORE = 1e4

WIDTH_B = 1024
HEAD_DIM_B = 64
N_HEADS_B = 16
N_KV_B = 2
GROUP_B = 8
KV_B = N_KV_B * HEAD_DIM_B
WIN_B = 128

IN_SIZES = (WIDTH_A, KV_A, KV_A, KV_A, KV_A, KV_A, KV_A, WIDTH_A, 3 * 8,
            WIDTH_B, KV_B, KV_B, WIDTH_B)

LANES = 128
SUBLANES = 8
CHUNK = 256

EP_NONE, EP_ROPE128, EP_ROPE64, EP_SILU, EP_SIGMOID, EP_ROPE128_Q, EP_ROPE64_Q = range(7)

_PROJ_LAYOUT = (
    ("qa", 4, EP_ROPE128_Q), ("kca", 1, EP_ROPE128),
    ("ksa", 1, EP_ROPE128), ("kwa", 1, EP_ROPE128), ("vca", 1, EP_NONE), ("vsa", 1, EP_NONE), ("vwa", 1, EP_NONE),
    ("za", 4, EP_SILU), ("ga", 1, EP_SIGMOID),
    ("zb", 4, EP_SILU), ("kb2", 1, EP_ROPE64),
    ("qb", 4, EP_ROPE64_Q), ("vb2", 1, EP_NONE),
)
_CHUNK_KINDS = tuple(k for _, n, k in _PROJ_LAYOUT for _ in range(n))
_CHUNK_START = {}
_c = 0
for _name, _n, _k in _PROJ_LAYOUT:
    _CHUNK_START[_name] = _c
    _c += _n
N_CHUNKS = _c
D_PROJ = N_CHUNKS * CHUNK
IN_TN = 1280
_IN_CPT = IN_TN // CHUNK
N_SLABS = D_PROJ // IN_TN


def _slab(name):
    return _CHUNK_START[name] // _IN_CPT


def _col(name):
    return (_CHUNK_START[name] % _IN_CPT) * CHUNK


def _dot(a, b):
    return jnp.dot(a, b, preferred_element_type=F32)


def _dot_nt(a, b):
    return lax.dot_general(a, b, (((1,), (1,)), ((), ())), preferred_element_type=F32)


IN_TM_NORM = 512
IN_TM = 1024
IN_SUB = 512

_TAB_GROUP = {EP_ROPE128: "r128", EP_ROPE128_Q: "r128q", EP_ROPE64: "r64", EP_ROPE64_Q: "r64q"}
_TAB_SIZE = {"r128": 2, "r128q": 2, "r64": 3, "r64q": 3}


def _slab_tab_groups(kinds):
    return tuple(dict.fromkeys(_TAB_GROUP[k] for k in kinds if k in _TAB_GROUP))


def _in_proj_kernel(kinds, with_norm, seq_len, strided_chunk, *refs):
    groups = _slab_tab_groups(kinds)
    refs = list(refs)
    if with_norm:
        x_ref, g_ref = refs.pop(0), refs.pop(0)
    else:
        h_ref = refs.pop(0)
    w_ref = refs.pop(0)
    tab_ref = refs.pop(0) if groups else None
    o_ref = refs.pop(0)
    if with_norm:
        h_ref = refs.pop(0)
        x = x_ref[...]
        ms = jnp.mean(x * x, axis=-1, keepdims=True)
        h_ref[...] = ((x * lax.rsqrt(ms + RMS_EPS)) * g_ref[...]).astype(BF16)
    if strided_chunk is not None:
        t_ref, stage_ref = refs.pop(0), refs.pop(0)
    tm = h_ref.shape[0]
    pos0 = (pl.program_id(0) % (seq_len // tm)) * tm
    base, n = {}, 0
    for grp in groups:
        base[grp] = n
        n += _TAB_SIZE[grp]

    def epilogue(kind, r, a):
        if kind == EP_NONE:
            return a
        if kind == EP_SILU:
            return a * jax.nn.sigmoid(a)
        if kind == EP_SIGMOID:
            return jax.nn.sigmoid(a)
        t = base[_TAB_GROUP[kind]]
        tab = lambda k: tab_ref[t + k, pl.ds(pl.multiple_of(pos0 + r * IN_SUB, IN_SUB), IN_SUB), :]
        if kind in (EP_ROPE128, EP_ROPE128_Q):
            return a * tab(0) + pltpu.roll(a, 64, 1) * tab(1)
        return a * tab(0) + pltpu.roll(a, 96, 1) * tab(1) + pltpu.roll(a, 32, 1) * tab(2)

    n_grp = IN_SUB // CMP_STRIDE

    def finish(kind, r, c, acc):
        rows = slice(r * IN_SUB, (r + 1) * IN_SUB)
        for sub in range(CHUNK // LANES):
            cols = slice(c * CHUNK + sub * LANES, c * CHUNK + (sub + 1) * LANES)
            val = epilogue(kind, r, acc[:, sub * LANES:(sub + 1) * LANES])
            o_ref[rows, cols] = val.astype(o_ref.dtype)
            if c == strided_chunk:
                stage_ref[sub] = val
                for k in range(CMP_STRIDE):
                    t_ref[sub, r * n_grp:(r + 1) * n_grp, k * LANES:(k + 1) * LANES] = (
                        stage_ref[sub, pl.ds(k, n_grp, stride=CMP_STRIDE), :].astype(t_ref.dtype))

    jobs = []
    for r in range(tm // IN_SUB):
        for c, kind in enumerate(kinds):
            jobs.append((lambda r=r, c=c: _dot_nt(h_ref[r * IN_SUB:(r + 1) * IN_SUB, :],
                                                  w_ref[c * CHUNK:(c + 1) * CHUNK, :]),
                         functools.partial(finish, kind, r, c)))
    _run_skewed(jobs)


def _rope_tables(S):
    def cs(d):
        inv = ROPE_THETA ** (-jnp.arange(0, d, 2, dtype=F32) / d)
        ang = jnp.arange(S, dtype=F32)[:, None] * inv[None, :]
        return jnp.cos(ang), jnp.sin(ang)

    c128, s128 = cs(HEAD_DIM_A)
    c64, s64 = cs(HEAD_DIM_B)
    z32 = jnp.zeros_like(s64)
    t128 = [jnp.concatenate([c128, c128], axis=1), jnp.concatenate([-s128, s128], axis=1)]
    t64 = [jnp.concatenate([c64, c64, c64, c64], axis=1),
           jnp.concatenate([-s64, z32, -s64, z32], axis=1),
           jnp.concatenate([z32, s64, z32, s64], axis=1)]
    qa = HEAD_DIM_A ** -0.5 * LOG2E
    qb = HEAD_DIM_B ** -0.5 * LOG2E
    return {"r128": t128, "r64": t64, "r128q": [t * qa for t in t128], "r64q": [t * qb for t in t64]}


W_PREP_TR = 256


def _w_in_pieces():
    offs = np.cumsum((0,) + IN_SIZES)
    names = ("qa", "kca", "vca", "ksa", "vsa", "kwa", "vwa", "za", "ga", "qb", "kb", "vb", "zb")
    start = {n: int(offs[i]) for i, n in enumerate(names)}
    width = dict(zip(names, IN_SIZES))
    n_g = 3 * GROUP_A
    halves = lambda n: [(start[n] + i * HEAD_DIM_B, HEAD_DIM_B) for i in (0, 0, 1, 1)]
    pieces = {n: [(start[n], width[n])] for n in ("qa", "kca", "ksa", "kwa", "vca", "vsa", "vwa", "za", "zb", "qb")}
    pieces["ga"] = [(start["ga"], n_g), (None, LANES - n_g), (start["ga"] + n_g, n_g), (None, LANES - n_g)]
    pieces["kb2"] = halves("kb")
    pieces["vb2"] = halves("vb")
    return [p for name, _, _ in _PROJ_LAYOUT for p in pieces[name]]


def _w_prep_kernel(w_ref, o_ref):
    row = 0
    parts, filled = [], 0
    for src, wd in _w_in_pieces():
        done = 0
        while done < wd:
            take = min(wd - done, LANES - filled)
            if src is None:
                parts.append(jnp.zeros((take, W_PREP_TR), F32))
            else:
                parts.append(w_ref[src + done:src + done + take, :])
            done += take
            filled += take
            if filled == LANES:
                blk = parts[0] if len(parts) == 1 else jnp.concatenate(parts, axis=0)
                o_ref[row:row + LANES, :] = blk.astype(o_ref.dtype)
                row += LANES
                parts, filled = [], 0
    assert row == D_PROJ and not parts


def _prep_w_in(w_in):
    wt = w_in[0].T
    d_in = wt.shape[0]
    return pl.pallas_call(
        _w_prep_kernel,
        grid=(D_MODEL // W_PREP_TR,),
        in_specs=[pl.BlockSpec((d_in, W_PREP_TR), lambda i: (0, i))],
        out_specs=pl.BlockSpec((D_PROJ, W_PREP_TR), lambda i: (0, i)),
        out_shape=jax.ShapeDtypeStruct((D_PROJ, D_MODEL), BF16),
        compiler_params=pltpu.CompilerParams(dimension_semantics=("arbitrary",)),
        name="w_prep",
    )(wt)


def _in_proj(x2, g, w_all, tabs, S):
    M = x2.shape[0]
    params = pltpu.CompilerParams(dimension_semantics=("arbitrary",), vmem_limit_bytes=48 * 1024 * 1024)
    out_sds = jax.ShapeDtypeStruct((M, IN_TN), BF16)
    slabs, strided, h = [], [], None
    for j in range(N_SLABS):
        kinds = _CHUNK_KINDS[j * _IN_CPT:(j + 1) * _IN_CPT]
        with_norm = j == 0
        tm = IN_TM_NORM if with_norm else IN_TM
        groups = _slab_tab_groups(kinds)
        act_spec = pl.BlockSpec((tm, D_MODEL), lambda i: (i, 0))
        in_specs = [act_spec, pl.BlockSpec((1, D_MODEL), lambda i: (0, 0))] if with_norm else [act_spec]
        args = [x2, g] if with_norm else [h]
        in_specs.append(pl.BlockSpec((IN_TN, D_MODEL), lambda i, j=j: (j, 0)))
        args.append(w_all)
        if groups:
            tab = jnp.stack([t for grp in groups for t in tabs[grp]])
            in_specs.append(pl.BlockSpec(tab.shape, lambda i: (0, 0, 0)))
            args.append(tab)
        out_specs = [pl.BlockSpec((tm, IN_TN), lambda i: (i, 0))]
        out_shape = [out_sds]
        if with_norm:
            out_specs.append(act_spec)
            out_shape.append(jax.ShapeDtypeStruct((M, D_MODEL), BF16))
        strided_chunk = None
        for name in ("kca", "vca"):
            if _slab(name) == j:
                strided_chunk = _col(name) // CHUNK
        scratch = []
        if strided_chunk is not None:
            out_specs.append(pl.BlockSpec((N_KV_A, tm // CMP_STRIDE, _HALF), lambda i: (0, i, 0)))
            out_shape.append(jax.ShapeDtypeStruct((N_KV_A, M // CMP_STRIDE, _HALF), BF16))
            scratch.append(pltpu.VMEM((N_KV_A, IN_SUB, HEAD_DIM_A), F32))
        res = list(pl.pallas_call(
            functools.partial(_in_proj_kernel, kinds, with_norm, S, strided_chunk),
            grid=(M // tm,),
            in_specs=in_specs,
            out_specs=out_specs,
            out_shape=out_shape,
            scratch_shapes=scratch,
            compiler_params=params,
            name=f"in_proj_{j}",
        )(*args))
        slabs.append(res.pop(0))
        if with_norm:
            h = res.pop(0)
        if strided_chunk is not None:
            strided.append(res.pop(0))
    return slabs, strided


CMP_TM = 256
_HALF = CMP_STRIDE * HEAD_DIM_A


def _compress_kernel(tk_ref, tv_ref, w1k_ref, w1v_ref, w2k_ref, w2v_ref, posk_ref, posv_ref, kc_ref, vc_ref):
    for t_ref, w1_ref, w2_ref, pos_ref, o_ref in ((tk_ref, w1k_ref, w2k_ref, posk_ref, kc_ref),
                                                  (tv_ref, w1v_ref, w2v_ref, posv_ref, vc_ref)):
        t = t_ref[...]
        u = _dot(t, w1_ref[0:_HALF, :])
        v = _dot(t, w1_ref[_HALF:2 * _HALF, :])
        pb = _dot(pos_ref[...], w1_ref[...])
        pre = u + pltpu.roll(v, CMP_TM - 1, 0) + pb[0:1]
        hid = pre * jax.nn.sigmoid(pre)
        out = _dot(hid.astype(BF16), w2_ref[...])
        row = lax.broadcasted_iota(jnp.int32, out.shape, 0)
        n_cp = LANES
        out = jnp.where((row & (n_cp - 1)) == n_cp - 1, 0.0, out)
        o_ref[...] = out.astype(o_ref.dtype)


def _compress(tk, tv, w1k, w1v, w2k, w2v, posk, posv):
    R = tk.shape[0]
    rows = pl.BlockSpec((CMP_TM, _HALF), lambda i: (i, 0))
    whole = lambda a: pl.BlockSpec(a.shape, lambda i: (0, 0))
    o_spec = pl.BlockSpec((CMP_TM, HEAD_DIM_A), lambda i: (i, 0))
    o_sds = jax.ShapeDtypeStruct((R, HEAD_DIM_A), BF16)
    return pl.pallas_call(
        _compress_kernel,
        grid=(R // CMP_TM,),
        in_specs=[rows, rows, whole(w1k), whole(w1v), whole(w2k), whole(w2v), whole(posk), whole(posv)],
        out_specs=(o_spec, o_spec),
        out_shape=(o_sds, o_sds),
        compiler_params=pltpu.CompilerParams(dimension_semantics=("arbitrary",)),
        name="compress",
    )(tk, tv, w1k, w1v, w2k, w2v, posk, posv)


TQ = 128
ONES_ROWS = 16


def _transpose_values(v_ref, vt_ref, n_rows):
    n_blk = vt_ref.shape[0]
    for blk in range(n_blk):
        vt = v_ref[0, blk * LANES:(blk + 1) * LANES, :].astype(F32).T
        vt_ref[blk, 0:n_rows, :] = vt[0:n_rows].astype(vt_ref.dtype)
        vt_ref[blk, n_rows:n_rows + ONES_ROWS, :] = jnp.ones((ONES_ROWS, LANES), vt_ref.dtype)


def _run_skewed(jobs):
    pending = [jobs[0][0]()]
    for i, (_, finish) in enumerate(jobs):
        if i + 1 < len(jobs):
            pending.append(jobs[i + 1][0]())
        finish(pending[i])
        pending[i] = None


def _softmax_step(s, m, acc, vt):
    m_new = jnp.maximum(m, jnp.max(s, axis=0, keepdims=True))
    p = jnp.exp2(s - m_new).astype(BF16)
    return m_new, jnp.exp2(m - m_new) * acc + _dot(vt, p)


SEL_TK = 256
N_SELP = 32
PAIR = 2 * TQ
_VROWS_A = HEAD_DIM_A + ONES_ROWS
_WIN_TILES = (256, 256, 128)
_N_SEL_JOBS = 9


def _attn_a_kernel(qlo_ref, qhi_ref, ksel_ref, vsel_ref, kwin_ref, vwin_ref, kc_ref, vc_ref,
                   glo_ref, ghi_ref, zlo_ref, zhi_ref, ov_ref, ylo_ref, yhi_ref,
                   vselt_ref, vwint_ref, qs_ref, bias_ref, acc_ref, out_ref):
    i = pl.program_id(2)
    n_qt = 2 * pl.num_programs(2)
    n_pairs = GROUP_A // 2
    t0s = (i * TQ, (n_qt - 1 - i) * TQ)
    q_refs, g_refs, z_refs, y_refs = (qlo_ref, qhi_ref), (glo_ref, ghi_ref), (zlo_ref, zhi_ref), (ylo_ref, yhi_ref)

    @pl.when(i == 0)
    def _():
        _transpose_values(vsel_ref, vselt_ref, HEAD_DIM_A)
        _transpose_values(vwin_ref, vwint_ref, HEAD_DIM_A)

    lane = lax.broadcasted_iota(jnp.int32, (1, TQ), 1)
    lane2 = jnp.concatenate([lane, lane], axis=1)
    qpair, gts = [], []
    for t in range(2):
        q = q_refs[t][0]
        pairs = [jnp.concatenate([q[:, h * LANES:(h + 1) * LANES] for h in (2 * p, 2 * p + 1)], axis=0)
                 for p in range(n_pairs)]
        for p in range(n_pairs):
            qs_ref[t, p] = pairs[p]
        qpair.append(pairs)
        gts.append(g_refs[t][0].astype(F32).T)

    def gate(t, p, branch):
        return jnp.concatenate(
            [gts[t][3 * h + branch:3 * h + branch + 1] for h in (2 * p, 2 * p + 1)], axis=1)

    kc = kc_ref[0]
    vct = vc_ref[0].astype(F32).T.astype(BF16)
    cend = lax.broadcasted_iota(jnp.int32, (LANES, PAIR), 0) * CMP_STRIDE + (CMP_BLOCK - 1)
    psums = [[], []]

    def cmp_scores(t, p):
        return jnp.where(cend <= t0s[t] + lane2, _dot_nt(kc, qpair[t][p]), NEG_INF)

    def cmp_finish(t, p, s):
        m = jnp.max(s, axis=0, keepdims=True)
        e = jnp.where(cend <= t0s[t] + lane2, jnp.exp2(s - m), 0.0)
        pc = e / jnp.maximum(jnp.sum(e, axis=0, keepdims=True), TINY)
        psums[t].append(pc[:, 0:TQ] + pc[:, TQ:PAIR])
        out_ref[t, p] = gate(t, p, 0) * _dot(vct, pc.astype(BF16))

    ksts = [pl.multiple_of(jnp.maximum(t0 - WIN_A, 0), TQ) for t0 in t0s]
    win_m = [[jnp.full((1, PAIR), NEG_INF, F32) for _ in range(n_pairs)] for _ in range(2)]
    win_acc = [[jnp.zeros((_VROWS_A, PAIR), F32) for _ in range(n_pairs)] for _ in range(2)]

    def win_scores(t, p, off, tk):
        kt = kwin_ref[0, pl.ds(pl.multiple_of(ksts[t] + off, LANES), tk), :]
        dpos = (t0s[t] + lane2) - (ksts[t] + off + lax.broadcasted_iota(jnp.int32, (tk, PAIR), 0))
        return jnp.where((dpos >= 0) & (dpos < WIN_A), _dot_nt(kt, qpair[t][p]), NEG_INF)

    def win_finish(t, p, off, tk, s):
        vt = jnp.concatenate(
            [vwint_ref[ksts[t] // LANES + off // LANES + b] for b in range(tk // LANES)], axis=1)
        win_m[t][p], win_acc[t][p] = _softmax_step(s, win_m[t][p], win_acc[t][p], vt)

    jobs = [(functools.partial(cmp_scores, t, p), functools.partial(cmp_finish, t, p))
            for t in range(2) for p in range(n_pairs)]
    off = 0
    for tk in _WIN_TILES:
        jobs += [(functools.partial(win_scores, t, p, off, tk), functools.partial(win_finish, t, p, off, tk))
                 for t in range(2) for p in range(n_pairs)]
        off += tk
    _run_skewed(jobs)
    for t in range(2):
        for p in range(n_pairs):
            a = win_acc[t][p]
            out_ref[t, p] = out_ref[t, p] + (
                gate(t, p, 2) / jnp.maximum(a[HEAD_DIM_A:HEAD_DIM_A + 1], TINY)) * a[0:HEAD_DIM_A]

    ov = ov_ref[...]
    jidx = lax.broadcasted_iota(jnp.int32, (N_SELP, TQ), 0)
    for t in range(2):
        psum = psums[t][0] + psums[t][1]
        p_hi = psum.astype(BF16)
        r1 = psum - p_hi.astype(F32)
        p_mid = r1.astype(BF16)
        p_lo = (r1 - p_mid.astype(F32)).astype(BF16)
        psel = _dot(ov, p_hi) + _dot(ov, p_mid) + _dot(ov, p_lo)
        tq = t0s[t] + lane
        cur = tq // SEL_BLOCK
        forced = (jidx == 0) | (jidx == cur) | (jidx == cur - 1)
        score = jnp.where(forced, FORCE_SCORE, jnp.where(jidx * SEL_BLOCK <= tq, psel, -1.0))
        rank = jnp.zeros(psel.shape, F32)
        for r in range(N_SELP):
            row = jnp.broadcast_to(score[r:r + 1, :], score.shape)
            rank = rank + jnp.where(jidx > r, jnp.where(row >= score, 1.0, 0.0), jnp.where(row > score, 1.0, 0.0))
        bias_t = jnp.where(rank < SEL_TOPK, 0.0, NEG_INF)
        for j in range(N_SELP):
            bias_ref[t, j] = jnp.broadcast_to(bias_t[j:j + 1, :], (SUBLANES, TQ))

    blk_per_tile = SEL_TK // SEL_BLOCK
    rep = SEL_BLOCK // SUBLANES
    n_lo = i // (SEL_TK // TQ) + 1
    acc_ref[...] = jnp.zeros(acc_ref.shape, F32)
    ms = [[jnp.full((1, PAIR), NEG_INF, F32) for _ in range(n_pairs)] for _ in range(2)]
    ctx = {}

    def sel_scores(k, p):
        if p == 0:
            hi = k >= n_lo
            t = hi.astype(jnp.int32)
            j = jnp.where(hi, k - n_lo, k)
            ks = pl.multiple_of(j * SEL_TK, SEL_TK)
            bt = jnp.concatenate(
                [bias_ref[t, j * blk_per_tile + b] for b in range(blk_per_tile) for _ in range(rep)], axis=0)
            bt = jnp.concatenate([bt, bt], axis=1)
            kpos = ks + lax.broadcasted_iota(jnp.int32, (SEL_TK, PAIR), 0)
            bt = jnp.where(kpos <= jnp.where(hi, t0s[1], t0s[0]) + lane2, bt, NEG_INF)
            ctx[k] = (hi, t, j, ks, bt)
        hi, t, j, ks, bt = ctx[k]
        return _dot_nt(ksel_ref[0, pl.ds(ks, SEL_TK), :], qs_ref[t, p]) + bt

    def sel_finish(k, p, s):
        hi, t, j, ks, bt = ctx[k]
        vt = jnp.concatenate([vselt_ref[j * (SEL_TK // LANES) + b] for b in range(SEL_TK // LANES)], axis=1)
        m_new, acc_new = _softmax_step(s, jnp.where(hi, ms[1][p], ms[0][p]), acc_ref[t, p], vt)
        acc_ref[t, p] = acc_new
        ms[0][p] = jnp.where(hi, ms[0][p], m_new)
        ms[1][p] = jnp.where(hi, m_new, ms[1][p])

    sel_jobs = [(functools.partial(sel_scores, k, p), functools.partial(sel_finish, k, p))
                for k in range(_N_SEL_JOBS) for p in range(n_pairs)]
    _run_skewed(sel_jobs)

    for t in range(2):
        z = z_refs[t][0]
        for p in range(n_pairs):
            a = acc_ref[t, p]
            o = out_ref[t, p] + (gate(t, p, 1) / jnp.maximum(a[HEAD_DIM_A:HEAD_DIM_A + 1], TINY)) * a[0:HEAD_DIM_A]
            for h in (2 * p, 2 * p + 1):
                oh = o[:, (h % 2) * TQ:(h % 2 + 1) * TQ].T
                y_refs[t][0, :, h * LANES:(h + 1) * LANES] = (
                    oh * z[:, h * LANES:(h + 1) * LANES].astype(F32)).astype(y_refs[t].dtype)


def _attn_a(slabs, kc, vc, ov):
    B, S, _ = slabs[0].shape
    sl = lambda name: slabs[_slab(name)]
    n_cp = S // CMP_STRIDE
    n_qt = S // TQ
    assert n_cp == LANES and S // SEL_BLOCK == N_SELP and (n_qt // 2) // (SEL_TK // TQ) * 2 + 1 == _N_SEL_JOBS
    wq = GROUP_A * HEAD_DIM_A
    cb = lambda name: _col(name) // LANES
    kv_spec = lambda name: pl.BlockSpec((1, S, LANES), lambda b, k, i, c=cb(name): (b, 0, c + k))
    lo = lambda width, c: pl.BlockSpec((1, TQ, width), lambda b, k, i: (b, i, c + k))
    hi = lambda width, c: pl.BlockSpec((1, TQ, width), lambda b, k, i: (b, n_qt - 1 - i, c + k))
    y_sds = jax.ShapeDtypeStruct((B, S // 2, WIDTH_A), BF16)
    return pl.pallas_call(
        _attn_a_kernel,
        grid=(B, N_KV_A, n_qt // 2),
        in_specs=[
            lo(wq, _col("qa") // wq), hi(wq, _col("qa") // wq),
            kv_spec("ksa"), kv_spec("vsa"), kv_spec("kwa"), kv_spec("vwa"),
            pl.BlockSpec((1, n_cp, LANES), lambda b, k, i: (k * B + b, 0, 0)),
            pl.BlockSpec((1, n_cp, LANES), lambda b, k, i: (k * B + b, 0, 0)),
            lo(LANES, cb("ga")), hi(LANES, cb("ga")),
            lo(wq, _col("za") // wq), hi(wq, _col("za") // wq),
            pl.BlockSpec((N_SELP, LANES), lambda b, k, i: (0, 0)),
        ],
        out_specs=(pl.BlockSpec((1, TQ, wq), lambda b, k, i: (b, i, k)),
                   pl.BlockSpec((1, TQ, wq), lambda b, k, i: (b, n_qt // 2 - 1 - i, k))),
        out_shape=(y_sds, y_sds),
        scratch_shapes=[
            pltpu.VMEM((S // LANES, _VROWS_A, LANES), BF16),
            pltpu.VMEM((S // LANES, _VROWS_A, LANES), BF16),
            pltpu.VMEM((2, GROUP_A // 2, PAIR, HEAD_DIM_A), BF16),
            pltpu.VMEM((2, N_SELP, SUBLANES, TQ), F32),
            pltpu.VMEM((2, GROUP_A // 2, _VROWS_A, PAIR), F32),
            pltpu.VMEM((2, GROUP_A // 2, HEAD_DIM_A, PAIR), F32),
        ],
        compiler_params=pltpu.CompilerParams(
            dimension_semantics=("arbitrary", "arbitrary", "arbitrary"), vmem_limit_bytes=48 * 1024 * 1024),
        name="attn_a",
    )(sl("qa"), sl("qa"), sl("ksa"), sl("vsa"), sl("kwa"), sl("vwa"), kc, vc,
      sl("ga"), sl("ga"), sl("za"), sl("za"), ov)


_VROWS_B = HEAD_DIM_B + ONES_ROWS


B_QT = 4


def _attn_b_kernel(sink_ref, q_ref, k_ref, v_ref, z_ref, y_ref, vt_ref):
    kv = pl.program_id(1)
    qi = pl.program_id(2)
    n_keys = WIN_B + TQ

    @pl.when(qi == 0)
    def _():
        _transpose_values(v_ref, vt_ref, HEAD_DIM_B)

    left = lax.broadcasted_iota(jnp.int32, (TQ, LANES), 1) < HEAD_DIM_B
    zero = jnp.zeros((TQ, LANES), q_ref.dtype)
    lane = lax.broadcasted_iota(jnp.int32, (1, TQ), 1)
    lane2 = jnp.concatenate([lane, lane], axis=1)
    krow = lax.broadcasted_iota(jnp.int32, (n_keys, PAIR), 0)
    sinks = [jnp.concatenate(
        [jnp.full((1, TQ), sink_ref[kv * GROUP_B + 2 * c + i] * LOG2E, F32) for i in range(2)], axis=1)
        for c in range(GROUP_B // 2)]
    ctx = {}

    def tile_ctx(u):
        if u not in ctx:
            t0 = (qi * B_QT + u) * TQ
            kst = pl.multiple_of(jnp.maximum(t0 - WIN_B, 0), TQ)
            dpos = (t0 + lane2) - (kst + krow)
            bias = jnp.where((dpos >= 0) & (dpos < WIN_B), 0.0, NEG_INF)
            ctx[u] = (kst, bias)
        return ctx[u]

    def scores(u, c):
        kst, bias = tile_ctx(u)
        pair = q_ref[0, u * TQ:(u + 1) * TQ, c * LANES:(c + 1) * LANES]
        qp = jnp.concatenate([jnp.where(left, pair, zero), jnp.where(left, zero, pair)], axis=0)
        kt = k_ref[0, pl.ds(kst, n_keys), :]
        return _dot_nt(kt, qp) + bias

    def finish(u, c, s):
        kst, _ = tile_ctx(u)
        vt = jnp.concatenate([vt_ref[kst // LANES + b] for b in range(n_keys // LANES)], axis=1)
        m = jnp.maximum(jnp.max(s, axis=0, keepdims=True), sinks[c])
        p = jnp.exp2(s - m).astype(BF16)
        a = _dot(vt, p)
        den = jnp.maximum(a[HEAD_DIM_B:HEAD_DIM_B + 1] + jnp.exp2(sinks[c] - m), TINY)
        o = a[0:HEAD_DIM_B] / den
        o2 = jnp.concatenate([o[:, 0:TQ], o[:, TQ:PAIR]], axis=0)
        z = z_ref[0, u * TQ:(u + 1) * TQ, c * LANES:(c + 1) * LANES]
        y_ref[0, u * TQ:(u + 1) * TQ, c * LANES:(c + 1) * LANES] = (o2.T * z.astype(F32)).astype(y_ref.dtype)

    _run_skewed([(functools.partial(scores, u, c), functools.partial(finish, u, c))
                 for u in range(B_QT) for c in range(GROUP_B // 2)])


def _attn_b(sinks, slabs):
    B, S, _ = slabs[0].shape
    sl = lambda name: slabs[_slab(name)]
    wq = GROUP_B * HEAD_DIM_B
    cb = lambda name: _col(name) // LANES
    grid_spec = pltpu.PrefetchScalarGridSpec(
        num_scalar_prefetch=1,
        grid=(B, N_KV_B, S // (B_QT * TQ)),
        in_specs=[
            pl.BlockSpec((1, B_QT * TQ, wq), lambda b, k, i, s: (b, i, _col("qb") // wq + k)),
            pl.BlockSpec((1, S, LANES), lambda b, k, i, s, c=cb("kb2"): (b, 0, c + k)),
            pl.BlockSpec((1, S, LANES), lambda b, k, i, s, c=cb("vb2"): (b, 0, c + k)),
            pl.BlockSpec((1, B_QT * TQ, wq), lambda b, k, i, s: (b, i, _col("zb") // wq + k)),
        ],
        out_specs=pl.BlockSpec((1, B_QT * TQ, wq), lambda b, k, i, s: (b, i, k)),
        scratch_shapes=[pltpu.VMEM((S // LANES, _VROWS_B, LANES), BF16)],
    )
    return pl.pallas_call(
        _attn_b_kernel,
        grid_spec=grid_spec,
        out_shape=jax.ShapeDtypeStruct((B, S, WIDTH_B), BF16),
        compiler_params=pltpu.CompilerParams(
            dimension_semantics=("arbitrary", "arbitrary", "arbitrary")),
        name="attn_b",
    )(sinks, sl("qb"), sl("kb2"), sl("vb2"), sl("zb"))


OUT_TM = 512


def _out_proj_kernel(tiles_per_half, ylo_ref, yhi_ref, yb_ref, wa_ref, wb_ref, x_ref, g_ref, o_ref):
    in_lo = (pl.program_id(0) % (2 * tiles_per_half)) < tiles_per_half
    ya = jnp.where(in_lo, ylo_ref[...], yhi_ref[...])
    r = x_ref[...] + _dot(ya, wa_ref[...]) + _dot(yb_ref[...], wb_ref[...])
    ms = jnp.mean(r * r, axis=-1, keepdims=True)
    o_ref[...] = (r * lax.rsqrt(ms + RMS_EPS)) * g_ref[...]


def _out_proj(ya_lo, ya_hi, yb2, w_out, x2, g, S):
    M = x2.shape[0]
    tph = S // 2 // OUT_TM
    half = lambda i: (i // (2 * tph)) * tph
    t = lambda i: i % (2 * tph)
    return pl.pallas_call(
        functools.partial(_out_proj_kernel, tph),
        grid=(M // OUT_TM,),
        in_specs=[
            pl.BlockSpec((OUT_TM, WIDTH_A), lambda i: (half(i) + jnp.minimum(t(i), tph - 1), 0)),
            pl.BlockSpec((OUT_TM, WIDTH_A), lambda i: (half(i) + jnp.maximum(t(i) - tph, 0), 0)),
            pl.BlockSpec((OUT_TM, WIDTH_B), lambda i: (i, 0)),
            pl.BlockSpec((WIDTH_A, D_MODEL), lambda i: (0, 0)),
            pl.BlockSpec((WIDTH_B, D_MODEL), lambda i: (WIDTH_A // WIDTH_B, 0)),
            pl.BlockSpec((OUT_TM, D_MODEL), lambda i: (i, 0)),
            pl.BlockSpec((1, D_MODEL), lambda i: (0, 0)),
        ],
        out_specs=pl.BlockSpec((OUT_TM, D_MODEL), lambda i: (i, 0)),
        out_shape=jax.ShapeDtypeStruct((M, D_MODEL), F32),
        compiler_params=pltpu.CompilerParams(
            dimension_semantics=("arbitrary",), vmem_limit_bytes=56 * 1024 * 1024),
        name="out_proj",
    )(ya_lo, ya_hi, yb2, w_out, w_out, x2, g)


def _overlap_matrix(n_cp):
    c_start = np.arange(n_cp) * CMP_STRIDE
    j_start = np.arange(N_SELP) * SEL_BLOCK
    ov = (c_start[None, :] < j_start[:, None] + SEL_BLOCK) & (c_start[None, :] + CMP_BLOCK > j_start[:, None])
    return jnp.asarray(ov, BF16)


def kernel(x, w_in, cmp_k_w1, cmp_k_w2, cmp_v_w1, cmp_v_w2, cmp_k_pos, cmp_v_pos, sinks, w_out, norm_g, final_g):
    B, S, D = x.shape
    assert D == D_MODEL and w_in.shape[0] == 1
    x2 = x.reshape(B * S, D)

    w_all = _prep_w_in(w_in)
    tabs = _rope_tables(S)
    slabs, strided = _in_proj(x2, norm_g[0].reshape(1, D), w_all, tabs, S)
    slabs = [p.reshape(B, S, IN_TN) for p in slabs]

    n_cp = S // CMP_STRIDE
    tk, tv = [t.reshape(N_KV_A * B * n_cp, _HALF) for t in strided]
    flat_pos = lambda p: jnp.broadcast_to(p.reshape(1, CMP_BLOCK * HEAD_DIM_A), (8, CMP_BLOCK * HEAD_DIM_A))
    kc, vc = _compress(tk, tv, cmp_k_w1[0].astype(BF16), cmp_v_w1[0].astype(BF16),
                       cmp_k_w2[0].astype(BF16), cmp_v_w2[0].astype(BF16),
                       flat_pos(cmp_k_pos[0]).astype(BF16), flat_pos(cmp_v_pos[0]).astype(BF16))
    kc, vc = [c.reshape(N_KV_A * B, n_cp, HEAD_DIM_A) for c in (kc, vc)]

    ya_lo, ya_hi = _attn_a(slabs, kc, vc, _overlap_matrix(n_cp))
    y_b = _attn_b(sinks[0], slabs)

    out = _out_proj(ya_lo.reshape(B * S // 2, WIDTH_A), ya_hi.reshape(B * S // 2, WIDTH_A),
                    y_b.reshape(B * S, WIDTH_B), w_out[0].astype(BF16), x2, final_g.reshape(1, D), S)
    return out.reshape(B, S, D)
```

```python
import functools
import math

import numpy as np
import jax
import jax.numpy as jnp
from jax import lax
from jax.experimental import pallas as pl
from jax.experimental.pallas import tpu as pltpu

F32 = jnp.float32
BF16 = jnp.bfloat16

D_MODEL = 2048
ROPE_THETA = 10000.0
RMS_EPS = 1e-6
NEG_INF = -1e30
TINY = 1e-30
LOG2E = math.log2(math.e)

WIDTH_A = 1024
HEAD_DIM_A = 128
N_KV_A = 2
GROUP_A = 4
KV_A = N_KV_A * HEAD_DIM_A
CMP_BLOCK = 32
CMP_STRIDE = 16
CMP_HIDDEN = 256
SEL_BLOCK = 64
SEL_TOPK = 16
WIN_A = 512
FORCE_SCORE = 1e4

WIDTH_B = 1024
HEAD_DIM_B = 64
N_HEADS_B = 16
N_KV_B = 2
GROUP_B = 8
KV_B = N_KV_B * HEAD_DIM_B
WIN_B = 128

IN_SIZES = (WIDTH_A, KV_A, KV_A, KV_A, KV_A, KV_A, KV_A, WIDTH_A, 3 * 8,
            WIDTH_B, KV_B, KV_B, WIDTH_B)

LANES = 128
SUBLANES = 8
CHUNK = 256

EP_NONE, EP_ROPE128, EP_ROPE64, EP_SILU, EP_SIGMOID, EP_ROPE128_Q, EP_ROPE64_Q = range(7)

_PROJ_LAYOUT = (
    ("qa", 4, EP_ROPE128_Q), ("kca", 1, EP_ROPE128),
    ("ksa", 1, EP_ROPE128), ("kwa", 1, EP_ROPE128), ("vca", 1, EP_NONE), ("vsa", 1, EP_NONE), ("vwa", 1, EP_NONE),
    ("za", 4, EP_SILU), ("ga", 1, EP_SIGMOID),
    ("zb", 4, EP_SILU), ("kb2", 1, EP_ROPE64),
    ("qb", 4, EP_ROPE64_Q), ("vb2", 1, EP_NONE),
)
_CHUNK_KINDS = tuple(k for _, n, k in _PROJ_LAYOUT for _ in range(n))
_CHUNK_START = {}
_c = 0
for _name, _n, _k in _PROJ_LAYOUT:
    _CHUNK_START[_name] = _c
    _c += _n
N_CHUNKS = _c
D_PROJ = N_CHUNKS * CHUNK
IN_TN = 1280
_IN_CPT = IN_TN // CHUNK
N_SLABS = D_PROJ // IN_TN


def _slab(name):
    return _CHUNK_START[name] // _IN_CPT


def _col(name):
    return (_CHUNK_START[name] % _IN_CPT) * CHUNK


def _dot(a, b):
    return jnp.dot(a, b, preferred_element_type=F32)


def _dot_nt(a, b):
    return lax.dot_general(a, b, (((1,), (1,)), ((), ())), preferred_element_type=F32)


IN_TM_NORM = 512
IN_TM = 1024
IN_SUB = 512

_TAB_GROUP = {EP_ROPE128: "r128", EP_ROPE128_Q: "r128q", EP_ROPE64: "r64", EP_ROPE64_Q: "r64q"}
_TAB_SIZE = {"r128": 2, "r128q": 2, "r64": 3, "r64q": 3}


def _slab_tab_groups(kinds):
    return tuple(dict.fromkeys(_TAB_GROUP[k] for k in kinds if k in _TAB_GROUP))


def _in_proj_kernel(kinds, with_norm, seq_len, strided_chunk, *refs):
    groups = _slab_tab_groups(kinds)
    refs = list(refs)
    if with_norm:
        x_ref, g_ref = refs.pop(0), refs.pop(0)
    else:
        h_ref = refs.pop(0)
    w_ref = refs.pop(0)
    tab_ref = refs.pop(0) if groups else None
    o_ref = refs.pop(0)
    if with_norm:
        h_ref = refs.pop(0)
        x = x_ref[...]
        ms = jnp.mean(x * x, axis=-1, keepdims=True)
        h_ref[...] = ((x * lax.rsqrt(ms + RMS_EPS)) * g_ref[...]).astype(BF16)
    if strided_chunk is not None:
        t_ref, stage_ref = refs.pop(0), refs.pop(0)
    tm = h_ref.shape[0]
    pos0 = (pl.program_id(0) % (seq_len // tm)) * tm
    base, n = {}, 0
    for grp in groups:
        base[grp] = n
        n += _TAB_SIZE[grp]

    def epilogue(kind, r, a):
        if kind == EP_NONE:
            return a
        if kind == EP_SILU:
            return a * jax.nn.sigmoid(a)
        if kind == EP_SIGMOID:
            return jax.nn.sigmoid(a)
        t = base[_TAB_GROUP[kind]]
        tab = lambda k: tab_ref[t + k, pl.ds(pl.multiple_of(pos0 + r * IN_SUB, IN_SUB), IN_SUB), :]
        if kind in (EP_ROPE128, EP_ROPE128_Q):
            return a * tab(0) + pltpu.roll(a, 64, 1) * tab(1)
        return a * tab(0) + pltpu.roll(a, 96, 1) * tab(1) + pltpu.roll(a, 32, 1) * tab(2)

    n_grp = IN_SUB // CMP_STRIDE

    def finish(kind, r, c, acc):
        rows = slice(r * IN_SUB, (r + 1) * IN_SUB)
        for sub in range(CHUNK // LANES):
            cols = slice(c * CHUNK + sub * LANES, c * CHUNK + (sub + 1) * LANES)
            val = epilogue(kind, r, acc[:, sub * LANES:(sub + 1) * LANES])
            o_ref[rows, cols] = val.astype(o_ref.dtype)
            if c == strided_chunk:
                stage_ref[sub] = val
                for k in range(CMP_STRIDE):
                    t_ref[sub, r * n_grp:(r + 1) * n_grp, k * LANES:(k + 1) * LANES] = (
                        stage_ref[sub, pl.ds(k, n_grp, stride=CMP_STRIDE), :].astype(t_ref.dtype))

    jobs = []
    for r in range(tm // IN_SUB):
        for c, kind in enumerate(kinds):
            jobs.append((lambda r=r, c=c: _dot_nt(h_ref[r * IN_SUB:(r + 1) * IN_SUB, :],
                                                  w_ref[c * CHUNK:(c + 1) * CHUNK, :]),
                         functools.partial(finish, kind, r, c)))
    _run_skewed(jobs)


def _rope_tables(S):
    def cs(d):
        inv = ROPE_THETA ** (-jnp.arange(0, d, 2, dtype=F32) / d)
        ang = jnp.arange(S, dtype=F32)[:, None] * inv[None, :]
        return jnp.cos(ang), jnp.sin(ang)

    c128, s128 = cs(HEAD_DIM_A)
    c64, s64 = cs(HEAD_DIM_B)
    z32 = jnp.zeros_like(s64)
    t128 = [jnp.concatenate([c128, c128], axis=1), jnp.concatenate([-s128, s128], axis=1)]
    t64 = [jnp.concatenate([c64, c64, c64, c64], axis=1),
           jnp.concatenate([-s64, z32, -s64, z32], axis=1),
           jnp.concatenate([z32, s64, z32, s64], axis=1)]
    qa = HEAD_DIM_A ** -0.5 * LOG2E
    qb = HEAD_DIM_B ** -0.5 * LOG2E
    return {"r128": t128, "r64": t64, "r128q": [t * qa for t in t128], "r64q": [t * qb for t in t64]}


W_PREP_TR = 256


def _w_in_pieces():
    offs = np.cumsum((0,) + IN_SIZES)
    names = ("qa", "kca", "vca", "ksa", "vsa", "kwa", "vwa", "za", "ga", "qb", "kb", "vb", "zb")
    start = {n: int(offs[i]) for i, n in enumerate(names)}
    width = dict(zip(names, IN_SIZES))
    n_g = 3 * GROUP_A
    halves = lambda n: [(start[n] + i * HEAD_DIM_B, HEAD_DIM_B) for i in (0, 0, 1, 1)]
    pieces = {n: [(start[n], width[n])] for n in ("qa", "kca", "ksa", "kwa", "vca", "vsa", "vwa", "za", "zb", "qb")}
    pieces["ga"] = [(start["ga"], n_g), (None, LANES - n_g), (start["ga"] + n_g, n_g), (None, LANES - n_g)]
    pieces["kb2"] = halves("kb")
    pieces["vb2"] = halves("vb")
    return [p for name, _, _ in _PROJ_LAYOUT for p in pieces[name]]


def _w_prep_kernel(w_ref, o_ref):
    row = 0
    parts, filled = [], 0
    for src, wd in _w_in_pieces():
        done = 0
        while done < wd:
            take = min(wd - done, LANES - filled)
            if src is None:
                parts.append(jnp.zeros((take, W_PREP_TR), F32))
            else:
                parts.append(w_ref[src + done:src + done + take, :])
            done += take
            filled += take
            if filled == LANES:
                blk = parts[0] if len(parts) == 1 else jnp.concatenate(parts, axis=0)
                o_ref[row:row + LANES, :] = blk.astype(o_ref.dtype)
                row += LANES
                parts, filled = [], 0
    assert row == D_PROJ and not parts


def _prep_w_in(w_in):
    wt = w_in[0].T
    d_in = wt.shape[0]
    return pl.pallas_call(
        _w_prep_kernel,
        grid=(D_MODEL // W_PREP_TR,),
        in_specs=[pl.BlockSpec((d_in, W_PREP_TR), lambda i: (0, i))],
        out_specs=pl.BlockSpec((D_PROJ, W_PREP_TR), lambda i: (0, i)),
        out_shape=jax.ShapeDtypeStruct((D_PROJ, D_MODEL), BF16),
        compiler_params=pltpu.CompilerParams(dimension_semantics=("arbitrary",)),
        name="w_prep",
    )(wt)


def _in_proj(x2, g, w_all, tabs, S):
    M = x2.shape[0]
    params = pltpu.CompilerParams(dimension_semantics=("arbitrary",), vmem_limit_bytes=48 * 1024 * 1024)
    out_sds = jax.ShapeDtypeStruct((M, IN_TN), BF16)
    slabs, strided, h = [], [], None
    for j in range(N_SLABS):
        kinds = _CHUNK_KINDS[j * _IN_CPT:(j + 1) * _IN_CPT]
        with_norm = j == 0
        tm = IN_TM_NORM if with_norm else IN_TM
        groups = _slab_tab_groups(kinds)
        act_spec = pl.BlockSpec((tm, D_MODEL), lambda i: (i, 0))
        in_specs = [act_spec, pl.BlockSpec((1, D_MODEL), lambda i: (0, 0))] if with_norm else [act_spec]
        args = [x2, g] if with_norm else [h]
        in_specs.append(pl.BlockSpec((IN_TN, D_MODEL), lambda i, j=j: (j, 0)))
        args.append(w_all)
        if groups:
            tab = jnp.stack([t for grp in groups for t in tabs[grp]])
            in_specs.append(pl.BlockSpec(tab.shape, lambda i: (0, 0, 0)))
            args.append(tab)
        out_specs = [pl.BlockSpec((tm, IN_TN), lambda i: (i, 0))]
        out_shape = [out_sds]
        if with_norm:
            out_specs.append(act_spec)
            out_shape.append(jax.ShapeDtypeStruct((M, D_MODEL), BF16))
        strided_chunk = None
        for name in ("kca", "vca"):
            if _slab(name) == j:
                strided_chunk = _col(name) // CHUNK
        scratch = []
        if strided_chunk is not None:
            out_specs.append(pl.BlockSpec((N_KV_A, tm // CMP_STRIDE, _HALF), lambda i: (0, i, 0)))
            out_shape.append(jax.ShapeDtypeStruct((N_KV_A, M // CMP_STRIDE, _HALF), BF16))
            scratch.append(pltpu.VMEM((N_KV_A, IN_SUB, HEAD_DIM_A), F32))
        res = list(pl.pallas_call(
            functools.partial(_in_proj_kernel, kinds, with_norm, S, strided_chunk),
            grid=(M // tm,),
            in_specs=in_specs,
            out_specs=out_specs,
            out_shape=out_shape,
            scratch_shapes=scratch,
            compiler_params=params,
            name=f"in_proj_{j}",
        )(*args))
        slabs.append(res.pop(0))
        if with_norm:
            h = res.pop(0)
        if strided_chunk is not None:
            strided.append(res.pop(0))
    return slabs, strided


CMP_TM = 256
_HALF = CMP_STRIDE * HEAD_DIM_A


def _compress_kernel(tk_ref, tv_ref, w1k_ref, w1v_ref, w2k_ref, w2v_ref, posk_ref, posv_ref, kc_ref, vc_ref):
    for t_ref, w1_ref, w2_ref, pos_ref, o_ref in ((tk_ref, w1k_ref, w2k_ref, posk_ref, kc_ref),
                                                  (tv_ref, w1v_ref, w2v_ref, posv_ref, vc_ref)):
        t = t_ref[...]
        u = _dot(t, w1_ref[0:_HALF, :])
        v = _dot(t, w1_ref[_HALF:2 * _HALF, :])
        pb = _dot(pos_ref[...], w1_ref[...])
        pre = u + pltpu.roll(v, CMP_TM - 1, 0) + pb[0:1]
        hid = pre * jax.nn.sigmoid(pre)
        out = _dot(hid.astype(BF16), w2_ref[...])
        row = lax.broadcasted_iota(jnp.int32, out.shape, 0)
        n_cp = LANES
        out = jnp.where((row & (n_cp - 1)) == n_cp - 1, 0.0, out)
        o_ref[...] = out.astype(o_ref.dtype)


def _compress(tk, tv, w1k, w1v, w2k, w2v, posk, posv):
    R = tk.shape[0]
    rows = pl.BlockSpec((CMP_TM, _HALF), lambda i: (i, 0))
    whole = lambda a: pl.BlockSpec(a.shape, lambda i: (0, 0))
    o_spec = pl.BlockSpec((CMP_TM, HEAD_DIM_A), lambda i: (i, 0))
    o_sds = jax.ShapeDtypeStruct((R, HEAD_DIM_A), BF16)
    return pl.pallas_call(
        _compress_kernel,
        grid=(R // CMP_TM,),
        in_specs=[rows, rows, whole(w1k), whole(w1v), whole(w2k), whole(w2v), whole(posk), whole(posv)],
        out_specs=(o_spec, o_spec),
        out_shape=(o_sds, o_sds),
        compiler_params=pltpu.CompilerParams(dimension_semantics=("arbitrary",)),
        name="compress",
    )(tk, tv, w1k, w1v, w2k, w2v, posk, posv)


TQ = 128
ATTN_AHEAD = 4
ONES_ROWS = 16


def _transpose_values(v_ref, vt_ref, n_rows):
    n_blk = vt_ref.shape[0]
    for blk in range(n_blk):
        vt = v_ref[0, blk * LANES:(blk + 1) * LANES, :].astype(F32).T
        vt_ref[blk, 0:n_rows, :] = vt[0:n_rows].astype(vt_ref.dtype)
        vt_ref[blk, n_rows:n_rows + ONES_ROWS, :] = jnp.ones((ONES_ROWS, LANES), vt_ref.dtype)


def _run_skewed(jobs, ahead=1):
    pending = [job[0]() for job in jobs[:ahead]]
    for i, (_, finish) in enumerate(jobs):
        if i + ahead < len(jobs):
            pending.append(jobs[i + ahead][0]())
        finish(pending[i])
        pending[i] = None


def _softmax_step(s, m, acc, vt):
    m_new = jnp.maximum(m, jnp.max(s, axis=0, keepdims=True))
    p = jnp.exp2(s - m_new).astype(BF16)
    return m_new, jnp.exp2(m - m_new) * acc + _dot(vt, p)


SEL_TK = 256
N_SELP = 32
PAIR = 2 * TQ
_VROWS_A = HEAD_DIM_A + ONES_ROWS
_WIN_TILES = (256, 256, 128)
_N_SEL_JOBS = 9


def _attn_a_kernel(qlo_ref, qhi_ref, ksel_ref, vsel_ref, kwin_ref, vwin_ref, kc_ref, vc_ref,
                   glo_ref, ghi_ref, zlo_ref, zhi_ref, ov_ref, ylo_ref, yhi_ref,
                   vselt_ref, vwint_ref, qs_ref, bias_ref, acc_ref, out_ref):
    i = pl.program_id(2)
    n_qt = 2 * pl.num_programs(2)
    n_pairs = GROUP_A // 2
    t0s = (i * TQ, (n_qt - 1 - i) * TQ)
    q_refs, g_refs, z_refs, y_refs = (qlo_ref, qhi_ref), (glo_ref, ghi_ref), (zlo_ref, zhi_ref), (ylo_ref, yhi_ref)

    @pl.when(i == 0)
    def _():
        _transpose_values(vsel_ref, vselt_ref, HEAD_DIM_A)
        _transpose_values(vwin_ref, vwint_ref, HEAD_DIM_A)

    lane = lax.broadcasted_iota(jnp.int32, (1, TQ), 1)
    lane2 = jnp.concatenate([lane, lane], axis=1)
    qpair, gts = [], []
    for t in range(2):
        q = q_refs[t][0]
        pairs = [jnp.concatenate([q[:, h * LANES:(h + 1) * LANES] for h in (2 * p, 2 * p + 1)], axis=0)
                 for p in range(n_pairs)]
        for p in range(n_pairs):
            qs_ref[t, p] = pairs[p]
        qpair.append(pairs)
        gts.append(g_refs[t][0].astype(F32).T)

    def gate(t, p, branch):
        return jnp.concatenate(
            [gts[t][3 * h + branch:3 * h + branch + 1] for h in (2 * p, 2 * p + 1)], axis=1)

    kc = kc_ref[0]
    vct = vc_ref[0].astype(F32).T.astype(BF16)
    cend = lax.broadcasted_iota(jnp.int32, (LANES, PAIR), 0) * CMP_STRIDE + (CMP_BLOCK - 1)
    psums = [[], []]

    def cmp_scores(t, p):
        return jnp.where(cend <= t0s[t] + lane2, _dot_nt(kc, qpair[t][p]), NEG_INF)

    def cmp_finish(t, p, s):
        m = jnp.max(s, axis=0, keepdims=True)
        e = jnp.where(cend <= t0s[t] + lane2, jnp.exp2(s - m), 0.0)
        pc = e / jnp.maximum(jnp.sum(e, axis=0, keepdims=True), TINY)
        psums[t].append(pc[:, 0:TQ] + pc[:, TQ:PAIR])
        out_ref[t, p] = gate(t, p, 0) * _dot(vct, pc.astype(BF16))

    ksts = [pl.multiple_of(jnp.maximum(t0 - WIN_A, 0), TQ) for t0 in t0s]
    win_m = [[jnp.full((1, PAIR), NEG_INF, F32) for _ in range(n_pairs)] for _ in range(2)]
    win_acc = [[jnp.zeros((_VROWS_A, PAIR), F32) for _ in range(n_pairs)] for _ in range(2)]

    def win_scores(t, p, off, tk):
        kt = kwin_ref[0, pl.ds(pl.multiple_of(ksts[t] + off, LANES), tk), :]
        dpos = (t0s[t] + lane2) - (ksts[t] + off + lax.broadcasted_iota(jnp.int32, (tk, PAIR), 0))
        return jnp.where((dpos >= 0) & (dpos < WIN_A), _dot_nt(kt, qpair[t][p]), NEG_INF)

    def win_finish(t, p, off, tk, s):
        vt = jnp.concatenate(
            [vwint_ref[ksts[t] // LANES + off // LANES + b] for b in range(tk // LANES)], axis=1)
        win_m[t][p], win_acc[t][p] = _softmax_step(s, win_m[t][p], win_acc[t][p], vt)

    jobs = [(functools.partial(cmp_scores, t, p), functools.partial(cmp_finish, t, p))
            for t in range(2) for p in range(n_pairs)]
    off = 0
    for tk in _WIN_TILES:
        jobs += [(functools.partial(win_scores, t, p, off, tk), functools.partial(win_finish, t, p, off, tk))
                 for t in range(2) for p in range(n_pairs)]
        off += tk
    _run_skewed(jobs, ATTN_AHEAD)
    for t in range(2):
        for p in range(n_pairs):
            a = win_acc[t][p]
            out_ref[t, p] = out_ref[t, p] + (
                gate(t, p, 2) / jnp.maximum(a[HEAD_DIM_A:HEAD_DIM_A + 1], TINY)) * a[0:HEAD_DIM_A]

    ov = ov_ref[...]
    jidx = lax.broadcasted_iota(jnp.int32, (N_SELP, TQ), 0)
    for t in range(2):
        psum = psums[t][0] + psums[t][1]
        p_hi = psum.astype(BF16)
        r1 = psum - p_hi.astype(F32)
        p_mid = r1.astype(BF16)
        p_lo = (r1 - p_mid.astype(F32)).astype(BF16)
        psel = _dot(ov, p_hi) + _dot(ov, p_mid) + _dot(ov, p_lo)
        tq = t0s[t] + lane
        cur = tq // SEL_BLOCK
        forced = (jidx == 0) | (jidx == cur) | (jidx == cur - 1)
        score = jnp.where(forced, FORCE_SCORE, jnp.where(jidx * SEL_BLOCK <= tq, psel, -1.0))
        rank = jnp.zeros(psel.shape, F32)
        for r in range(N_SELP):
            row = jnp.broadcast_to(score[r:r + 1, :], score.shape)
            rank = rank + jnp.where(jidx > r, jnp.where(row >= score, 1.0, 0.0), jnp.where(row > score, 1.0, 0.0))
        bias_t = jnp.where(rank < SEL_TOPK, 0.0, NEG_INF)
        for j in range(N_SELP):
            bias_ref[t, j] = jnp.broadcast_to(bias_t[j:j + 1, :], (SUBLANES, TQ))

    blk_per_tile = SEL_TK // SEL_BLOCK
    rep = SEL_BLOCK // SUBLANES
    n_lo = i // (SEL_TK // TQ) + 1
    acc_ref[...] = jnp.zeros(acc_ref.shape, F32)
    ms = [[jnp.full((1, PAIR), NEG_INF, F32) for _ in range(n_pairs)] for _ in range(2)]
    ctx = {}

    def sel_scores(k, p):
        if p == 0:
            hi = k >= n_lo
            t = hi.astype(jnp.int32)
            j = jnp.where(hi, k - n_lo, k)
            ks = pl.multiple_of(j * SEL_TK, SEL_TK)
            bt = jnp.concatenate(
                [bias_ref[t, j * blk_per_tile + b] for b in range(blk_per_tile) for _ in range(rep)], axis=0)
            bt = jnp.concatenate([bt, bt], axis=1)
            kpos = ks + lax.broadcasted_iota(jnp.int32, (SEL_TK, PAIR), 0)
            bt = jnp.where(kpos <= jnp.where(hi, t0s[1], t0s[0]) + lane2, bt, NEG_INF)
            ctx[k] = (hi, t, j, ks, bt)
        hi, t, j, ks, bt = ctx[k]
        return _dot_nt(ksel_ref[0, pl.ds(ks, SEL_TK), :], qs_ref[t, p]) + bt

    def sel_finish(k, p, s):
        hi, t, j, ks, bt = ctx[k]
        vt = jnp.concatenate([vselt_ref[j * (SEL_TK // LANES) + b] for b in range(SEL_TK // LANES)], axis=1)
        m_new, acc_new = _softmax_step(s, jnp.where(hi, ms[1][p], ms[0][p]), acc_ref[t, p], vt)
        acc_ref[t, p] = acc_new
        ms[0][p] = jnp.where(hi, ms[0][p], m_new)
        ms[1][p] = jnp.where(hi, m_new, ms[1][p])

    sel_jobs = [(functools.partial(sel_scores, k, p), functools.partial(sel_finish, k, p))
                for k in range(_N_SEL_JOBS) for p in range(n_pairs)]
    _run_skewed(sel_jobs, ATTN_AHEAD)

    for t in range(2):
        z = z_refs[t][0]
        for p in range(n_pairs):
            a = acc_ref[t, p]
            o = out_ref[t, p] + (gate(t, p, 1) / jnp.maximum(a[HEAD_DIM_A:HEAD_DIM_A + 1], TINY)) * a[0:HEAD_DIM_A]
            for h in (2 * p, 2 * p + 1):
                oh = o[:, (h % 2) * TQ:(h % 2 + 1) * TQ].T
                y_refs[t][0, :, h * LANES:(h + 1) * LANES] = (
                    oh * z[:, h * LANES:(h + 1) * LANES].astype(F32)).astype(y_refs[t].dtype)


def _attn_a(slabs, kc, vc, ov):
    B, S, _ = slabs[0].shape
    sl = lambda name: slabs[_slab(name)]
    n_cp = S // CMP_STRIDE
    n_qt = S // TQ
    assert n_cp == LANES and S // SEL_BLOCK == N_SELP and (n_qt // 2) // (SEL_TK // TQ) * 2 + 1 == _N_SEL_JOBS
    wq = GROUP_A * HEAD_DIM_A
    cb = lambda name: _col(name) // LANES
    kv_spec = lambda name: pl.BlockSpec((1, S, LANES), lambda b, k, i, c=cb(name): (b, 0, c + k))
    lo = lambda width, c: pl.BlockSpec((1, TQ, width), lambda b, k, i: (b, i, c + k))
    hi = lambda width, c: pl.BlockSpec((1, TQ, width), lambda b, k, i: (b, n_qt - 1 - i, c + k))
    y_sds = jax.ShapeDtypeStruct((B, S // 2, WIDTH_A), BF16)
    return pl.pallas_call(
        _attn_a_kernel,
        grid=(B, N_KV_A, n_qt // 2),
        in_specs=[
            lo(wq, _col("qa") // wq), hi(wq, _col("qa") // wq),
            kv_spec("ksa"), kv_spec("vsa"), kv_spec("kwa"), kv_spec("vwa"),
            pl.BlockSpec((1, n_cp, LANES), lambda b, k, i: (k * B + b, 0, 0)),
            pl.BlockSpec((1, n_cp, LANES), lambda b, k, i: (k * B + b, 0, 0)),
            lo(LANES, cb("ga")), hi(LANES, cb("ga")),
            lo(wq, _col("za") // wq), hi(wq, _col("za") // wq),
            pl.BlockSpec((N_SELP, LANES), lambda b, k, i: (0, 0)),
        ],
        out_specs=(pl.BlockSpec((1, TQ, wq), lambda b, k, i: (b, i, k)),
                   pl.BlockSpec((1, TQ, wq), lambda b, k, i: (b, n_qt // 2 - 1 - i, k))),
        out_shape=(y_sds, y_sds),
        scratch_shapes=[
            pltpu.VMEM((S // LANES, _VROWS_A, LANES), BF16),
            pltpu.VMEM((S // LANES, _VROWS_A, LANES), BF16),
            pltpu.VMEM((2, GROUP_A // 2, PAIR, HEAD_DIM_A), BF16),
            pltpu.VMEM((2, N_SELP, SUBLANES, TQ), F32),
            pltpu.VMEM((2, GROUP_A // 2, _VROWS_A, PAIR), F32),
            pltpu.VMEM((2, GROUP_A // 2, HEAD_DIM_A, PAIR), F32),
        ],
        compiler_params=pltpu.CompilerParams(
            dimension_semantics=("arbitrary", "arbitrary", "arbitrary"), vmem_limit_bytes=48 * 1024 * 1024),
        name="attn_a",
    )(sl("qa"), sl("qa"), sl("ksa"), sl("vsa"), sl("kwa"), sl("vwa"), kc, vc,
      sl("ga"), sl("ga"), sl("za"), sl("za"), ov)


_VROWS_B = HEAD_DIM_B + ONES_ROWS


B_QT = 4


def _attn_b_kernel(sink_ref, q_ref, k_ref, v_ref, z_ref, y_ref, vt_ref):
    kv = pl.program_id(1)
    qi = pl.program_id(2)
    n_keys = WIN_B + TQ

    @pl.when(qi == 0)
    def _():
        _transpose_values(v_ref, vt_ref, HEAD_DIM_B)

    left = lax.broadcasted_iota(jnp.int32, (TQ, LANES), 1) < HEAD_DIM_B
    zero = jnp.zeros((TQ, LANES), q_ref.dtype)
    lane = lax.broadcasted_iota(jnp.int32, (1, TQ), 1)
    lane2 = jnp.concatenate([lane, lane], axis=1)
    krow = lax.broadcasted_iota(jnp.int32, (n_keys, PAIR), 0)
    sinks = [jnp.concatenate(
        [jnp.full((1, TQ), sink_ref[kv * GROUP_B + 2 * c + i] * LOG2E, F32) for i in range(2)], axis=1)
        for c in range(GROUP_B // 2)]
    ctx = {}

    def tile_ctx(u):
        if u not in ctx:
            t0 = (qi * B_QT + u) * TQ
            kst = pl.multiple_of(jnp.maximum(t0 - WIN_B, 0), TQ)
            dpos = (t0 + lane2) - (kst + krow)
            bias = jnp.where((dpos >= 0) & (dpos < WIN_B), 0.0, NEG_INF)
            ctx[u] = (kst, bias)
        return ctx[u]

    def scores(u, c):
        kst, bias = tile_ctx(u)
        pair = q_ref[0, u * TQ:(u + 1) * TQ, c * LANES:(c + 1) * LANES]
        qp = jnp.concatenate([jnp.where(left, pair, zero), jnp.where(left, zero, pair)], axis=0)
        kt = k_ref[0, pl.ds(kst, n_keys), :]
        return _dot_nt(kt, qp) + bias

    def finish(u, c, s):
        kst, _ = tile_ctx(u)
        vt = jnp.concatenate([vt_ref[kst // LANES + b] for b in range(n_keys // LANES)], axis=1)
        m = jnp.maximum(jnp.max(s, axis=0, keepdims=True), sinks[c])
        p = jnp.exp2(s - m).astype(BF16)
        a = _dot(vt, p)
        den = jnp.maximum(a[HEAD_DIM_B:HEAD_DIM_B + 1] + jnp.exp2(sinks[c] - m), TINY)
        o = a[0:HEAD_DIM_B] / den
        o2 = jnp.concatenate([o[:, 0:TQ], o[:, TQ:PAIR]], axis=0)
        z = z_ref[0, u * TQ:(u + 1) * TQ, c * LANES:(c + 1) * LANES]
        y_ref[0, u * TQ:(u + 1) * TQ, c * LANES:(c + 1) * LANES] = (o2.T * z.astype(F32)).astype(y_ref.dtype)

    _run_skewed([(functools.partial(scores, u, c), functools.partial(finish, u, c))
                 for u in range(B_QT) for c in range(GROUP_B // 2)], ATTN_AHEAD)


def _attn_b(sinks, slabs):
    B, S, _ = slabs[0].shape
    sl = lambda name: slabs[_slab(name)]
    wq = GROUP_B * HEAD_DIM_B
    cb = lambda name: _col(name) // LANES
    grid_spec = pltpu.PrefetchScalarGridSpec(
        num_scalar_prefetch=1,
        grid=(B, N_KV_B, S // (B_QT * TQ)),
        in_specs=[
            pl.BlockSpec((1, B_QT * TQ, wq), lambda b, k, i, s: (b, i, _col("qb") // wq + k)),
            pl.BlockSpec((1, S, LANES), lambda b, k, i, s, c=cb("kb2"): (b, 0, c + k)),
            pl.BlockSpec((1, S, LANES), lambda b, k, i, s, c=cb("vb2"): (b, 0, c + k)),
            pl.BlockSpec((1, B_QT * TQ, wq), lambda b, k, i, s: (b, i, _col("zb") // wq + k)),
        ],
        out_specs=pl.BlockSpec((1, B_QT * TQ, wq), lambda b, k, i, s: (b, i, k)),
        scratch_shapes=[pltpu.VMEM((S // LANES, _VROWS_B, LANES), BF16)],
    )
    return pl.pallas_call(
        _attn_b_kernel,
        grid_spec=grid_spec,
        out_shape=jax.ShapeDtypeStruct((B, S, WIDTH_B), BF16),
        compiler_params=pltpu.CompilerParams(
            dimension_semantics=("arbitrary", "arbitrary", "arbitrary")),
        name="attn_b",
    )(sinks, sl("qb"), sl("kb2"), sl("vb2"), sl("zb"))


OUT_TM = 512


def _out_proj_kernel(tiles_per_half, ylo_ref, yhi_ref, yb_ref, wa_ref, wb_ref, x_ref, g_ref, o_ref):
    in_lo = (pl.program_id(0) % (2 * tiles_per_half)) < tiles_per_half
    ya = jnp.where(in_lo, ylo_ref[...], yhi_ref[...])
    r = x_ref[...] + _dot(ya, wa_ref[...]) + _dot(yb_ref[...], wb_ref[...])
    ms = jnp.mean(r * r, axis=-1, keepdims=True)
    o_ref[...] = (r * lax.rsqrt(ms + RMS_EPS)) * g_ref[...]


def _out_proj(ya_lo, ya_hi, yb2, w_out, x2, g, S):
    M = x2.shape[0]
    tph = S // 2 // OUT_TM
    half = lambda i: (i // (2 * tph)) * tph
    t = lambda i: i % (2 * tph)
    return pl.pallas_call(
        functools.partial(_out_proj_kernel, tph),
        grid=(M // OUT_TM,),
        in_specs=[
            pl.BlockSpec((OUT_TM, WIDTH_A), lambda i: (half(i) + jnp.minimum(t(i), tph - 1), 0)),
            pl.BlockSpec((OUT_TM, WIDTH_A), lambda i: (half(i) + jnp.maximum(t(i) - tph, 0), 0)),
            pl.BlockSpec((OUT_TM, WIDTH_B), lambda i: (i, 0)),
            pl.BlockSpec((WIDTH_A, D_MODEL), lambda i: (0, 0)),
            pl.BlockSpec((WIDTH_B, D_MODEL), lambda i: (WIDTH_A // WIDTH_B, 0)),
            pl.BlockSpec((OUT_TM, D_MODEL), lambda i: (i, 0)),
            pl.BlockSpec((1, D_MODEL), lambda i: (0, 0)),
        ],
        out_specs=pl.BlockSpec((OUT_TM, D_MODEL), lambda i: (i, 0)),
        out_shape=jax.ShapeDtypeStruct((M, D_MODEL), F32),
        compiler_params=pltpu.CompilerParams(
            dimension_semantics=("arbitrary",), vmem_limit_bytes=56 * 1024 * 1024),
        name="out_proj",
    )(ya_lo, ya_hi, yb2, w_out, w_out, x2, g)


def _overlap_matrix(n_cp):
    c_start = np.arange(n_cp) * CMP_STRIDE
    j_start = np.arange(N_SELP) * SEL_BLOCK
    ov = (c_start[None, :] < j_start[:, None] + SEL_BLOCK) & (c_start[None, :] + CMP_BLOCK > j_start[:, None])
    return jnp.asarray(ov, BF16)


def kernel(x, w_in, cmp_k_w1, cmp_k_w2, cmp_v_w1, cmp_v_w2, cmp_k_pos, cmp_v_pos, sinks, w_out, norm_g, final_g):
    B, S, D = x.shape
    assert D == D_MODEL and w_in.shape[0] == 1
    x2 = x.reshape(B * S, D)

    w_all = _prep_w_in(w_in)
    tabs = _rope_tables(S)
    slabs, strided = _in_proj(x2, norm_g[0].reshape(1, D), w_all, tabs, S)
    slabs = [p.reshape(B, S, IN_TN) for p in slabs]

    n_cp = S // CMP_STRIDE
    tk, tv = [t.reshape(N_KV_A * B * n_cp, _HALF) for t in strided]
    flat_pos = lambda p: jnp.broadcast_to(p.reshape(1, CMP_BLOCK * HEAD_DIM_A), (8, CMP_BLOCK * HEAD_DIM_A))
    kc, vc = _compress(tk, tv, cmp_k_w1[0].astype(BF16), cmp_v_w1[0].astype(BF16),
                       cmp_k_w2[0].astype(BF16), cmp_v_w2[0].astype(BF16),
                       flat_pos(cmp_k_pos[0]).astype(BF16), flat_pos(cmp_v_pos[0]).astype(BF16))
    kc, vc = [c.reshape(N_KV_A * B, n_cp, HEAD_DIM_A) for c in (kc, vc)]

    ya_lo, ya_hi = _attn_a(slabs, kc, vc, _overlap_matrix(n_cp))
    y_b = _attn_b(sinks[0], slabs)

    out = _out_proj(ya_lo.reshape(B * S // 2, WIDTH_A), ya_hi.reshape(B * S // 2, WIDTH_A),
                    y_b.reshape(B * S, WIDTH_B), w_out[0].astype(BF16), x2, final_g.reshape(1, D), S)
    return out.reshape(B, S, D)
```

```python
import functools
import math

import numpy as np
import jax
import jax.numpy as jnp
from jax import lax
from jax.experimental import pallas as pl
from jax.experimental.pallas import tpu as pltpu

F32 = jnp.float32
BF16 = jnp.bfloat16

D_MODEL = 2048
ROPE_THETA = 10000.0
RMS_EPS = 1e-6
NEG_INF = -1e30
TINY = 1e-30
LOG2E = math.log2(math.e)

WIDTH_A = 1024
HEAD_DIM_A = 128
N_KV_A = 2
GROUP_A = 4
KV_A = N_KV_A * HEAD_DIM_A
CMP_BLOCK = 32
CMP_STRIDE = 16
CMP_HIDDEN = 256
SEL_BLOCK = 64
SEL_TOPK = 16
WIN_A = 512
FORCE_SCORE = 1e4

WIDTH_B = 1024
HEAD_DIM_B = 64
N_HEADS_B = 16
N_KV_B = 2
GROUP_B = 8
KV_B = N_KV_B * HEAD_DIM_B
WIN_B = 128

IN_SIZES = (WIDTH_A, KV_A, KV_A, KV_A, KV_A, KV_A, KV_A, WIDTH_A, 3 * 8,
            WIDTH_B, KV_B, KV_B, WIDTH_B)

LANES = 128
SUBLANES = 8
CHUNK = 256

EP_NONE, EP_ROPE128, EP_ROPE64, EP_SILU, EP_SIGMOID, EP_ROPE128_Q, EP_ROPE64_Q = range(7)

_PROJ_LAYOUT = (
    ("qa", 4, EP_ROPE128_Q), ("kca", 1, EP_ROPE128),
    ("ksa", 1, EP_ROPE128), ("kwa", 1, EP_ROPE128), ("vca", 1, EP_NONE), ("vsa", 1, EP_NONE), ("vwa", 1, EP_NONE),
    ("za", 4, EP_SILU), ("ga", 1, EP_SIGMOID),
    ("zb", 4, EP_SILU), ("kb2", 1, EP_ROPE64),
    ("qb", 4, EP_ROPE64_Q), ("vb2", 1, EP_NONE),
)
_CHUNK_KINDS = tuple(k for _, n, k in _PROJ_LAYOUT for _ in range(n))
_CHUNK_START = {}
_c = 0
for _name, _n, _k in _PROJ_LAYOUT:
    _CHUNK_START[_name] = _c
    _c += _n
N_CHUNKS = _c
D_PROJ = N_CHUNKS * CHUNK
IN_TN = 1280
_IN_CPT = IN_TN // CHUNK
N_SLABS = D_PROJ // IN_TN


def _slab(name):
    return _CHUNK_START[name] // _IN_CPT


def _col(name):
    return (_CHUNK_START[name] % _IN_CPT) * CHUNK


def _dot(a, b):
    return jnp.dot(a, b, preferred_element_type=F32)


def _dot_nt(a, b):
    return lax.dot_general(a, b, (((1,), (1,)), ((), ())), preferred_element_type=F32)


IN_TM_NORM = 512
IN_TM = 1024
IN_SUB = 512

_TAB_GROUP = {EP_ROPE128: "r128", EP_ROPE128_Q: "r128q", EP_ROPE64: "r64", EP_ROPE64_Q: "r64q"}
_TAB_SIZE = {"r128": 2, "r128q": 2, "r64": 3, "r64q": 3}


def _slab_tab_groups(kinds):
    return tuple(dict.fromkeys(_TAB_GROUP[k] for k in kinds if k in _TAB_GROUP))


def _in_proj_kernel(kinds, with_norm, seq_len, strided_chunk, *refs):
    groups = _slab_tab_groups(kinds)
    refs = list(refs)
    if with_norm:
        x_ref, g_ref = refs.pop(0), refs.pop(0)
    else:
        h_ref = refs.pop(0)
    w_ref = refs.pop(0)
    tab_ref = refs.pop(0) if groups else None
    o_ref = refs.pop(0)
    if with_norm:
        h_ref = refs.pop(0)
        x = x_ref[...]
        ms = jnp.mean(x * x, axis=-1, keepdims=True)
        h_ref[...] = ((x * lax.rsqrt(ms + RMS_EPS)) * g_ref[...]).astype(BF16)
    if strided_chunk is not None:
        t_ref, stage_ref = refs.pop(0), refs.pop(0)
    tm = h_ref.shape[0]
    pos0 = (pl.program_id(0) % (seq_len // tm)) * tm
    base, n = {}, 0
    for grp in groups:
        base[grp] = n
        n += _TAB_SIZE[grp]

    def epilogue(kind, r, a):
        if kind == EP_NONE:
            return a
        if kind == EP_SILU:
            return a * jax.nn.sigmoid(a)
        if kind == EP_SIGMOID:
            return jax.nn.sigmoid(a)
        t = base[_TAB_GROUP[kind]]
        tab = lambda k: tab_ref[t + k, pl.ds(pl.multiple_of(pos0 + r * IN_SUB, IN_SUB), IN_SUB), :]
        if kind in (EP_ROPE128, EP_ROPE128_Q):
            return a * tab(0) + pltpu.roll(a, 64, 1) * tab(1)
        return a * tab(0) + pltpu.roll(a, 96, 1) * tab(1) + pltpu.roll(a, 32, 1) * tab(2)

    n_grp = IN_SUB // CMP_STRIDE

    def finish(kind, r, c, acc):
        rows = slice(r * IN_SUB, (r + 1) * IN_SUB)
        for sub in range(CHUNK // LANES):
            cols = slice(c * CHUNK + sub * LANES, c * CHUNK + (sub + 1) * LANES)
            val = epilogue(kind, r, acc[:, sub * LANES:(sub + 1) * LANES])
            o_ref[rows, cols] = val.astype(o_ref.dtype)
            if c == strided_chunk:
                stage_ref[sub] = val
                for k in range(CMP_STRIDE):
                    t_ref[sub, r * n_grp:(r + 1) * n_grp, k * LANES:(k + 1) * LANES] = (
                        stage_ref[sub, pl.ds(k, n_grp, stride=CMP_STRIDE), :].astype(t_ref.dtype))

    jobs = []
    for r in range(tm // IN_SUB):
        for c, kind in enumerate(kinds):
            jobs.append((lambda r=r, c=c: _dot_nt(h_ref[r * IN_SUB:(r + 1) * IN_SUB, :],
                                                  w_ref[c * CHUNK:(c + 1) * CHUNK, :]),
                         functools.partial(finish, kind, r, c)))
    _run_skewed(jobs)


def _rope_tables(S):
    def cs(d):
        inv = ROPE_THETA ** (-jnp.arange(0, d, 2, dtype=F32) / d)
        ang = jnp.arange(S, dtype=F32)[:, None] * inv[None, :]
        return jnp.cos(ang), jnp.sin(ang)

    c128, s128 = cs(HEAD_DIM_A)
    c64, s64 = cs(HEAD_DIM_B)
    z32 = jnp.zeros_like(s64)
    t128 = [jnp.concatenate([c128, c128], axis=1), jnp.concatenate([-s128, s128], axis=1)]
    t64 = [jnp.concatenate([c64, c64, c64, c64], axis=1),
           jnp.concatenate([-s64, z32, -s64, z32], axis=1),
           jnp.concatenate([z32, s64, z32, s64], axis=1)]
    qa = HEAD_DIM_A ** -0.5 * LOG2E
    qb = HEAD_DIM_B ** -0.5 * LOG2E
    return {"r128": t128, "r64": t64, "r128q": [t * qa for t in t128], "r64q": [t * qb for t in t64]}


W_PREP_TR = 256


def _w_in_pieces():
    offs = np.cumsum((0,) + IN_SIZES)
    names = ("qa", "kca", "vca", "ksa", "vsa", "kwa", "vwa", "za", "ga", "qb", "kb", "vb", "zb")
    start = {n: int(offs[i]) for i, n in enumerate(names)}
    width = dict(zip(names, IN_SIZES))
    n_g = 3 * GROUP_A
    halves = lambda n: [(start[n] + i * HEAD_DIM_B, HEAD_DIM_B) for i in (0, 0, 1, 1)]
    pieces = {n: [(start[n], width[n])] for n in ("qa", "kca", "ksa", "kwa", "vca", "vsa", "vwa", "za", "zb", "qb")}
    pieces["ga"] = [(start["ga"], n_g), (None, LANES - n_g), (start["ga"] + n_g, n_g), (None, LANES - n_g)]
    pieces["kb2"] = halves("kb")
    pieces["vb2"] = halves("vb")
    return [p for name, _, _ in _PROJ_LAYOUT for p in pieces[name]]


def _w_prep_kernel(w_ref, o_ref):
    row = 0
    parts, filled = [], 0
    for src, wd in _w_in_pieces():
        done = 0
        while done < wd:
            take = min(wd - done, LANES - filled)
            if src is None:
                parts.append(jnp.zeros((take, W_PREP_TR), F32))
            else:
                parts.append(w_ref[src + done:src + done + take, :])
            done += take
            filled += take
            if filled == LANES:
                blk = parts[0] if len(parts) == 1 else jnp.concatenate(parts, axis=0)
                o_ref[row:row + LANES, :] = blk.astype(o_ref.dtype)
                row += LANES
                parts, filled = [], 0
    assert row == D_PROJ and not parts


def _prep_w_in(w_in):
    wt = w_in[0].T
    d_in = wt.shape[0]
    return pl.pallas_call(
        _w_prep_kernel,
        grid=(D_MODEL // W_PREP_TR,),
        in_specs=[pl.BlockSpec((d_in, W_PREP_TR), lambda i: (0, i))],
        out_specs=pl.BlockSpec((D_PROJ, W_PREP_TR), lambda i: (0, i)),
        out_shape=jax.ShapeDtypeStruct((D_PROJ, D_MODEL), BF16),
        compiler_params=pltpu.CompilerParams(dimension_semantics=("arbitrary",)),
        name="w_prep",
    )(wt)


def _in_proj(x2, g, w_all, tabs, S):
    M = x2.shape[0]
    params = pltpu.CompilerParams(dimension_semantics=("arbitrary",), vmem_limit_bytes=48 * 1024 * 1024)
    out_sds = jax.ShapeDtypeStruct((M, IN_TN), BF16)
    slabs, strided, h = [], [], None
    for j in range(N_SLABS):
        kinds = _CHUNK_KINDS[j * _IN_CPT:(j + 1) * _IN_CPT]
        with_norm = j == 0
        tm = IN_TM_NORM if with_norm else IN_TM
        groups = _slab_tab_groups(kinds)
        act_spec = pl.BlockSpec((tm, D_MODEL), lambda i: (i, 0))
        in_specs = [act_spec, pl.BlockSpec((1, D_MODEL), lambda i: (0, 0))] if with_norm else [act_spec]
        args = [x2, g] if with_norm else [h]
        in_specs.append(pl.BlockSpec((IN_TN, D_MODEL), lambda i, j=j: (j, 0)))
        args.append(w_all)
        if groups:
            tab = jnp.stack([t for grp in groups for t in tabs[grp]])
            in_specs.append(pl.BlockSpec(tab.shape, lambda i: (0, 0, 0)))
            args.append(tab)
        out_specs = [pl.BlockSpec((tm, IN_TN), lambda i: (i, 0))]
        out_shape = [out_sds]
        if with_norm:
            out_specs.append(act_spec)
            out_shape.append(jax.ShapeDtypeStruct((M, D_MODEL), BF16))
        strided_chunk = None
        for name in ("kca", "vca"):
            if _slab(name) == j:
                strided_chunk = _col(name) // CHUNK
        scratch = []
        if strided_chunk is not None:
            out_specs.append(pl.BlockSpec((N_KV_A, tm // CMP_STRIDE, _HALF), lambda i: (0, i, 0)))
            out_shape.append(jax.ShapeDtypeStruct((N_KV_A, M // CMP_STRIDE, _HALF), BF16))
            scratch.append(pltpu.VMEM((N_KV_A, IN_SUB, HEAD_DIM_A), F32))
        res = list(pl.pallas_call(
            functools.partial(_in_proj_kernel, kinds, with_norm, S, strided_chunk),
            grid=(M // tm,),
            in_specs=in_specs,
            out_specs=out_specs,
            out_shape=out_shape,
            scratch_shapes=scratch,
            compiler_params=params,
            name=f"in_proj_{j}",
        )(*args))
        slabs.append(res.pop(0))
        if with_norm:
            h = res.pop(0)
        if strided_chunk is not None:
            strided.append(res.pop(0))
    return slabs, strided


CMP_TM = 256
_HALF = CMP_STRIDE * HEAD_DIM_A


def _compress_kernel(tk_ref, tv_ref, w1k_ref, w1v_ref, w2k_ref, w2v_ref, posk_ref, posv_ref, kc_ref, vc_ref):
    for t_ref, w1_ref, w2_ref, pos_ref, o_ref in ((tk_ref, w1k_ref, w2k_ref, posk_ref, kc_ref),
                                                  (tv_ref, w1v_ref, w2v_ref, posv_ref, vc_ref)):
        t = t_ref[...]
        u = _dot(t, w1_ref[0:_HALF, :])
        v = _dot(t, w1_ref[_HALF:2 * _HALF, :])
        pb = _dot(pos_ref[...], w1_ref[...])
        pre = u + pltpu.roll(v, CMP_TM - 1, 0) + pb[0:1]
        hid = pre * jax.nn.sigmoid(pre)
        out = _dot(hid.astype(BF16), w2_ref[...])
        row = lax.broadcasted_iota(jnp.int32, out.shape, 0)
        n_cp = LANES
        out = jnp.where((row & (n_cp - 1)) == n_cp - 1, 0.0, out)
        o_ref[...] = out.astype(o_ref.dtype)


def _compress(tk, tv, w1k, w1v, w2k, w2v, posk, posv):
    R = tk.shape[0]
    rows = pl.BlockSpec((CMP_TM, _HALF), lambda i: (i, 0))
    whole = lambda a: pl.BlockSpec(a.shape, lambda i: (0, 0))
    o_spec = pl.BlockSpec((CMP_TM, HEAD_DIM_A), lambda i: (i, 0))
    o_sds = jax.ShapeDtypeStruct((R, HEAD_DIM_A), BF16)
    return pl.pallas_call(
        _compress_kernel,
        grid=(R // CMP_TM,),
        in_specs=[rows, rows, whole(w1k), whole(w1v), whole(w2k), whole(w2v), whole(posk), whole(posv)],
        out_specs=(o_spec, o_spec),
        out_shape=(o_sds, o_sds),
        compiler_params=pltpu.CompilerParams(dimension_semantics=("arbitrary",)),
        name="compress",
    )(tk, tv, w1k, w1v, w2k, w2v, posk, posv)


TQ = 128
ATTN_AHEAD = 4
ONES_ROWS = 16


def _transpose_values(v_ref, vt_ref, n_rows):
    n_blk = vt_ref.shape[0]
    for blk in range(n_blk):
        vt = v_ref[0, blk * LANES:(blk + 1) * LANES, :].astype(F32).T
        vt_ref[blk, 0:n_rows, :] = vt[0:n_rows].astype(vt_ref.dtype)
        vt_ref[blk, n_rows:n_rows + ONES_ROWS, :] = jnp.ones((ONES_ROWS, LANES), vt_ref.dtype)


def _run_skewed(jobs, ahead=1):
    pending = [job[0]() for job in jobs[:ahead]]
    for i, (_, finish) in enumerate(jobs):
        if i + ahead < len(jobs):
            pending.append(jobs[i + ahead][0]())
        finish(pending[i])
        pending[i] = None


def _softmax_step(s, m, acc, vt):
    m_new = jnp.maximum(m, jnp.max(s, axis=0, keepdims=True))
    p = jnp.exp2(s - m_new).astype(BF16)
    return m_new, jnp.exp2(m - m_new) * acc + _dot(vt, p)


SEL_TK = 256
N_SELP = 32
PAIR = 2 * TQ
_VROWS_A = HEAD_DIM_A + ONES_ROWS
_WIN_TILES = (256, 256, 128)
_N_SEL_JOBS = 9


def _attn_a_kernel(qlo_ref, qhi_ref, ksel_ref, vsel_ref, kwin_ref, vwin_ref, kc_ref, vc_ref,
                   glo_ref, ghi_ref, zlo_ref, zhi_ref, ov_ref, wtab_ref, ctab_ref, ylo_ref, yhi_ref,
                   vselt_ref, vwint_ref, qs_ref, bias_ref, acc_ref, out_ref):
    i = pl.program_id(2)
    n_qt = 2 * pl.num_programs(2)
    n_pairs = GROUP_A // 2
    t0s = (i * TQ, (n_qt - 1 - i) * TQ)
    q_refs, g_refs, z_refs, y_refs = (qlo_ref, qhi_ref), (glo_ref, ghi_ref), (zlo_ref, zhi_ref), (ylo_ref, yhi_ref)

    @pl.when(i == 0)
    def _():
        _transpose_values(vsel_ref, vselt_ref, HEAD_DIM_A)
        _transpose_values(vwin_ref, vwint_ref, HEAD_DIM_A)

    lane = lax.broadcasted_iota(jnp.int32, (1, TQ), 1)
    lane2 = jnp.concatenate([lane, lane], axis=1)
    qpair, gts = [], []
    for t in range(2):
        q = q_refs[t][0]
        pairs = [jnp.concatenate([q[:, h * LANES:(h + 1) * LANES] for h in (2 * p, 2 * p + 1)], axis=0)
                 for p in range(n_pairs)]
        for p in range(n_pairs):
            qs_ref[t, p] = pairs[p]
        qpair.append(pairs)
        gts.append(g_refs[t][0].astype(F32).T)

    def gate(t, p, branch):
        return jnp.concatenate(
            [gts[t][3 * h + branch:3 * h + branch + 1] for h in (2 * p, 2 * p + 1)], axis=1)

    kc = kc_ref[0]
    vct = vc_ref[0].astype(F32).T.astype(BF16)
    cend = lax.broadcasted_iota(jnp.int32, (LANES, PAIR), 0) * CMP_STRIDE + (CMP_BLOCK - 1)
    psums = [[], []]

    def cmp_scores(t, p):
        return jnp.where(cend <= t0s[t] + lane2, _dot_nt(kc, qpair[t][p]), NEG_INF)

    def cmp_finish(t, p, s):
        m = jnp.max(s, axis=0, keepdims=True)
        e = jnp.where(cend <= t0s[t] + lane2, jnp.exp2(s - m), 0.0)
        pc = e / jnp.maximum(jnp.sum(e, axis=0, keepdims=True), TINY)
        psums[t].append(pc[:, 0:TQ] + pc[:, TQ:PAIR])
        out_ref[t, p] = gate(t, p, 0) * _dot(vct, pc.astype(BF16))

    ksts = [pl.multiple_of(jnp.maximum(t0 - WIN_A, 0), TQ) for t0 in t0s]
    win_m = [[jnp.full((1, PAIR), NEG_INF, F32) for _ in range(n_pairs)] for _ in range(2)]
    win_acc = [[jnp.zeros((_VROWS_A, PAIR), F32) for _ in range(n_pairs)] for _ in range(2)]

    win_var = [jnp.minimum(t0 // TQ, WIN_A // TQ) for t0 in t0s]

    def win_scores(t, p, off, tk):
        kt = kwin_ref[0, pl.ds(pl.multiple_of(ksts[t] + off, LANES), tk), :]
        bias = wtab_ref[win_var[t], off:off + tk, :]
        return _dot_nt(kt, qpair[t][p]) + jnp.concatenate([bias, bias], axis=1)

    def win_finish(t, p, off, tk, s):
        vt = jnp.concatenate(
            [vwint_ref[ksts[t] // LANES + off // LANES + b] for b in range(tk // LANES)], axis=1)
        win_m[t][p], win_acc[t][p] = _softmax_step(s, win_m[t][p], win_acc[t][p], vt)

    jobs = [(functools.partial(cmp_scores, t, p), functools.partial(cmp_finish, t, p))
            for t in range(2) for p in range(n_pairs)]
    off = 0
    for tk in _WIN_TILES:
        jobs += [(functools.partial(win_scores, t, p, off, tk), functools.partial(win_finish, t, p, off, tk))
                 for t in range(2) for p in range(n_pairs)]
        off += tk
    _run_skewed(jobs, ATTN_AHEAD)
    for t in range(2):
        for p in range(n_pairs):
            a = win_acc[t][p]
            out_ref[t, p] = out_ref[t, p] + (
                gate(t, p, 2) / jnp.maximum(a[HEAD_DIM_A:HEAD_DIM_A + 1], TINY)) * a[0:HEAD_DIM_A]

    ov = ov_ref[...]
    jidx = lax.broadcasted_iota(jnp.int32, (N_SELP, TQ), 0)
    for t in range(2):
        psum = psums[t][0] + psums[t][1]
        p_hi = psum.astype(BF16)
        r1 = psum - p_hi.astype(F32)
        p_mid = r1.astype(BF16)
        p_lo = (r1 - p_mid.astype(F32)).astype(BF16)
        psel = _dot(ov, p_hi) + _dot(ov, p_mid) + _dot(ov, p_lo)
        tq = t0s[t] + lane
        cur = tq // SEL_BLOCK
        forced = (jidx == 0) | (jidx == cur) | (jidx == cur - 1)
        score = jnp.where(forced, FORCE_SCORE, jnp.where(jidx * SEL_BLOCK <= tq, psel, -1.0))
        rank = jnp.zeros(psel.shape, F32)
        for r in range(N_SELP):
            row = jnp.broadcast_to(score[r:r + 1, :], score.shape)
            rank = rank + jnp.where(jidx > r, jnp.where(row >= score, 1.0, 0.0), jnp.where(row > score, 1.0, 0.0))
        bias_t = jnp.where(rank < SEL_TOPK, 0.0, NEG_INF)
        for j in range(N_SELP):
            bias_ref[t, j] = jnp.broadcast_to(bias_t[j:j + 1, :], (SUBLANES, TQ))

    blk_per_tile = SEL_TK // SEL_BLOCK
    rep = SEL_BLOCK // SUBLANES
    n_lo = i // (SEL_TK // TQ) + 1
    acc_ref[...] = jnp.zeros(acc_ref.shape, F32)
    ms = [[jnp.full((1, PAIR), NEG_INF, F32) for _ in range(n_pairs)] for _ in range(2)]
    ctx = {}

    def sel_scores(k, p):
        if p == 0:
            hi = k >= n_lo
            t = hi.astype(jnp.int32)
            j = jnp.where(hi, k - n_lo, k)
            ks = pl.multiple_of(j * SEL_TK, SEL_TK)
            bt = jnp.concatenate(
                [bias_ref[t, j * blk_per_tile + b] for b in range(blk_per_tile) for _ in range(rep)], axis=0)
            qt = jnp.where(hi, n_qt - 1 - i, i)
            diag = j == qt // (SEL_TK // TQ)
            bt = bt + ctab_ref[jnp.where(diag, 1 + qt % (SEL_TK // TQ), 0)]
            bt = jnp.concatenate([bt, bt], axis=1)
            ctx[k] = (hi, t, j, ks, bt)
        hi, t, j, ks, bt = ctx[k]
        return _dot_nt(ksel_ref[0, pl.ds(ks, SEL_TK), :], qs_ref[t, p]) + bt

    def sel_finish(k, p, s):
        hi, t, j, ks, bt = ctx[k]
        vt = jnp.concatenate([vselt_ref[j * (SEL_TK // LANES) + b] for b in range(SEL_TK // LANES)], axis=1)
        m_new, acc_new = _softmax_step(s, jnp.where(hi, ms[1][p], ms[0][p]), acc_ref[t, p], vt)
        acc_ref[t, p] = acc_new
        ms[0][p] = jnp.where(hi, ms[0][p], m_new)
        ms[1][p] = jnp.where(hi, m_new, ms[1][p])

    sel_jobs = [(functools.partial(sel_scores, k, p), functools.partial(sel_finish, k, p))
                for k in range(_N_SEL_JOBS) for p in range(n_pairs)]
    _run_skewed(sel_jobs, ATTN_AHEAD)

    for t in range(2):
        z = z_refs[t][0]
        for p in range(n_pairs):
            a = acc_ref[t, p]
            o = out_ref[t, p] + (gate(t, p, 1) / jnp.maximum(a[HEAD_DIM_A:HEAD_DIM_A + 1], TINY)) * a[0:HEAD_DIM_A]
            for h in (2 * p, 2 * p + 1):
                oh = o[:, (h % 2) * TQ:(h % 2 + 1) * TQ].T
                y_refs[t][0, :, h * LANES:(h + 1) * LANES] = (
                    oh * z[:, h * LANES:(h + 1) * LANES].astype(F32)).astype(y_refs[t].dtype)


def _mask_tables():
    lane = np.arange(TQ)[None, :]
    row = np.arange(WIN_A + TQ)[:, None]
    wtab = []
    for v in range(WIN_A // TQ + 1):
        t0 = v * TQ
        dpos = (t0 + lane) - (max(t0 - WIN_A, 0) + row)
        wtab.append(np.where((dpos >= 0) & (dpos < WIN_A), 0.0, NEG_INF))
    row = np.arange(SEL_TK)[:, None]
    ctab = [np.zeros((SEL_TK, TQ))]
    for odd in range(SEL_TK // TQ):
        ctab.append(np.where(row <= odd * TQ + lane, 0.0, NEG_INF))
    return jnp.asarray(np.stack(wtab), F32), jnp.asarray(np.stack(ctab), F32)


def _attn_a(slabs, kc, vc, ov):
    B, S, _ = slabs[0].shape
    sl = lambda name: slabs[_slab(name)]
    n_cp = S // CMP_STRIDE
    n_qt = S // TQ
    assert n_cp == LANES and S // SEL_BLOCK == N_SELP and (n_qt // 2) // (SEL_TK // TQ) * 2 + 1 == _N_SEL_JOBS
    wq = GROUP_A * HEAD_DIM_A
    cb = lambda name: _col(name) // LANES
    kv_spec = lambda name: pl.BlockSpec((1, S, LANES), lambda b, k, i, c=cb(name): (b, 0, c + k))
    lo = lambda width, c: pl.BlockSpec((1, TQ, width), lambda b, k, i: (b, i, c + k))
    hi = lambda width, c: pl.BlockSpec((1, TQ, width), lambda b, k, i: (b, n_qt - 1 - i, c + k))
    y_sds = jax.ShapeDtypeStruct((B, S // 2, WIDTH_A), BF16)
    wtab, ctab = _mask_tables()
    return pl.pallas_call(
        _attn_a_kernel,
        grid=(B, N_KV_A, n_qt // 2),
        in_specs=[
            lo(wq, _col("qa") // wq), hi(wq, _col("qa") // wq),
            kv_spec("ksa"), kv_spec("vsa"), kv_spec("kwa"), kv_spec("vwa"),
            pl.BlockSpec((1, n_cp, LANES), lambda b, k, i: (k * B + b, 0, 0)),
            pl.BlockSpec((1, n_cp, LANES), lambda b, k, i: (k * B + b, 0, 0)),
            lo(LANES, cb("ga")), hi(LANES, cb("ga")),
            lo(wq, _col("za") // wq), hi(wq, _col("za") // wq),
            pl.BlockSpec((N_SELP, LANES), lambda b, k, i: (0, 0)),
            pl.BlockSpec(wtab.shape, lambda b, k, i: (0, 0, 0)),
            pl.BlockSpec(ctab.shape, lambda b, k, i: (0, 0, 0)),
        ],
        out_specs=(pl.BlockSpec((1, TQ, wq), lambda b, k, i: (b, i, k)),
                   pl.BlockSpec((1, TQ, wq), lambda b, k, i: (b, n_qt // 2 - 1 - i, k))),
        out_shape=(y_sds, y_sds),
        scratch_shapes=[
            pltpu.VMEM((S // LANES, _VROWS_A, LANES), BF16),
            pltpu.VMEM((S // LANES, _VROWS_A, LANES), BF16),
            pltpu.VMEM((2, GROUP_A // 2, PAIR, HEAD_DIM_A), BF16),
            pltpu.VMEM((2, N_SELP, SUBLANES, TQ), F32),
            pltpu.VMEM((2, GROUP_A // 2, _VROWS_A, PAIR), F32),
            pltpu.VMEM((2, GROUP_A // 2, HEAD_DIM_A, PAIR), F32),
        ],
        compiler_params=pltpu.CompilerParams(
            dimension_semantics=("arbitrary", "arbitrary", "arbitrary"), vmem_limit_bytes=48 * 1024 * 1024),
        name="attn_a",
    )(sl("qa"), sl("qa"), sl("ksa"), sl("vsa"), sl("kwa"), sl("vwa"), kc, vc,
      sl("ga"), sl("ga"), sl("za"), sl("za"), ov, wtab, ctab)


_VROWS_B = HEAD_DIM_B + ONES_ROWS


B_QT = 4


def _attn_b_kernel(sink_ref, q_ref, k_ref, v_ref, z_ref, y_ref, vt_ref):
    kv = pl.program_id(1)
    qi = pl.program_id(2)
    n_keys = WIN_B + TQ

    @pl.when(qi == 0)
    def _():
        _transpose_values(v_ref, vt_ref, HEAD_DIM_B)

    left = lax.broadcasted_iota(jnp.int32, (TQ, LANES), 1) < HEAD_DIM_B
    zero = jnp.zeros((TQ, LANES), q_ref.dtype)
    lane = lax.broadcasted_iota(jnp.int32, (1, TQ), 1)
    lane2 = jnp.concatenate([lane, lane], axis=1)
    krow = lax.broadcasted_iota(jnp.int32, (n_keys, PAIR), 0)
    sinks = [jnp.concatenate(
        [jnp.full((1, TQ), sink_ref[kv * GROUP_B + 2 * c + i] * LOG2E, F32) for i in range(2)], axis=1)
        for c in range(GROUP_B // 2)]
    ctx = {}

    def tile_ctx(u):
        if u not in ctx:
            t0 = (qi * B_QT + u) * TQ
            kst = pl.multiple_of(jnp.maximum(t0 - WIN_B, 0), TQ)
            dpos = (t0 + lane2) - (kst + krow)
            bias = jnp.where((dpos >= 0) & (dpos < WIN_B), 0.0, NEG_INF)
            ctx[u] = (kst, bias)
        return ctx[u]

    def scores(u, c):
        kst, bias = tile_ctx(u)
        pair = q_ref[0, u * TQ:(u + 1) * TQ, c * LANES:(c + 1) * LANES]
        qp = jnp.concatenate([jnp.where(left, pair, zero), jnp.where(left, zero, pair)], axis=0)
        kt = k_ref[0, pl.ds(kst, n_keys), :]
        return _dot_nt(kt, qp) + bias

    def finish(u, c, s):
        kst, _ = tile_ctx(u)
        vt = jnp.concatenate([vt_ref[kst // LANES + b] for b in range(n_keys // LANES)], axis=1)
        m = jnp.maximum(jnp.max(s, axis=0, keepdims=True), sinks[c])
        p = jnp.exp2(s - m).astype(BF16)
        a = _dot(vt, p)
        den = jnp.maximum(a[HEAD_DIM_B:HEAD_DIM_B + 1] + jnp.exp2(sinks[c] - m), TINY)
        o = a[0:HEAD_DIM_B] / den
        o2 = jnp.concatenate([o[:, 0:TQ], o[:, TQ:PAIR]], axis=0)
        z = z_ref[0, u * TQ:(u + 1) * TQ, c * LANES:(c + 1) * LANES]
        y_ref[0, u * TQ:(u + 1) * TQ, c * LANES:(c + 1) * LANES] = (o2.T * z.astype(F32)).astype(y_ref.dtype)

    _run_skewed([(functools.partial(scores, u, c), functools.partial(finish, u, c))
                 for u in range(B_QT) for c in range(GROUP_B // 2)], ATTN_AHEAD)


def _attn_b(sinks, slabs):
    B, S, _ = slabs[0].shape
    sl = lambda name: slabs[_slab(name)]
    wq = GROUP_B * HEAD_DIM_B
    cb = lambda name: _col(name) // LANES
    grid_spec = pltpu.PrefetchScalarGridSpec(
        num_scalar_prefetch=1,
        grid=(B, N_KV_B, S // (B_QT * TQ)),
        in_specs=[
            pl.BlockSpec((1, B_QT * TQ, wq), lambda b, k, i, s: (b, i, _col("qb") // wq + k)),
            pl.BlockSpec((1, S, LANES), lambda b, k, i, s, c=cb("kb2"): (b, 0, c + k)),
            pl.BlockSpec((1, S, LANES), lambda b, k, i, s, c=cb("vb2"): (b, 0, c + k)),
            pl.BlockSpec((1, B_QT * TQ, wq), lambda b, k, i, s: (b, i, _col("zb") // wq + k)),
        ],
        out_specs=pl.BlockSpec((1, B_QT * TQ, wq), lambda b, k, i, s: (b, i, k)),
        scratch_shapes=[pltpu.VMEM((S // LANES, _VROWS_B, LANES), BF16)],
    )
    return pl.pallas_call(
        _attn_b_kernel,
        grid_spec=grid_spec,
        out_shape=jax.ShapeDtypeStruct((B, S, WIDTH_B), BF16),
        compiler_params=pltpu.CompilerParams(
            dimension_semantics=("arbitrary", "arbitrary", "arbitrary")),
        name="attn_b",
    )(sinks, sl("qb"), sl("kb2"), sl("vb2"), sl("zb"))


OUT_TM = 512


def _out_proj_kernel(tiles_per_half, ylo_ref, yhi_ref, yb_ref, wa_ref, wb_ref, x_ref, g_ref, o_ref):
    in_lo = (pl.program_id(0) % (2 * tiles_per_half)) < tiles_per_half
    ya = jnp.where(in_lo, ylo_ref[...], yhi_ref[...])
    r = x_ref[...] + _dot(ya, wa_ref[...]) + _dot(yb_ref[...], wb_ref[...])
    ms = jnp.mean(r * r, axis=-1, keepdims=True)
    o_ref[...] = (r * lax.rsqrt(ms + RMS_EPS)) * g_ref[...]


def _out_proj(ya_lo, ya_hi, yb2, w_out, x2, g, S):
    M = x2.shape[0]
    tph = S // 2 // OUT_TM
    half = lambda i: (i // (2 * tph)) * tph
    t = lambda i: i % (2 * tph)
    return pl.pallas_call(
        functools.partial(_out_proj_kernel, tph),
        grid=(M // OUT_TM,),
        in_specs=[
            pl.BlockSpec((OUT_TM, WIDTH_A), lambda i: (half(i) + jnp.minimum(t(i), tph - 1), 0)),
            pl.BlockSpec((OUT_TM, WIDTH_A), lambda i: (half(i) + jnp.maximum(t(i) - tph, 0), 0)),
            pl.BlockSpec((OUT_TM, WIDTH_B), lambda i: (i, 0)),
            pl.BlockSpec((WIDTH_A, D_MODEL), lambda i: (0, 0)),
            pl.BlockSpec((WIDTH_B, D_MODEL), lambda i: (WIDTH_A // WIDTH_B, 0)),
            pl.BlockSpec((OUT_TM, D_MODEL), lambda i: (i, 0)),
            pl.BlockSpec((1, D_MODEL), lambda i: (0, 0)),
        ],
        out_specs=pl.BlockSpec((OUT_TM, D_MODEL), lambda i: (i, 0)),
        out_shape=jax.ShapeDtypeStruct((M, D_MODEL), F32),
        compiler_params=pltpu.CompilerParams(
            dimension_semantics=("arbitrary",), vmem_limit_bytes=56 * 1024 * 1024),
        name="out_proj",
    )(ya_lo, ya_hi, yb2, w_out, w_out, x2, g)


def _overlap_matrix(n_cp):
    c_start = np.arange(n_cp) * CMP_STRIDE
    j_start = np.arange(N_SELP) * SEL_BLOCK
    ov = (c_start[None, :] < j_start[:, None] + SEL_BLOCK) & (c_start[None, :] + CMP_BLOCK > j_start[:, None])
    return jnp.asarray(ov, BF16)


def kernel(x, w_in, cmp_k_w1, cmp_k_w2, cmp_v_w1, cmp_v_w2, cmp_k_pos, cmp_v_pos, sinks, w_out, norm_g, final_g):
    B, S, D = x.shape
    assert D == D_MODEL and w_in.shape[0] == 1
    x2 = x.reshape(B * S, D)

    w_all = _prep_w_in(w_in)
    tabs = _rope_tables(S)
    slabs, strided = _in_proj(x2, norm_g[0].reshape(1, D), w_all, tabs, S)
    slabs = [p.reshape(B, S, IN_TN) for p in slabs]

    n_cp = S // CMP_STRIDE
    tk, tv = [t.reshape(N_KV_A * B * n_cp, _HALF) for t in strided]
    flat_pos = lambda p: jnp.broadcast_to(p.reshape(1, CMP_BLOCK * HEAD_DIM_A), (8, CMP_BLOCK * HEAD_DIM_A))
    kc, vc = _compress(tk, tv, cmp_k_w1[0].astype(BF16), cmp_v_w1[0].astype(BF16),
                       cmp_k_w2[0].astype(BF16), cmp_v_w2[0].astype(BF16),
                       flat_pos(cmp_k_pos[0]).astype(BF16), flat_pos(cmp_v_pos[0]).astype(BF16))
    kc, vc = [c.reshape(N_KV_A * B, n_cp, HEAD_DIM_A) for c in (kc, vc)]

    ya_lo, ya_hi = _attn_a(slabs, kc, vc, _overlap_matrix(n_cp))
    y_b = _attn_b(sinks[0], slabs)

    out = _out_proj(ya_lo.reshape(B * S // 2, WIDTH_A), ya_hi.reshape(B * S // 2, WIDTH_A),
                    y_b.reshape(B * S, WIDTH_B), w_out[0].astype(BF16), x2, final_g.reshape(1, D), S)
    return out.reshape(B, S, D)
```

```python
import functools
import math

import numpy as np
import jax
import jax.numpy as jnp
from jax import lax
from jax.experimental import pallas as pl
from jax.experimental.pallas import tpu as pltpu

F32 = jnp.float32
BF16 = jnp.bfloat16

D_MODEL = 2048
ROPE_THETA = 10000.0
RMS_EPS = 1e-6
NEG_INF = -1e30
TINY = 1e-30
LOG2E = math.log2(math.e)

WIDTH_A = 1024
HEAD_DIM_A = 128
N_KV_A = 2
GROUP_A = 4
KV_A = N_KV_A * HEAD_DIM_A
CMP_BLOCK = 32
CMP_STRIDE = 16
CMP_HIDDEN = 256
SEL_BLOCK = 64
SEL_TOPK = 16
WIN_A = 512
FORCE_SCORE = 1e4

WIDTH_B = 1024
HEAD_DIM_B = 64
N_HEADS_B = 16
N_KV_B = 2
GROUP_B = 8
KV_B = N_KV_B * HEAD_DIM_B
WIN_B = 128

IN_SIZES = (WIDTH_A, KV_A, KV_A, KV_A, KV_A, KV_A, KV_A, WIDTH_A, 3 * 8,
            WIDTH_B, KV_B, KV_B, WIDTH_B)

LANES = 128
SUBLANES = 8
CHUNK = 256

EP_NONE, EP_ROPE128, EP_ROPE64, EP_SILU, EP_SIGMOID, EP_ROPE128_Q, EP_ROPE64_Q = range(7)

_PROJ_LAYOUT = (
    ("qa", 4, EP_ROPE128_Q), ("kca", 1, EP_ROPE128),
    ("ksa", 1, EP_ROPE128), ("kwa", 1, EP_ROPE128), ("vca", 1, EP_NONE), ("vsa", 1, EP_NONE), ("vwa", 1, EP_NONE),
    ("za", 4, EP_SILU), ("ga", 1, EP_SIGMOID),
    ("zb", 4, EP_SILU), ("kb2", 1, EP_ROPE64),
    ("qb", 4, EP_ROPE64_Q), ("vb2", 1, EP_NONE),
)
_CHUNK_KINDS = tuple(k for _, n, k in _PROJ_LAYOUT for _ in range(n))
_CHUNK_START = {}
_c = 0
for _name, _n, _k in _PROJ_LAYOUT:
    _CHUNK_START[_name] = _c
    _c += _n
N_CHUNKS = _c
D_PROJ = N_CHUNKS * CHUNK
IN_TN = 1280
_IN_CPT = IN_TN // CHUNK
N_SLABS = D_PROJ // IN_TN


def _slab(name):
    return _CHUNK_START[name] // _IN_CPT


def _col(name):
    return (_CHUNK_START[name] % _IN_CPT) * CHUNK


def _dot(a, b):
    return jnp.dot(a, b, preferred_element_type=F32)


def _dot_nt(a, b):
    return lax.dot_general(a, b, (((1,), (1,)), ((), ())), preferred_element_type=F32)


IN_TM_NORM = 512
IN_TM = 2048
IN_SUB = 512

_TAB_GROUP = {EP_ROPE128: "r128", EP_ROPE128_Q: "r128q", EP_ROPE64: "r64", EP_ROPE64_Q: "r64q"}
_TAB_SIZE = {"r128": 2, "r128q": 2, "r64": 3, "r64q": 3}


def _slab_tab_groups(kinds):
    return tuple(dict.fromkeys(_TAB_GROUP[k] for k in kinds if k in _TAB_GROUP))


def _in_proj_kernel(kinds, with_norm, seq_len, strided_chunk, *refs):
    groups = _slab_tab_groups(kinds)
    refs = list(refs)
    if with_norm:
        x_ref, g_ref = refs.pop(0), refs.pop(0)
    else:
        h_ref = refs.pop(0)
    w_ref = refs.pop(0)
    tab_ref = refs.pop(0) if groups else None
    o_ref = refs.pop(0)
    if with_norm:
        h_ref = refs.pop(0)
        x = x_ref[...]
        ms = jnp.mean(x * x, axis=-1, keepdims=True)
        h_ref[...] = ((x * lax.rsqrt(ms + RMS_EPS)) * g_ref[...]).astype(BF16)
    if strided_chunk is not None:
        t_ref, stage_ref = refs.pop(0), refs.pop(0)
    tm = h_ref.shape[0]
    pos0 = (pl.program_id(0) % (seq_len // tm)) * tm
    base, n = {}, 0
    for grp in groups:
        base[grp] = n
        n += _TAB_SIZE[grp]

    def epilogue(kind, r, a):
        if kind == EP_NONE:
            return a
        if kind == EP_SILU:
            return a * jax.nn.sigmoid(a)
        if kind == EP_SIGMOID:
            return jax.nn.sigmoid(a)
        t = base[_TAB_GROUP[kind]]
        tab = lambda k: tab_ref[t + k, pl.ds(pl.multiple_of(pos0 + r * IN_SUB, IN_SUB), IN_SUB), :]
        if kind in (EP_ROPE128, EP_ROPE128_Q):
            return a * tab(0) + pltpu.roll(a, 64, 1) * tab(1)
        return a * tab(0) + pltpu.roll(a, 96, 1) * tab(1) + pltpu.roll(a, 32, 1) * tab(2)

    n_grp = IN_SUB // CMP_STRIDE

    def finish(kind, r, c, acc):
        rows = slice(r * IN_SUB, (r + 1) * IN_SUB)
        for sub in range(CHUNK // LANES):
            cols = slice(c * CHUNK + sub * LANES, c * CHUNK + (sub + 1) * LANES)
            val = epilogue(kind, r, acc[:, sub * LANES:(sub + 1) * LANES])
            o_ref[rows, cols] = val.astype(o_ref.dtype)
            if c == strided_chunk:
                stage_ref[sub] = val
                for k in range(CMP_STRIDE):
                    t_ref[sub, r * n_grp:(r + 1) * n_grp, k * LANES:(k + 1) * LANES] = (
                        stage_ref[sub, pl.ds(k, n_grp, stride=CMP_STRIDE), :].astype(t_ref.dtype))

    jobs = []
    for r in range(tm // IN_SUB):
        for c, kind in enumerate(kinds):
            jobs.append((lambda r=r, c=c: _dot_nt(h_ref[r * IN_SUB:(r + 1) * IN_SUB, :],
                                                  w_ref[c * CHUNK:(c + 1) * CHUNK, :]),
                         functools.partial(finish, kind, r, c)))
    _run_skewed(jobs)


def _rope_tables(S):
    def cs(d):
        inv = ROPE_THETA ** (-jnp.arange(0, d, 2, dtype=F32) / d)
        ang = jnp.arange(S, dtype=F32)[:, None] * inv[None, :]
        return jnp.cos(ang), jnp.sin(ang)

    c128, s128 = cs(HEAD_DIM_A)
    c64, s64 = cs(HEAD_DIM_B)
    z32 = jnp.zeros_like(s64)
    t128 = [jnp.concatenate([c128, c128], axis=1), jnp.concatenate([-s128, s128], axis=1)]
    t64 = [jnp.concatenate([c64, c64, c64, c64], axis=1),
           jnp.concatenate([-s64, z32, -s64, z32], axis=1),
           jnp.concatenate([z32, s64, z32, s64], axis=1)]
    qa = HEAD_DIM_A ** -0.5 * LOG2E
    qb = HEAD_DIM_B ** -0.5 * LOG2E
    return {"r128": t128, "r64": t64, "r128q": [t * qa for t in t128], "r64q": [t * qb for t in t64]}


W_PREP_TR = 256


def _w_in_pieces():
    offs = np.cumsum((0,) + IN_SIZES)
    names = ("qa", "kca", "vca", "ksa", "vsa", "kwa", "vwa", "za", "ga", "qb", "kb", "vb", "zb")
    start = {n: int(offs[i]) for i, n in enumerate(names)}
    width = dict(zip(names, IN_SIZES))
    n_g = 3 * GROUP_A
    halves = lambda n: [(start[n] + i * HEAD_DIM_B, HEAD_DIM_B) for i in (0, 0, 1, 1)]
    pieces = {n: [(start[n], width[n])] for n in ("qa", "kca", "ksa", "kwa", "vca", "vsa", "vwa", "za", "zb", "qb")}
    pieces["ga"] = [(start["ga"], n_g), (None, LANES - n_g), (start["ga"] + n_g, n_g), (None, LANES - n_g)]
    pieces["kb2"] = halves("kb")
    pieces["vb2"] = halves("vb")
    return [p for name, _, _ in _PROJ_LAYOUT for p in pieces[name]]


def _w_prep_kernel(w_ref, o_ref):
    row = 0
    parts, filled = [], 0
    for src, wd in _w_in_pieces():
        done = 0
        while done < wd:
            take = min(wd - done, LANES - filled)
            if src is None:
                parts.append(jnp.zeros((take, W_PREP_TR), F32))
            else:
                parts.append(w_ref[src + done:src + done + take, :])
            done += take
            filled += take
            if filled == LANES:
                blk = parts[0] if len(parts) == 1 else jnp.concatenate(parts, axis=0)
                o_ref[row:row + LANES, :] = blk.astype(o_ref.dtype)
                row += LANES
                parts, filled = [], 0
    assert row == D_PROJ and not parts


def _prep_w_in(w_in):
    wt = w_in[0].T
    d_in = wt.shape[0]
    return pl.pallas_call(
        _w_prep_kernel,
        grid=(D_MODEL // W_PREP_TR,),
        in_specs=[pl.BlockSpec((d_in, W_PREP_TR), lambda i: (0, i))],
        out_specs=pl.BlockSpec((D_PROJ, W_PREP_TR), lambda i: (0, i)),
        out_shape=jax.ShapeDtypeStruct((D_PROJ, D_MODEL), BF16),
        compiler_params=pltpu.CompilerParams(dimension_semantics=("arbitrary",)),
        name="w_prep",
    )(wt)


def _in_proj(x2, g, w_all, tabs, S):
    M = x2.shape[0]
    params = pltpu.CompilerParams(dimension_semantics=("arbitrary",), vmem_limit_bytes=48 * 1024 * 1024)
    out_sds = jax.ShapeDtypeStruct((M, IN_TN), BF16)
    slabs, strided, h = [], [], None
    for j in range(N_SLABS):
        kinds = _CHUNK_KINDS[j * _IN_CPT:(j + 1) * _IN_CPT]
        with_norm = j == 0
        tm = IN_TM_NORM if with_norm else IN_TM
        groups = _slab_tab_groups(kinds)
        act_spec = pl.BlockSpec((tm, D_MODEL), lambda i: (i, 0))
        in_specs = [act_spec, pl.BlockSpec((1, D_MODEL), lambda i: (0, 0))] if with_norm else [act_spec]
        args = [x2, g] if with_norm else [h]
        in_specs.append(pl.BlockSpec((IN_TN, D_MODEL), lambda i, j=j: (j, 0)))
        args.append(w_all)
        if groups:
            tab = jnp.stack([t for grp in groups for t in tabs[grp]])
            in_specs.append(pl.BlockSpec(tab.shape, lambda i: (0, 0, 0)))
            args.append(tab)
        out_specs = [pl.BlockSpec((tm, IN_TN), lambda i: (i, 0))]
        out_shape = [out_sds]
        if with_norm:
            out_specs.append(act_spec)
            out_shape.append(jax.ShapeDtypeStruct((M, D_MODEL), BF16))
        strided_chunk = None
        for name in ("kca", "vca"):
            if _slab(name) == j:
                strided_chunk = _col(name) // CHUNK
        scratch = []
        if strided_chunk is not None:
            out_specs.append(pl.BlockSpec((N_KV_A, tm // CMP_STRIDE, _HALF), lambda i: (0, i, 0)))
            out_shape.append(jax.ShapeDtypeStruct((N_KV_A, M // CMP_STRIDE, _HALF), BF16))
            scratch.append(pltpu.VMEM((N_KV_A, IN_SUB, HEAD_DIM_A), F32))
        res = list(pl.pallas_call(
            functools.partial(_in_proj_kernel, kinds, with_norm, S, strided_chunk),
            grid=(M // tm,),
            in_specs=in_specs,
            out_specs=out_specs,
            out_shape=out_shape,
            scratch_shapes=scratch,
            compiler_params=params,
            name=f"in_proj_{j}",
        )(*args))
        slabs.append(res.pop(0))
        if with_norm:
            h = res.pop(0)
        if strided_chunk is not None:
            strided.append(res.pop(0))
    return slabs, strided


CMP_TM = 256
_HALF = CMP_STRIDE * HEAD_DIM_A


def _compress_kernel(tk_ref, tv_ref, w1k_ref, w1v_ref, w2k_ref, w2v_ref, posk_ref, posv_ref, kc_ref, vc_ref):
    for t_ref, w1_ref, w2_ref, pos_ref, o_ref in ((tk_ref, w1k_ref, w2k_ref, posk_ref, kc_ref),
                                                  (tv_ref, w1v_ref, w2v_ref, posv_ref, vc_ref)):
        t = t_ref[...]
        u = _dot(t, w1_ref[0:_HALF, :])
        v = _dot(t, w1_ref[_HALF:2 * _HALF, :])
        pb = _dot(pos_ref[...], w1_ref[...])
        pre = u + pltpu.roll(v, CMP_TM - 1, 0) + pb[0:1]
        hid = pre * jax.nn.sigmoid(pre)
        out = _dot(hid.astype(BF16), w2_ref[...])
        row = lax.broadcasted_iota(jnp.int32, out.shape, 0)
        n_cp = LANES
        out = jnp.where((row & (n_cp - 1)) == n_cp - 1, 0.0, out)
        o_ref[...] = out.astype(o_ref.dtype)


def _compress(tk, tv, w1k, w1v, w2k, w2v, posk, posv):
    R = tk.shape[0]
    rows = pl.BlockSpec((CMP_TM, _HALF), lambda i: (i, 0))
    whole = lambda a: pl.BlockSpec(a.shape, lambda i: (0, 0))
    o_spec = pl.BlockSpec((CMP_TM, HEAD_DIM_A), lambda i: (i, 0))
    o_sds = jax.ShapeDtypeStruct((R, HEAD_DIM_A), BF16)
    return pl.pallas_call(
        _compress_kernel,
        grid=(R // CMP_TM,),
        in_specs=[rows, rows, whole(w1k), whole(w1v), whole(w2k), whole(w2v), whole(posk), whole(posv)],
        out_specs=(o_spec, o_spec),
        out_shape=(o_sds, o_sds),
        compiler_params=pltpu.CompilerParams(dimension_semantics=("arbitrary",)),
        name="compress",
    )(tk, tv, w1k, w1v, w2k, w2v, posk, posv)


TQ = 128
ATTN_AHEAD = 4
ONES_ROWS = 16


def _transpose_values(v_ref, vt_ref, n_rows):
    n_blk = vt_ref.shape[0]
    for blk in range(n_blk):
        vt = v_ref[0, blk * LANES:(blk + 1) * LANES, :].astype(F32).T
        vt_ref[blk, 0:n_rows, :] = vt[0:n_rows].astype(vt_ref.dtype)
        vt_ref[blk, n_rows:n_rows + ONES_ROWS, :] = jnp.ones((ONES_ROWS, LANES), vt_ref.dtype)


def _run_skewed(jobs, ahead=1):
    pending = [job[0]() for job in jobs[:ahead]]
    for i, (_, finish) in enumerate(jobs):
        if i + ahead < len(jobs):
            pending.append(jobs[i + ahead][0]())
        finish(pending[i])
        pending[i] = None


def _softmax_step(s, m, acc, vt):
    m_new = jnp.maximum(m, jnp.max(s, axis=0, keepdims=True))
    p = jnp.exp2(s - m_new).astype(BF16)
    return m_new, jnp.exp2(m - m_new) * acc + _dot(vt, p)


SEL_TK = 256
N_SELP = 32
PAIR = 2 * TQ
_VROWS_A = HEAD_DIM_A + ONES_ROWS
_WIN_TILES = (256, 256, 128)
_N_SEL_JOBS = 9


def _attn_a_kernel(qlo_ref, qhi_ref, ksel_ref, vsel_ref, kwin_ref, vwin_ref, kc_ref, vc_ref,
                   glo_ref, ghi_ref, zlo_ref, zhi_ref, ov_ref, wtab_ref, ctab_ref, ylo_ref, yhi_ref,
                   vselt_ref, vwint_ref, qs_ref, bias_ref, acc_ref, out_ref):
    i = pl.program_id(2)
    n_qt = 2 * pl.num_programs(2)
    n_pairs = GROUP_A // 2
    t0s = (i * TQ, (n_qt - 1 - i) * TQ)
    q_refs, g_refs, z_refs, y_refs = (qlo_ref, qhi_ref), (glo_ref, ghi_ref), (zlo_ref, zhi_ref), (ylo_ref, yhi_ref)

    @pl.when(i == 0)
    def _():
        _transpose_values(vsel_ref, vselt_ref, HEAD_DIM_A)
        _transpose_values(vwin_ref, vwint_ref, HEAD_DIM_A)

    lane = lax.broadcasted_iota(jnp.int32, (1, TQ), 1)
    lane2 = jnp.concatenate([lane, lane], axis=1)
    qpair, gts = [], []
    for t in range(2):
        q = q_refs[t][0]
        pairs = [jnp.concatenate([q[:, h * LANES:(h + 1) * LANES] for h in (2 * p, 2 * p + 1)], axis=0)
                 for p in range(n_pairs)]
        for p in range(n_pairs):
            qs_ref[t, p] = pairs[p]
        qpair.append(pairs)
        gts.append(g_refs[t][0].astype(F32).T)

    def gate(t, p, branch):
        return jnp.concatenate(
            [gts[t][3 * h + branch:3 * h + branch + 1] for h in (2 * p, 2 * p + 1)], axis=1)

    kc = kc_ref[0]
    vct = vc_ref[0].astype(F32).T.astype(BF16)
    cend = lax.broadcasted_iota(jnp.int32, (LANES, PAIR), 0) * CMP_STRIDE + (CMP_BLOCK - 1)
    psums = [[], []]

    def cmp_scores(t, p):
        return jnp.where(cend <= t0s[t] + lane2, _dot_nt(kc, qpair[t][p]), NEG_INF)

    def cmp_finish(t, p, s):
        m = jnp.max(s, axis=0, keepdims=True)
        e = jnp.where(cend <= t0s[t] + lane2, jnp.exp2(s - m), 0.0)
        pc = e / jnp.maximum(jnp.sum(e, axis=0, keepdims=True), TINY)
        psums[t].append(pc[:, 0:TQ] + pc[:, TQ:PAIR])
        out_ref[t, p] = gate(t, p, 0) * _dot(vct, pc.astype(BF16))

    ksts = [pl.multiple_of(jnp.maximum(t0 - WIN_A, 0), TQ) for t0 in t0s]
    win_m = [[jnp.full((1, PAIR), NEG_INF, F32) for _ in range(n_pairs)] for _ in range(2)]
    win_acc = [[jnp.zeros((_VROWS_A, PAIR), F32) for _ in range(n_pairs)] for _ in range(2)]

    win_var = [jnp.minimum(t0 // TQ, WIN_A // TQ) for t0 in t0s]

    def win_scores(t, p, off, tk):
        kt = kwin_ref[0, pl.ds(pl.multiple_of(ksts[t] + off, LANES), tk), :]
        bias = wtab_ref[win_var[t], off:off + tk, :]
        return _dot_nt(kt, qpair[t][p]) + jnp.concatenate([bias, bias], axis=1)

    def win_finish(t, p, off, tk, s):
        vt = jnp.concatenate(
            [vwint_ref[ksts[t] // LANES + off // LANES + b] for b in range(tk // LANES)], axis=1)
        win_m[t][p], win_acc[t][p] = _softmax_step(s, win_m[t][p], win_acc[t][p], vt)

    jobs = [(functools.partial(cmp_scores, t, p), functools.partial(cmp_finish, t, p))
            for t in range(2) for p in range(n_pairs)]
    off = 0
    for tk in _WIN_TILES:
        jobs += [(functools.partial(win_scores, t, p, off, tk), functools.partial(win_finish, t, p, off, tk))
                 for t in range(2) for p in range(n_pairs)]
        off += tk
    _run_skewed(jobs, ATTN_AHEAD)
    for t in range(2):
        for p in range(n_pairs):
            a = win_acc[t][p]
            out_ref[t, p] = out_ref[t, p] + (
                gate(t, p, 2) / jnp.maximum(a[HEAD_DIM_A:HEAD_DIM_A + 1], TINY)) * a[0:HEAD_DIM_A]

    ov = ov_ref[...]
    jidx = lax.broadcasted_iota(jnp.int32, (N_SELP, TQ), 0)
    for t in range(2):
        psum = psums[t][0] + psums[t][1]
        p_hi = psum.astype(BF16)
        r1 = psum - p_hi.astype(F32)
        p_mid = r1.astype(BF16)
        p_lo = (r1 - p_mid.astype(F32)).astype(BF16)
        psel = _dot(ov, p_hi) + _dot(ov, p_mid) + _dot(ov, p_lo)
        tq = t0s[t] + lane
        cur = tq // SEL_BLOCK
        forced = (jidx == 0) | (jidx == cur) | (jidx == cur - 1)
        score = jnp.where(forced, FORCE_SCORE, jnp.where(jidx * SEL_BLOCK <= tq, psel, -1.0))
        rank = jnp.zeros(psel.shape, F32)
        for r in range(N_SELP):
            row = jnp.broadcast_to(score[r:r + 1, :], score.shape)
            rank = rank + jnp.where(jidx > r, jnp.where(row >= score, 1.0, 0.0), jnp.where(row > score, 1.0, 0.0))
        bias_t = jnp.where(rank < SEL_TOPK, 0.0, NEG_INF)
        for j in range(N_SELP):
            bias_ref[t, j] = jnp.broadcast_to(bias_t[j:j + 1, :], (SUBLANES, TQ))

    blk_per_tile = SEL_TK // SEL_BLOCK
    rep = SEL_BLOCK // SUBLANES
    n_lo = i // (SEL_TK // TQ) + 1
    acc_ref[...] = jnp.zeros(acc_ref.shape, F32)
    ms = [[jnp.full((1, PAIR), NEG_INF, F32) for _ in range(n_pairs)] for _ in range(2)]
    ctx = {}

    def sel_scores(k, p):
        if p == 0:
            hi = k >= n_lo
            t = hi.astype(jnp.int32)
            j = jnp.where(hi, k - n_lo, k)
            ks = pl.multiple_of(j * SEL_TK, SEL_TK)
            bt = jnp.concatenate(
                [bias_ref[t, j * blk_per_tile + b] for b in range(blk_per_tile) for _ in range(rep)], axis=0)
            qt = jnp.where(hi, n_qt - 1 - i, i)
            diag = j == qt // (SEL_TK // TQ)
            bt = bt + ctab_ref[jnp.where(diag, 1 + qt % (SEL_TK // TQ), 0)]
            bt = jnp.concatenate([bt, bt], axis=1)
            ctx[k] = (hi, t, j, ks, bt)
        hi, t, j, ks, bt = ctx[k]
        return _dot_nt(ksel_ref[0, pl.ds(ks, SEL_TK), :], qs_ref[t, p]) + bt

    def sel_finish(k, p, s):
        hi, t, j, ks, bt = ctx[k]
        vt = jnp.concatenate([vselt_ref[j * (SEL_TK // LANES) + b] for b in range(SEL_TK // LANES)], axis=1)
        m_new, acc_new = _softmax_step(s, jnp.where(hi, ms[1][p], ms[0][p]), acc_ref[t, p], vt)
        acc_ref[t, p] = acc_new
        ms[0][p] = jnp.where(hi, ms[0][p], m_new)
        ms[1][p] = jnp.where(hi, m_new, ms[1][p])

    sel_jobs = [(functools.partial(sel_scores, k, p), functools.partial(sel_finish, k, p))
                for k in range(_N_SEL_JOBS) for p in range(n_pairs)]
    _run_skewed(sel_jobs, ATTN_AHEAD)

    for t in range(2):
        z = z_refs[t][0]
        for p in range(n_pairs):
            a = acc_ref[t, p]
            o = out_ref[t, p] + (gate(t, p, 1) / jnp.maximum(a[HEAD_DIM_A:HEAD_DIM_A + 1], TINY)) * a[0:HEAD_DIM_A]
            for h in (2 * p, 2 * p + 1):
                oh = o[:, (h % 2) * TQ:(h % 2 + 1) * TQ].T
                y_refs[t][0, :, h * LANES:(h + 1) * LANES] = (
                    oh * z[:, h * LANES:(h + 1) * LANES].astype(F32)).astype(y_refs[t].dtype)


def _mask_tables():
    lane = np.arange(TQ)[None, :]
    row = np.arange(WIN_A + TQ)[:, None]
    wtab = []
    for v in range(WIN_A // TQ + 1):
        t0 = v * TQ
        dpos = (t0 + lane) - (max(t0 - WIN_A, 0) + row)
        wtab.append(np.where((dpos >= 0) & (dpos < WIN_A), 0.0, NEG_INF))
    row = np.arange(SEL_TK)[:, None]
    ctab = [np.zeros((SEL_TK, TQ))]
    for odd in range(SEL_TK // TQ):
        ctab.append(np.where(row <= odd * TQ + lane, 0.0, NEG_INF))
    return jnp.asarray(np.stack(wtab), F32), jnp.asarray(np.stack(ctab), F32)


def _attn_a(slabs, kc, vc, ov):
    B, S, _ = slabs[0].shape
    sl = lambda name: slabs[_slab(name)]
    n_cp = S // CMP_STRIDE
    n_qt = S // TQ
    assert n_cp == LANES and S // SEL_BLOCK == N_SELP and (n_qt // 2) // (SEL_TK // TQ) * 2 + 1 == _N_SEL_JOBS
    wq = GROUP_A * HEAD_DIM_A
    cb = lambda name: _col(name) // LANES
    kv_spec = lambda name: pl.BlockSpec((1, S, LANES), lambda b, k, i, c=cb(name): (b, 0, c + k))
    lo = lambda width, c: pl.BlockSpec((1, TQ, width), lambda b, k, i: (b, i, c + k))
    hi = lambda width, c: pl.BlockSpec((1, TQ, width), lambda b, k, i: (b, n_qt - 1 - i, c + k))
    y_sds = jax.ShapeDtypeStruct((B, S // 2, WIDTH_A), BF16)
    wtab, ctab = _mask_tables()
    return pl.pallas_call(
        _attn_a_kernel,
        grid=(B, N_KV_A, n_qt // 2),
        in_specs=[
            lo(wq, _col("qa") // wq), hi(wq, _col("qa") // wq),
            kv_spec("ksa"), kv_spec("vsa"), kv_spec("kwa"), kv_spec("vwa"),
            pl.BlockSpec((1, n_cp, LANES), lambda b, k, i: (k * B + b, 0, 0)),
            pl.BlockSpec((1, n_cp, LANES), lambda b, k, i: (k * B + b, 0, 0)),
            lo(LANES, cb("ga")), hi(LANES, cb("ga")),
            lo(wq, _col("za") // wq), hi(wq, _col("za") // wq),
            pl.BlockSpec((N_SELP, LANES), lambda b, k, i: (0, 0)),
            pl.BlockSpec(wtab.shape, lambda b, k, i: (0, 0, 0)),
            pl.BlockSpec(ctab.shape, lambda b, k, i: (0, 0, 0)),
        ],
        out_specs=(pl.BlockSpec((1, TQ, wq), lambda b, k, i: (b, i, k)),
                   pl.BlockSpec((1, TQ, wq), lambda b, k, i: (b, n_qt // 2 - 1 - i, k))),
        out_shape=(y_sds, y_sds),
        scratch_shapes=[
            pltpu.VMEM((S // LANES, _VROWS_A, LANES), BF16),
            pltpu.VMEM((S // LANES, _VROWS_A, LANES), BF16),
            pltpu.VMEM((2, GROUP_A // 2, PAIR, HEAD_DIM_A), BF16),
            pltpu.VMEM((2, N_SELP, SUBLANES, TQ), F32),
            pltpu.VMEM((2, GROUP_A // 2, _VROWS_A, PAIR), F32),
            pltpu.VMEM((2, GROUP_A // 2, HEAD_DIM_A, PAIR), F32),
        ],
        compiler_params=pltpu.CompilerParams(
            dimension_semantics=("arbitrary", "arbitrary", "arbitrary"), vmem_limit_bytes=48 * 1024 * 1024),
        name="attn_a",
    )(sl("qa"), sl("qa"), sl("ksa"), sl("vsa"), sl("kwa"), sl("vwa"), kc, vc,
      sl("ga"), sl("ga"), sl("za"), sl("za"), ov, wtab, ctab)


_VROWS_B = HEAD_DIM_B + ONES_ROWS


B_QT = 4


def _attn_b_kernel(sink_ref, q_ref, k_ref, v_ref, z_ref, y_ref, vt_ref):
    kv = pl.program_id(1)
    qi = pl.program_id(2)
    n_keys = WIN_B + TQ

    @pl.when(qi == 0)
    def _():
        _transpose_values(v_ref, vt_ref, HEAD_DIM_B)

    left = lax.broadcasted_iota(jnp.int32, (TQ, LANES), 1) < HEAD_DIM_B
    zero = jnp.zeros((TQ, LANES), q_ref.dtype)
    lane = lax.broadcasted_iota(jnp.int32, (1, TQ), 1)
    lane2 = jnp.concatenate([lane, lane], axis=1)
    krow = lax.broadcasted_iota(jnp.int32, (n_keys, PAIR), 0)
    sinks = [jnp.concatenate(
        [jnp.full((1, TQ), sink_ref[kv * GROUP_B + 2 * c + i] * LOG2E, F32) for i in range(2)], axis=1)
        for c in range(GROUP_B // 2)]
    ctx = {}

    def tile_ctx(u):
        if u not in ctx:
            t0 = (qi * B_QT + u) * TQ
            kst = pl.multiple_of(jnp.maximum(t0 - WIN_B, 0), TQ)
            dpos = (t0 + lane2) - (kst + krow)
            bias = jnp.where((dpos >= 0) & (dpos < WIN_B), 0.0, NEG_INF)
            ctx[u] = (kst, bias)
        return ctx[u]

    def scores(u, c):
        kst, bias = tile_ctx(u)
        pair = q_ref[0, u * TQ:(u + 1) * TQ, c * LANES:(c + 1) * LANES]
        qp = jnp.concatenate([jnp.where(left, pair, zero), jnp.where(left, zero, pair)], axis=0)
        kt = k_ref[0, pl.ds(kst, n_keys), :]
        return _dot_nt(kt, qp) + bias

    def finish(u, c, s):
        kst, _ = tile_ctx(u)
        vt = jnp.concatenate([vt_ref[kst // LANES + b] for b in range(n_keys // LANES)], axis=1)
        m = jnp.maximum(jnp.max(s, axis=0, keepdims=True), sinks[c])
        p = jnp.exp2(s - m).astype(BF16)
        a = _dot(vt, p)
        den = jnp.maximum(a[HEAD_DIM_B:HEAD_DIM_B + 1] + jnp.exp2(sinks[c] - m), TINY)
        o = a[0:HEAD_DIM_B] / den
        o2 = jnp.concatenate([o[:, 0:TQ], o[:, TQ:PAIR]], axis=0)
        z = z_ref[0, u * TQ:(u + 1) * TQ, c * LANES:(c + 1) * LANES]
        y_ref[0, u * TQ:(u + 1) * TQ, c * LANES:(c + 1) * LANES] = (o2.T * z.astype(F32)).astype(y_ref.dtype)

    _run_skewed([(functools.partial(scores, u, c), functools.partial(finish, u, c))
                 for u in range(B_QT) for c in range(GROUP_B // 2)], ATTN_AHEAD)


def _attn_b(sinks, slabs):
    B, S, _ = slabs[0].shape
    sl = lambda name: slabs[_slab(name)]
    wq = GROUP_B * HEAD_DIM_B
    cb = lambda name: _col(name) // LANES
    grid_spec = pltpu.PrefetchScalarGridSpec(
        num_scalar_prefetch=1,
        grid=(B, N_KV_B, S // (B_QT * TQ)),
        in_specs=[
            pl.BlockSpec((1, B_QT * TQ, wq), lambda b, k, i, s: (b, i, _col("qb") // wq + k)),
            pl.BlockSpec((1, S, LANES), lambda b, k, i, s, c=cb("kb2"): (b, 0, c + k)),
            pl.BlockSpec((1, S, LANES), lambda b, k, i, s, c=cb("vb2"): (b, 0, c + k)),
            pl.BlockSpec((1, B_QT * TQ, wq), lambda b, k, i, s: (b, i, _col("zb") // wq + k)),
        ],
        out_specs=pl.BlockSpec((1, B_QT * TQ, wq), lambda b, k, i, s: (b, i, k)),
        scratch_shapes=[pltpu.VMEM((S // LANES, _VROWS_B, LANES), BF16)],
    )
    return pl.pallas_call(
        _attn_b_kernel,
        grid_spec=grid_spec,
        out_shape=jax.ShapeDtypeStruct((B, S, WIDTH_B), BF16),
        compiler_params=pltpu.CompilerParams(
            dimension_semantics=("arbitrary", "arbitrary", "arbitrary")),
        name="attn_b",
    )(sinks, sl("qb"), sl("kb2"), sl("vb2"), sl("zb"))


OUT_TM = 512
OUT_SUB = 256


def _out_proj_kernel(tiles_per_half, ylo_ref, yhi_ref, yb_ref, wa_ref, wb_ref, x_ref, g_ref, o_ref):
    in_lo = (pl.program_id(0) % (2 * tiles_per_half)) < tiles_per_half

    def project(rows):
        ya = jnp.where(in_lo, ylo_ref[rows, :], yhi_ref[rows, :])
        return _dot(ya, wa_ref[...]) + _dot(yb_ref[rows, :], wb_ref[...])

    def finish(rows, acc):
        r = x_ref[rows, :] + acc
        ms = jnp.mean(r * r, axis=-1, keepdims=True)
        o_ref[rows, :] = (r * lax.rsqrt(ms + RMS_EPS)) * g_ref[...]

    chunks = [slice(r * OUT_SUB, (r + 1) * OUT_SUB) for r in range(OUT_TM // OUT_SUB)]
    _run_skewed([(functools.partial(project, rows), functools.partial(finish, rows)) for rows in chunks])


def _out_proj(ya_lo, ya_hi, yb2, w_out, x2, g, S):
    M = x2.shape[0]
    tph = S // 2 // OUT_TM
    half = lambda i: (i // (2 * tph)) * tph
    t = lambda i: i % (2 * tph)
    return pl.pallas_call(
        functools.partial(_out_proj_kernel, tph),
        grid=(M // OUT_TM,),
        in_specs=[
            pl.BlockSpec((OUT_TM, WIDTH_A), lambda i: (half(i) + jnp.minimum(t(i), tph - 1), 0)),
            pl.BlockSpec((OUT_TM, WIDTH_A), lambda i: (half(i) + jnp.maximum(t(i) - tph, 0), 0)),
            pl.BlockSpec((OUT_TM, WIDTH_B), lambda i: (i, 0)),
            pl.BlockSpec((WIDTH_A, D_MODEL), lambda i: (0, 0)),
            pl.BlockSpec((WIDTH_B, D_MODEL), lambda i: (WIDTH_A // WIDTH_B, 0)),
            pl.BlockSpec((OUT_TM, D_MODEL), lambda i: (i, 0)),
            pl.BlockSpec((1, D_MODEL), lambda i: (0, 0)),
        ],
        out_specs=pl.BlockSpec((OUT_TM, D_MODEL), lambda i: (i, 0)),
        out_shape=jax.ShapeDtypeStruct((M, D_MODEL), F32),
        compiler_params=pltpu.CompilerParams(
            dimension_semantics=("arbitrary",), vmem_limit_bytes=56 * 1024 * 1024),
        name="out_proj",
    )(ya_lo, ya_hi, yb2, w_out, w_out, x2, g)


def _overlap_matrix(n_cp):
    c_start = np.arange(n_cp) * CMP_STRIDE
    j_start = np.arange(N_SELP) * SEL_BLOCK
    ov = (c_start[None, :] < j_start[:, None] + SEL_BLOCK) & (c_start[None, :] + CMP_BLOCK > j_start[:, None])
    return jnp.asarray(ov, BF16)


def kernel(x, w_in, cmp_k_w1, cmp_k_w2, cmp_v_w1, cmp_v_w2, cmp_k_pos, cmp_v_pos, sinks, w_out, norm_g, final_g):
    B, S, D = x.shape
    assert D == D_MODEL and w_in.shape[0] == 1
    x2 = x.reshape(B * S, D)

    w_all = _prep_w_in(w_in)
    tabs = _rope_tables(S)
    slabs, strided = _in_proj(x2, norm_g[0].reshape(1, D), w_all, tabs, S)
    slabs = [p.reshape(B, S, IN_TN) for p in slabs]

    n_cp = S // CMP_STRIDE
    tk, tv = [t.reshape(N_KV_A * B * n_cp, _HALF) for t in strided]
    flat_pos = lambda p: jnp.broadcast_to(p.reshape(1, CMP_BLOCK * HEAD_DIM_A), (8, CMP_BLOCK * HEAD_DIM_A))
    kc, vc = _compress(tk, tv, cmp_k_w1[0].astype(BF16), cmp_v_w1[0].astype(BF16),
                       cmp_k_w2[0].astype(BF16), cmp_v_w2[0].astype(BF16),
                       flat_pos(cmp_k_pos[0]).astype(BF16), flat_pos(cmp_v_pos[0]).astype(BF16))
    kc, vc = [c.reshape(N_KV_A * B, n_cp, HEAD_DIM_A) for c in (kc, vc)]

    ya_lo, ya_hi = _attn_a(slabs, kc, vc, _overlap_matrix(n_cp))
    y_b = _attn_b(sinks[0], slabs)

    out = _out_proj(ya_lo.reshape(B * S // 2, WIDTH_A), ya_hi.reshape(B * S // 2, WIDTH_A),
                    y_b.reshape(B * S, WIDTH_B), w_out[0].astype(BF16), x2, final_g.reshape(1, D), S)
    return out.reshape(B, S, D)
```

```python
import functools
import math

import numpy as np
import jax
import jax.numpy as jnp
from jax import lax
from jax.experimental import pallas as pl
from jax.experimental.pallas import tpu as pltpu

F32 = jnp.float32
BF16 = jnp.bfloat16

D_MODEL = 2048
ROPE_THETA = 10000.0
RMS_EPS = 1e-6
NEG_INF = -1e30
TINY = 1e-30
LOG2E = math.log2(math.e)

WIDTH_A = 1024
HEAD_DIM_A = 128
N_KV_A = 2
GROUP_A = 4
KV_A = N_KV_A * HEAD_DIM_A
CMP_BLOCK = 32
CMP_STRIDE = 16
CMP_HIDDEN = 256
SEL_BLOCK = 64
SEL_TOPK = 16
WIN_A = 512
FORCE_SCORE = 1e4

WIDTH_B = 1024
HEAD_DIM_B = 64
N_HEADS_B = 16
N_KV_B = 2
GROUP_B = 8
KV_B = N_KV_B * HEAD_DIM_B
WIN_B = 128

IN_SIZES = (WIDTH_A, KV_A, KV_A, KV_A, KV_A, KV_A, KV_A, WIDTH_A, 3 * 8,
            WIDTH_B, KV_B, KV_B, WIDTH_B)

LANES = 128
SUBLANES = 8
CHUNK = 256

EP_NONE, EP_ROPE128, EP_ROPE64, EP_SILU, EP_SIGMOID, EP_ROPE128_Q, EP_ROPE64_Q = range(7)

_PROJ_LAYOUT = (
    ("qa", 4, EP_ROPE128_Q), ("kca", 1, EP_ROPE128),
    ("ksa", 1, EP_ROPE128), ("kwa", 1, EP_ROPE128), ("vca", 1, EP_NONE), ("vsa", 1, EP_NONE), ("vwa", 1, EP_NONE),
    ("za", 4, EP_SILU), ("ga", 1, EP_SIGMOID),
    ("zb", 4, EP_SILU), ("kb2", 1, EP_ROPE64),
    ("qb", 4, EP_ROPE64_Q), ("vb2", 1, EP_NONE),
)
_CHUNK_KINDS = tuple(k for _, n, k in _PROJ_LAYOUT for _ in range(n))
_CHUNK_START = {}
_c = 0
for _name, _n, _k in _PROJ_LAYOUT:
    _CHUNK_START[_name] = _c
    _c += _n
N_CHUNKS = _c
D_PROJ = N_CHUNKS * CHUNK
IN_TN = 1280
_IN_CPT = IN_TN // CHUNK
N_SLABS = D_PROJ // IN_TN


def _slab(name):
    return _CHUNK_START[name] // _IN_CPT


def _col(name):
    return (_CHUNK_START[name] % _IN_CPT) * CHUNK


def _dot(a, b):
    return jnp.dot(a, b, preferred_element_type=F32)


def _dot_nt(a, b):
    return lax.dot_general(a, b, (((1,), (1,)), ((), ())), preferred_element_type=F32)


IN_TM_NORM = 512
IN_TM = 1024
IN_SUB = 512

_TAB_GROUP = {EP_ROPE128: "r128", EP_ROPE128_Q: "r128q", EP_ROPE64: "r64", EP_ROPE64_Q: "r64q"}
_TAB_SIZE = {"r128": 2, "r128q": 2, "r64": 3, "r64q": 3}


def _slab_tab_groups(kinds):
    return tuple(dict.fromkeys(_TAB_GROUP[k] for k in kinds if k in _TAB_GROUP))


def _in_proj_kernel(kinds, with_norm, seq_len, strided_chunk, *refs):
    groups = _slab_tab_groups(kinds)
    refs = list(refs)
    if with_norm:
        x_ref, g_ref = refs.pop(0), refs.pop(0)
    else:
        h_ref = refs.pop(0)
    w_ref = refs.pop(0)
    tab_ref = refs.pop(0) if groups else None
    o_ref = refs.pop(0)
    if with_norm:
        h_ref = refs.pop(0)
        x = x_ref[...]
        ms = jnp.mean(x * x, axis=-1, keepdims=True)
        h_ref[...] = ((x * lax.rsqrt(ms + RMS_EPS)) * g_ref[...]).astype(BF16)
    if strided_chunk is not None:
        t_ref, stage_ref = refs.pop(0), refs.pop(0)
    tm = h_ref.shape[0]
    pos0 = (pl.program_id(0) % (seq_len // tm)) * tm
    base, n = {}, 0
    for grp in groups:
        base[grp] = n
        n += _TAB_SIZE[grp]

    def epilogue(kind, r, a):
        if kind == EP_NONE:
            return a
        if kind == EP_SILU:
            return a * jax.nn.sigmoid(a)
        if kind == EP_SIGMOID:
            return jax.nn.sigmoid(a)
        t = base[_TAB_GROUP[kind]]
        tab = lambda k: tab_ref[t + k, pl.ds(pl.multiple_of(pos0 + r * IN_SUB, IN_SUB), IN_SUB), :]
        if kind in (EP_ROPE128, EP_ROPE128_Q):
            return a * tab(0) + pltpu.roll(a, 64, 1) * tab(1)
        return a * tab(0) + pltpu.roll(a, 96, 1) * tab(1) + pltpu.roll(a, 32, 1) * tab(2)

    n_grp = IN_SUB // CMP_STRIDE

    def finish(kind, r, c, acc):
        rows = slice(r * IN_SUB, (r + 1) * IN_SUB)
        for sub in range(CHUNK // LANES):
            cols = slice(c * CHUNK + sub * LANES, c * CHUNK + (sub + 1) * LANES)
            val = epilogue(kind, r, acc[:, sub * LANES:(sub + 1) * LANES])
            o_ref[rows, cols] = val.astype(o_ref.dtype)
            if c == strided_chunk:
                stage_ref[sub] = val
                for k in range(CMP_STRIDE):
                    t_ref[sub, r * n_grp:(r + 1) * n_grp, k * LANES:(k + 1) * LANES] = (
                        stage_ref[sub, pl.ds(k, n_grp, stride=CMP_STRIDE), :].astype(t_ref.dtype))

    jobs = []
    for r in range(tm // IN_SUB):
        for c, kind in enumerate(kinds):
            jobs.append((lambda r=r, c=c: _dot_nt(h_ref[r * IN_SUB:(r + 1) * IN_SUB, :],
                                                  w_ref[c * CHUNK:(c + 1) * CHUNK, :]),
                         functools.partial(finish, kind, r, c)))
    _run_skewed(jobs)


def _rope_tables(S):
    def cs(d):
        inv = ROPE_THETA ** (-jnp.arange(0, d, 2, dtype=F32) / d)
        ang = jnp.arange(S, dtype=F32)[:, None] * inv[None, :]
        return jnp.cos(ang), jnp.sin(ang)

    c128, s128 = cs(HEAD_DIM_A)
    c64, s64 = cs(HEAD_DIM_B)
    z32 = jnp.zeros_like(s64)
    t128 = [jnp.concatenate([c128, c128], axis=1), jnp.concatenate([-s128, s128], axis=1)]
    t64 = [jnp.concatenate([c64, c64, c64, c64], axis=1),
           jnp.concatenate([-s64, z32, -s64, z32], axis=1),
           jnp.concatenate([z32, s64, z32, s64], axis=1)]
    qa = HEAD_DIM_A ** -0.5 * LOG2E
    qb = HEAD_DIM_B ** -0.5 * LOG2E
    return {"r128": t128, "r64": t64, "r128q": [t * qa for t in t128], "r64q": [t * qb for t in t64]}


W_PREP_TR = 256


def _w_in_pieces():
    offs = np.cumsum((0,) + IN_SIZES)
    names = ("qa", "kca", "vca", "ksa", "vsa", "kwa", "vwa", "za", "ga", "qb", "kb", "vb", "zb")
    start = {n: int(offs[i]) for i, n in enumerate(names)}
    width = dict(zip(names, IN_SIZES))
    n_g = 3 * GROUP_A
    halves = lambda n: [(start[n] + i * HEAD_DIM_B, HEAD_DIM_B) for i in (0, 0, 1, 1)]
    pieces = {n: [(start[n], width[n])] for n in ("qa", "kca", "ksa", "kwa", "vca", "vsa", "vwa", "za", "zb", "qb")}
    pieces["ga"] = [(start["ga"], n_g), (None, LANES - n_g), (start["ga"] + n_g, n_g), (None, LANES - n_g)]
    pieces["kb2"] = halves("kb")
    pieces["vb2"] = halves("vb")
    return [p for name, _, _ in _PROJ_LAYOUT for p in pieces[name]]


def _w_prep_kernel(w_ref, o_ref):
    row = 0
    parts, filled = [], 0
    for src, wd in _w_in_pieces():
        done = 0
        while done < wd:
            take = min(wd - done, LANES - filled)
            if src is None:
                parts.append(jnp.zeros((take, W_PREP_TR), F32))
            else:
                parts.append(w_ref[src + done:src + done + take, :])
            done += take
            filled += take
            if filled == LANES:
                blk = parts[0] if len(parts) == 1 else jnp.concatenate(parts, axis=0)
                o_ref[row:row + LANES, :] = blk.astype(o_ref.dtype)
                row += LANES
                parts, filled = [], 0
    assert row == D_PROJ and not parts


def _prep_w_in(w_in):
    wt = w_in[0].T
    d_in = wt.shape[0]
    return pl.pallas_call(
        _w_prep_kernel,
        grid=(D_MODEL // W_PREP_TR,),
        in_specs=[pl.BlockSpec((d_in, W_PREP_TR), lambda i: (0, i))],
        out_specs=pl.BlockSpec((D_PROJ, W_PREP_TR), lambda i: (0, i)),
        out_shape=jax.ShapeDtypeStruct((D_PROJ, D_MODEL), BF16),
        compiler_params=pltpu.CompilerParams(dimension_semantics=("arbitrary",)),
        name="w_prep",
    )(wt)


def _in_proj(x2, g, w_all, tabs, S):
    M = x2.shape[0]
    params = pltpu.CompilerParams(dimension_semantics=("arbitrary",), vmem_limit_bytes=48 * 1024 * 1024)
    out_sds = jax.ShapeDtypeStruct((M, IN_TN), BF16)
    slabs, strided, h = [], [], None
    for j in range(N_SLABS):
        kinds = _CHUNK_KINDS[j * _IN_CPT:(j + 1) * _IN_CPT]
        with_norm = j == 0
        tm = IN_TM_NORM if with_norm else IN_TM
        groups = _slab_tab_groups(kinds)
        act_spec = pl.BlockSpec((tm, D_MODEL), lambda i: (i, 0))
        in_specs = [act_spec, pl.BlockSpec((1, D_MODEL), lambda i: (0, 0))] if with_norm else [act_spec]
        args = [x2, g] if with_norm else [h]
        in_specs.append(pl.BlockSpec((IN_TN, D_MODEL), lambda i, j=j: (j, 0)))
        args.append(w_all)
        if groups:
            tab = jnp.stack([t for grp in groups for t in tabs[grp]])
            in_specs.append(pl.BlockSpec(tab.shape, lambda i: (0, 0, 0)))
            args.append(tab)
        out_specs = [pl.BlockSpec((tm, IN_TN), lambda i: (i, 0))]
        out_shape = [out_sds]
        if with_norm:
            out_specs.append(act_spec)
            out_shape.append(jax.ShapeDtypeStruct((M, D_MODEL), BF16))
        strided_chunk = None
        for name in ("kca", "vca"):
            if _slab(name) == j:
                strided_chunk = _col(name) // CHUNK
        scratch = []
        if strided_chunk is not None:
            out_specs.append(pl.BlockSpec((N_KV_A, tm // CMP_STRIDE, _HALF), lambda i: (0, i, 0)))
            out_shape.append(jax.ShapeDtypeStruct((N_KV_A, M // CMP_STRIDE, _HALF), BF16))
            scratch.append(pltpu.VMEM((N_KV_A, IN_SUB, HEAD_DIM_A), F32))
        res = list(pl.pallas_call(
            functools.partial(_in_proj_kernel, kinds, with_norm, S, strided_chunk),
            grid=(M // tm,),
            in_specs=in_specs,
            out_specs=out_specs,
            out_shape=out_shape,
            scratch_shapes=scratch,
            compiler_params=params,
            name=f"in_proj_{j}",
        )(*args))
        slabs.append(res.pop(0))
        if with_norm:
            h = res.pop(0)
        if strided_chunk is not None:
            strided.append(res.pop(0))
    return slabs, strided


CMP_TM = 256
_HALF = CMP_STRIDE * HEAD_DIM_A


def _compress_kernel(tk_ref, tv_ref, w1k_ref, w1v_ref, w2k_ref, w2v_ref, posk_ref, posv_ref, kc_ref, vc_ref):
    for t_ref, w1_ref, w2_ref, pos_ref, o_ref in ((tk_ref, w1k_ref, w2k_ref, posk_ref, kc_ref),
                                                  (tv_ref, w1v_ref, w2v_ref, posv_ref, vc_ref)):
        t = t_ref[...]
        u = _dot(t, w1_ref[0:_HALF, :])
        v = _dot(t, w1_ref[_HALF:2 * _HALF, :])
        pb = _dot(pos_ref[...], w1_ref[...])
        pre = u + pltpu.roll(v, CMP_TM - 1, 0) + pb[0:1]
        hid = pre * jax.nn.sigmoid(pre)
        out = _dot(hid.astype(BF16), w2_ref[...])
        row = lax.broadcasted_iota(jnp.int32, out.shape, 0)
        n_cp = LANES
        out = jnp.where((row & (n_cp - 1)) == n_cp - 1, 0.0, out)
        o_ref[...] = out.astype(o_ref.dtype)


def _compress(tk, tv, w1k, w1v, w2k, w2v, posk, posv):
    R = tk.shape[0]
    rows = pl.BlockSpec((CMP_TM, _HALF), lambda i: (i, 0))
    whole = lambda a: pl.BlockSpec(a.shape, lambda i: (0, 0))
    o_spec = pl.BlockSpec((CMP_TM, HEAD_DIM_A), lambda i: (i, 0))
    o_sds = jax.ShapeDtypeStruct((R, HEAD_DIM_A), BF16)
    return pl.pallas_call(
        _compress_kernel,
        grid=(R // CMP_TM,),
        in_specs=[rows, rows, whole(w1k), whole(w1v), whole(w2k), whole(w2v), whole(posk), whole(posv)],
        out_specs=(o_spec, o_spec),
        out_shape=(o_sds, o_sds),
        compiler_params=pltpu.CompilerParams(dimension_semantics=("arbitrary",)),
        name="compress",
    )(tk, tv, w1k, w1v, w2k, w2v, posk, posv)


TQ = 128
ATTN_AHEAD = 4
ONES_ROWS = 16


def _transpose_values(v_ref, vt_ref, n_rows):
    n_blk = vt_ref.shape[0]
    for blk in range(n_blk):
        vt = v_ref[0, blk * LANES:(blk + 1) * LANES, :].astype(F32).T
        vt_ref[blk, 0:n_rows, :] = vt[0:n_rows].astype(vt_ref.dtype)
        vt_ref[blk, n_rows:n_rows + ONES_ROWS, :] = jnp.ones((ONES_ROWS, LANES), vt_ref.dtype)


def _run_skewed(jobs, ahead=1):
    pending = [job[0]() for job in jobs[:ahead]]
    for i, (_, finish) in enumerate(jobs):
        if i + ahead < len(jobs):
            pending.append(jobs[i + ahead][0]())
        finish(pending[i])
        pending[i] = None


def _softmax_step(s, m, acc, vt):
    m_new = jnp.maximum(m, jnp.max(s, axis=0, keepdims=True))
    p = jnp.exp2(s - m_new).astype(BF16)
    return m_new, jnp.exp2(m - m_new) * acc + _dot(vt, p)


SEL_TK = 256
N_SELP = 32
PAIR = 2 * TQ
_VROWS_A = HEAD_DIM_A + ONES_ROWS
_VROWS_B = HEAD_DIM_B + ONES_ROWS
_WIN_TILES = (256, 256, 128)
_N_SEL_JOBS = 9


def _attn_kernel(sink_ref, qlo_ref, qhi_ref, ksel_ref, vsel_ref, kwin_ref, vwin_ref, kc_ref, vc_ref,
                 glo_ref, ghi_ref, zlo_ref, zhi_ref, ov_ref, wtab_ref, ctab_ref,
                 qblo_ref, qbhi_ref, kb_ref, vb_ref, zblo_ref, zbhi_ref, btab_ref,
                 ylo_ref, yhi_ref, yblo_ref, ybhi_ref,
                 vselt_ref, vwint_ref, vbt_ref, qs_ref, bias_ref, acc_ref, out_ref):
    kv = pl.program_id(1)
    i = pl.program_id(2)
    n_qt = 2 * pl.num_programs(2)
    n_pairs = GROUP_A // 2
    t0s = (i * TQ, (n_qt - 1 - i) * TQ)
    q_refs, g_refs, z_refs, y_refs = (qlo_ref, qhi_ref), (glo_ref, ghi_ref), (zlo_ref, zhi_ref), (ylo_ref, yhi_ref)
    qb_refs, zb_refs, yb_refs = (qblo_ref, qbhi_ref), (zblo_ref, zbhi_ref), (yblo_ref, ybhi_ref)

    @pl.when(i == 0)
    def _():
        _transpose_values(vsel_ref, vselt_ref, HEAD_DIM_A)
        _transpose_values(vwin_ref, vwint_ref, HEAD_DIM_A)
        _transpose_values(vb_ref, vbt_ref, HEAD_DIM_B)

    lane = lax.broadcasted_iota(jnp.int32, (1, TQ), 1)
    lane2 = jnp.concatenate([lane, lane], axis=1)
    qpair, gts = [], []
    for t in range(2):
        q = q_refs[t][0]
        pairs = [jnp.concatenate([q[:, h * LANES:(h + 1) * LANES] for h in (2 * p, 2 * p + 1)], axis=0)
                 for p in range(n_pairs)]
        for p in range(n_pairs):
            qs_ref[t, p] = pairs[p]
        qpair.append(pairs)
        gts.append(g_refs[t][0].astype(F32).T)

    def gate(t, p, branch):
        return jnp.concatenate(
            [gts[t][3 * h + branch:3 * h + branch + 1] for h in (2 * p, 2 * p + 1)], axis=1)

    kc = kc_ref[0]
    vct = vc_ref[0].astype(F32).T.astype(BF16)
    cend = lax.broadcasted_iota(jnp.int32, (LANES, PAIR), 0) * CMP_STRIDE + (CMP_BLOCK - 1)
    psums = [[], []]

    def cmp_scores(t, p):
        return jnp.where(cend <= t0s[t] + lane2, _dot_nt(kc, qpair[t][p]), NEG_INF)

    def cmp_finish(t, p, s):
        m = jnp.max(s, axis=0, keepdims=True)
        e = jnp.where(cend <= t0s[t] + lane2, jnp.exp2(s - m), 0.0)
        pc = e / jnp.maximum(jnp.sum(e, axis=0, keepdims=True), TINY)
        psums[t].append(pc[:, 0:TQ] + pc[:, TQ:PAIR])
        out_ref[t, p] = gate(t, p, 0) * _dot(vct, pc.astype(BF16))

    ksts = [pl.multiple_of(jnp.maximum(t0 - WIN_A, 0), TQ) for t0 in t0s]
    win_m = [[jnp.full((1, PAIR), NEG_INF, F32) for _ in range(n_pairs)] for _ in range(2)]
    win_acc = [[jnp.zeros((_VROWS_A, PAIR), F32) for _ in range(n_pairs)] for _ in range(2)]

    win_var = [jnp.minimum(t0 // TQ, WIN_A // TQ) for t0 in t0s]

    def win_scores(t, p, off, tk):
        kt = kwin_ref[0, pl.ds(pl.multiple_of(ksts[t] + off, LANES), tk), :]
        bias = wtab_ref[win_var[t], off:off + tk, :]
        return _dot_nt(kt, qpair[t][p]) + jnp.concatenate([bias, bias], axis=1)

    def win_finish(t, p, off, tk, s):
        vt = jnp.concatenate(
            [vwint_ref[ksts[t] // LANES + off // LANES + b] for b in range(tk // LANES)], axis=1)
        win_m[t][p], win_acc[t][p] = _softmax_step(s, win_m[t][p], win_acc[t][p], vt)
        if off + tk == sum(_WIN_TILES):
            a = win_acc[t][p]
            out_ref[t, p] = out_ref[t, p] + (
                gate(t, p, 2) / jnp.maximum(a[HEAD_DIM_A:HEAD_DIM_A + 1], TINY)) * a[0:HEAD_DIM_A]

    jobs = [(functools.partial(cmp_scores, t, p), functools.partial(cmp_finish, t, p))
            for t in range(2) for p in range(n_pairs)]
    off = 0
    for tk in _WIN_TILES:
        jobs += [(functools.partial(win_scores, t, p, off, tk), functools.partial(win_finish, t, p, off, tk))
                 for t in range(2) for p in range(n_pairs)]
        off += tk

    nb_keys = WIN_B + TQ
    left = lax.broadcasted_iota(jnp.int32, (TQ, LANES), 1) < HEAD_DIM_B
    zero = jnp.zeros((TQ, LANES), qblo_ref.dtype)
    kbs = [pl.multiple_of(jnp.maximum(t0 - WIN_B, 0), TQ) for t0 in t0s]
    b_var = [jnp.minimum(t0 // TQ, WIN_B // TQ) for t0 in t0s]

    def b_sink(c):
        return jnp.concatenate(
            [jnp.full((1, TQ), sink_ref[kv * GROUP_B + 2 * c + h] * LOG2E, F32) for h in range(2)], axis=1)

    def b_scores(t, c):
        pair = qb_refs[t][0, :, c * LANES:(c + 1) * LANES]
        qp = jnp.concatenate([jnp.where(left, pair, zero), jnp.where(left, zero, pair)], axis=0)
        bias = btab_ref[b_var[t]]
        return _dot_nt(kb_ref[0, pl.ds(kbs[t], nb_keys), :], qp) + jnp.concatenate([bias, bias], axis=1)

    def b_finish(t, c, s):
        vt = jnp.concatenate([vbt_ref[kbs[t] // LANES + b] for b in range(nb_keys // LANES)], axis=1)
        sink = b_sink(c)
        m = jnp.maximum(jnp.max(s, axis=0, keepdims=True), sink)
        p = jnp.exp2(s - m).astype(BF16)
        a = _dot(vt, p)
        den = jnp.maximum(a[HEAD_DIM_B:HEAD_DIM_B + 1] + jnp.exp2(sink - m), TINY)
        o = a[0:HEAD_DIM_B] / den
        o2 = jnp.concatenate([o[:, 0:TQ], o[:, TQ:PAIR]], axis=0)
        z = zb_refs[t][0, :, c * LANES:(c + 1) * LANES]
        yb_refs[t][0, :, c * LANES:(c + 1) * LANES] = (o2.T * z.astype(F32)).astype(yb_refs[t].dtype)

    jobs += [(functools.partial(b_scores, t, c), functools.partial(b_finish, t, c))
             for t in range(2) for c in range(GROUP_B // 2)]

    def select_blocks(t):
        ov = ov_ref[...]
        jidx = lax.broadcasted_iota(jnp.int32, (N_SELP, TQ), 0)
        psum = psums[t][0] + psums[t][1]
        p_hi = psum.astype(BF16)
        r1 = psum - p_hi.astype(F32)
        p_mid = r1.astype(BF16)
        p_lo = (r1 - p_mid.astype(F32)).astype(BF16)
        psel = _dot(ov, p_hi) + _dot(ov, p_mid) + _dot(ov, p_lo)
        tq = t0s[t] + lane
        cur = tq // SEL_BLOCK
        forced = (jidx == 0) | (jidx == cur) | (jidx == cur - 1)
        score = jnp.where(forced, FORCE_SCORE, jnp.where(jidx * SEL_BLOCK <= tq, psel, -1.0))
        rank = jnp.zeros(psel.shape, F32)
        for r in range(N_SELP):
            row = jnp.broadcast_to(score[r:r + 1, :], score.shape)
            rank = rank + jnp.where(jidx > r, jnp.where(row >= score, 1.0, 0.0), jnp.where(row > score, 1.0, 0.0))
        bias_t = jnp.where(rank < SEL_TOPK, 0.0, NEG_INF)
        for j in range(N_SELP):
            bias_ref[t, j] = jnp.broadcast_to(bias_t[j:j + 1, :], (SUBLANES, TQ))

    blk_per_tile = SEL_TK // SEL_BLOCK
    rep = SEL_BLOCK // SUBLANES
    n_lo = i // (SEL_TK // TQ) + 1
    acc_ref[...] = jnp.zeros(acc_ref.shape, F32)
    ms = [[jnp.full((1, PAIR), NEG_INF, F32) for _ in range(n_pairs)] for _ in range(2)]
    ctx = {}

    def sel_scores(k, p):
        if p == 0:
            if k == 0:
                select_blocks(0)
                select_blocks(1)
            hi = k >= n_lo
            t = hi.astype(jnp.int32)
            j = jnp.where(hi, k - n_lo, k)
            ks = pl.multiple_of(j * SEL_TK, SEL_TK)
            bt = jnp.concatenate(
                [bias_ref[t, j * blk_per_tile + b] for b in range(blk_per_tile) for _ in range(rep)], axis=0)
            qt = jnp.where(hi, n_qt - 1 - i, i)
            diag = j == qt // (SEL_TK // TQ)
            bt = bt + ctab_ref[jnp.where(diag, 1 + qt % (SEL_TK // TQ), 0)]
            bt = jnp.concatenate([bt, bt], axis=1)
            ctx[k] = (hi, t, j, ks, bt)
        hi, t, j, ks, bt = ctx[k]
        return _dot_nt(ksel_ref[0, pl.ds(ks, SEL_TK), :], qs_ref[t, p]) + bt

    def sel_finish(k, p, s):
        hi, t, j, ks, bt = ctx[k]
        vt = jnp.concatenate([vselt_ref[j * (SEL_TK // LANES) + b] for b in range(SEL_TK // LANES)], axis=1)
        m_new, acc_new = _softmax_step(s, jnp.where(hi, ms[1][p], ms[0][p]), acc_ref[t, p], vt)
        acc_ref[t, p] = acc_new
        ms[0][p] = jnp.where(hi, ms[0][p], m_new)
        ms[1][p] = jnp.where(hi, m_new, ms[1][p])

    sel_jobs = [(functools.partial(sel_scores, k, p), functools.partial(sel_finish, k, p))
                for k in range(_N_SEL_JOBS) for p in range(n_pairs)]
    _run_skewed(jobs + sel_jobs, ATTN_AHEAD)

    for t in range(2):
        z = z_refs[t][0]
        for p in range(n_pairs):
            a = acc_ref[t, p]
            o = out_ref[t, p] + (gate(t, p, 1) / jnp.maximum(a[HEAD_DIM_A:HEAD_DIM_A + 1], TINY)) * a[0:HEAD_DIM_A]
            for h in (2 * p, 2 * p + 1):
                oh = o[:, (h % 2) * TQ:(h % 2 + 1) * TQ].T
                y_refs[t][0, :, h * LANES:(h + 1) * LANES] = (
                    oh * z[:, h * LANES:(h + 1) * LANES].astype(F32)).astype(y_refs[t].dtype)


def _mask_tables():
    lane = np.arange(TQ)[None, :]
    row = np.arange(WIN_A + TQ)[:, None]
    wtab = []
    for v in range(WIN_A // TQ + 1):
        t0 = v * TQ
        dpos = (t0 + lane) - (max(t0 - WIN_A, 0) + row)
        wtab.append(np.where((dpos >= 0) & (dpos < WIN_A), 0.0, NEG_INF))
    row = np.arange(SEL_TK)[:, None]
    ctab = [np.zeros((SEL_TK, TQ))]
    for odd in range(SEL_TK // TQ):
        ctab.append(np.where(row <= odd * TQ + lane, 0.0, NEG_INF))
    row = np.arange(WIN_B + TQ)[:, None]
    btab = []
    for v in range(WIN_B // TQ + 1):
        t0 = v * TQ
        dpos = (t0 + lane) - (max(t0 - WIN_B, 0) + row)
        btab.append(np.where((dpos >= 0) & (dpos < WIN_B), 0.0, NEG_INF))
    return [jnp.asarray(np.stack(t), F32) for t in (wtab, ctab, btab)]


def _attention(slabs, kc, vc, ov, sinks):
    B, S, _ = slabs[0].shape
    sl = lambda name: slabs[_slab(name)]
    n_cp = S // CMP_STRIDE
    n_qt = S // TQ
    assert n_cp == LANES and S // SEL_BLOCK == N_SELP and (n_qt // 2) // (SEL_TK // TQ) * 2 + 1 == _N_SEL_JOBS
    assert N_KV_A == N_KV_B and GROUP_A * HEAD_DIM_A == GROUP_B * HEAD_DIM_B
    wq = GROUP_A * HEAD_DIM_A
    cb = lambda name: _col(name) // LANES
    kv_spec = lambda name: pl.BlockSpec((1, S, LANES), lambda b, k, i, s, c=cb(name): (b, 0, c + k))
    lo = lambda width, c: pl.BlockSpec((1, TQ, width), lambda b, k, i, s: (b, i, c + k))
    hi = lambda width, c: pl.BlockSpec((1, TQ, width), lambda b, k, i, s: (b, n_qt - 1 - i, c + k))
    whole = lambda a: pl.BlockSpec(a.shape, lambda b, k, i, s: (0,) * a.ndim)
    y_sds = jax.ShapeDtypeStruct((B, S // 2, WIDTH_A), BF16)
    y_lo = pl.BlockSpec((1, TQ, wq), lambda b, k, i, s: (b, i, k))
    y_hi = pl.BlockSpec((1, TQ, wq), lambda b, k, i, s: (b, n_qt // 2 - 1 - i, k))
    wtab, ctab, btab = _mask_tables()
    grid_spec = pltpu.PrefetchScalarGridSpec(
        num_scalar_prefetch=1,
        grid=(B, N_KV_A, n_qt // 2),
        in_specs=[
            lo(wq, _col("qa") // wq), hi(wq, _col("qa") // wq),
            kv_spec("ksa"), kv_spec("vsa"), kv_spec("kwa"), kv_spec("vwa"),
            pl.BlockSpec((1, n_cp, LANES), lambda b, k, i, s: (k * B + b, 0, 0)),
            pl.BlockSpec((1, n_cp, LANES), lambda b, k, i, s: (k * B + b, 0, 0)),
            lo(LANES, cb("ga")), hi(LANES, cb("ga")),
            lo(wq, _col("za") // wq), hi(wq, _col("za") // wq),
            whole(ov), whole(wtab), whole(ctab),
            lo(wq, _col("qb") // wq), hi(wq, _col("qb") // wq),
            kv_spec("kb2"), kv_spec("vb2"),
            lo(wq, _col("zb") // wq), hi(wq, _col("zb") // wq),
            whole(btab),
        ],
        out_specs=(y_lo, y_hi, y_lo, y_hi),
        scratch_shapes=[
            pltpu.VMEM((S // LANES, _VROWS_A, LANES), BF16),
            pltpu.VMEM((S // LANES, _VROWS_A, LANES), BF16),
            pltpu.VMEM((S // LANES, _VROWS_B, LANES), BF16),
            pltpu.VMEM((2, GROUP_A // 2, PAIR, HEAD_DIM_A), BF16),
            pltpu.VMEM((2, N_SELP, SUBLANES, TQ), F32),
            pltpu.VMEM((2, GROUP_A // 2, _VROWS_A, PAIR), F32),
            pltpu.VMEM((2, GROUP_A // 2, HEAD_DIM_A, PAIR), F32),
        ],
    )
    return pl.pallas_call(
        _attn_kernel,
        grid_spec=grid_spec,
        out_shape=(y_sds, y_sds, y_sds, y_sds),
        compiler_params=pltpu.CompilerParams(
            dimension_semantics=("arbitrary", "arbitrary", "arbitrary"), vmem_limit_bytes=48 * 1024 * 1024),
        name="attn",
    )(sinks, sl("qa"), sl("qa"), sl("ksa"), sl("vsa"), sl("kwa"), sl("vwa"), kc, vc,
      sl("ga"), sl("ga"), sl("za"), sl("za"), ov, wtab, ctab,
      sl("qb"), sl("qb"), sl("kb2"), sl("vb2"), sl("zb"), sl("zb"), btab)


OUT_TM = 512


def _out_proj_kernel(tiles_per_half, yalo_ref, yahi_ref, yblo_ref, ybhi_ref, wa_ref, wb_ref, x_ref, g_ref, o_ref):
    in_lo = (pl.program_id(0) % (2 * tiles_per_half)) < tiles_per_half
    ya = jnp.where(in_lo, yalo_ref[...], yahi_ref[...])
    yb = jnp.where(in_lo, yblo_ref[...], ybhi_ref[...])
    r = x_ref[...] + _dot(ya, wa_ref[...]) + _dot(yb, wb_ref[...])
    ms = jnp.mean(r * r, axis=-1, keepdims=True)
    o_ref[...] = (r * lax.rsqrt(ms + RMS_EPS)) * g_ref[...]


def _out_proj(ya_lo, ya_hi, yb_lo, yb_hi, w_out, x2, g, S):
    M = x2.shape[0]
    tph = S // 2 // OUT_TM
    half = lambda i: (i // (2 * tph)) * tph
    t = lambda i: i % (2 * tph)
    lo_spec = lambda w: pl.BlockSpec((OUT_TM, w), lambda i: (half(i) + jnp.minimum(t(i), tph - 1), 0))
    hi_spec = lambda w: pl.BlockSpec((OUT_TM, w), lambda i: (half(i) + jnp.maximum(t(i) - tph, 0), 0))
    return pl.pallas_call(
        functools.partial(_out_proj_kernel, tph),
        grid=(M // OUT_TM,),
        in_specs=[
            lo_spec(WIDTH_A), hi_spec(WIDTH_A), lo_spec(WIDTH_B), hi_spec(WIDTH_B),
            pl.BlockSpec((WIDTH_A, D_MODEL), lambda i: (0, 0)),
            pl.BlockSpec((WIDTH_B, D_MODEL), lambda i: (WIDTH_A // WIDTH_B, 0)),
            pl.BlockSpec((OUT_TM, D_MODEL), lambda i: (i, 0)),
            pl.BlockSpec((1, D_MODEL), lambda i: (0, 0)),
        ],
        out_specs=pl.BlockSpec((OUT_TM, D_MODEL), lambda i: (i, 0)),
        out_shape=jax.ShapeDtypeStruct((M, D_MODEL), F32),
        compiler_params=pltpu.CompilerParams(
            dimension_semantics=("arbitrary",), vmem_limit_bytes=56 * 1024 * 1024),
        name="out_proj",
    )(ya_lo, ya_hi, yb_lo, yb_hi, w_out, w_out, x2, g)


def _overlap_matrix(n_cp):
    c_start = np.arange(n_cp) * CMP_STRIDE
    j_start = np.arange(N_SELP) * SEL_BLOCK
    ov = (c_start[None, :] < j_start[:, None] + SEL_BLOCK) & (c_start[None, :] + CMP_BLOCK > j_start[:, None])
    return jnp.asarray(ov, BF16)


def kernel(x, w_in, cmp_k_w1, cmp_k_w2, cmp_v_w1, cmp_v_w2, cmp_k_pos, cmp_v_pos, sinks, w_out, norm_g, final_g):
    B, S, D = x.shape
    assert D == D_MODEL and w_in.shape[0] == 1
    x2 = x.reshape(B * S, D)

    w_all = _prep_w_in(w_in)
    tabs = _rope_tables(S)
    slabs, strided = _in_proj(x2, norm_g[0].reshape(1, D), w_all, tabs, S)
    slabs = [p.reshape(B, S, IN_TN) for p in slabs]

    n_cp = S // CMP_STRIDE
    tk, tv = [t.reshape(N_KV_A * B * n_cp, _HALF) for t in strided]
    flat_pos = lambda p: jnp.broadcast_to(p.reshape(1, CMP_BLOCK * HEAD_DIM_A), (8, CMP_BLOCK * HEAD_DIM_A))
    kc, vc = _compress(tk, tv, cmp_k_w1[0].astype(BF16), cmp_v_w1[0].astype(BF16),
                       cmp_k_w2[0].astype(BF16), cmp_v_w2[0].astype(BF16),
                       flat_pos(cmp_k_pos[0]).astype(BF16), flat_pos(cmp_v_pos[0]).astype(BF16))
    kc, vc = [c.reshape(N_KV_A * B, n_cp, HEAD_DIM_A) for c in (kc, vc)]

    ys = _attention(slabs, kc, vc, _overlap_matrix(n_cp), sinks[0])
    out = _out_proj(*[y.reshape(B * S // 2, y.shape[2]) for y in ys], w_out[0].astype(BF16), x2,
                    final_g.reshape(1, D), S)
    return out.reshape(B, S, D)
```

```python
import functools
import math

import numpy as np
import jax
import jax.numpy as jnp
from jax import lax
from jax.experimental import pallas as pl
from jax.experimental.pallas import tpu as pltpu

F32 = jnp.float32
BF16 = jnp.bfloat16

D_MODEL = 2048
ROPE_THETA = 10000.0
RMS_EPS = 1e-6
NEG_INF = -1e30
TINY = 1e-30
LOG2E = math.log2(math.e)

WIDTH_A = 1024
HEAD_DIM_A = 128
N_KV_A = 2
GROUP_A = 4
KV_A = N_KV_A * HEAD_DIM_A
CMP_BLOCK = 32
CMP_STRIDE = 16
CMP_HIDDEN = 256
SEL_BLOCK = 64
SEL_TOPK = 16
WIN_A = 512
FORCE_SCORE = 1e4

WIDTH_B = 1024
HEAD_DIM_B = 64
N_HEADS_B = 16
N_KV_B = 2
GROUP_B = 8
KV_B = N_KV_B * HEAD_DIM_B
WIN_B = 128

IN_SIZES = (WIDTH_A, KV_A, KV_A, KV_A, KV_A, KV_A, KV_A, WIDTH_A, 3 * 8,
            WIDTH_B, KV_B, KV_B, WIDTH_B)

LANES = 128
SUBLANES = 8
CHUNK = 256

EP_NONE, EP_ROPE128, EP_ROPE64, EP_SILU, EP_SIGMOID, EP_ROPE128_Q, EP_ROPE64_Q = range(7)

_PROJ_LAYOUT = (
    ("qa", 8, EP_ROPE128_Q), ("kca", 2, EP_ROPE128), ("ksa", 2, EP_ROPE128), ("kwa", 2, EP_ROPE128),
    ("vca", 2, EP_NONE),
    ("vsa", 2, EP_NONE), ("vwa", 2, EP_NONE), ("za", 8, EP_SILU), ("kb2", 2, EP_ROPE64),
    ("ga", 1, EP_SIGMOID), ("vb", 1, EP_NONE),
    ("zb", 8, EP_SILU), ("qb", 8, EP_ROPE64_Q),
)
_UNIT_KINDS = tuple(k for _, n, k in _PROJ_LAYOUT for _ in range(n))
_UNIT_START = {}
_c = 0
for _name, _n, _k in _PROJ_LAYOUT:
    _UNIT_START[_name] = _c
    _c += _n
D_PROJ = _c * LANES
IN_TN = 2048
_IN_CPT = IN_TN // CHUNK
_IN_UPT = IN_TN // LANES
N_SLABS = D_PROJ // IN_TN


def _slab(name):
    return _UNIT_START[name] // _IN_UPT


def _col(name):
    return (_UNIT_START[name] % _IN_UPT) * LANES


def _dot(a, b):
    return jnp.dot(a, b, preferred_element_type=F32)


def _dot_nt(a, b):
    return lax.dot_general(a, b, (((1,), (1,)), ((), ())), preferred_element_type=F32)


IN_TM_NORM = 512
IN_TM = 1024
IN_SUB = 512

_TAB_GROUP = {EP_ROPE128: "r128", EP_ROPE128_Q: "r128q", EP_ROPE64: "r64", EP_ROPE64_Q: "r64q"}
_TAB_SIZE = {"r128": 2, "r128q": 2, "r64": 3, "r64q": 3}


def _slab_tab_groups(kinds):
    return tuple(dict.fromkeys(_TAB_GROUP[k] for k in kinds if k in _TAB_GROUP))


def _in_proj_kernel(kinds, with_norm, seq_len, strided_chunks, *refs):
    groups = _slab_tab_groups(kinds)
    refs = list(refs)
    if with_norm:
        x_ref, g_ref = refs.pop(0), refs.pop(0)
    else:
        h_ref = refs.pop(0)
    w_ref = refs.pop(0)
    tab_ref = refs.pop(0) if groups else None
    o_ref = refs.pop(0)
    if with_norm:
        h_ref = refs.pop(0)
        x = x_ref[...]
        ms = jnp.mean(x * x, axis=-1, keepdims=True)
        h_ref[...] = ((x * lax.rsqrt(ms + RMS_EPS)) * g_ref[...]).astype(BF16)
    t_refs = {c: refs.pop(0) for c in strided_chunks}
    if strided_chunks:
        stage_ref = refs.pop(0)
    tm = h_ref.shape[0]
    pos0 = (pl.program_id(0) % (seq_len // tm)) * tm
    base, n = {}, 0
    for grp in groups:
        base[grp] = n
        n += _TAB_SIZE[grp]

    def epilogue(kind, r, a):
        if kind == EP_NONE:
            return a
        if kind == EP_SILU:
            return a * jax.nn.sigmoid(a)
        if kind == EP_SIGMOID:
            return jax.nn.sigmoid(a)
        t = base[_TAB_GROUP[kind]]
        tab = lambda k: tab_ref[t + k, pl.ds(pl.multiple_of(pos0 + r * IN_SUB, IN_SUB), IN_SUB), :]
        if kind in (EP_ROPE128, EP_ROPE128_Q):
            return a * tab(0) + pltpu.roll(a, 64, 1) * tab(1)
        return a * tab(0) + pltpu.roll(a, 96, 1) * tab(1) + pltpu.roll(a, 32, 1) * tab(2)

    n_grp = IN_SUB // CMP_STRIDE

    subs = CHUNK // LANES

    def finish(r, c, acc):
        rows = slice(r * IN_SUB, (r + 1) * IN_SUB)
        for sub in range(subs):
            cols = slice(c * CHUNK + sub * LANES, c * CHUNK + (sub + 1) * LANES)
            val = epilogue(kinds[c * subs + sub], r, acc[:, sub * LANES:(sub + 1) * LANES])
            o_ref[rows, cols] = val.astype(o_ref.dtype)
            if c in t_refs:
                slot = strided_chunks.index(c) * subs + sub
                stage_ref[slot] = val
                for k in range(CMP_STRIDE):
                    t_refs[c][sub, r * n_grp:(r + 1) * n_grp, k * LANES:(k + 1) * LANES] = (
                        stage_ref[slot, pl.ds(k, n_grp, stride=CMP_STRIDE), :].astype(t_refs[c].dtype))

    jobs = []
    for r in range(tm // IN_SUB):
        for c in range(len(kinds) // subs):
            jobs.append((lambda r=r, c=c: _dot_nt(h_ref[r * IN_SUB:(r + 1) * IN_SUB, :],
                                                  w_ref[c * CHUNK:(c + 1) * CHUNK, :]),
                         functools.partial(finish, r, c)))
    _run_skewed(jobs)


def _rope_tables(S):
    def cs(d):
        inv = ROPE_THETA ** (-jnp.arange(0, d, 2, dtype=F32) / d)
        ang = jnp.arange(S, dtype=F32)[:, None] * inv[None, :]
        return jnp.cos(ang), jnp.sin(ang)

    c128, s128 = cs(HEAD_DIM_A)
    c64, s64 = cs(HEAD_DIM_B)
    z32 = jnp.zeros_like(s64)
    t128 = [jnp.concatenate([c128, c128], axis=1), jnp.concatenate([-s128, s128], axis=1)]
    t64 = [jnp.concatenate([c64, c64, c64, c64], axis=1),
           jnp.concatenate([-s64, z32, -s64, z32], axis=1),
           jnp.concatenate([z32, s64, z32, s64], axis=1)]
    qa = HEAD_DIM_A ** -0.5 * LOG2E
    qb = HEAD_DIM_B ** -0.5 * LOG2E
    return {"r128": t128, "r64": t64, "r128q": [t * qa for t in t128], "r64q": [t * qb for t in t64]}


W_PREP_TR = 256


def _w_in_pieces():
    offs = np.cumsum((0,) + IN_SIZES)
    names = ("qa", "kca", "vca", "ksa", "vsa", "kwa", "vwa", "za", "ga", "qb", "kb", "vb", "zb")
    start = {n: int(offs[i]) for i, n in enumerate(names)}
    width = dict(zip(names, IN_SIZES))
    n_g = 3 * GROUP_A
    halves = lambda n: [(start[n] + i * HEAD_DIM_B, HEAD_DIM_B) for i in (0, 0, 1, 1)]
    pieces = {n: [(start[n], width[n])]
              for n in ("qa", "kca", "ksa", "kwa", "vca", "vsa", "vwa", "za", "zb", "qb", "vb")}
    pad = LANES // N_KV_A - n_g
    pieces["ga"] = [(start["ga"], n_g), (None, pad), (start["ga"] + n_g, n_g), (None, pad)]
    pieces["kb2"] = halves("kb")
    return [p for name, _, _ in _PROJ_LAYOUT for p in pieces[name]]


def _w_prep_kernel(w_ref, o_ref):
    row = 0
    parts, filled = [], 0
    for src, wd in _w_in_pieces():
        done = 0
        while done < wd:
            take = min(wd - done, LANES - filled)
            if src is None:
                parts.append(jnp.zeros((take, W_PREP_TR), F32))
            else:
                parts.append(w_ref[src + done:src + done + take, :])
            done += take
            filled += take
            if filled == LANES:
                blk = parts[0] if len(parts) == 1 else jnp.concatenate(parts, axis=0)
                o_ref[row:row + LANES, :] = blk.astype(o_ref.dtype)
                row += LANES
                parts, filled = [], 0
    assert row == D_PROJ and not parts


def _prep_w_in(w_in):
    wt = w_in[0].T
    d_in = wt.shape[0]
    return pl.pallas_call(
        _w_prep_kernel,
        grid=(D_MODEL // W_PREP_TR,),
        in_specs=[pl.BlockSpec((d_in, W_PREP_TR), lambda i: (0, i))],
        out_specs=pl.BlockSpec((D_PROJ, W_PREP_TR), lambda i: (0, i)),
        out_shape=jax.ShapeDtypeStruct((D_PROJ, D_MODEL), BF16),
        compiler_params=pltpu.CompilerParams(dimension_semantics=("arbitrary",)),
        name="w_prep",
    )(wt)


def _in_proj(x2, g, w_all, tabs, S):
    M = x2.shape[0]
    params = pltpu.CompilerParams(dimension_semantics=("arbitrary",), vmem_limit_bytes=48 * 1024 * 1024)
    out_sds = jax.ShapeDtypeStruct((M, IN_TN), BF16)
    slabs, strided, h = [], {}, None
    for j in range(N_SLABS):
        kinds = _UNIT_KINDS[j * _IN_UPT:(j + 1) * _IN_UPT]
        with_norm = j == 0
        tm = IN_TM_NORM if with_norm else IN_TM
        groups = _slab_tab_groups(kinds)
        act_spec = pl.BlockSpec((tm, D_MODEL), lambda i: (i, 0))
        in_specs = [act_spec, pl.BlockSpec((1, D_MODEL), lambda i: (0, 0))] if with_norm else [act_spec]
        args = [x2, g] if with_norm else [h]
        in_specs.append(pl.BlockSpec((IN_TN, D_MODEL), lambda i, j=j: (j, 0)))
        args.append(w_all)
        if groups:
            tab = jnp.stack([t for grp in groups for t in tabs[grp]])
            in_specs.append(pl.BlockSpec(tab.shape, lambda i: (0, 0, 0)))
            args.append(tab)
        out_specs = [pl.BlockSpec((tm, IN_TN), lambda i: (i, 0))]
        out_shape = [out_sds]
        if with_norm:
            out_specs.append(act_spec)
            out_shape.append(jax.ShapeDtypeStruct((M, D_MODEL), BF16))
        strided_names = [name for name in ("kca", "vca") if _slab(name) == j]
        strided_chunks = tuple(_col(name) // CHUNK for name in strided_names)
        scratch = []
        for _ in strided_chunks:
            out_specs.append(pl.BlockSpec((N_KV_A, tm // CMP_STRIDE, _HALF), lambda i: (0, i, 0)))
            out_shape.append(jax.ShapeDtypeStruct((N_KV_A, M // CMP_STRIDE, _HALF), BF16))
        if strided_chunks:
            scratch.append(pltpu.VMEM((len(strided_chunks) * N_KV_A, IN_SUB, HEAD_DIM_A), F32))
        res = list(pl.pallas_call(
            functools.partial(_in_proj_kernel, kinds, with_norm, S, strided_chunks),
            grid=(M // tm,),
            in_specs=in_specs,
            out_specs=out_specs,
            out_shape=out_shape,
            scratch_shapes=scratch,
            compiler_params=params,
            name=f"in_proj_{j}",
        )(*args))
        slabs.append(res.pop(0))
        if with_norm:
            h = res.pop(0)
        for name in strided_names:
            strided[name] = res.pop(0)
    return slabs, strided


CMP_TM = 256
_HALF = CMP_STRIDE * HEAD_DIM_A


def _compress_kernel(tk_ref, tv_ref, w1k_ref, w1v_ref, w2k_ref, w2v_ref, posk_ref, posv_ref, kc_ref, vc_ref,
                     pb_ref):
    @pl.when(pl.program_id(0) == 0)
    def _():
        pb_ref[0] = _dot(posk_ref[...], w1k_ref[...])
        pb_ref[1] = _dot(posv_ref[...], w1v_ref[...])

    for n, (t_ref, w1_ref, w2_ref, o_ref) in enumerate(((tk_ref, w1k_ref, w2k_ref, kc_ref),
                                                        (tv_ref, w1v_ref, w2v_ref, vc_ref))):
        t = t_ref[...]
        u = _dot(t, w1_ref[0:_HALF, :])
        v = _dot(t, w1_ref[_HALF:2 * _HALF, :])
        pre = u + pltpu.roll(v, CMP_TM - 1, 0) + pb_ref[n, 0:1]
        hid = pre * jax.nn.sigmoid(pre)
        out = _dot(hid.astype(BF16), w2_ref[...])
        row = lax.broadcasted_iota(jnp.int32, out.shape, 0)
        n_cp = LANES
        out = jnp.where((row & (n_cp - 1)) == n_cp - 1, 0.0, out)
        o_ref[...] = out.astype(o_ref.dtype)


def _compress(tk, tv, w1k, w1v, w2k, w2v, posk, posv):
    R = tk.shape[0]
    rows = pl.BlockSpec((CMP_TM, _HALF), lambda i: (i, 0))
    whole = lambda a: pl.BlockSpec(a.shape, lambda i: (0, 0))
    o_spec = pl.BlockSpec((CMP_TM, HEAD_DIM_A), lambda i: (i, 0))
    o_sds = jax.ShapeDtypeStruct((R, HEAD_DIM_A), BF16)
    return pl.pallas_call(
        _compress_kernel,
        grid=(R // CMP_TM,),
        in_specs=[rows, rows, whole(w1k), whole(w1v), whole(w2k), whole(w2v), whole(posk), whole(posv)],
        out_specs=(o_spec, o_spec),
        out_shape=(o_sds, o_sds),
        scratch_shapes=[pltpu.VMEM((2, SUBLANES, CMP_HIDDEN), F32)],
        compiler_params=pltpu.CompilerParams(dimension_semantics=("arbitrary",)),
        name="compress",
    )(tk, tv, w1k, w1v, w2k, w2v, posk, posv)


TQ = 128
ATTN_AHEAD = 4
ONES_ROWS = 16


def _transpose_values(v_ref, vt_ref, n_rows, second_half=None):
    n_blk = vt_ref.shape[0]
    for blk in range(n_blk):
        vt = v_ref[0, blk * LANES:(blk + 1) * LANES, :].astype(F32).T
        if second_half is not None:
            vt = jnp.where(second_half == 0, vt[0:n_rows], vt[n_rows:2 * n_rows])
        vt_ref[blk, 0:n_rows, :] = vt[0:n_rows].astype(vt_ref.dtype)
        vt_ref[blk, n_rows:n_rows + ONES_ROWS, :] = jnp.ones((ONES_ROWS, LANES), vt_ref.dtype)


def _run_skewed(jobs, ahead=1):
    pending = [job[0]() for job in jobs[:ahead]]
    for i, (_, finish) in enumerate(jobs):
        if i + ahead < len(jobs):
            pending.append(jobs[i + ahead][0]())
        finish(pending[i])
        pending[i] = None


def _softmax_step(s, m, acc, vt):
    m_new = jnp.maximum(m, jnp.max(s, axis=0, keepdims=True))
    p = jnp.exp2(s - m_new).astype(BF16)
    return m_new, jnp.exp2(m - m_new) * acc + _dot(vt, p)


SEL_TK = 256
N_SELP = 32
PAIR = 2 * TQ
_VROWS_A = HEAD_DIM_A + ONES_ROWS
_VROWS_B = HEAD_DIM_B + ONES_ROWS
_WIN_TILES = (256, 256, 128)
_N_SEL_JOBS = 9


def _attn_kernel(sink_ref, qlo_ref, qhi_ref, ksel_ref, vsel_ref, kwin_ref, vwin_ref, kc_ref, vc_ref,
                 glo_ref, ghi_ref, zlo_ref, zhi_ref, ov_ref, wtab_ref, ctab_ref,
                 qblo_ref, qbhi_ref, kb_ref, vb_ref, zblo_ref, zbhi_ref, btab_ref,
                 ylo_ref, yhi_ref, yblo_ref, ybhi_ref,
                 vselt_ref, vwint_ref, vbt_ref, qs_ref, bias_ref, acc_ref, out_ref):
    kv = pl.program_id(1)
    i = pl.program_id(2)
    n_qt = 2 * pl.num_programs(2)
    n_pairs = GROUP_A // 2
    t0s = (i * TQ, (n_qt - 1 - i) * TQ)
    q_refs, g_refs, z_refs, y_refs = (qlo_ref, qhi_ref), (glo_ref, ghi_ref), (zlo_ref, zhi_ref), (ylo_ref, yhi_ref)
    qb_refs, zb_refs, yb_refs = (qblo_ref, qbhi_ref), (zblo_ref, zbhi_ref), (yblo_ref, ybhi_ref)

    @pl.when(i == 0)
    def _():
        _transpose_values(vsel_ref, vselt_ref, HEAD_DIM_A)
        _transpose_values(vwin_ref, vwint_ref, HEAD_DIM_A)
        _transpose_values(vb_ref, vbt_ref, HEAD_DIM_B, kv)

    lane = lax.broadcasted_iota(jnp.int32, (1, TQ), 1)
    lane2 = jnp.concatenate([lane, lane], axis=1)
    qpair, gts = [], []
    for t in range(2):
        q = q_refs[t][0]
        pairs = [jnp.concatenate([q[:, h * LANES:(h + 1) * LANES] for h in (2 * p, 2 * p + 1)], axis=0)
                 for p in range(n_pairs)]
        for p in range(n_pairs):
            qs_ref[t, p] = pairs[p]
        qpair.append(pairs)
        gt = g_refs[t][0].astype(F32).T
        gts.append(jnp.where(kv == 0, gt[0:LANES // 2], gt[LANES // 2:LANES]))

    def gate(t, p, branch):
        return jnp.concatenate(
            [gts[t][3 * h + branch:3 * h + branch + 1] for h in (2 * p, 2 * p + 1)], axis=1)

    kc = kc_ref[0]
    vct = vc_ref[0].astype(F32).T.astype(BF16)
    cend = lax.broadcasted_iota(jnp.int32, (LANES, PAIR), 0) * CMP_STRIDE + (CMP_BLOCK - 1)
    psums = [[], []]

    def cmp_scores(t, p):
        return jnp.where(cend <= t0s[t] + lane2, _dot_nt(kc, qpair[t][p]), NEG_INF)

    def cmp_finish(t, p, s):
        m = jnp.max(s, axis=0, keepdims=True)
        e = jnp.where(cend <= t0s[t] + lane2, jnp.exp2(s - m), 0.0)
        pc = e / jnp.maximum(jnp.sum(e, axis=0, keepdims=True), TINY)
        psums[t].append(pc[:, 0:TQ] + pc[:, TQ:PAIR])
        out_ref[t, p] = gate(t, p, 0) * _dot(vct, pc.astype(BF16))

    ksts = [pl.multiple_of(jnp.maximum(t0 - WIN_A, 0), TQ) for t0 in t0s]
    win_m = [[jnp.full((1, PAIR), NEG_INF, F32) for _ in range(n_pairs)] for _ in range(2)]
    win_acc = [[jnp.zeros((_VROWS_A, PAIR), F32) for _ in range(n_pairs)] for _ in range(2)]

    win_var = [jnp.minimum(t0 // TQ, WIN_A // TQ) for t0 in t0s]

    def win_scores(t, p, off, tk):
        kt = kwin_ref[0, pl.ds(pl.multiple_of(ksts[t] + off, LANES), tk), :]
        bias = wtab_ref[win_var[t], off:off + tk, :]
        return _dot_nt(kt, qpair[t][p]) + jnp.concatenate([bias, bias], axis=1)

    def win_finish(t, p, off, tk, s):
        vt = jnp.concatenate(
            [vwint_ref[ksts[t] // LANES + off // LANES + b] for b in range(tk // LANES)], axis=1)
        win_m[t][p], win_acc[t][p] = _softmax_step(s, win_m[t][p], win_acc[t][p], vt)
        if off + tk == sum(_WIN_TILES):
            a = win_acc[t][p]
            out_ref[t, p] = out_ref[t, p] + (
                gate(t, p, 2) / jnp.maximum(a[HEAD_DIM_A:HEAD_DIM_A + 1], TINY)) * a[0:HEAD_DIM_A]

    jobs = [(functools.partial(cmp_scores, t, p), functools.partial(cmp_finish, t, p))
            for t in range(2) for p in range(n_pairs)]
    off = 0
    for tk in _WIN_TILES:
        jobs += [(functools.partial(win_scores, t, p, off, tk), functools.partial(win_finish, t, p, off, tk))
                 for t in range(2) for p in range(n_pairs)]
        off += tk

    nb_keys = WIN_B + TQ
    left = lax.broadcasted_iota(jnp.int32, (TQ, LANES), 1) < HEAD_DIM_B
    zero = jnp.zeros((TQ, LANES), qblo_ref.dtype)
    kbs = [pl.multiple_of(jnp.maximum(t0 - WIN_B, 0), TQ) for t0 in t0s]
    b_var = [jnp.minimum(t0 // TQ, WIN_B // TQ) for t0 in t0s]

    def b_sink(c):
        return jnp.concatenate(
            [jnp.full((1, TQ), sink_ref[kv * GROUP_B + 2 * c + h] * LOG2E, F32) for h in range(2)], axis=1)

    def b_scores(t, c):
        pair = qb_refs[t][0, :, c * LANES:(c + 1) * LANES]
        qp = jnp.concatenate([jnp.where(left, pair, zero), jnp.where(left, zero, pair)], axis=0)
        bias = btab_ref[b_var[t]]
        return _dot_nt(kb_ref[0, pl.ds(kbs[t], nb_keys), :], qp) + jnp.concatenate([bias, bias], axis=1)

    def b_finish(t, c, s):
        vt = jnp.concatenate([vbt_ref[kbs[t] // LANES + b] for b in range(nb_keys // LANES)], axis=1)
        sink = b_sink(c)
        m = jnp.maximum(jnp.max(s, axis=0, keepdims=True), sink)
        p = jnp.exp2(s - m).astype(BF16)
        a = _dot(vt, p)
        den = jnp.maximum(a[HEAD_DIM_B:HEAD_DIM_B + 1] + jnp.exp2(sink - m), TINY)
        o = a[0:HEAD_DIM_B] / den
        o2 = jnp.concatenate([o[:, 0:TQ], o[:, TQ:PAIR]], axis=0)
        z = zb_refs[t][0, :, c * LANES:(c + 1) * LANES]
        yb_refs[t][0, :, c * LANES:(c + 1) * LANES] = (o2.T * z.astype(F32)).astype(yb_refs[t].dtype)

    b_jobs = [[(functools.partial(b_scores, t, c), functools.partial(b_finish, t, c))
               for c in range(GROUP_B // 2)] for t in range(2)]

    def select_blocks(t):
        ov = ov_ref[...]
        jidx = lax.broadcasted_iota(jnp.int32, (N_SELP, TQ), 0)
        psum = psums[t][0] + psums[t][1]
        p_hi = psum.astype(BF16)
        r1 = psum - p_hi.astype(F32)
        p_mid = r1.astype(BF16)
        p_lo = (r1 - p_mid.astype(F32)).astype(BF16)
        psel = _dot(ov, p_hi) + _dot(ov, p_mid) + _dot(ov, p_lo)
        tq = t0s[t] + lane
        cur = tq // SEL_BLOCK
        forced = (jidx == 0) | (jidx == cur) | (jidx == cur - 1)
        score = jnp.where(forced, FORCE_SCORE, jnp.where(jidx * SEL_BLOCK <= tq, psel, -1.0))
        rank = jnp.zeros(psel.shape, F32)
        for r in range(N_SELP):
            row = jnp.broadcast_to(score[r:r + 1, :], score.shape)
            rank = rank + jnp.where(jidx > r, jnp.where(row >= score, 1.0, 0.0), jnp.where(row > score, 1.0, 0.0))
        bias_t = jnp.where(rank < SEL_TOPK, 0.0, NEG_INF)
        for j in range(N_SELP):
            bias_ref[t, j] = jnp.broadcast_to(bias_t[j:j + 1, :], (SUBLANES, TQ))

    blk_per_tile = SEL_TK // SEL_BLOCK
    rep = SEL_BLOCK // SUBLANES
    n_lo = i // (SEL_TK // TQ) + 1
    acc_ref[...] = jnp.zeros(acc_ref.shape, F32)
    ms = [[jnp.full((1, PAIR), NEG_INF, F32) for _ in range(n_pairs)] for _ in range(2)]
    ctx = {}

    def sel_scores(k, p):
        if p == 0:
            if k == 0:
                select_blocks(0)
                select_blocks(1)
            hi = k >= n_lo
            t = hi.astype(jnp.int32)
            j = jnp.where(hi, k - n_lo, k)
            ks = pl.multiple_of(j * SEL_TK, SEL_TK)
            bt = jnp.concatenate(
                [bias_ref[t, j * blk_per_tile + b] for b in range(blk_per_tile) for _ in range(rep)], axis=0)
            qt = jnp.where(hi, n_qt - 1 - i, i)
            diag = j == qt // (SEL_TK // TQ)
            bt = bt + ctab_ref[jnp.where(diag, 1 + qt % (SEL_TK // TQ), 0)]
            bt = jnp.concatenate([bt, bt], axis=1)
            ctx[k] = (hi, t, j, ks, bt)
        hi, t, j, ks, bt = ctx[k]
        return _dot_nt(ksel_ref[0, pl.ds(ks, SEL_TK), :], qs_ref[t, p]) + bt

    def sel_finish(k, p, s):
        hi, t, j, ks, bt = ctx[k]
        vt = jnp.concatenate([vselt_ref[j * (SEL_TK // LANES) + b] for b in range(SEL_TK // LANES)], axis=1)
        m_new, acc_new = _softmax_step(s, jnp.where(hi, ms[1][p], ms[0][p]), acc_ref[t, p], vt)
        acc_ref[t, p] = acc_new
        ms[0][p] = jnp.where(hi, ms[0][p], m_new)
        ms[1][p] = jnp.where(hi, m_new, ms[1][p])

    sel_jobs = [(functools.partial(sel_scores, k, p), functools.partial(sel_finish, k, p))
                for k in range(_N_SEL_JOBS) for p in range(n_pairs)]
    _run_skewed(jobs + b_jobs[0] + sel_jobs + b_jobs[1], ATTN_AHEAD)

    for t in range(2):
        z = z_refs[t][0]
        for p in range(n_pairs):
            a = acc_ref[t, p]
            o = out_ref[t, p] + (gate(t, p, 1) / jnp.maximum(a[HEAD_DIM_A:HEAD_DIM_A + 1], TINY)) * a[0:HEAD_DIM_A]
            for h in (2 * p, 2 * p + 1):
                oh = o[:, (h % 2) * TQ:(h % 2 + 1) * TQ].T
                y_refs[t][0, :, h * LANES:(h + 1) * LANES] = (
                    oh * z[:, h * LANES:(h + 1) * LANES].astype(F32)).astype(y_refs[t].dtype)


def _mask_tables():
    lane = np.arange(TQ)[None, :]
    row = np.arange(WIN_A + TQ)[:, None]
    wtab = []
    for v in range(WIN_A // TQ + 1):
        t0 = v * TQ
        dpos = (t0 + lane) - (max(t0 - WIN_A, 0) + row)
        wtab.append(np.where((dpos >= 0) & (dpos < WIN_A), 0.0, NEG_INF))
    row = np.arange(SEL_TK)[:, None]
    ctab = [np.zeros((SEL_TK, TQ))]
    for odd in range(SEL_TK // TQ):
        ctab.append(np.where(row <= odd * TQ + lane, 0.0, NEG_INF))
    row = np.arange(WIN_B + TQ)[:, None]
    btab = []
    for v in range(WIN_B // TQ + 1):
        t0 = v * TQ
        dpos = (t0 + lane) - (max(t0 - WIN_B, 0) + row)
        btab.append(np.where((dpos >= 0) & (dpos < WIN_B), 0.0, NEG_INF))
    return [jnp.asarray(np.stack(t), F32) for t in (wtab, ctab, btab)]


def _attention(slabs, kc, vc, ov, sinks):
    B, S, _ = slabs[0].shape
    sl = lambda name: slabs[_slab(name)]
    n_cp = S // CMP_STRIDE
    n_qt = S // TQ
    assert n_cp == LANES and S // SEL_BLOCK == N_SELP and (n_qt // 2) // (SEL_TK // TQ) * 2 + 1 == _N_SEL_JOBS
    assert N_KV_A == N_KV_B and GROUP_A * HEAD_DIM_A == GROUP_B * HEAD_DIM_B
    wq = GROUP_A * HEAD_DIM_A
    cb = lambda name: _col(name) // LANES
    kv_spec = lambda name: pl.BlockSpec((1, S, LANES), lambda b, k, i, s, c=cb(name): (b, 0, c + k))
    lo = lambda width, c: pl.BlockSpec((1, TQ, width), lambda b, k, i, s: (b, i, c + k))
    hi = lambda width, c: pl.BlockSpec((1, TQ, width), lambda b, k, i, s: (b, n_qt - 1 - i, c + k))
    whole = lambda a: pl.BlockSpec(a.shape, lambda b, k, i, s: (0,) * a.ndim)
    y_sds = jax.ShapeDtypeStruct((B, S // 2, WIDTH_A), BF16)
    y_lo = pl.BlockSpec((1, TQ, wq), lambda b, k, i, s: (b, i, k))
    y_hi = pl.BlockSpec((1, TQ, wq), lambda b, k, i, s: (b, n_qt // 2 - 1 - i, k))
    wtab, ctab, btab = _mask_tables()
    grid_spec = pltpu.PrefetchScalarGridSpec(
        num_scalar_prefetch=1,
        grid=(B, N_KV_A, n_qt // 2),
        in_specs=[
            lo(wq, _col("qa") // wq), hi(wq, _col("qa") // wq),
            kv_spec("ksa"), kv_spec("vsa"), kv_spec("kwa"), kv_spec("vwa"),
            pl.BlockSpec((1, n_cp, LANES), lambda b, k, i, s: (k * B + b, 0, 0)),
            pl.BlockSpec((1, n_cp, LANES), lambda b, k, i, s: (k * B + b, 0, 0)),
            pl.BlockSpec((1, TQ, LANES), lambda b, k, i, s: (b, i, cb("ga"))),
            pl.BlockSpec((1, TQ, LANES), lambda b, k, i, s: (b, n_qt - 1 - i, cb("ga"))),
            lo(wq, _col("za") // wq), hi(wq, _col("za") // wq),
            whole(ov), whole(wtab), whole(ctab),
            lo(wq, _col("qb") // wq), hi(wq, _col("qb") // wq),
            kv_spec("kb2"), pl.BlockSpec((1, S, LANES), lambda b, k, i, s: (b, 0, cb("vb"))),
            lo(wq, _col("zb") // wq), hi(wq, _col("zb") // wq),
            whole(btab),
        ],
        out_specs=(y_lo, y_hi, y_lo, y_hi),
        scratch_shapes=[
            pltpu.VMEM((S // LANES, _VROWS_A, LANES), BF16),
            pltpu.VMEM((S // LANES, _VROWS_A, LANES), BF16),
            pltpu.VMEM((S // LANES, _VROWS_B, LANES), BF16),
            pltpu.VMEM((2, GROUP_A // 2, PAIR, HEAD_DIM_A), BF16),
            pltpu.VMEM((2, N_SELP, SUBLANES, TQ), F32),
            pltpu.VMEM((2, GROUP_A // 2, _VROWS_A, PAIR), F32),
            pltpu.VMEM((2, GROUP_A // 2, HEAD_DIM_A, PAIR), F32),
        ],
    )
    return pl.pallas_call(
        _attn_kernel,
        grid_spec=grid_spec,
        out_shape=(y_sds, y_sds, y_sds, y_sds),
        compiler_params=pltpu.CompilerParams(
            dimension_semantics=("arbitrary", "arbitrary", "arbitrary"), vmem_limit_bytes=48 * 1024 * 1024),
        name="attn",
    )(sinks, sl("qa"), sl("qa"), sl("ksa"), sl("vsa"), sl("kwa"), sl("vwa"), kc, vc,
      sl("ga"), sl("ga"), sl("za"), sl("za"), ov, wtab, ctab,
      sl("qb"), sl("qb"), sl("kb2"), sl("vb"), sl("zb"), sl("zb"), btab)


OUT_TM = 512


def _out_proj_kernel(tiles_per_half, yalo_ref, yahi_ref, yblo_ref, ybhi_ref, wa_ref, wb_ref, x_ref, g_ref, o_ref):
    in_lo = (pl.program_id(0) % (2 * tiles_per_half)) < tiles_per_half
    ya = jnp.where(in_lo, yalo_ref[...], yahi_ref[...])
    yb = jnp.where(in_lo, yblo_ref[...], ybhi_ref[...])
    r = x_ref[...] + _dot(ya, wa_ref[...]) + _dot(yb, wb_ref[...])
    ms = jnp.mean(r * r, axis=-1, keepdims=True)
    o_ref[...] = (r * lax.rsqrt(ms + RMS_EPS)) * g_ref[...]


def _out_proj(ya_lo, ya_hi, yb_lo, yb_hi, w_out, x2, g, S):
    M = x2.shape[0]
    tph = S // 2 // OUT_TM
    half = lambda i: (i // (2 * tph)) * tph
    t = lambda i: i % (2 * tph)
    lo_spec = lambda w: pl.BlockSpec((OUT_TM, w), lambda i: (half(i) + jnp.minimum(t(i), tph - 1), 0))
    hi_spec = lambda w: pl.BlockSpec((OUT_TM, w), lambda i: (half(i) + jnp.maximum(t(i) - tph, 0), 0))
    return pl.pallas_call(
        functools.partial(_out_proj_kernel, tph),
        grid=(M // OUT_TM,),
        in_specs=[
            lo_spec(WIDTH_A), hi_spec(WIDTH_A), lo_spec(WIDTH_B), hi_spec(WIDTH_B),
            pl.BlockSpec((WIDTH_A, D_MODEL), lambda i: (0, 0)),
            pl.BlockSpec((WIDTH_B, D_MODEL), lambda i: (WIDTH_A // WIDTH_B, 0)),
            pl.BlockSpec((OUT_TM, D_MODEL), lambda i: (i, 0)),
            pl.BlockSpec((1, D_MODEL), lambda i: (0, 0)),
        ],
        out_specs=pl.BlockSpec((OUT_TM, D_MODEL), lambda i: (i, 0)),
        out_shape=jax.ShapeDtypeStruct((M, D_MODEL), F32),
        compiler_params=pltpu.CompilerParams(
            dimension_semantics=("arbitrary",), vmem_limit_bytes=56 * 1024 * 1024),
        name="out_proj",
    )(ya_lo, ya_hi, yb_lo, yb_hi, w_out, w_out, x2, g)


def _overlap_matrix(n_cp):
    c_start = np.arange(n_cp) * CMP_STRIDE
    j_start = np.arange(N_SELP) * SEL_BLOCK
    ov = (c_start[None, :] < j_start[:, None] + SEL_BLOCK) & (c_start[None, :] + CMP_BLOCK > j_start[:, None])
    return jnp.asarray(ov, BF16)


def kernel(x, w_in, cmp_k_w1, cmp_k_w2, cmp_v_w1, cmp_v_w2, cmp_k_pos, cmp_v_pos, sinks, w_out, norm_g, final_g):
    B, S, D = x.shape
    assert D == D_MODEL and w_in.shape[0] == 1
    x2 = x.reshape(B * S, D)

    w_all = _prep_w_in(w_in)
    tabs = _rope_tables(S)
    slabs, strided = _in_proj(x2, norm_g[0].reshape(1, D), w_all, tabs, S)
    slabs = [p.reshape(B, S, IN_TN) for p in slabs]

    n_cp = S // CMP_STRIDE
    tk, tv = [strided[n].reshape(N_KV_A * B * n_cp, _HALF) for n in ("kca", "vca")]
    flat_pos = lambda p: jnp.broadcast_to(p.reshape(1, CMP_BLOCK * HEAD_DIM_A), (8, CMP_BLOCK * HEAD_DIM_A))
    kc, vc = _compress(tk, tv, cmp_k_w1[0].astype(BF16), cmp_v_w1[0].astype(BF16),
                       cmp_k_w2[0].astype(BF16), cmp_v_w2[0].astype(BF16),
                       flat_pos(cmp_k_pos[0]).astype(BF16), flat_pos(cmp_v_pos[0]).astype(BF16))
    kc, vc = [c.reshape(N_KV_A * B, n_cp, HEAD_DIM_A) for c in (kc, vc)]

    ys = _attention(slabs, kc, vc, _overlap_matrix(n_cp), sinks[0])
    out = _out_proj(*[y.reshape(B * S // 2, y.shape[2]) for y in ys], w_out[0].astype(BF16), x2,
                    final_g.reshape(1, D), S)
    return out.reshape(B, S, D)
```

```python
import functools
import math

import numpy as np
import jax
import jax.numpy as jnp
from jax import lax
from jax.experimental import pallas as pl
from jax.experimental.pallas import tpu as pltpu

F32 = jnp.float32
BF16 = jnp.bfloat16

D_MODEL = 2048
ROPE_THETA = 10000.0
RMS_EPS = 1e-6
NEG_INF = -1e30
TINY = 1e-30
LOG2E = math.log2(math.e)

WIDTH_A = 1024
HEAD_DIM_A = 128
N_KV_A = 2
GROUP_A = 4
KV_A = N_KV_A * HEAD_DIM_A
CMP_BLOCK = 32
CMP_STRIDE = 16
CMP_HIDDEN = 256
SEL_BLOCK = 64
SEL_TOPK = 16
WIN_A = 512
FORCE_SCORE = 1e4

WIDTH_B = 1024
HEAD_DIM_B = 64
N_HEADS_B = 16
N_KV_B = 2
GROUP_B = 8
KV_B = N_KV_B * HEAD_DIM_B
WIN_B = 128

IN_SIZES = (WIDTH_A, KV_A, KV_A, KV_A, KV_A, KV_A, KV_A, WIDTH_A, 3 * 8,
            WIDTH_B, KV_B, KV_B, WIDTH_B)

LANES = 128
SUBLANES = 8
CHUNK = 256

EP_NONE, EP_ROPE128, EP_ROPE64, EP_SILU, EP_SIGMOID, EP_ROPE128_Q, EP_ROPE64_Q = range(7)

_PROJ_LAYOUT = (
    ("qa", 8, EP_ROPE128_Q), ("kca", 2, EP_ROPE128), ("ksa", 2, EP_ROPE128), ("kwa", 2, EP_ROPE128),
    ("vca", 2, EP_NONE),
    ("vsa", 2, EP_NONE), ("vwa", 2, EP_NONE), ("za", 8, EP_SILU), ("kb2", 2, EP_ROPE64),
    ("ga", 1, EP_SIGMOID), ("vb", 1, EP_NONE),
    ("zb", 8, EP_SILU), ("qb", 8, EP_ROPE64_Q),
)
_UNIT_KINDS = tuple(k for _, n, k in _PROJ_LAYOUT for _ in range(n))
_UNIT_START = {}
_c = 0
for _name, _n, _k in _PROJ_LAYOUT:
    _UNIT_START[_name] = _c
    _c += _n
D_PROJ = _c * LANES
IN_TN = 2048
_IN_CPT = IN_TN // CHUNK
_IN_UPT = IN_TN // LANES
N_SLABS = D_PROJ // IN_TN


def _slab(name):
    return _UNIT_START[name] // _IN_UPT


def _col(name):
    return (_UNIT_START[name] % _IN_UPT) * LANES


def _dot(a, b):
    return jnp.dot(a, b, preferred_element_type=F32)


def _dot_nt(a, b):
    return lax.dot_general(a, b, (((1,), (1,)), ((), ())), preferred_element_type=F32)


IN_TM_NORM = 512
IN_TM = 1024
IN_SUB = 512

_TAB_GROUP = {EP_ROPE128: "r128", EP_ROPE128_Q: "r128q", EP_ROPE64: "r64", EP_ROPE64_Q: "r64q"}
_TAB_SIZE = {"r128": 2, "r128q": 2, "r64": 3, "r64q": 3}


def _slab_tab_groups(kinds):
    return tuple(dict.fromkeys(_TAB_GROUP[k] for k in kinds if k in _TAB_GROUP))


def _in_proj_kernel(kinds, with_norm, seq_len, strided_chunks, *refs):
    groups = _slab_tab_groups(kinds)
    refs = list(refs)
    if with_norm:
        x_ref, g_ref = refs.pop(0), refs.pop(0)
    else:
        h_ref = refs.pop(0)
    w_ref = refs.pop(0)
    tab_ref = refs.pop(0) if groups else None
    o_ref = refs.pop(0)
    if with_norm:
        h_ref = refs.pop(0)
        x = x_ref[...]
        ms = jnp.mean(x * x, axis=-1, keepdims=True)
        h_ref[...] = ((x * lax.rsqrt(ms + RMS_EPS)) * g_ref[...]).astype(BF16)
    t_refs = {c: refs.pop(0) for c in strided_chunks}
    if strided_chunks:
        stage_ref = refs.pop(0)
    tm = h_ref.shape[0]
    pos0 = (pl.program_id(0) % (seq_len // tm)) * tm
    base, n = {}, 0
    for grp in groups:
        base[grp] = n
        n += _TAB_SIZE[grp]

    def epilogue(kind, r, a):
        if kind == EP_NONE:
            return a
        if kind == EP_SILU:
            return a * jax.nn.sigmoid(a)
        if kind == EP_SIGMOID:
            return jax.nn.sigmoid(a)
        t = base[_TAB_GROUP[kind]]
        tab = lambda k: tab_ref[t + k, pl.ds(pl.multiple_of(pos0 + r * IN_SUB, IN_SUB), IN_SUB), :]
        if kind in (EP_ROPE128, EP_ROPE128_Q):
            return a * tab(0) + pltpu.roll(a, 64, 1) * tab(1)
        return a * tab(0) + pltpu.roll(a, 96, 1) * tab(1) + pltpu.roll(a, 32, 1) * tab(2)

    n_grp = IN_SUB // CMP_STRIDE

    subs = CHUNK // LANES

    def finish(r, c, acc):
        rows = slice(r * IN_SUB, (r + 1) * IN_SUB)
        for sub in range(subs):
            cols = slice(c * CHUNK + sub * LANES, c * CHUNK + (sub + 1) * LANES)
            val = epilogue(kinds[c * subs + sub], r, acc[:, sub * LANES:(sub + 1) * LANES])
            o_ref[rows, cols] = val.astype(o_ref.dtype)
            if c in t_refs:
                slot = strided_chunks.index(c) * subs + sub
                stage_ref[slot] = val
                for k in range(CMP_STRIDE):
                    t_refs[c][sub, r * n_grp:(r + 1) * n_grp, k * LANES:(k + 1) * LANES] = (
                        stage_ref[slot, pl.ds(k, n_grp, stride=CMP_STRIDE), :].astype(t_refs[c].dtype))

    jobs = []
    for r in range(tm // IN_SUB):
        for c in range(len(kinds) // subs):
            jobs.append((lambda r=r, c=c: _dot_nt(h_ref[r * IN_SUB:(r + 1) * IN_SUB, :],
                                                  w_ref[c * CHUNK:(c + 1) * CHUNK, :]),
                         functools.partial(finish, r, c)))
    _run_skewed(jobs)


def _rope_tables(S):
    def cs(d):
        inv = ROPE_THETA ** (-jnp.arange(0, d, 2, dtype=F32) / d)
        ang = jnp.arange(S, dtype=F32)[:, None] * inv[None, :]
        return jnp.cos(ang), jnp.sin(ang)

    c128, s128 = cs(HEAD_DIM_A)
    c64, s64 = cs(HEAD_DIM_B)
    z32 = jnp.zeros_like(s64)
    t128 = [jnp.concatenate([c128, c128], axis=1), jnp.concatenate([-s128, s128], axis=1)]
    t64 = [jnp.concatenate([c64, c64, c64, c64], axis=1),
           jnp.concatenate([-s64, z32, -s64, z32], axis=1),
           jnp.concatenate([z32, s64, z32, s64], axis=1)]
    qa = HEAD_DIM_A ** -0.5 * LOG2E
    qb = HEAD_DIM_B ** -0.5 * LOG2E
    return {"r128": t128, "r64": t64, "r128q": [t * qa for t in t128], "r64q": [t * qb for t in t64]}


W_PREP_TR = 256


def _w_in_pieces():
    offs = np.cumsum((0,) + IN_SIZES)
    names = ("qa", "kca", "vca", "ksa", "vsa", "kwa", "vwa", "za", "ga", "qb", "kb", "vb", "zb")
    start = {n: int(offs[i]) for i, n in enumerate(names)}
    width = dict(zip(names, IN_SIZES))
    n_g = 3 * GROUP_A
    halves = lambda n: [(start[n] + i * HEAD_DIM_B, HEAD_DIM_B) for i in (0, 0, 1, 1)]
    pieces = {n: [(start[n], width[n])]
              for n in ("qa", "kca", "ksa", "kwa", "vca", "vsa", "vwa", "za", "zb", "qb", "vb")}
    pad = LANES // N_KV_A - n_g
    pieces["ga"] = [(start["ga"], n_g), (None, pad), (start["ga"] + n_g, n_g), (None, pad)]
    pieces["kb2"] = halves("kb")
    return [p for name, _, _ in _PROJ_LAYOUT for p in pieces[name]]


def _w_prep_kernel(w_ref, o_ref):
    row = 0
    parts, filled = [], 0
    for src, wd in _w_in_pieces():
        done = 0
        while done < wd:
            take = min(wd - done, LANES - filled)
            if src is None:
                parts.append(jnp.zeros((take, W_PREP_TR), F32))
            else:
                parts.append(w_ref[src + done:src + done + take, :])
            done += take
            filled += take
            if filled == LANES:
                blk = parts[0] if len(parts) == 1 else jnp.concatenate(parts, axis=0)
                o_ref[row:row + LANES, :] = blk.astype(o_ref.dtype)
                row += LANES
                parts, filled = [], 0
    assert row == D_PROJ and not parts


def _prep_w_in(w_in):
    wt = w_in[0].T
    d_in = wt.shape[0]
    return pl.pallas_call(
        _w_prep_kernel,
        grid=(D_MODEL // W_PREP_TR,),
        in_specs=[pl.BlockSpec((d_in, W_PREP_TR), lambda i: (0, i))],
        out_specs=pl.BlockSpec((D_PROJ, W_PREP_TR), lambda i: (0, i)),
        out_shape=jax.ShapeDtypeStruct((D_PROJ, D_MODEL), BF16),
        compiler_params=pltpu.CompilerParams(dimension_semantics=("arbitrary",)),
        name="w_prep",
    )(wt)


def _in_proj(x2, g, w_all, tabs, S):
    M = x2.shape[0]
    params = pltpu.CompilerParams(dimension_semantics=("arbitrary",), vmem_limit_bytes=48 * 1024 * 1024)
    out_sds = jax.ShapeDtypeStruct((M, IN_TN), BF16)
    slabs, strided, h = [], {}, None
    for j in range(N_SLABS):
        kinds = _UNIT_KINDS[j * _IN_UPT:(j + 1) * _IN_UPT]
        with_norm = j == 0
        tm = IN_TM_NORM if with_norm else IN_TM
        groups = _slab_tab_groups(kinds)
        act_spec = pl.BlockSpec((tm, D_MODEL), lambda i: (i, 0))
        in_specs = [act_spec, pl.BlockSpec((1, D_MODEL), lambda i: (0, 0))] if with_norm else [act_spec]
        args = [x2, g] if with_norm else [h]
        in_specs.append(pl.BlockSpec((IN_TN, D_MODEL), lambda i, j=j: (j, 0)))
        args.append(w_all)
        if groups:
            tab = jnp.stack([t for grp in groups for t in tabs[grp]])
            in_specs.append(pl.BlockSpec(tab.shape, lambda i: (0, 0, 0)))
            args.append(tab)
        out_specs = [pl.BlockSpec((tm, IN_TN), lambda i: (i, 0))]
        out_shape = [out_sds]
        if with_norm:
            out_specs.append(act_spec)
            out_shape.append(jax.ShapeDtypeStruct((M, D_MODEL), BF16))
        strided_names = [name for name in ("kca", "vca") if _slab(name) == j]
        strided_chunks = tuple(_col(name) // CHUNK for name in strided_names)
        scratch = []
        for _ in strided_chunks:
            out_specs.append(pl.BlockSpec((N_KV_A, tm // CMP_STRIDE, _HALF), lambda i: (0, i, 0)))
            out_shape.append(jax.ShapeDtypeStruct((N_KV_A, M // CMP_STRIDE, _HALF), BF16))
        if strided_chunks:
            scratch.append(pltpu.VMEM((len(strided_chunks) * N_KV_A, IN_SUB, HEAD_DIM_A), F32))
        res = list(pl.pallas_call(
            functools.partial(_in_proj_kernel, kinds, with_norm, S, strided_chunks),
            grid=(M // tm,),
            in_specs=in_specs,
            out_specs=out_specs,
            out_shape=out_shape,
            scratch_shapes=scratch,
            compiler_params=params,
            name=f"in_proj_{j}",
        )(*args))
        slabs.append(res.pop(0))
        if with_norm:
            h = res.pop(0)
        for name in strided_names:
            strided[name] = res.pop(0)
    return slabs, strided


CMP_TM = 256
_HALF = CMP_STRIDE * HEAD_DIM_A


def _compress_kernel(tk_ref, tv_ref, w1k_ref, w1v_ref, w2k_ref, w2v_ref, posk_ref, posv_ref, kc_ref, vc_ref,
                     pb_ref):
    @pl.when(pl.program_id(0) == 0)
    def _():
        pb_ref[0] = _dot(posk_ref[...], w1k_ref[...])
        pb_ref[1] = _dot(posv_ref[...], w1v_ref[...])

    for n, (t_ref, w1_ref, w2_ref, o_ref) in enumerate(((tk_ref, w1k_ref, w2k_ref, kc_ref),
                                                        (tv_ref, w1v_ref, w2v_ref, vc_ref))):
        t = t_ref[...]
        u = _dot(t, w1_ref[0:_HALF, :])
        v = _dot(t, w1_ref[_HALF:2 * _HALF, :])
        pre = u + pltpu.roll(v, CMP_TM - 1, 0) + pb_ref[n, 0:1]
        hid = pre * jax.nn.sigmoid(pre)
        out = _dot(hid.astype(BF16), w2_ref[...])
        row = lax.broadcasted_iota(jnp.int32, out.shape, 0)
        n_cp = LANES
        out = jnp.where((row & (n_cp - 1)) == n_cp - 1, 0.0, out)
        o_ref[...] = out.astype(o_ref.dtype)


def _compress(tk, tv, w1k, w1v, w2k, w2v, posk, posv):
    R = tk.shape[0]
    rows = pl.BlockSpec((CMP_TM, _HALF), lambda i: (i, 0))
    whole = lambda a: pl.BlockSpec(a.shape, lambda i: (0, 0))
    o_spec = pl.BlockSpec((CMP_TM, HEAD_DIM_A), lambda i: (i, 0))
    o_sds = jax.ShapeDtypeStruct((R, HEAD_DIM_A), BF16)
    return pl.pallas_call(
        _compress_kernel,
        grid=(R // CMP_TM,),
        in_specs=[rows, rows, whole(w1k), whole(w1v), whole(w2k), whole(w2v), whole(posk), whole(posv)],
        out_specs=(o_spec, o_spec),
        out_shape=(o_sds, o_sds),
        scratch_shapes=[pltpu.VMEM((2, SUBLANES, CMP_HIDDEN), F32)],
        compiler_params=pltpu.CompilerParams(dimension_semantics=("arbitrary",)),
        name="compress",
    )(tk, tv, w1k, w1v, w2k, w2v, posk, posv)


TQ = 128
ATTN_AHEAD = 8
ONES_ROWS = 16


def _transpose_values(v_ref, vt_ref, n_rows, second_half=None):
    n_blk = vt_ref.shape[0]
    for blk in range(n_blk):
        vt = v_ref[0, blk * LANES:(blk + 1) * LANES, :].astype(F32).T
        if second_half is not None:
            vt = jnp.where(second_half == 0, vt[0:n_rows], vt[n_rows:2 * n_rows])
        vt_ref[blk, 0:n_rows, :] = vt[0:n_rows].astype(vt_ref.dtype)
        vt_ref[blk, n_rows:n_rows + ONES_ROWS, :] = jnp.ones((ONES_ROWS, LANES), vt_ref.dtype)


def _run_skewed(jobs, ahead=1):
    pending = [job[0]() for job in jobs[:ahead]]
    for i, (_, finish) in enumerate(jobs):
        if i + ahead < len(jobs):
            pending.append(jobs[i + ahead][0]())
        finish(pending[i])
        pending[i] = None


def _softmax_step(s, m, acc, vt):
    m_new = jnp.maximum(m, jnp.max(s, axis=0, keepdims=True))
    p = jnp.exp2(s - m_new).astype(BF16)
    return m_new, jnp.exp2(m - m_new) * acc + _dot(vt, p)


SEL_TK = 256
N_SELP = 32
PAIR = 2 * TQ
_VROWS_A = HEAD_DIM_A + ONES_ROWS
_VROWS_B = HEAD_DIM_B + ONES_ROWS
_WIN_TILES = (256, 256, 128)
_N_SEL_JOBS = 9


def _attn_kernel(sink_ref, qlo_ref, qhi_ref, ksel_ref, vsel_ref, kwin_ref, vwin_ref, kc_ref, vc_ref,
                 glo_ref, ghi_ref, zlo_ref, zhi_ref, ov_ref, wtab_ref, ctab_ref,
                 qblo_ref, qbhi_ref, kb_ref, vb_ref, zblo_ref, zbhi_ref, btab_ref,
                 ylo_ref, yhi_ref, yblo_ref, ybhi_ref,
                 vselt_ref, vwint_ref, vbt_ref, qs_ref, bias_ref, acc_ref, out_ref):
    kv = pl.program_id(1)
    i = pl.program_id(2)
    n_qt = 2 * pl.num_programs(2)
    n_pairs = GROUP_A // 2
    t0s = (i * TQ, (n_qt - 1 - i) * TQ)
    q_refs, g_refs, z_refs, y_refs = (qlo_ref, qhi_ref), (glo_ref, ghi_ref), (zlo_ref, zhi_ref), (ylo_ref, yhi_ref)
    qb_refs, zb_refs, yb_refs = (qblo_ref, qbhi_ref), (zblo_ref, zbhi_ref), (yblo_ref, ybhi_ref)

    @pl.when(i == 0)
    def _():
        _transpose_values(vsel_ref, vselt_ref, HEAD_DIM_A)
        _transpose_values(vwin_ref, vwint_ref, HEAD_DIM_A)
        _transpose_values(vb_ref, vbt_ref, HEAD_DIM_B, kv)

    lane = lax.broadcasted_iota(jnp.int32, (1, TQ), 1)
    lane2 = jnp.concatenate([lane, lane], axis=1)
    qpair, gts = [], []
    for t in range(2):
        q = q_refs[t][0]
        pairs = [jnp.concatenate([q[:, h * LANES:(h + 1) * LANES] for h in (2 * p, 2 * p + 1)], axis=0)
                 for p in range(n_pairs)]
        for p in range(n_pairs):
            qs_ref[t, p] = pairs[p]
        qpair.append(pairs)
        gt = g_refs[t][0].astype(F32).T
        gts.append(jnp.where(kv == 0, gt[0:LANES // 2], gt[LANES // 2:LANES]))

    def gate(t, p, branch):
        return jnp.concatenate(
            [gts[t][3 * h + branch:3 * h + branch + 1] for h in (2 * p, 2 * p + 1)], axis=1)

    kc = kc_ref[0]
    vct = vc_ref[0].astype(F32).T.astype(BF16)
    cend = lax.broadcasted_iota(jnp.int32, (LANES, PAIR), 0) * CMP_STRIDE + (CMP_BLOCK - 1)
    psums = [[], []]

    def cmp_scores(t, p):
        return jnp.where(cend <= t0s[t] + lane2, _dot_nt(kc, qpair[t][p]), NEG_INF)

    def cmp_finish(t, p, s):
        m = jnp.max(s, axis=0, keepdims=True)
        e = jnp.where(cend <= t0s[t] + lane2, jnp.exp2(s - m), 0.0)
        pc = e / jnp.maximum(jnp.sum(e, axis=0, keepdims=True), TINY)
        psums[t].append(pc[:, 0:TQ] + pc[:, TQ:PAIR])
        out_ref[t, p] = gate(t, p, 0) * _dot(vct, pc.astype(BF16))

    ksts = [pl.multiple_of(jnp.maximum(t0 - WIN_A, 0), TQ) for t0 in t0s]
    win_m = [[jnp.full((1, PAIR), NEG_INF, F32) for _ in range(n_pairs)] for _ in range(2)]
    win_acc = [[jnp.zeros((_VROWS_A, PAIR), F32) for _ in range(n_pairs)] for _ in range(2)]

    win_var = [jnp.minimum(t0 // TQ, WIN_A // TQ) for t0 in t0s]

    def win_scores(t, p, off, tk):
        kt = kwin_ref[0, pl.ds(pl.multiple_of(ksts[t] + off, LANES), tk), :]
        bias = wtab_ref[win_var[t], off:off + tk, :]
        return _dot_nt(kt, qpair[t][p]) + jnp.concatenate([bias, bias], axis=1)

    def win_finish(t, p, off, tk, s):
        vt = jnp.concatenate(
            [vwint_ref[ksts[t] // LANES + off // LANES + b] for b in range(tk // LANES)], axis=1)
        win_m[t][p], win_acc[t][p] = _softmax_step(s, win_m[t][p], win_acc[t][p], vt)
        if off + tk == sum(_WIN_TILES):
            a = win_acc[t][p]
            out_ref[t, p] = out_ref[t, p] + (
                gate(t, p, 2) / jnp.maximum(a[HEAD_DIM_A:HEAD_DIM_A + 1], TINY)) * a[0:HEAD_DIM_A]

    jobs = [(functools.partial(cmp_scores, t, p), functools.partial(cmp_finish, t, p))
            for t in range(2) for p in range(n_pairs)]
    off = 0
    for tk in _WIN_TILES:
        jobs += [(functools.partial(win_scores, t, p, off, tk), functools.partial(win_finish, t, p, off, tk))
                 for t in range(2) for p in range(n_pairs)]
        off += tk

    nb_keys = WIN_B + TQ
    left = lax.broadcasted_iota(jnp.int32, (TQ, LANES), 1) < HEAD_DIM_B
    zero = jnp.zeros((TQ, LANES), qblo_ref.dtype)
    kbs = [pl.multiple_of(jnp.maximum(t0 - WIN_B, 0), TQ) for t0 in t0s]
    b_var = [jnp.minimum(t0 // TQ, WIN_B // TQ) for t0 in t0s]

    def b_sink(c):
        return jnp.concatenate(
            [jnp.full((1, TQ), sink_ref[kv * GROUP_B + 2 * c + h] * LOG2E, F32) for h in range(2)], axis=1)

    def b_scores(t, c):
        pair = qb_refs[t][0, :, c * LANES:(c + 1) * LANES]
        qp = jnp.concatenate([jnp.where(left, pair, zero), jnp.where(left, zero, pair)], axis=0)
        bias = btab_ref[b_var[t]]
        return _dot_nt(kb_ref[0, pl.ds(kbs[t], nb_keys), :], qp) + jnp.concatenate([bias, bias], axis=1)

    def b_finish(t, c, s):
        vt = jnp.concatenate([vbt_ref[kbs[t] // LANES + b] for b in range(nb_keys // LANES)], axis=1)
        sink = b_sink(c)
        m = jnp.maximum(jnp.max(s, axis=0, keepdims=True), sink)
        p = jnp.exp2(s - m).astype(BF16)
        a = _dot(vt, p)
        den = jnp.maximum(a[HEAD_DIM_B:HEAD_DIM_B + 1] + jnp.exp2(sink - m), TINY)
        o = a[0:HEAD_DIM_B] / den
        o2 = jnp.concatenate([o[:, 0:TQ], o[:, TQ:PAIR]], axis=0)
        z = zb_refs[t][0, :, c * LANES:(c + 1) * LANES]
        yb_refs[t][0, :, c * LANES:(c + 1) * LANES] = (o2.T * z.astype(F32)).astype(yb_refs[t].dtype)

    b_jobs = [[(functools.partial(b_scores, t, c), functools.partial(b_finish, t, c))
               for c in range(GROUP_B // 2)] for t in range(2)]

    def select_blocks(t):
        ov = ov_ref[...]
        jidx = lax.broadcasted_iota(jnp.int32, (N_SELP, TQ), 0)
        psum = psums[t][0] + psums[t][1]
        p_hi = psum.astype(BF16)
        r1 = psum - p_hi.astype(F32)
        p_mid = r1.astype(BF16)
        p_lo = (r1 - p_mid.astype(F32)).astype(BF16)
        psel = _dot(ov, p_hi) + _dot(ov, p_mid) + _dot(ov, p_lo)
        tq = t0s[t] + lane
        cur = tq // SEL_BLOCK
        forced = (jidx == 0) | (jidx == cur) | (jidx == cur - 1)
        score = jnp.where(forced, FORCE_SCORE, jnp.where(jidx * SEL_BLOCK <= tq, psel, -1.0))
        rank = jnp.zeros(psel.shape, F32)
        for r in range(N_SELP):
            row = jnp.broadcast_to(score[r:r + 1, :], score.shape)
            rank = rank + jnp.where(jidx > r, jnp.where(row >= score, 1.0, 0.0), jnp.where(row > score, 1.0, 0.0))
        bias_t = jnp.where(rank < SEL_TOPK, 0.0, NEG_INF)
        for j in range(N_SELP):
            bias_ref[t, j] = jnp.broadcast_to(bias_t[j:j + 1, :], (SUBLANES, TQ))

    blk_per_tile = SEL_TK // SEL_BLOCK
    rep = SEL_BLOCK // SUBLANES
    n_lo = i // (SEL_TK // TQ) + 1
    acc_ref[...] = jnp.zeros(acc_ref.shape, F32)
    ms = [[jnp.full((1, PAIR), NEG_INF, F32) for _ in range(n_pairs)] for _ in range(2)]
    ctx = {}

    def sel_scores(k, p):
        if p == 0:
            if k == 0:
                select_blocks(0)
                select_blocks(1)
            hi = k >= n_lo
            t = hi.astype(jnp.int32)
            j = jnp.where(hi, k - n_lo, k)
            ks = pl.multiple_of(j * SEL_TK, SEL_TK)
            bt = jnp.concatenate(
                [bias_ref[t, j * blk_per_tile + b] for b in range(blk_per_tile) for _ in range(rep)], axis=0)
            qt = jnp.where(hi, n_qt - 1 - i, i)
            diag = j == qt // (SEL_TK // TQ)
            bt = bt + ctab_ref[jnp.where(diag, 1 + qt % (SEL_TK // TQ), 0)]
            bt = jnp.concatenate([bt, bt], axis=1)
            ctx[k] = (hi, t, j, ks, bt)
        hi, t, j, ks, bt = ctx[k]
        return _dot_nt(ksel_ref[0, pl.ds(ks, SEL_TK), :], qs_ref[t, p]) + bt

    def sel_finish(k, p, s):
        hi, t, j, ks, bt = ctx[k]
        vt = jnp.concatenate([vselt_ref[j * (SEL_TK // LANES) + b] for b in range(SEL_TK // LANES)], axis=1)
        m_new, acc_new = _softmax_step(s, jnp.where(hi, ms[1][p], ms[0][p]), acc_ref[t, p], vt)
        acc_ref[t, p] = acc_new
        ms[0][p] = jnp.where(hi, ms[0][p], m_new)
        ms[1][p] = jnp.where(hi, m_new, ms[1][p])

    sel_jobs = [(functools.partial(sel_scores, k, p), functools.partial(sel_finish, k, p))
                for k in range(_N_SEL_JOBS) for p in range(n_pairs)]
    _run_skewed(jobs + b_jobs[0] + sel_jobs + b_jobs[1], ATTN_AHEAD)

    for t in range(2):
        z = z_refs[t][0]
        for p in range(n_pairs):
            a = acc_ref[t, p]
            o = out_ref[t, p] + (gate(t, p, 1) / jnp.maximum(a[HEAD_DIM_A:HEAD_DIM_A + 1], TINY)) * a[0:HEAD_DIM_A]
            for h in (2 * p, 2 * p + 1):
                oh = o[:, (h % 2) * TQ:(h % 2 + 1) * TQ].T
                y_refs[t][0, :, h * LANES:(h + 1) * LANES] = (
                    oh * z[:, h * LANES:(h + 1) * LANES].astype(F32)).astype(y_refs[t].dtype)


def _mask_tables():
    lane = np.arange(TQ)[None, :]
    row = np.arange(WIN_A + TQ)[:, None]
    wtab = []
    for v in range(WIN_A // TQ + 1):
        t0 = v * TQ
        dpos = (t0 + lane) - (max(t0 - WIN_A, 0) + row)
        wtab.append(np.where((dpos >= 0) & (dpos < WIN_A), 0.0, NEG_INF))
    row = np.arange(SEL_TK)[:, None]
    ctab = [np.zeros((SEL_TK, TQ))]
    for odd in range(SEL_TK // TQ):
        ctab.append(np.where(row <= odd * TQ + lane, 0.0, NEG_INF))
    row = np.arange(WIN_B + TQ)[:, None]
    btab = []
    for v in range(WIN_B // TQ + 1):
        t0 = v * TQ
        dpos = (t0 + lane) - (max(t0 - WIN_B, 0) + row)
        btab.append(np.where((dpos >= 0) & (dpos < WIN_B), 0.0, NEG_INF))
    return [jnp.asarray(np.stack(t), F32) for t in (wtab, ctab, btab)]


def _attention(slabs, kc, vc, ov, sinks):
    B, S, _ = slabs[0].shape
    sl = lambda name: slabs[_slab(name)]
    n_cp = S // CMP_STRIDE
    n_qt = S // TQ
    assert n_cp == LANES and S // SEL_BLOCK == N_SELP and (n_qt // 2) // (SEL_TK // TQ) * 2 + 1 == _N_SEL_JOBS
    assert N_KV_A == N_KV_B and GROUP_A * HEAD_DIM_A == GROUP_B * HEAD_DIM_B
    wq = GROUP_A * HEAD_DIM_A
    cb = lambda name: _col(name) // LANES
    kv_spec = lambda name: pl.BlockSpec((1, S, LANES), lambda b, k, i, s, c=cb(name): (b, 0, c + k))
    lo = lambda width, c: pl.BlockSpec((1, TQ, width), lambda b, k, i, s: (b, i, c + k))
    hi = lambda width, c: pl.BlockSpec((1, TQ, width), lambda b, k, i, s: (b, n_qt - 1 - i, c + k))
    whole = lambda a: pl.BlockSpec(a.shape, lambda b, k, i, s: (0,) * a.ndim)
    y_sds = jax.ShapeDtypeStruct((B, S // 2, WIDTH_A), BF16)
    y_lo = pl.BlockSpec((1, TQ, wq), lambda b, k, i, s: (b, i, k))
    y_hi = pl.BlockSpec((1, TQ, wq), lambda b, k, i, s: (b, n_qt // 2 - 1 - i, k))
    wtab, ctab, btab = _mask_tables()
    grid_spec = pltpu.PrefetchScalarGridSpec(
        num_scalar_prefetch=1,
        grid=(B, N_KV_A, n_qt // 2),
        in_specs=[
            lo(wq, _col("qa") // wq), hi(wq, _col("qa") // wq),
            kv_spec("ksa"), kv_spec("vsa"), kv_spec("kwa"), kv_spec("vwa"),
            pl.BlockSpec((1, n_cp, LANES), lambda b, k, i, s: (k * B + b, 0, 0)),
            pl.BlockSpec((1, n_cp, LANES), lambda b, k, i, s: (k * B + b, 0, 0)),
            pl.BlockSpec((1, TQ, LANES), lambda b, k, i, s: (b, i, cb("ga"))),
            pl.BlockSpec((1, TQ, LANES), lambda b, k, i, s: (b, n_qt - 1 - i, cb("ga"))),
            lo(wq, _col("za") // wq), hi(wq, _col("za") // wq),
            whole(ov), whole(wtab), whole(ctab),
            lo(wq, _col("qb") // wq), hi(wq, _col("qb") // wq),
            kv_spec("kb2"), pl.BlockSpec((1, S, LANES), lambda b, k, i, s: (b, 0, cb("vb"))),
            lo(wq, _col("zb") // wq), hi(wq, _col("zb") // wq),
            whole(btab),
        ],
        out_specs=(y_lo, y_hi, y_lo, y_hi),
        scratch_shapes=[
            pltpu.VMEM((S // LANES, _VROWS_A, LANES), BF16),
            pltpu.VMEM((S // LANES, _VROWS_A, LANES), BF16),
            pltpu.VMEM((S // LANES, _VROWS_B, LANES), BF16),
            pltpu.VMEM((2, GROUP_A // 2, PAIR, HEAD_DIM_A), BF16),
            pltpu.VMEM((2, N_SELP, SUBLANES, TQ), F32),
            pltpu.VMEM((2, GROUP_A // 2, _VROWS_A, PAIR), F32),
            pltpu.VMEM((2, GROUP_A // 2, HEAD_DIM_A, PAIR), F32),
        ],
    )
    return pl.pallas_call(
        _attn_kernel,
        grid_spec=grid_spec,
        out_shape=(y_sds, y_sds, y_sds, y_sds),
        compiler_params=pltpu.CompilerParams(
            dimension_semantics=("arbitrary", "arbitrary", "arbitrary"), vmem_limit_bytes=48 * 1024 * 1024),
        name="attn",
    )(sinks, sl("qa"), sl("qa"), sl("ksa"), sl("vsa"), sl("kwa"), sl("vwa"), kc, vc,
      sl("ga"), sl("ga"), sl("za"), sl("za"), ov, wtab, ctab,
      sl("qb"), sl("qb"), sl("kb2"), sl("vb"), sl("zb"), sl("zb"), btab)


OUT_TM = 512


def _out_proj_kernel(tiles_per_half, yalo_ref, yahi_ref, yblo_ref, ybhi_ref, wa_ref, wb_ref, x_ref, g_ref, o_ref):
    in_lo = (pl.program_id(0) % (2 * tiles_per_half)) < tiles_per_half
    ya = jnp.where(in_lo, yalo_ref[...], yahi_ref[...])
    yb = jnp.where(in_lo, yblo_ref[...], ybhi_ref[...])
    r = x_ref[...] + _dot(ya, wa_ref[...]) + _dot(yb, wb_ref[...])
    ms = jnp.mean(r * r, axis=-1, keepdims=True)
    o_ref[...] = (r * lax.rsqrt(ms + RMS_EPS)) * g_ref[...]


def _out_proj(ya_lo, ya_hi, yb_lo, yb_hi, w_out, x2, g, S):
    M = x2.shape[0]
    tph = S // 2 // OUT_TM
    half = lambda i: (i // (2 * tph)) * tph
    t = lambda i: i % (2 * tph)
    lo_spec = lambda w: pl.BlockSpec((OUT_TM, w), lambda i: (half(i) + jnp.minimum(t(i), tph - 1), 0))
    hi_spec = lambda w: pl.BlockSpec((OUT_TM, w), lambda i: (half(i) + jnp.maximum(t(i) - tph, 0), 0))
    return pl.pallas_call(
        functools.partial(_out_proj_kernel, tph),
        grid=(M // OUT_TM,),
        in_specs=[
            lo_spec(WIDTH_A), hi_spec(WIDTH_A), lo_spec(WIDTH_B), hi_spec(WIDTH_B),
            pl.BlockSpec((WIDTH_A, D_MODEL), lambda i: (0, 0)),
            pl.BlockSpec((WIDTH_B, D_MODEL), lambda i: (WIDTH_A // WIDTH_B, 0)),
            pl.BlockSpec((OUT_TM, D_MODEL), lambda i: (i, 0)),
            pl.BlockSpec((1, D_MODEL), lambda i: (0, 0)),
        ],
        out_specs=pl.BlockSpec((OUT_TM, D_MODEL), lambda i: (i, 0)),
        out_shape=jax.ShapeDtypeStruct((M, D_MODEL), F32),
        compiler_params=pltpu.CompilerParams(
            dimension_semantics=("arbitrary",), vmem_limit_bytes=56 * 1024 * 1024),
        name="out_proj",
    )(ya_lo, ya_hi, yb_lo, yb_hi, w_out, w_out, x2, g)


def _overlap_matrix(n_cp):
    c_start = np.arange(n_cp) * CMP_STRIDE
    j_start = np.arange(N_SELP) * SEL_BLOCK
    ov = (c_start[None, :] < j_start[:, None] + SEL_BLOCK) & (c_start[None, :] + CMP_BLOCK > j_start[:, None])
    return jnp.asarray(ov, BF16)


def kernel(x, w_in, cmp_k_w1, cmp_k_w2, cmp_v_w1, cmp_v_w2, cmp_k_pos, cmp_v_pos, sinks, w_out, norm_g, final_g):
    B, S, D = x.shape
    assert D == D_MODEL and w_in.shape[0] == 1
    x2 = x.reshape(B * S, D)

    w_all = _prep_w_in(w_in)
    tabs = _rope_tables(S)
    slabs, strided = _in_proj(x2, norm_g[0].reshape(1, D), w_all, tabs, S)
    slabs = [p.reshape(B, S, IN_TN) for p in slabs]

    n_cp = S // CMP_STRIDE
    tk, tv = [strided[n].reshape(N_KV_A * B * n_cp, _HALF) for n in ("kca", "vca")]
    flat_pos = lambda p: jnp.broadcast_to(p.reshape(1, CMP_BLOCK * HEAD_DIM_A), (8, CMP_BLOCK * HEAD_DIM_A))
    kc, vc = _compress(tk, tv, cmp_k_w1[0].astype(BF16), cmp_v_w1[0].astype(BF16),
                       cmp_k_w2[0].astype(BF16), cmp_v_w2[0].astype(BF16),
                       flat_pos(cmp_k_pos[0]).astype(BF16), flat_pos(cmp_v_pos[0]).astype(BF16))
    kc, vc = [c.reshape(N_KV_A * B, n_cp, HEAD_DIM_A) for c in (kc, vc)]

    ys = _attention(slabs, kc, vc, _overlap_matrix(n_cp), sinks[0])
    out = _out_proj(*[y.reshape(B * S // 2, y.shape[2]) for y in ys], w_out[0].astype(BF16), x2,
                    final_g.reshape(1, D), S)
    return out.reshape(B, S, D)
```

```python
import functools
import math

import numpy as np
import jax
import jax.numpy as jnp
from jax import lax
from jax.experimental import pallas as pl
from jax.experimental.pallas import tpu as pltpu

F32 = jnp.float32
BF16 = jnp.bfloat16

D_MODEL = 2048
ROPE_THETA = 10000.0
RMS_EPS = 1e-6
NEG_INF = -1e30
TINY = 1e-30
LOG2E = math.log2(math.e)

WIDTH_A = 1024
HEAD_DIM_A = 128
N_KV_A = 2
GROUP_A = 4
KV_A = N_KV_A * HEAD_DIM_A
CMP_BLOCK = 32
CMP_STRIDE = 16
CMP_HIDDEN = 256
SEL_BLOCK = 64
SEL_TOPK = 16
WIN_A = 512
FORCE_SCORE = 1e4

WIDTH_B = 1024
HEAD_DIM_B = 64
N_HEADS_B = 16
N_KV_B = 2
GROUP_B = 8
KV_B = N_KV_B * HEAD_DIM_B
WIN_B = 128

IN_SIZES = (WIDTH_A, KV_A, KV_A, KV_A, KV_A, KV_A, KV_A, WIDTH_A, 3 * 8,
            WIDTH_B, KV_B, KV_B, WIDTH_B)

LANES = 128
SUBLANES = 8
CHUNK = 256

EP_NONE, EP_ROPE128, EP_ROPE64, EP_SILU, EP_SIGMOID, EP_ROPE128_Q, EP_ROPE64_Q = range(7)

_PROJ_LAYOUT = (
    ("qa", 8, EP_ROPE128_Q), ("kca", 2, EP_ROPE128), ("ksa", 2, EP_ROPE128), ("kwa", 2, EP_ROPE128),
    ("vca", 2, EP_NONE),
    ("vsa", 2, EP_NONE), ("vwa", 2, EP_NONE), ("za", 8, EP_SILU), ("kb2", 2, EP_ROPE64),
    ("ga", 1, EP_SIGMOID), ("vb", 1, EP_NONE),
    ("zb", 8, EP_SILU), ("qb", 8, EP_ROPE64_Q),
)
_UNIT_KINDS = tuple(k for _, n, k in _PROJ_LAYOUT for _ in range(n))
_UNIT_START = {}
_c = 0
for _name, _n, _k in _PROJ_LAYOUT:
    _UNIT_START[_name] = _c
    _c += _n
D_PROJ = _c * LANES
IN_TN = 2048
_IN_CPT = IN_TN // CHUNK
_IN_UPT = IN_TN // LANES
N_SLABS = D_PROJ // IN_TN


def _slab(name):
    return _UNIT_START[name] // _IN_UPT


def _col(name):
    return (_UNIT_START[name] % _IN_UPT) * LANES


def _dot(a, b):
    return jnp.dot(a, b, preferred_element_type=F32)


def _dot_nt(a, b):
    return lax.dot_general(a, b, (((1,), (1,)), ((), ())), preferred_element_type=F32)


IN_TM_NORM = 512
IN_TM = 1024
IN_SUB = 512

_TAB_GROUP = {EP_ROPE128: "r128", EP_ROPE128_Q: "r128q", EP_ROPE64: "r64", EP_ROPE64_Q: "r64q"}
_TAB_SIZE = {"r128": 2, "r128q": 2, "r64": 3, "r64q": 3}


def _slab_tab_groups(kinds):
    return tuple(dict.fromkeys(_TAB_GROUP[k] for k in kinds if k in _TAB_GROUP))


def _in_proj_kernel(kinds, with_norm, seq_len, strided_chunks, *refs):
    groups = _slab_tab_groups(kinds)
    refs = list(refs)
    if with_norm:
        x_ref, g_ref = refs.pop(0), refs.pop(0)
    else:
        h_ref = refs.pop(0)
    w_ref = refs.pop(0)
    tab_ref = refs.pop(0) if groups else None
    o_ref = refs.pop(0)
    if with_norm:
        h_ref = refs.pop(0)
        x = x_ref[...]
        ms = jnp.mean(x * x, axis=-1, keepdims=True)
        h_ref[...] = ((x * lax.rsqrt(ms + RMS_EPS)) * g_ref[...]).astype(BF16)
    t_refs = {c: refs.pop(0) for c in strided_chunks}
    if strided_chunks:
        stage_ref = refs.pop(0)
    tm = h_ref.shape[0]
    pos0 = (pl.program_id(0) % (seq_len // tm)) * tm
    base, n = {}, 0
    for grp in groups:
        base[grp] = n
        n += _TAB_SIZE[grp]

    def epilogue(kind, r, a):
        if kind == EP_NONE:
            return a
        if kind == EP_SILU:
            return a * jax.nn.sigmoid(a)
        if kind == EP_SIGMOID:
            return jax.nn.sigmoid(a)
        t = base[_TAB_GROUP[kind]]
        tab = lambda k: tab_ref[t + k, pl.ds(pl.multiple_of(pos0 + r * IN_SUB, IN_SUB), IN_SUB), :]
        if kind in (EP_ROPE128, EP_ROPE128_Q):
            return a * tab(0) + pltpu.roll(a, 64, 1) * tab(1)
        return a * tab(0) + pltpu.roll(a, 96, 1) * tab(1) + pltpu.roll(a, 32, 1) * tab(2)

    n_grp = IN_SUB // CMP_STRIDE

    subs = CHUNK // LANES

    def finish(r, c, acc):
        rows = slice(r * IN_SUB, (r + 1) * IN_SUB)
        for sub in range(subs):
            cols = slice(c * CHUNK + sub * LANES, c * CHUNK + (sub + 1) * LANES)
            val = epilogue(kinds[c * subs + sub], r, acc[:, sub * LANES:(sub + 1) * LANES])
            o_ref[rows, cols] = val.astype(o_ref.dtype)
            if c in t_refs:
                slot = strided_chunks.index(c) * subs + sub
                stage_ref[slot] = val
                for k in range(CMP_STRIDE):
                    t_refs[c][sub, r * n_grp:(r + 1) * n_grp, k * LANES:(k + 1) * LANES] = (
                        stage_ref[slot, pl.ds(k, n_grp, stride=CMP_STRIDE), :].astype(t_refs[c].dtype))

    jobs = []
    for r in range(tm // IN_SUB):
        for c in range(len(kinds) // subs):
            jobs.append((lambda r=r, c=c: _dot_nt(h_ref[r * IN_SUB:(r + 1) * IN_SUB, :],
                                                  w_ref[c * CHUNK:(c + 1) * CHUNK, :]),
                         functools.partial(finish, r, c)))
    _run_skewed(jobs)


def _rope_tables(S):
    def cs(d):
        inv = ROPE_THETA ** (-jnp.arange(0, d, 2, dtype=F32) / d)
        ang = jnp.arange(S, dtype=F32)[:, None] * inv[None, :]
        return jnp.cos(ang), jnp.sin(ang)

    c128, s128 = cs(HEAD_DIM_A)
    c64, s64 = cs(HEAD_DIM_B)
    z32 = jnp.zeros_like(s64)
    t128 = [jnp.concatenate([c128, c128], axis=1), jnp.concatenate([-s128, s128], axis=1)]
    t64 = [jnp.concatenate([c64, c64, c64, c64], axis=1),
           jnp.concatenate([-s64, z32, -s64, z32], axis=1),
           jnp.concatenate([z32, s64, z32, s64], axis=1)]
    qa = HEAD_DIM_A ** -0.5 * LOG2E
    qb = HEAD_DIM_B ** -0.5 * LOG2E
    return {"r128": t128, "r64": t64, "r128q": [t * qa for t in t128], "r64q": [t * qb for t in t64]}


W_PREP_TR = 256


def _w_in_pieces():
    offs = np.cumsum((0,) + IN_SIZES)
    names = ("qa", "kca", "vca", "ksa", "vsa", "kwa", "vwa", "za", "ga", "qb", "kb", "vb", "zb")
    start = {n: int(offs[i]) for i, n in enumerate(names)}
    width = dict(zip(names, IN_SIZES))
    n_g = 3 * GROUP_A
    halves = lambda n: [(start[n] + i * HEAD_DIM_B, HEAD_DIM_B) for i in (0, 0, 1, 1)]
    pieces = {n: [(start[n], width[n])]
              for n in ("qa", "kca", "ksa", "kwa", "vca", "vsa", "vwa", "za", "zb", "qb", "vb")}
    pad = LANES // N_KV_A - n_g
    pieces["ga"] = [(start["ga"], n_g), (None, pad), (start["ga"] + n_g, n_g), (None, pad)]
    pieces["kb2"] = halves("kb")
    return [p for name, _, _ in _PROJ_LAYOUT for p in pieces[name]]


def _w_prep_kernel(w_ref, o_ref):
    row = 0
    parts, filled = [], 0
    for src, wd in _w_in_pieces():
        done = 0
        while done < wd:
            take = min(wd - done, LANES - filled)
            if src is None:
                parts.append(jnp.zeros((take, W_PREP_TR), F32))
            else:
                parts.append(w_ref[src + done:src + done + take, :])
            done += take
            filled += take
            if filled == LANES:
                blk = parts[0] if len(parts) == 1 else jnp.concatenate(parts, axis=0)
                o_ref[row:row + LANES, :] = blk.astype(o_ref.dtype)
                row += LANES
                parts, filled = [], 0
    assert row == D_PROJ and not parts


def _prep_w_in(w_in):
    wt = w_in[0].T
    d_in = wt.shape[0]
    return pl.pallas_call(
        _w_prep_kernel,
        grid=(D_MODEL // W_PREP_TR,),
        in_specs=[pl.BlockSpec((d_in, W_PREP_TR), lambda i: (0, i))],
        out_specs=pl.BlockSpec((D_PROJ, W_PREP_TR), lambda i: (0, i)),
        out_shape=jax.ShapeDtypeStruct((D_PROJ, D_MODEL), BF16),
        compiler_params=pltpu.CompilerParams(dimension_semantics=("arbitrary",)),
        name="w_prep",
    )(wt)


def _in_proj(x2, g, w_all, tabs, S):
    M = x2.shape[0]
    params = pltpu.CompilerParams(dimension_semantics=("arbitrary",), vmem_limit_bytes=48 * 1024 * 1024)
    out_sds = jax.ShapeDtypeStruct((M, IN_TN), BF16)
    slabs, strided, h = [], {}, None
    for j in range(N_SLABS):
        kinds = _UNIT_KINDS[j * _IN_UPT:(j + 1) * _IN_UPT]
        with_norm = j == 0
        tm = IN_TM_NORM if with_norm else IN_TM
        groups = _slab_tab_groups(kinds)
        act_spec = pl.BlockSpec((tm, D_MODEL), lambda i: (i, 0))
        in_specs = [act_spec, pl.BlockSpec((1, D_MODEL), lambda i: (0, 0))] if with_norm else [act_spec]
        args = [x2, g] if with_norm else [h]
        in_specs.append(pl.BlockSpec((IN_TN, D_MODEL), lambda i, j=j: (j, 0)))
        args.append(w_all)
        if groups:
            tab = jnp.stack([t for grp in groups for t in tabs[grp]])
            in_specs.append(pl.BlockSpec(tab.shape, lambda i: (0, 0, 0)))
            args.append(tab)
        out_specs = [pl.BlockSpec((tm, IN_TN), lambda i: (i, 0))]
        out_shape = [out_sds]
        if with_norm:
            out_specs.append(act_spec)
            out_shape.append(jax.ShapeDtypeStruct((M, D_MODEL), BF16))
        strided_names = [name for name in ("kca", "vca") if _slab(name) == j]
        strided_chunks = tuple(_col(name) // CHUNK for name in strided_names)
        scratch = []
        for _ in strided_chunks:
            out_specs.append(pl.BlockSpec((N_KV_A, tm // CMP_STRIDE, _HALF), lambda i: (0, i, 0)))
            out_shape.append(jax.ShapeDtypeStruct((N_KV_A, M // CMP_STRIDE, _HALF), BF16))
        if strided_chunks:
            scratch.append(pltpu.VMEM((len(strided_chunks) * N_KV_A, IN_SUB, HEAD_DIM_A), F32))
        res = list(pl.pallas_call(
            functools.partial(_in_proj_kernel, kinds, with_norm, S, strided_chunks),
            grid=(M // tm,),
            in_specs=in_specs,
            out_specs=out_specs,
            out_shape=out_shape,
            scratch_shapes=scratch,
            compiler_params=params,
            name=f"in_proj_{j}",
        )(*args))
        slabs.append(res.pop(0))
        if with_norm:
            h = res.pop(0)
        for name in strided_names:
            strided[name] = res.pop(0)
    return slabs, strided


CMP_TM = 256
_HALF = CMP_STRIDE * HEAD_DIM_A


def _compress_kernel(tk_ref, tv_ref, w1k_ref, w1v_ref, w2k_ref, w2v_ref, posk_ref, posv_ref, kc_ref, vc_ref,
                     pb_ref):
    @pl.when(pl.program_id(0) == 0)
    def _():
        pb_ref[0] = _dot(posk_ref[...], w1k_ref[...])
        pb_ref[1] = _dot(posv_ref[...], w1v_ref[...])

    for n, (t_ref, w1_ref, w2_ref, o_ref) in enumerate(((tk_ref, w1k_ref, w2k_ref, kc_ref),
                                                        (tv_ref, w1v_ref, w2v_ref, vc_ref))):
        t = t_ref[...]
        u = _dot(t, w1_ref[0:_HALF, :])
        v = _dot(t, w1_ref[_HALF:2 * _HALF, :])
        pre = u + pltpu.roll(v, CMP_TM - 1, 0) + pb_ref[n, 0:1]
        hid = pre * jax.nn.sigmoid(pre)
        out = _dot(hid.astype(BF16), w2_ref[...])
        row = lax.broadcasted_iota(jnp.int32, out.shape, 0)
        n_cp = LANES
        out = jnp.where((row & (n_cp - 1)) == n_cp - 1, 0.0, out)
        o_ref[...] = out.astype(o_ref.dtype)


def _compress(tk, tv, w1k, w1v, w2k, w2v, posk, posv):
    R = tk.shape[0]
    rows = pl.BlockSpec((CMP_TM, _HALF), lambda i: (i, 0))
    whole = lambda a: pl.BlockSpec(a.shape, lambda i: (0, 0))
    o_spec = pl.BlockSpec((CMP_TM, HEAD_DIM_A), lambda i: (i, 0))
    o_sds = jax.ShapeDtypeStruct((R, HEAD_DIM_A), BF16)
    return pl.pallas_call(
        _compress_kernel,
        grid=(R // CMP_TM,),
        in_specs=[rows, rows, whole(w1k), whole(w1v), whole(w2k), whole(w2v), whole(posk), whole(posv)],
        out_specs=(o_spec, o_spec),
        out_shape=(o_sds, o_sds),
        scratch_shapes=[pltpu.VMEM((2, SUBLANES, CMP_HIDDEN), F32)],
        compiler_params=pltpu.CompilerParams(dimension_semantics=("arbitrary",)),
        name="compress",
    )(tk, tv, w1k, w1v, w2k, w2v, posk, posv)


TQ = 128
ATTN_AHEAD = 8
ONES_ROWS = 16


def _transpose_values(v_ref, vt_ref, n_rows, second_half=None):
    n_blk = vt_ref.shape[0]
    for blk in range(n_blk):
        vt = v_ref[0, blk * LANES:(blk + 1) * LANES, :].astype(F32).T
        if second_half is not None:
            vt = jnp.where(second_half == 0, vt[0:n_rows], vt[n_rows:2 * n_rows])
        vt_ref[blk, 0:n_rows, :] = vt[0:n_rows].astype(vt_ref.dtype)
        vt_ref[blk, n_rows:n_rows + ONES_ROWS, :] = jnp.ones((ONES_ROWS, LANES), vt_ref.dtype)


def _run_skewed(jobs, ahead=1):
    pending = [job[0]() for job in jobs[:ahead]]
    for i, (_, finish) in enumerate(jobs):
        if i + ahead < len(jobs):
            pending.append(jobs[i + ahead][0]())
        finish(pending[i])
        pending[i] = None


def _softmax_step(s, m, acc, vt):
    m_new = jnp.maximum(m, jnp.max(s, axis=0, keepdims=True))
    p = jnp.exp2(s - m_new).astype(BF16)
    return m_new, jnp.exp2(m - m_new) * acc + _dot(vt, p)


SEL_TK = 256
N_SELP = 32
PAIR = 2 * TQ
N_TILE_PAIRS = 2
N_TILES = 2 * N_TILE_PAIRS
_VROWS_A = HEAD_DIM_A + ONES_ROWS
_VROWS_B = HEAD_DIM_B + ONES_ROWS
_WIN_TILES = (256, 256, 128)
_N_SEL_JOBS = 9


def _query_tile(t, i, n_qt):
    low = i if t // 2 == 0 else n_qt // 2 - 1 - i
    return low if t % 2 == 0 else n_qt - 1 - low


def _attn_kernel(*refs):
    refs = list(refs)
    take = lambda n: [refs.pop(0) for _ in range(n)]
    sink_ref, = take(1)
    q_refs = take(N_TILES)
    ksel_ref, vsel_ref, kwin_ref, vwin_ref, kc_ref, vc_ref = take(6)
    g_refs, z_refs = take(N_TILES), take(N_TILES)
    ov_ref, wtab_ref, ctab_ref = take(3)
    qb_refs = take(N_TILES)
    kb_ref, vb_ref = take(2)
    zb_refs = take(N_TILES)
    btab_ref, = take(1)
    y_refs, yb_refs = take(N_TILES), take(N_TILES)
    vselt_ref, vwint_ref, vbt_ref, qs_ref, bias_ref, acc_ref, out_ref = refs

    kv = pl.program_id(1)
    i = pl.program_id(2)
    n_qt = N_TILES * pl.num_programs(2)
    n_pairs = GROUP_A // 2
    tiles = range(N_TILES)
    qts = [_query_tile(t, i, n_qt) for t in tiles]
    t0s = [qt * TQ for qt in qts]

    @pl.when(i == 0)
    def _():
        _transpose_values(vsel_ref, vselt_ref, HEAD_DIM_A)
        _transpose_values(vwin_ref, vwint_ref, HEAD_DIM_A)
        _transpose_values(vb_ref, vbt_ref, HEAD_DIM_B, kv)

    lane = lax.broadcasted_iota(jnp.int32, (1, TQ), 1)
    lane2 = jnp.concatenate([lane, lane], axis=1)
    qpair, gts = [], []
    for t in tiles:
        q = q_refs[t][0]
        pairs = [jnp.concatenate([q[:, h * LANES:(h + 1) * LANES] for h in (2 * p, 2 * p + 1)], axis=0)
                 for p in range(n_pairs)]
        for p in range(n_pairs):
            qs_ref[t, p] = pairs[p]
        qpair.append(pairs)
        gt = g_refs[t][0].astype(F32).T
        gts.append(jnp.where(kv == 0, gt[0:LANES // 2], gt[LANES // 2:LANES]))

    def gate(t, p, branch):
        return jnp.concatenate(
            [gts[t][3 * h + branch:3 * h + branch + 1] for h in (2 * p, 2 * p + 1)], axis=1)

    kc = kc_ref[0]
    vct = vc_ref[0].astype(F32).T.astype(BF16)
    cend = lax.broadcasted_iota(jnp.int32, (LANES, PAIR), 0) * CMP_STRIDE + (CMP_BLOCK - 1)
    psums = [[] for _ in tiles]

    def cmp_scores(t, p):
        return jnp.where(cend <= t0s[t] + lane2, _dot_nt(kc, qpair[t][p]), NEG_INF)

    def cmp_finish(t, p, s):
        m = jnp.max(s, axis=0, keepdims=True)
        e = jnp.where(cend <= t0s[t] + lane2, jnp.exp2(s - m), 0.0)
        pc = e / jnp.maximum(jnp.sum(e, axis=0, keepdims=True), TINY)
        psums[t].append(pc[:, 0:TQ] + pc[:, TQ:PAIR])
        out_ref[t, p] = gate(t, p, 0) * _dot(vct, pc.astype(BF16))

    ksts = [pl.multiple_of(jnp.maximum(t0 - WIN_A, 0), TQ) for t0 in t0s]
    win_m = [[jnp.full((1, PAIR), NEG_INF, F32) for _ in range(n_pairs)] for _ in tiles]
    win_acc = [[jnp.zeros((_VROWS_A, PAIR), F32) for _ in range(n_pairs)] for _ in tiles]

    win_var = [jnp.minimum(qt, WIN_A // TQ) for qt in qts]

    def win_scores(t, p, off, tk):
        kt = kwin_ref[0, pl.ds(pl.multiple_of(ksts[t] + off, LANES), tk), :]
        bias = wtab_ref[win_var[t], off:off + tk, :]
        return _dot_nt(kt, qpair[t][p]) + jnp.concatenate([bias, bias], axis=1)

    def win_finish(t, p, off, tk, s):
        vt = jnp.concatenate(
            [vwint_ref[ksts[t] // LANES + off // LANES + b] for b in range(tk // LANES)], axis=1)
        win_m[t][p], win_acc[t][p] = _softmax_step(s, win_m[t][p], win_acc[t][p], vt)
        if off + tk == sum(_WIN_TILES):
            a = win_acc[t][p]
            out_ref[t, p] = out_ref[t, p] + (
                gate(t, p, 2) / jnp.maximum(a[HEAD_DIM_A:HEAD_DIM_A + 1], TINY)) * a[0:HEAD_DIM_A]

    jobs = [(functools.partial(cmp_scores, t, p), functools.partial(cmp_finish, t, p))
            for t in tiles for p in range(n_pairs)]
    off = 0
    for tk in _WIN_TILES:
        jobs += [(functools.partial(win_scores, t, p, off, tk), functools.partial(win_finish, t, p, off, tk))
                 for t in tiles for p in range(n_pairs)]
        off += tk

    nb_keys = WIN_B + TQ
    left = lax.broadcasted_iota(jnp.int32, (TQ, LANES), 1) < HEAD_DIM_B
    zero = jnp.zeros((TQ, LANES), qb_refs[0].dtype)
    kbs = [pl.multiple_of(jnp.maximum(t0 - WIN_B, 0), TQ) for t0 in t0s]
    b_var = [jnp.minimum(qt, WIN_B // TQ) for qt in qts]

    def b_sink(c):
        return jnp.concatenate(
            [jnp.full((1, TQ), sink_ref[kv * GROUP_B + 2 * c + h] * LOG2E, F32) for h in range(2)], axis=1)

    def b_scores(t, c):
        pair = qb_refs[t][0, :, c * LANES:(c + 1) * LANES]
        qp = jnp.concatenate([jnp.where(left, pair, zero), jnp.where(left, zero, pair)], axis=0)
        bias = btab_ref[b_var[t]]
        return _dot_nt(kb_ref[0, pl.ds(kbs[t], nb_keys), :], qp) + jnp.concatenate([bias, bias], axis=1)

    def b_finish(t, c, s):
        vt = jnp.concatenate([vbt_ref[kbs[t] // LANES + b] for b in range(nb_keys // LANES)], axis=1)
        sink = b_sink(c)
        m = jnp.maximum(jnp.max(s, axis=0, keepdims=True), sink)
        p = jnp.exp2(s - m).astype(BF16)
        a = _dot(vt, p)
        den = jnp.maximum(a[HEAD_DIM_B:HEAD_DIM_B + 1] + jnp.exp2(sink - m), TINY)
        o = a[0:HEAD_DIM_B] / den
        o2 = jnp.concatenate([o[:, 0:TQ], o[:, TQ:PAIR]], axis=0)
        z = zb_refs[t][0, :, c * LANES:(c + 1) * LANES]
        yb_refs[t][0, :, c * LANES:(c + 1) * LANES] = (o2.T * z.astype(F32)).astype(yb_refs[t].dtype)

    b_jobs = [[(functools.partial(b_scores, t, c), functools.partial(b_finish, t, c))
               for c in range(GROUP_B // 2)] for t in tiles]

    def select_blocks(t):
        ov = ov_ref[...]
        jidx = lax.broadcasted_iota(jnp.int32, (N_SELP, TQ), 0)
        psum = psums[t][0] + psums[t][1]
        p_hi = psum.astype(BF16)
        r1 = psum - p_hi.astype(F32)
        p_mid = r1.astype(BF16)
        p_lo = (r1 - p_mid.astype(F32)).astype(BF16)
        psel = _dot(ov, p_hi) + _dot(ov, p_mid) + _dot(ov, p_lo)
        tq = t0s[t] + lane
        cur = tq // SEL_BLOCK
        forced = (jidx == 0) | (jidx == cur) | (jidx == cur - 1)
        score = jnp.where(forced, FORCE_SCORE, jnp.where(jidx * SEL_BLOCK <= tq, psel, -1.0))
        rank = jnp.zeros(psel.shape, F32)
        for r in range(N_SELP):
            row = jnp.broadcast_to(score[r:r + 1, :], score.shape)
            rank = rank + jnp.where(jidx > r, jnp.where(row >= score, 1.0, 0.0), jnp.where(row > score, 1.0, 0.0))
        bias_t = jnp.where(rank < SEL_TOPK, 0.0, NEG_INF)
        for j in range(N_SELP):
            bias_ref[t, j] = jnp.broadcast_to(bias_t[j:j + 1, :], (SUBLANES, TQ))

    blk_per_tile = SEL_TK // SEL_BLOCK
    rep = SEL_BLOCK // SUBLANES
    acc_ref[...] = jnp.zeros(acc_ref.shape, F32)
    ms = [[jnp.full((1, PAIR), NEG_INF, F32) for _ in range(n_pairs)] for _ in tiles]
    ctx = {}

    def sel_scores(g, k, p):
        if p == 0:
            if g == 0 and k == 0:
                for t in tiles:
                    select_blocks(t)
            n_lo = qts[2 * g] // (SEL_TK // TQ) + 1
            hi = k >= n_lo
            t = 2 * g + hi.astype(jnp.int32)
            j = jnp.where(hi, k - n_lo, k)
            ks = pl.multiple_of(j * SEL_TK, SEL_TK)
            bt = jnp.concatenate(
                [bias_ref[t, j * blk_per_tile + b] for b in range(blk_per_tile) for _ in range(rep)], axis=0)
            qt = jnp.where(hi, qts[2 * g + 1], qts[2 * g])
            diag = j == qt // (SEL_TK // TQ)
            bt = bt + ctab_ref[jnp.where(diag, 1 + qt % (SEL_TK // TQ), 0)]
            bt = jnp.concatenate([bt, bt], axis=1)
            ctx[g, k] = (hi, t, j, ks, bt)
        hi, t, j, ks, bt = ctx[g, k]
        return _dot_nt(ksel_ref[0, pl.ds(ks, SEL_TK), :], qs_ref[t, p]) + bt

    def sel_finish(g, k, p, s):
        hi, t, j, ks, bt = ctx[g, k]
        lo_t, hi_t = 2 * g, 2 * g + 1
        vt = jnp.concatenate([vselt_ref[j * (SEL_TK // LANES) + b] for b in range(SEL_TK // LANES)], axis=1)
        m_new, acc_new = _softmax_step(s, jnp.where(hi, ms[hi_t][p], ms[lo_t][p]), acc_ref[t, p], vt)
        acc_ref[t, p] = acc_new
        ms[lo_t][p] = jnp.where(hi, ms[lo_t][p], m_new)
        ms[hi_t][p] = jnp.where(hi, m_new, ms[hi_t][p])

    sel_jobs = [(functools.partial(sel_scores, g, k, p), functools.partial(sel_finish, g, k, p))
                for g in range(N_TILE_PAIRS) for k in range(_N_SEL_JOBS) for p in range(n_pairs)]
    b_before = [job for t in tiles if t % 2 == 0 for job in b_jobs[t]]
    b_after = [job for t in tiles if t % 2 == 1 for job in b_jobs[t]]
    _run_skewed(jobs + b_before + sel_jobs + b_after, ATTN_AHEAD)

    for t in tiles:
        z = z_refs[t][0]
        for p in range(n_pairs):
            a = acc_ref[t, p]
            o = out_ref[t, p] + (gate(t, p, 1) / jnp.maximum(a[HEAD_DIM_A:HEAD_DIM_A + 1], TINY)) * a[0:HEAD_DIM_A]
            for h in (2 * p, 2 * p + 1):
                oh = o[:, (h % 2) * TQ:(h % 2 + 1) * TQ].T
                y_refs[t][0, :, h * LANES:(h + 1) * LANES] = (
                    oh * z[:, h * LANES:(h + 1) * LANES].astype(F32)).astype(y_refs[t].dtype)


def _mask_tables():
    lane = np.arange(TQ)[None, :]
    row = np.arange(WIN_A + TQ)[:, None]
    wtab = []
    for v in range(WIN_A // TQ + 1):
        t0 = v * TQ
        dpos = (t0 + lane) - (max(t0 - WIN_A, 0) + row)
        wtab.append(np.where((dpos >= 0) & (dpos < WIN_A), 0.0, NEG_INF))
    row = np.arange(SEL_TK)[:, None]
    ctab = [np.zeros((SEL_TK, TQ))]
    for odd in range(SEL_TK // TQ):
        ctab.append(np.where(row <= odd * TQ + lane, 0.0, NEG_INF))
    row = np.arange(WIN_B + TQ)[:, None]
    btab = []
    for v in range(WIN_B // TQ + 1):
        t0 = v * TQ
        dpos = (t0 + lane) - (max(t0 - WIN_B, 0) + row)
        btab.append(np.where((dpos >= 0) & (dpos < WIN_B), 0.0, NEG_INF))
    return [jnp.asarray(np.stack(t), F32) for t in (wtab, ctab, btab)]


def _attention(slabs, kc, vc, ov, sinks):
    B, S, _ = slabs[0].shape
    sl = lambda name: slabs[_slab(name)]
    n_cp = S // CMP_STRIDE
    n_qt = S // TQ
    n_steps = n_qt // N_TILES
    assert n_cp == LANES and S // SEL_BLOCK == N_SELP and (n_qt // 2) // (SEL_TK // TQ) * 2 + 1 == _N_SEL_JOBS
    assert N_KV_A == N_KV_B and GROUP_A * HEAD_DIM_A == GROUP_B * HEAD_DIM_B and N_TILE_PAIRS in (1, 2)
    wq = GROUP_A * HEAD_DIM_A
    cb = lambda name: _col(name) // LANES
    kv_spec = lambda name: pl.BlockSpec((1, S, LANES), lambda b, k, i, s, c=cb(name): (b, 0, c + k))
    whole = lambda a: pl.BlockSpec(a.shape, lambda b, k, i, s: (0,) * a.ndim)

    def tile_specs(width, c, per_kv=True):
        return [pl.BlockSpec((1, TQ, width),
                             lambda b, k, i, s, t=t: (b, _query_tile(t, i, n_qt), c + (k if per_kv else 0)))
                for t in range(N_TILES)]

    part = [_query_tile(t, 0, n_qt) // n_steps for t in range(N_TILES)]
    assert sorted(part) == list(range(N_TILES))
    y_specs = [pl.BlockSpec((1, TQ, wq), lambda b, k, i, s, t=t: (b, _query_tile(t, i, n_qt) - part[t] * n_steps, k))
               for t in range(N_TILES)]
    y_sds = jax.ShapeDtypeStruct((B, S // N_TILES, WIDTH_A), BF16)
    wtab, ctab, btab = _mask_tables()
    grid_spec = pltpu.PrefetchScalarGridSpec(
        num_scalar_prefetch=1,
        grid=(B, N_KV_A, n_steps),
        in_specs=[
            *tile_specs(wq, _col("qa") // wq),
            kv_spec("ksa"), kv_spec("vsa"), kv_spec("kwa"), kv_spec("vwa"),
            pl.BlockSpec((1, n_cp, LANES), lambda b, k, i, s: (k * B + b, 0, 0)),
            pl.BlockSpec((1, n_cp, LANES), lambda b, k, i, s: (k * B + b, 0, 0)),
            *tile_specs(LANES, cb("ga"), per_kv=False),
            *tile_specs(wq, _col("za") // wq),
            whole(ov), whole(wtab), whole(ctab),
            *tile_specs(wq, _col("qb") // wq),
            kv_spec("kb2"), pl.BlockSpec((1, S, LANES), lambda b, k, i, s: (b, 0, cb("vb"))),
            *tile_specs(wq, _col("zb") // wq),
            whole(btab),
        ],
        out_specs=y_specs + y_specs,
        scratch_shapes=[
            pltpu.VMEM((S // LANES, _VROWS_A, LANES), BF16),
            pltpu.VMEM((S // LANES, _VROWS_A, LANES), BF16),
            pltpu.VMEM((S // LANES, _VROWS_B, LANES), BF16),
            pltpu.VMEM((N_TILES, GROUP_A // 2, PAIR, HEAD_DIM_A), BF16),
            pltpu.VMEM((N_TILES, N_SELP, SUBLANES, TQ), F32),
            pltpu.VMEM((N_TILES, GROUP_A // 2, _VROWS_A, PAIR), F32),
            pltpu.VMEM((N_TILES, GROUP_A // 2, HEAD_DIM_A, PAIR), F32),
        ],
    )
    per_tile = lambda name: [sl(name)] * N_TILES
    outs = pl.pallas_call(
        _attn_kernel,
        grid_spec=grid_spec,
        out_shape=[y_sds] * (2 * N_TILES),
        compiler_params=pltpu.CompilerParams(
            dimension_semantics=("arbitrary", "arbitrary", "arbitrary"), vmem_limit_bytes=48 * 1024 * 1024),
        name="attn",
    )(sinks, *per_tile("qa"), sl("ksa"), sl("vsa"), sl("kwa"), sl("vwa"), kc, vc,
      *per_tile("ga"), *per_tile("za"), ov, wtab, ctab,
      *per_tile("qb"), sl("kb2"), sl("vb"), *per_tile("zb"), btab)
    in_order = lambda ys: [ys[part.index(n)] for n in range(N_TILES)]
    return in_order(outs[:N_TILES]), in_order(outs[N_TILES:])


def _out_proj_kernel(n_parts, *refs):
    ya_refs, yb_refs = refs[:n_parts], refs[n_parts:2 * n_parts]
    wa_ref, wb_ref, x_ref, g_ref, o_ref = refs[2 * n_parts:]
    part = pl.program_id(0) % n_parts

    def pick(y_refs):
        y = y_refs[n_parts - 1][...]
        for n in range(n_parts - 2, -1, -1):
            y = jnp.where(part == n, y_refs[n][...], y)
        return y

    r = x_ref[...] + _dot(pick(ya_refs), wa_ref[...]) + _dot(pick(yb_refs), wb_ref[...])
    ms = jnp.mean(r * r, axis=-1, keepdims=True)
    o_ref[...] = (r * lax.rsqrt(ms + RMS_EPS)) * g_ref[...]


def _out_proj(ya_parts, yb_parts, w_out, x2, g, S):
    M = x2.shape[0]
    n_parts = len(ya_parts)
    tm = S // n_parts
    y_spec = lambda w: pl.BlockSpec((tm, w), lambda i: (i // n_parts, 0))
    return pl.pallas_call(
        functools.partial(_out_proj_kernel, n_parts),
        grid=(M // tm,),
        in_specs=[
            *[y_spec(WIDTH_A)] * n_parts, *[y_spec(WIDTH_B)] * n_parts,
            pl.BlockSpec((WIDTH_A, D_MODEL), lambda i: (0, 0)),
            pl.BlockSpec((WIDTH_B, D_MODEL), lambda i: (WIDTH_A // WIDTH_B, 0)),
            pl.BlockSpec((tm, D_MODEL), lambda i: (i, 0)),
            pl.BlockSpec((1, D_MODEL), lambda i: (0, 0)),
        ],
        out_specs=pl.BlockSpec((tm, D_MODEL), lambda i: (i, 0)),
        out_shape=jax.ShapeDtypeStruct((M, D_MODEL), F32),
        compiler_params=pltpu.CompilerParams(
            dimension_semantics=("arbitrary",), vmem_limit_bytes=56 * 1024 * 1024),
        name="out_proj",
    )(*ya_parts, *yb_parts, w_out, w_out, x2, g)


def _overlap_matrix(n_cp):
    c_start = np.arange(n_cp) * CMP_STRIDE
    j_start = np.arange(N_SELP) * SEL_BLOCK
    ov = (c_start[None, :] < j_start[:, None] + SEL_BLOCK) & (c_start[None, :] + CMP_BLOCK > j_start[:, None])
    return jnp.asarray(ov, BF16)


def kernel(x, w_in, cmp_k_w1, cmp_k_w2, cmp_v_w1, cmp_v_w2, cmp_k_pos, cmp_v_pos, sinks, w_out, norm_g, final_g):
    B, S, D = x.shape
    assert D == D_MODEL and w_in.shape[0] == 1
    x2 = x.reshape(B * S, D)

    w_all = _prep_w_in(w_in)
    tabs = _rope_tables(S)
    slabs, strided = _in_proj(x2, norm_g[0].reshape(1, D), w_all, tabs, S)
    slabs = [p.reshape(B, S, IN_TN) for p in slabs]

    n_cp = S // CMP_STRIDE
    tk, tv = [strided[n].reshape(N_KV_A * B * n_cp, _HALF) for n in ("kca", "vca")]
    flat_pos = lambda p: jnp.broadcast_to(p.reshape(1, CMP_BLOCK * HEAD_DIM_A), (8, CMP_BLOCK * HEAD_DIM_A))
    kc, vc = _compress(tk, tv, cmp_k_w1[0].astype(BF16), cmp_v_w1[0].astype(BF16),
                       cmp_k_w2[0].astype(BF16), cmp_v_w2[0].astype(BF16),
                       flat_pos(cmp_k_pos[0]).astype(BF16), flat_pos(cmp_v_pos[0]).astype(BF16))
    kc, vc = [c.reshape(N_KV_A * B, n_cp, HEAD_DIM_A) for c in (kc, vc)]

    ya_parts, yb_parts = _attention(slabs, kc, vc, _overlap_matrix(n_cp), sinks[0])
    flat = lambda ys: [y.reshape(B * y.shape[1], y.shape[2]) for y in ys]
    out = _out_proj(flat(ya_parts), flat(yb_parts), w_out[0].astype(BF16), x2, final_g.reshape(1, D), S)
    return out.reshape(B, S, D)
```

```python
import functools
import math

import numpy as np
import jax
import jax.numpy as jnp
from jax import lax
from jax.experimental import pallas as pl
from jax.experimental.pallas import tpu as pltpu

F32 = jnp.float32
BF16 = jnp.bfloat16

D_MODEL = 2048
ROPE_THETA = 10000.0
RMS_EPS = 1e-6
NEG_INF = -1e30
TINY = 1e-30
LOG2E = math.log2(math.e)

WIDTH_A = 1024
HEAD_DIM_A = 128
N_KV_A = 2
GROUP_A = 4
KV_A = N_KV_A * HEAD_DIM_A
CMP_BLOCK = 32
CMP_STRIDE = 16
CMP_HIDDEN = 256
SEL_BLOCK = 64
SEL_TOPK = 16
WIN_A = 512
FORCE_SCORE = 1e4

WIDTH_B = 1024
HEAD_DIM_B = 64
N_HEADS_B = 16
N_KV_B = 2
GROUP_B = 8
KV_B = N_KV_B * HEAD_DIM_B
WIN_B = 128

IN_SIZES = (WIDTH_A, KV_A, KV_A, KV_A, KV_A, KV_A, KV_A, WIDTH_A, 3 * 8,
            WIDTH_B, KV_B, KV_B, WIDTH_B)

LANES = 128
SUBLANES = 8
CHUNK = 256

EP_NONE, EP_ROPE128, EP_ROPE64, EP_SILU, EP_SIGMOID, EP_ROPE128_Q, EP_ROPE64_Q = range(7)

_PROJ_LAYOUT = (
    ("qa", 8, EP_ROPE128_Q), ("kca", 2, EP_ROPE128), ("ksa", 2, EP_ROPE128), ("kwa", 2, EP_ROPE128),
    ("vca", 2, EP_NONE),
    ("vsa", 2, EP_NONE), ("vwa", 2, EP_NONE), ("za", 8, EP_SILU), ("kb2", 2, EP_ROPE64),
    ("ga", 1, EP_SIGMOID), ("vb", 1, EP_NONE),
    ("zb", 8, EP_SILU), ("qb", 8, EP_ROPE64_Q),
)
_UNIT_KINDS = tuple(k for _, n, k in _PROJ_LAYOUT for _ in range(n))
_UNIT_START = {}
_c = 0
for _name, _n, _k in _PROJ_LAYOUT:
    _UNIT_START[_name] = _c
    _c += _n
D_PROJ = _c * LANES
IN_TN = 2048
_IN_CPT = IN_TN // CHUNK
_IN_UPT = IN_TN // LANES
N_SLABS = D_PROJ // IN_TN


def _slab(name):
    return _UNIT_START[name] // _IN_UPT


def _col(name):
    return (_UNIT_START[name] % _IN_UPT) * LANES


def _dot(a, b):
    return jnp.dot(a, b, preferred_element_type=F32)


def _dot_nt(a, b):
    return lax.dot_general(a, b, (((1,), (1,)), ((), ())), preferred_element_type=F32)


IN_TM_NORM = 512
IN_TM = 1024
IN_SUB = 512

_TAB_GROUP = {EP_ROPE128: "r128", EP_ROPE128_Q: "r128q", EP_ROPE64: "r64", EP_ROPE64_Q: "r64q"}
_TAB_SIZE = {"r128": 2, "r128q": 2, "r64": 3, "r64q": 3}


def _slab_tab_groups(kinds):
    return tuple(dict.fromkeys(_TAB_GROUP[k] for k in kinds if k in _TAB_GROUP))


def _in_proj_kernel(kinds, with_norm, seq_len, strided_chunks, *refs):
    groups = _slab_tab_groups(kinds)
    refs = list(refs)
    if with_norm:
        x_ref, g_ref = refs.pop(0), refs.pop(0)
    else:
        h_ref = refs.pop(0)
    w_ref = refs.pop(0)
    tab_ref = refs.pop(0) if groups else None
    o_ref = refs.pop(0)
    if with_norm:
        h_ref = refs.pop(0)
        x = x_ref[...]
        ms = jnp.mean(x * x, axis=-1, keepdims=True)
        h_ref[...] = ((x * lax.rsqrt(ms + RMS_EPS)) * g_ref[...]).astype(BF16)
    t_refs = {c: refs.pop(0) for c in strided_chunks}
    if strided_chunks:
        stage_ref = refs.pop(0)
    tm = h_ref.shape[0]
    pos0 = (pl.program_id(0) % (seq_len // tm)) * tm
    base, n = {}, 0
    for grp in groups:
        base[grp] = n
        n += _TAB_SIZE[grp]

    def epilogue(kind, r, a):
        if kind == EP_NONE:
            return a
        if kind == EP_SILU:
            return a * jax.nn.sigmoid(a)
        if kind == EP_SIGMOID:
            return jax.nn.sigmoid(a)
        t = base[_TAB_GROUP[kind]]
        tab = lambda k: tab_ref[t + k, pl.ds(pl.multiple_of(pos0 + r * IN_SUB, IN_SUB), IN_SUB), :]
        if kind in (EP_ROPE128, EP_ROPE128_Q):
            return a * tab(0) + pltpu.roll(a, 64, 1) * tab(1)
        return a * tab(0) + pltpu.roll(a, 96, 1) * tab(1) + pltpu.roll(a, 32, 1) * tab(2)

    n_grp = IN_SUB // CMP_STRIDE

    subs = CHUNK // LANES

    def finish(r, c, acc):
        rows = slice(r * IN_SUB, (r + 1) * IN_SUB)
        for sub in range(subs):
            cols = slice(c * CHUNK + sub * LANES, c * CHUNK + (sub + 1) * LANES)
            val = epilogue(kinds[c * subs + sub], r, acc[:, sub * LANES:(sub + 1) * LANES])
            o_ref[rows, cols] = val.astype(o_ref.dtype)
            if c in t_refs:
                slot = strided_chunks.index(c) * subs + sub
                stage_ref[slot] = val
                for k in range(CMP_STRIDE):
                    t_refs[c][sub, r * n_grp:(r + 1) * n_grp, k * LANES:(k + 1) * LANES] = (
                        stage_ref[slot, pl.ds(k, n_grp, stride=CMP_STRIDE), :].astype(t_refs[c].dtype))

    jobs = []
    for r in range(tm // IN_SUB):
        for c in range(len(kinds) // subs):
            jobs.append((lambda r=r, c=c: _dot_nt(h_ref[r * IN_SUB:(r + 1) * IN_SUB, :],
                                                  w_ref[c * CHUNK:(c + 1) * CHUNK, :]),
                         functools.partial(finish, r, c)))
    _run_skewed(jobs)


def _rope_tables(S):
    def cs(d):
        inv = ROPE_THETA ** (-jnp.arange(0, d, 2, dtype=F32) / d)
        ang = jnp.arange(S, dtype=F32)[:, None] * inv[None, :]
        return jnp.cos(ang), jnp.sin(ang)

    c128, s128 = cs(HEAD_DIM_A)
    c64, s64 = cs(HEAD_DIM_B)
    z32 = jnp.zeros_like(s64)
    t128 = [jnp.concatenate([c128, c128], axis=1), jnp.concatenate([-s128, s128], axis=1)]
    t64 = [jnp.concatenate([c64, c64, c64, c64], axis=1),
           jnp.concatenate([-s64, z32, -s64, z32], axis=1),
           jnp.concatenate([z32, s64, z32, s64], axis=1)]
    qa = HEAD_DIM_A ** -0.5 * LOG2E
    qb = HEAD_DIM_B ** -0.5 * LOG2E
    return {"r128": t128, "r64": t64, "r128q": [t * qa for t in t128], "r64q": [t * qb for t in t64]}


W_PREP_TR = 256


def _w_in_pieces():
    offs = np.cumsum((0,) + IN_SIZES)
    names = ("qa", "kca", "vca", "ksa", "vsa", "kwa", "vwa", "za", "ga", "qb", "kb", "vb", "zb")
    start = {n: int(offs[i]) for i, n in enumerate(names)}
    width = dict(zip(names, IN_SIZES))
    n_g = 3 * GROUP_A
    halves = lambda n: [(start[n] + i * HEAD_DIM_B, HEAD_DIM_B) for i in (0, 0, 1, 1)]
    pieces = {n: [(start[n], width[n])]
              for n in ("qa", "kca", "ksa", "kwa", "vca", "vsa", "vwa", "za", "zb", "qb", "vb")}
    pad = LANES // N_KV_A - n_g
    pieces["ga"] = [(start["ga"], n_g), (None, pad), (start["ga"] + n_g, n_g), (None, pad)]
    pieces["kb2"] = halves("kb")
    return [p for name, _, _ in _PROJ_LAYOUT for p in pieces[name]]


def _w_prep_kernel(w_ref, o_ref):
    row = 0
    parts, filled = [], 0
    for src, wd in _w_in_pieces():
        done = 0
        while done < wd:
            take = min(wd - done, LANES - filled)
            if src is None:
                parts.append(jnp.zeros((take, W_PREP_TR), F32))
            else:
                parts.append(w_ref[src + done:src + done + take, :])
            done += take
            filled += take
            if filled == LANES:
                blk = parts[0] if len(parts) == 1 else jnp.concatenate(parts, axis=0)
                o_ref[row:row + LANES, :] = blk.astype(o_ref.dtype)
                row += LANES
                parts, filled = [], 0
    assert row == D_PROJ and not parts


def _prep_w_in(w_in):
    wt = w_in[0].T
    d_in = wt.shape[0]
    return pl.pallas_call(
        _w_prep_kernel,
        grid=(D_MODEL // W_PREP_TR,),
        in_specs=[pl.BlockSpec((d_in, W_PREP_TR), lambda i: (0, i))],
        out_specs=pl.BlockSpec((D_PROJ, W_PREP_TR), lambda i: (0, i)),
        out_shape=jax.ShapeDtypeStruct((D_PROJ, D_MODEL), BF16),
        compiler_params=pltpu.CompilerParams(dimension_semantics=("arbitrary",)),
        name="w_prep",
    )(wt)


def _in_proj(x2, g, w_all, tabs, S):
    M = x2.shape[0]
    params = pltpu.CompilerParams(dimension_semantics=("arbitrary",), vmem_limit_bytes=48 * 1024 * 1024)
    out_sds = jax.ShapeDtypeStruct((M, IN_TN), BF16)
    slabs, strided, h = [], {}, None
    for j in range(N_SLABS):
        kinds = _UNIT_KINDS[j * _IN_UPT:(j + 1) * _IN_UPT]
        with_norm = j == 0
        tm = IN_TM_NORM if with_norm else IN_TM
        groups = _slab_tab_groups(kinds)
        act_spec = pl.BlockSpec((tm, D_MODEL), lambda i: (i, 0))
        in_specs = [act_spec, pl.BlockSpec((1, D_MODEL), lambda i: (0, 0))] if with_norm else [act_spec]
        args = [x2, g] if with_norm else [h]
        in_specs.append(pl.BlockSpec((IN_TN, D_MODEL), lambda i, j=j: (j, 0)))
        args.append(w_all)
        if groups:
            tab = jnp.stack([t for grp in groups for t in tabs[grp]])
            in_specs.append(pl.BlockSpec(tab.shape, lambda i: (0, 0, 0)))
            args.append(tab)
        out_specs = [pl.BlockSpec((tm, IN_TN), lambda i: (i, 0))]
        out_shape = [out_sds]
        if with_norm:
            out_specs.append(act_spec)
            out_shape.append(jax.ShapeDtypeStruct((M, D_MODEL), BF16))
        strided_names = [name for name in ("kca", "vca") if _slab(name) == j]
        strided_chunks = tuple(_col(name) // CHUNK for name in strided_names)
        scratch = []
        for _ in strided_chunks:
            out_specs.append(pl.BlockSpec((N_KV_A, tm // CMP_STRIDE, _HALF), lambda i: (0, i, 0)))
            out_shape.append(jax.ShapeDtypeStruct((N_KV_A, M // CMP_STRIDE, _HALF), BF16))
        if strided_chunks:
            scratch.append(pltpu.VMEM((len(strided_chunks) * N_KV_A, IN_SUB, HEAD_DIM_A), F32))
        res = list(pl.pallas_call(
            functools.partial(_in_proj_kernel, kinds, with_norm, S, strided_chunks),
            grid=(M // tm,),
            in_specs=in_specs,
            out_specs=out_specs,
            out_shape=out_shape,
            scratch_shapes=scratch,
            compiler_params=params,
            name=f"in_proj_{j}",
        )(*args))
        slabs.append(res.pop(0))
        if with_norm:
            h = res.pop(0)
        for name in strided_names:
            strided[name] = res.pop(0)
    return slabs, strided


CMP_TM = 256
_HALF = CMP_STRIDE * HEAD_DIM_A


def _compress_kernel(tk_ref, tv_ref, w1k_ref, w1v_ref, w2k_ref, w2v_ref, posk_ref, posv_ref, kc_ref, vc_ref,
                     pb_ref):
    @pl.when(pl.program_id(0) == 0)
    def _():
        pb_ref[0] = _dot(posk_ref[...], w1k_ref[...])
        pb_ref[1] = _dot(posv_ref[...], w1v_ref[...])

    for n, (t_ref, w1_ref, w2_ref, o_ref) in enumerate(((tk_ref, w1k_ref, w2k_ref, kc_ref),
                                                        (tv_ref, w1v_ref, w2v_ref, vc_ref))):
        t = t_ref[...]
        u = _dot(t, w1_ref[0:_HALF, :])
        v = _dot(t, w1_ref[_HALF:2 * _HALF, :])
        pre = u + pltpu.roll(v, CMP_TM - 1, 0) + pb_ref[n, 0:1]
        hid = pre * jax.nn.sigmoid(pre)
        out = _dot(hid.astype(BF16), w2_ref[...])
        row = lax.broadcasted_iota(jnp.int32, out.shape, 0)
        n_cp = LANES
        out = jnp.where((row & (n_cp - 1)) == n_cp - 1, 0.0, out)
        o_ref[...] = out.astype(o_ref.dtype)


def _compress(tk, tv, w1k, w1v, w2k, w2v, posk, posv):
    R = tk.shape[0]
    rows = pl.BlockSpec((CMP_TM, _HALF), lambda i: (i, 0))
    whole = lambda a: pl.BlockSpec(a.shape, lambda i: (0, 0))
    o_spec = pl.BlockSpec((CMP_TM, HEAD_DIM_A), lambda i: (i, 0))
    o_sds = jax.ShapeDtypeStruct((R, HEAD_DIM_A), BF16)
    return pl.pallas_call(
        _compress_kernel,
        grid=(R // CMP_TM,),
        in_specs=[rows, rows, whole(w1k), whole(w1v), whole(w2k), whole(w2v), whole(posk), whole(posv)],
        out_specs=(o_spec, o_spec),
        out_shape=(o_sds, o_sds),
        scratch_shapes=[pltpu.VMEM((2, SUBLANES, CMP_HIDDEN), F32)],
        compiler_params=pltpu.CompilerParams(dimension_semantics=("arbitrary",)),
        name="compress",
    )(tk, tv, w1k, w1v, w2k, w2v, posk, posv)


TQ = 128
ATTN_AHEAD = 8
ONES_ROWS = 16


def _transpose_values(v_ref, vt_ref, n_rows, second_half=None):
    n_blk = vt_ref.shape[0]
    for blk in range(n_blk):
        vt = v_ref[0, blk * LANES:(blk + 1) * LANES, :].astype(F32).T
        if second_half is not None:
            vt = jnp.where(second_half == 0, vt[0:n_rows], vt[n_rows:2 * n_rows])
        vt_ref[blk, 0:n_rows, :] = vt[0:n_rows].astype(vt_ref.dtype)
        vt_ref[blk, n_rows:n_rows + ONES_ROWS, :] = jnp.ones((ONES_ROWS, LANES), vt_ref.dtype)


def _run_skewed(jobs, ahead=1):
    pending = [job[0]() for job in jobs[:ahead]]
    for i, (_, finish) in enumerate(jobs):
        if i + ahead < len(jobs):
            pending.append(jobs[i + ahead][0]())
        finish(pending[i])
        pending[i] = None


def _softmax_step(s, m, acc, vt):
    m_new = jnp.maximum(m, jnp.max(s, axis=0, keepdims=True))
    p = jnp.exp2(s - m_new).astype(BF16)
    return m_new, jnp.exp2(m - m_new) * acc + _dot(vt, p)


SEL_TK = 256
N_SELP = 32
PAIR = 2 * TQ
N_TILE_PAIRS = 2
N_TILES = 2 * N_TILE_PAIRS
_VROWS_A = HEAD_DIM_A + ONES_ROWS
_VROWS_B = HEAD_DIM_B + ONES_ROWS
_WIN_TILES = (256, 256, 128)
_N_SEL_JOBS = 9


def _query_tile(t, i, n_qt):
    low = i if t // 2 == 0 else n_qt // 2 - 1 - i
    return low if t % 2 == 0 else n_qt - 1 - low


def _attn_kernel(*refs):
    refs = list(refs)
    take = lambda n: [refs.pop(0) for _ in range(n)]
    sink_ref, = take(1)
    q_refs = take(N_TILES)
    ksel_ref, vsel_ref, kwin_ref, vwin_ref, kc_ref, vc_ref = take(6)
    g_refs, z_refs = take(N_TILES), take(N_TILES)
    ov_ref, wtab_ref, ctab_ref = take(3)
    qb_refs = take(N_TILES)
    kb_ref, vb_ref = take(2)
    zb_refs = take(N_TILES)
    btab_ref, = take(1)
    y_refs, yb_refs = take(N_TILES), take(N_TILES)
    vselt_ref, vwint_ref, vbt_ref, qs_ref, bias_ref, acc_ref, out_ref = refs

    kv = pl.program_id(1)
    i = pl.program_id(2)
    n_qt = N_TILES * pl.num_programs(2)
    n_pairs = GROUP_A // 2
    tiles = range(N_TILES)
    qts = [_query_tile(t, i, n_qt) for t in tiles]
    t0s = [qt * TQ for qt in qts]

    @pl.when(i == 0)
    def _():
        _transpose_values(vsel_ref, vselt_ref, HEAD_DIM_A)
        _transpose_values(vwin_ref, vwint_ref, HEAD_DIM_A)
        _transpose_values(vb_ref, vbt_ref, HEAD_DIM_B, kv)

    lane = lax.broadcasted_iota(jnp.int32, (1, TQ), 1)
    lane2 = jnp.concatenate([lane, lane], axis=1)
    qpair, gts = [], []
    for t in tiles:
        q = q_refs[t][0]
        pairs = [jnp.concatenate([q[:, h * LANES:(h + 1) * LANES] for h in (2 * p, 2 * p + 1)], axis=0)
                 for p in range(n_pairs)]
        for p in range(n_pairs):
            qs_ref[t, p] = pairs[p]
        qpair.append(pairs)
        gt = g_refs[t][0].astype(F32).T
        gts.append(jnp.where(kv == 0, gt[0:LANES // 2], gt[LANES // 2:LANES]))

    def gate(t, p, branch):
        return jnp.concatenate(
            [gts[t][3 * h + branch:3 * h + branch + 1] for h in (2 * p, 2 * p + 1)], axis=1)

    kc = kc_ref[0]
    vct = vc_ref[0].astype(F32).T.astype(BF16)
    cend = lax.broadcasted_iota(jnp.int32, (LANES, PAIR), 0) * CMP_STRIDE + (CMP_BLOCK - 1)
    psums = [[] for _ in tiles]

    def cmp_scores(t, p):
        return jnp.where(cend <= t0s[t] + lane2, _dot_nt(kc, qpair[t][p]), NEG_INF)

    def cmp_finish(t, p, s):
        m = jnp.max(s, axis=0, keepdims=True)
        e = jnp.where(cend <= t0s[t] + lane2, jnp.exp2(s - m), 0.0)
        pc = e / jnp.maximum(jnp.sum(e, axis=0, keepdims=True), TINY)
        psums[t].append(pc[:, 0:TQ] + pc[:, TQ:PAIR])
        out_ref[t, p] = gate(t, p, 0) * _dot(vct, pc.astype(BF16))

    ksts = [pl.multiple_of(jnp.maximum(t0 - WIN_A, 0), TQ) for t0 in t0s]
    win_m = [[jnp.full((1, PAIR), NEG_INF, F32) for _ in range(n_pairs)] for _ in tiles]
    win_acc = [[jnp.zeros((_VROWS_A, PAIR), F32) for _ in range(n_pairs)] for _ in tiles]

    win_var = [jnp.minimum(qt, WIN_A // TQ) for qt in qts]

    def win_scores(t, p, off, tk):
        kt = kwin_ref[0, pl.ds(pl.multiple_of(ksts[t] + off, LANES), tk), :]
        bias = wtab_ref[win_var[t], off:off + tk, :]
        return _dot_nt(kt, qpair[t][p]) + jnp.concatenate([bias, bias], axis=1)

    def win_finish(t, p, off, tk, s):
        vt = jnp.concatenate(
            [vwint_ref[ksts[t] // LANES + off // LANES + b] for b in range(tk // LANES)], axis=1)
        win_m[t][p], win_acc[t][p] = _softmax_step(s, win_m[t][p], win_acc[t][p], vt)
        if off + tk == sum(_WIN_TILES):
            a = win_acc[t][p]
            out_ref[t, p] = out_ref[t, p] + (
                gate(t, p, 2) / jnp.maximum(a[HEAD_DIM_A:HEAD_DIM_A + 1], TINY)) * a[0:HEAD_DIM_A]

    jobs = [(functools.partial(cmp_scores, t, p), functools.partial(cmp_finish, t, p))
            for t in tiles for p in range(n_pairs)]
    off = 0
    for tk in _WIN_TILES:
        jobs += [(functools.partial(win_scores, t, p, off, tk), functools.partial(win_finish, t, p, off, tk))
                 for t in tiles for p in range(n_pairs)]
        off += tk

    nb_keys = WIN_B + TQ
    left = lax.broadcasted_iota(jnp.int32, (TQ, LANES), 1) < HEAD_DIM_B
    zero = jnp.zeros((TQ, LANES), qb_refs[0].dtype)
    kbs = [pl.multiple_of(jnp.maximum(t0 - WIN_B, 0), TQ) for t0 in t0s]
    b_var = [jnp.minimum(qt, WIN_B // TQ) for qt in qts]

    def b_sink(c):
        return jnp.concatenate(
            [jnp.full((1, TQ), sink_ref[kv * GROUP_B + 2 * c + h] * LOG2E, F32) for h in range(2)], axis=1)

    def b_scores(t, c):
        pair = qb_refs[t][0, :, c * LANES:(c + 1) * LANES]
        qp = jnp.concatenate([jnp.where(left, pair, zero), jnp.where(left, zero, pair)], axis=0)
        bias = btab_ref[b_var[t]]
        return _dot_nt(kb_ref[0, pl.ds(kbs[t], nb_keys), :], qp) + jnp.concatenate([bias, bias], axis=1)

    def b_finish(t, c, s):
        vt = jnp.concatenate([vbt_ref[kbs[t] // LANES + b] for b in range(nb_keys // LANES)], axis=1)
        sink = b_sink(c)
        m = jnp.maximum(jnp.max(s, axis=0, keepdims=True), sink)
        p = jnp.exp2(s - m).astype(BF16)
        a = _dot(vt, p)
        den = jnp.maximum(a[HEAD_DIM_B:HEAD_DIM_B + 1] + jnp.exp2(sink - m), TINY)
        o = a[0:HEAD_DIM_B] / den
        o2 = jnp.concatenate([o[:, 0:TQ], o[:, TQ:PAIR]], axis=0)
        z = zb_refs[t][0, :, c * LANES:(c + 1) * LANES]
        yb_refs[t][0, :, c * LANES:(c + 1) * LANES] = (o2.T * z.astype(F32)).astype(yb_refs[t].dtype)

    b_jobs = [[(functools.partial(b_scores, t, c), functools.partial(b_finish, t, c))
               for c in range(GROUP_B // 2)] for t in tiles]

    def select_blocks(t):
        ov = ov_ref[...]
        jidx = lax.broadcasted_iota(jnp.int32, (N_SELP, TQ), 0)
        psum = psums[t][0] + psums[t][1]
        p_hi = psum.astype(BF16)
        r1 = psum - p_hi.astype(F32)
        p_mid = r1.astype(BF16)
        p_lo = (r1 - p_mid.astype(F32)).astype(BF16)
        psel = _dot(ov, p_hi) + _dot(ov, p_mid) + _dot(ov, p_lo)
        tq = t0s[t] + lane
        cur = tq // SEL_BLOCK
        forced = (jidx == 0) | (jidx == cur) | (jidx == cur - 1)
        score = jnp.where(forced, FORCE_SCORE, jnp.where(jidx * SEL_BLOCK <= tq, psel, -1.0))
        rank = jnp.zeros(psel.shape, F32)
        for r in range(N_SELP):
            row = jnp.broadcast_to(score[r:r + 1, :], score.shape)
            rank = rank + jnp.where(jidx > r, jnp.where(row >= score, 1.0, 0.0), jnp.where(row > score, 1.0, 0.0))
        bias_t = jnp.where(rank < SEL_TOPK, 0.0, NEG_INF)
        for j in range(N_SELP):
            bias_ref[t, j] = jnp.broadcast_to(bias_t[j:j + 1, :], (SUBLANES, TQ))

    blk_per_tile = SEL_TK // SEL_BLOCK
    rep = SEL_BLOCK // SUBLANES
    acc_ref[...] = jnp.zeros(acc_ref.shape, F32)
    ms = [[jnp.full((1, PAIR), NEG_INF, F32) for _ in range(n_pairs)] for _ in tiles]
    ctx = {}

    def sel_scores(g, k, p):
        if p == 0:
            if g == 0 and k == 0:
                for t in tiles:
                    select_blocks(t)
            n_lo = qts[2 * g] // (SEL_TK // TQ) + 1
            hi = k >= n_lo
            t = 2 * g + hi.astype(jnp.int32)
            j = jnp.where(hi, k - n_lo, k)
            ks = pl.multiple_of(j * SEL_TK, SEL_TK)
            bt = jnp.concatenate(
                [bias_ref[t, j * blk_per_tile + b] for b in range(blk_per_tile) for _ in range(rep)], axis=0)
            qt = jnp.where(hi, qts[2 * g + 1], qts[2 * g])
            diag = j == qt // (SEL_TK // TQ)
            bt = bt + ctab_ref[jnp.where(diag, 1 + qt % (SEL_TK // TQ), 0)]
            bt = jnp.concatenate([bt, bt], axis=1)
            ctx[g, k] = (hi, t, j, ks, bt)
        hi, t, j, ks, bt = ctx[g, k]
        return _dot_nt(ksel_ref[0, pl.ds(ks, SEL_TK), :], qs_ref[t, p]) + bt

    def sel_finish(g, k, p, s):
        hi, t, j, ks, bt = ctx[g, k]
        lo_t, hi_t = 2 * g, 2 * g + 1
        vt = jnp.concatenate([vselt_ref[j * (SEL_TK // LANES) + b] for b in range(SEL_TK // LANES)], axis=1)
        m_new, acc_new = _softmax_step(s, jnp.where(hi, ms[hi_t][p], ms[lo_t][p]), acc_ref[t, p], vt)
        acc_ref[t, p] = acc_new
        ms[lo_t][p] = jnp.where(hi, ms[lo_t][p], m_new)
        ms[hi_t][p] = jnp.where(hi, m_new, ms[hi_t][p])

    sel_jobs = [(functools.partial(sel_scores, g, k, p), functools.partial(sel_finish, g, k, p))
                for g in range(N_TILE_PAIRS) for k in range(_N_SEL_JOBS) for p in range(n_pairs)]
    b_before = [job for t in tiles if t % 2 == 0 for job in b_jobs[t]]
    b_after = [job for t in tiles if t % 2 == 1 for job in b_jobs[t]]
    _run_skewed(jobs + b_before + sel_jobs + b_after, ATTN_AHEAD)

    for t in tiles:
        z = z_refs[t][0]
        for p in range(n_pairs):
            a = acc_ref[t, p]
            o = out_ref[t, p] + (gate(t, p, 1) / jnp.maximum(a[HEAD_DIM_A:HEAD_DIM_A + 1], TINY)) * a[0:HEAD_DIM_A]
            for h in (2 * p, 2 * p + 1):
                oh = o[:, (h % 2) * TQ:(h % 2 + 1) * TQ].T
                y_refs[t][0, :, h * LANES:(h + 1) * LANES] = (
                    oh * z[:, h * LANES:(h + 1) * LANES].astype(F32)).astype(y_refs[t].dtype)


def _mask_tables():
    lane = np.arange(TQ)[None, :]
    row = np.arange(WIN_A + TQ)[:, None]
    wtab = []
    for v in range(WIN_A // TQ + 1):
        t0 = v * TQ
        dpos = (t0 + lane) - (max(t0 - WIN_A, 0) + row)
        wtab.append(np.where((dpos >= 0) & (dpos < WIN_A), 0.0, NEG_INF))
    row = np.arange(SEL_TK)[:, None]
    ctab = [np.zeros((SEL_TK, TQ))]
    for odd in range(SEL_TK // TQ):
        ctab.append(np.where(row <= odd * TQ + lane, 0.0, NEG_INF))
    row = np.arange(WIN_B + TQ)[:, None]
    btab = []
    for v in range(WIN_B // TQ + 1):
        t0 = v * TQ
        dpos = (t0 + lane) - (max(t0 - WIN_B, 0) + row)
        btab.append(np.where((dpos >= 0) & (dpos < WIN_B), 0.0, NEG_INF))
    return [jnp.asarray(np.stack(t), F32) for t in (wtab, ctab, btab)]


def _attention(slabs, kc, vc, ov, sinks):
    B, S, _ = slabs[0].shape
    sl = lambda name: slabs[_slab(name)]
    n_cp = S // CMP_STRIDE
    n_qt = S // TQ
    n_steps = n_qt // N_TILES
    assert n_cp == LANES and S // SEL_BLOCK == N_SELP and (n_qt // 2) // (SEL_TK // TQ) * 2 + 1 == _N_SEL_JOBS
    assert N_KV_A == N_KV_B and GROUP_A * HEAD_DIM_A == GROUP_B * HEAD_DIM_B and N_TILE_PAIRS in (1, 2)
    wq = GROUP_A * HEAD_DIM_A
    cb = lambda name: _col(name) // LANES
    kv_spec = lambda name: pl.BlockSpec((1, S, LANES), lambda b, k, i, s, c=cb(name): (b, 0, c + k))
    whole = lambda a: pl.BlockSpec(a.shape, lambda b, k, i, s: (0,) * a.ndim)

    def tile_specs(width, c, per_kv=True):
        return [pl.BlockSpec((1, TQ, width),
                             lambda b, k, i, s, t=t: (b, _query_tile(t, i, n_qt), c + (k if per_kv else 0)))
                for t in range(N_TILES)]

    part = [_query_tile(t, 0, n_qt) // n_steps for t in range(N_TILES)]
    assert sorted(part) == list(range(N_TILES))
    y_specs = [pl.BlockSpec((1, TQ, wq), lambda b, k, i, s, t=t: (b, _query_tile(t, i, n_qt) - part[t] * n_steps, k))
               for t in range(N_TILES)]
    y_sds = jax.ShapeDtypeStruct((B, S // N_TILES, WIDTH_A), BF16)
    wtab, ctab, btab = _mask_tables()
    grid_spec = pltpu.PrefetchScalarGridSpec(
        num_scalar_prefetch=1,
        grid=(B, N_KV_A, n_steps),
        in_specs=[
            *tile_specs(wq, _col("qa") // wq),
            kv_spec("ksa"), kv_spec("vsa"), kv_spec("kwa"), kv_spec("vwa"),
            pl.BlockSpec((1, n_cp, LANES), lambda b, k, i, s: (k * B + b, 0, 0)),
            pl.BlockSpec((1, n_cp, LANES), lambda b, k, i, s: (k * B + b, 0, 0)),
            *tile_specs(LANES, cb("ga"), per_kv=False),
            *tile_specs(wq, _col("za") // wq),
            whole(ov), whole(wtab), whole(ctab),
            *tile_specs(wq, _col("qb") // wq),
            kv_spec("kb2"), pl.BlockSpec((1, S, LANES), lambda b, k, i, s: (b, 0, cb("vb"))),
            *tile_specs(wq, _col("zb") // wq),
            whole(btab),
        ],
        out_specs=y_specs + y_specs,
        scratch_shapes=[
            pltpu.VMEM((S // LANES, _VROWS_A, LANES), BF16),
            pltpu.VMEM((S // LANES, _VROWS_A, LANES), BF16),
            pltpu.VMEM((S // LANES, _VROWS_B, LANES), BF16),
            pltpu.VMEM((N_TILES, GROUP_A // 2, PAIR, HEAD_DIM_A), BF16),
            pltpu.VMEM((N_TILES, N_SELP, SUBLANES, TQ), F32),
            pltpu.VMEM((N_TILES, GROUP_A // 2, _VROWS_A, PAIR), F32),
            pltpu.VMEM((N_TILES, GROUP_A // 2, HEAD_DIM_A, PAIR), F32),
        ],
    )
    per_tile = lambda name: [sl(name)] * N_TILES
    outs = pl.pallas_call(
        _attn_kernel,
        grid_spec=grid_spec,
        out_shape=[y_sds] * (2 * N_TILES),
        compiler_params=pltpu.CompilerParams(
            dimension_semantics=("arbitrary", "arbitrary", "arbitrary"), vmem_limit_bytes=48 * 1024 * 1024),
        name="attn",
    )(sinks, *per_tile("qa"), sl("ksa"), sl("vsa"), sl("kwa"), sl("vwa"), kc, vc,
      *per_tile("ga"), *per_tile("za"), ov, wtab, ctab,
      *per_tile("qb"), sl("kb2"), sl("vb"), *per_tile("zb"), btab)
    in_order = lambda ys: [ys[part.index(n)] for n in range(N_TILES)]
    return in_order(outs[:N_TILES]), in_order(outs[N_TILES:])


def _out_proj_kernel(n_parts, *refs):
    ya_refs, yb_refs = refs[:n_parts], refs[n_parts:2 * n_parts]
    wa_ref, wb_ref, x_ref, g_ref, o_ref = refs[2 * n_parts:]
    part = pl.program_id(0) % n_parts

    def pick(y_refs):
        y = y_refs[n_parts - 1][...]
        for n in range(n_parts - 2, -1, -1):
            y = jnp.where(part == n, y_refs[n][...], y)
        return y

    r = x_ref[...] + _dot(pick(ya_refs), wa_ref[...]) + _dot(pick(yb_refs), wb_ref[...])
    ms = jnp.mean(r * r, axis=-1, keepdims=True)
    o_ref[...] = (r * lax.rsqrt(ms + RMS_EPS)) * g_ref[...]


def _out_proj(ya_parts, yb_parts, w_out, x2, g, S):
    M = x2.shape[0]
    n_parts = len(ya_parts)
    tm = S // n_parts
    n_seq = M // S
    y_spec = lambda w, n: pl.BlockSpec(
        (tm, w), lambda i: (jnp.minimum((i + n_parts - 1 - n) // n_parts, n_seq - 1), 0))
    return pl.pallas_call(
        functools.partial(_out_proj_kernel, n_parts),
        grid=(M // tm,),
        in_specs=[
            *[y_spec(WIDTH_A, n) for n in range(n_parts)], *[y_spec(WIDTH_B, n) for n in range(n_parts)],
            pl.BlockSpec((WIDTH_A, D_MODEL), lambda i: (0, 0)),
            pl.BlockSpec((WIDTH_B, D_MODEL), lambda i: (WIDTH_A // WIDTH_B, 0)),
            pl.BlockSpec((tm, D_MODEL), lambda i: (i, 0)),
            pl.BlockSpec((1, D_MODEL), lambda i: (0, 0)),
        ],
        out_specs=pl.BlockSpec((tm, D_MODEL), lambda i: (i, 0)),
        out_shape=jax.ShapeDtypeStruct((M, D_MODEL), F32),
        compiler_params=pltpu.CompilerParams(
            dimension_semantics=("arbitrary",), vmem_limit_bytes=56 * 1024 * 1024),
        name="out_proj",
    )(*ya_parts, *yb_parts, w_out, w_out, x2, g)


def _overlap_matrix(n_cp):
    c_start = np.arange(n_cp) * CMP_STRIDE
    j_start = np.arange(N_SELP) * SEL_BLOCK
    ov = (c_start[None, :] < j_start[:, None] + SEL_BLOCK) & (c_start[None, :] + CMP_BLOCK > j_start[:, None])
    return jnp.asarray(ov, BF16)


def kernel(x, w_in, cmp_k_w1, cmp_k_w2, cmp_v_w1, cmp_v_w2, cmp_k_pos, cmp_v_pos, sinks, w_out, norm_g, final_g):
    B, S, D = x.shape
    assert D == D_MODEL and w_in.shape[0] == 1
    x2 = x.reshape(B * S, D)

    w_all = _prep_w_in(w_in)
    tabs = _rope_tables(S)
    slabs, strided = _in_proj(x2, norm_g[0].reshape(1, D), w_all, tabs, S)
    slabs = [p.reshape(B, S, IN_TN) for p in slabs]

    n_cp = S // CMP_STRIDE
    tk, tv = [strided[n].reshape(N_KV_A * B * n_cp, _HALF) for n in ("kca", "vca")]
    flat_pos = lambda p: jnp.broadcast_to(p.reshape(1, CMP_BLOCK * HEAD_DIM_A), (8, CMP_BLOCK * HEAD_DIM_A))
    kc, vc = _compress(tk, tv, cmp_k_w1[0].astype(BF16), cmp_v_w1[0].astype(BF16),
                       cmp_k_w2[0].astype(BF16), cmp_v_w2[0].astype(BF16),
                       flat_pos(cmp_k_pos[0]).astype(BF16), flat_pos(cmp_v_pos[0]).astype(BF16))
    kc, vc = [c.reshape(N_KV_A * B, n_cp, HEAD_DIM_A) for c in (kc, vc)]

    ya_parts, yb_parts = _attention(slabs, kc, vc, _overlap_matrix(n_cp), sinks[0])
    flat = lambda ys: [y.reshape(B * y.shape[1], y.shape[2]) for y in ys]
    out = _out_proj(flat(ya_parts), flat(yb_parts), w_out[0].astype(BF16), x2, final_g.reshape(1, D), S)
    return out.reshape(B, S, D)
```

```python
import functools
import math

import numpy as np
import jax
import jax.numpy as jnp
from jax import lax
from jax.experimental import pallas as pl
from jax.experimental.pallas import tpu as pltpu

F32 = jnp.float32
BF16 = jnp.bfloat16

D_MODEL = 2048
ROPE_THETA = 10000.0
RMS_EPS = 1e-6
NEG_INF = -1e30
TINY = 1e-30
LOG2E = math.log2(math.e)

WIDTH_A = 1024
HEAD_DIM_A = 128
N_KV_A = 2
GROUP_A = 4
KV_A = N_KV_A * HEAD_DIM_A
CMP_BLOCK = 32
CMP_STRIDE = 16
CMP_HIDDEN = 256
SEL_BLOCK = 64
SEL_TOPK = 16
WIN_A = 512
FORCE_SCORE = 1e4

WIDTH_B = 1024
HEAD_DIM_B = 64
N_HEADS_B = 16
N_KV_B = 2
GROUP_B = 8
KV_B = N_KV_B * HEAD_DIM_B
WIN_B = 128

IN_SIZES = (WIDTH_A, KV_A, KV_A, KV_A, KV_A, KV_A, KV_A, WIDTH_A, 3 * 8,
            WIDTH_B, KV_B, KV_B, WIDTH_B)

LANES = 128
SUBLANES = 8
CHUNK = 256

EP_NONE, EP_ROPE128, EP_ROPE64, EP_SILU, EP_SIGMOID, EP_ROPE128_Q, EP_ROPE64_Q = range(7)

_PROJ_LAYOUT = (
    ("qa", 8, EP_ROPE128_Q), ("kca", 2, EP_ROPE128), ("ksa", 2, EP_ROPE128), ("kwa", 2, EP_ROPE128),
    ("vca", 2, EP_NONE),
    ("vsa", 2, EP_NONE), ("vwa", 2, EP_NONE), ("za", 8, EP_SILU), ("kb2", 2, EP_ROPE64),
    ("ga", 1, EP_SIGMOID), ("vb", 1, EP_NONE),
    ("zb", 8, EP_SILU), ("qb", 8, EP_ROPE64_Q),
)
_UNIT_KINDS = tuple(k for _, n, k in _PROJ_LAYOUT for _ in range(n))
_UNIT_START = {}
_c = 0
for _name, _n, _k in _PROJ_LAYOUT:
    _UNIT_START[_name] = _c
    _c += _n
D_PROJ = _c * LANES
IN_TN = 2048
_IN_CPT = IN_TN // CHUNK
_IN_UPT = IN_TN // LANES
N_SLABS = D_PROJ // IN_TN


def _slab(name):
    return _UNIT_START[name] // _IN_UPT


def _col(name):
    return (_UNIT_START[name] % _IN_UPT) * LANES


def _dot(a, b):
    return jnp.dot(a, b, preferred_element_type=F32)


def _dot_nt(a, b):
    return lax.dot_general(a, b, (((1,), (1,)), ((), ())), preferred_element_type=F32)


IN_TM_NORM = 512
IN_TM = 1024
IN_SUB = 512

_TAB_GROUP = {EP_ROPE128: "r128", EP_ROPE128_Q: "r128q", EP_ROPE64: "r64", EP_ROPE64_Q: "r64q"}
_TAB_SIZE = {"r128": 2, "r128q": 2, "r64": 3, "r64q": 3}


def _slab_tab_groups(kinds):
    return tuple(dict.fromkeys(_TAB_GROUP[k] for k in kinds if k in _TAB_GROUP))


def _in_proj_kernel(kinds, with_norm, seq_len, strided_chunks, *refs):
    groups = _slab_tab_groups(kinds)
    refs = list(refs)
    if with_norm:
        x_ref, g_ref = refs.pop(0), refs.pop(0)
    else:
        h_ref = refs.pop(0)
    w_ref = refs.pop(0)
    tab_ref = refs.pop(0) if groups else None
    o_ref = refs.pop(0)
    if with_norm:
        h_ref = refs.pop(0)
        x = x_ref[...]
        ms = jnp.mean(x * x, axis=-1, keepdims=True)
        h_ref[...] = ((x * lax.rsqrt(ms + RMS_EPS)) * g_ref[...]).astype(BF16)
    t_refs = {c: refs.pop(0) for c in strided_chunks}
    if strided_chunks:
        stage_ref = refs.pop(0)
    tm = h_ref.shape[0]
    pos0 = (pl.program_id(0) % (seq_len // tm)) * tm
    base, n = {}, 0
    for grp in groups:
        base[grp] = n
        n += _TAB_SIZE[grp]

    def epilogue(kind, r, a):
        if kind == EP_NONE:
            return a
        if kind == EP_SILU:
            return a * jax.nn.sigmoid(a)
        if kind == EP_SIGMOID:
            return jax.nn.sigmoid(a)
        t = base[_TAB_GROUP[kind]]
        tab = lambda k: tab_ref[t + k, pl.ds(pl.multiple_of(pos0 + r * IN_SUB, IN_SUB), IN_SUB), :]
        if kind in (EP_ROPE128, EP_ROPE128_Q):
            return a * tab(0) + pltpu.roll(a, 64, 1) * tab(1)
        return a * tab(0) + pltpu.roll(a, 96, 1) * tab(1) + pltpu.roll(a, 32, 1) * tab(2)

    n_grp = IN_SUB // CMP_STRIDE

    subs = CHUNK // LANES

    def finish(r, c, acc):
        rows = slice(r * IN_SUB, (r + 1) * IN_SUB)
        for sub in range(subs):
            cols = slice(c * CHUNK + sub * LANES, c * CHUNK + (sub + 1) * LANES)
            val = epilogue(kinds[c * subs + sub], r, acc[:, sub * LANES:(sub + 1) * LANES])
            o_ref[rows, cols] = val.astype(o_ref.dtype)
            if c in t_refs:
                slot = strided_chunks.index(c) * subs + sub
                stage_ref[slot] = val
                for k in range(CMP_STRIDE):
                    t_refs[c][sub, r * n_grp:(r + 1) * n_grp, k * LANES:(k + 1) * LANES] = (
                        stage_ref[slot, pl.ds(k, n_grp, stride=CMP_STRIDE), :].astype(t_refs[c].dtype))

    jobs = []
    for r in range(tm // IN_SUB):
        for c in range(len(kinds) // subs):
            jobs.append((lambda r=r, c=c: _dot_nt(h_ref[r * IN_SUB:(r + 1) * IN_SUB, :],
                                                  w_ref[c * CHUNK:(c + 1) * CHUNK, :]),
                         functools.partial(finish, r, c)))
    _run_skewed(jobs)


def _rope_tables(S):
    def cs(d):
        inv = ROPE_THETA ** (-jnp.arange(0, d, 2, dtype=F32) / d)
        ang = jnp.arange(S, dtype=F32)[:, None] * inv[None, :]
        return jnp.cos(ang), jnp.sin(ang)

    c128, s128 = cs(HEAD_DIM_A)
    c64, s64 = cs(HEAD_DIM_B)
    z32 = jnp.zeros_like(s64)
    t128 = [jnp.concatenate([c128, c128], axis=1), jnp.concatenate([-s128, s128], axis=1)]
    t64 = [jnp.concatenate([c64, c64, c64, c64], axis=1),
           jnp.concatenate([-s64, z32, -s64, z32], axis=1),
           jnp.concatenate([z32, s64, z32, s64], axis=1)]
    qa = HEAD_DIM_A ** -0.5 * LOG2E
    qb = HEAD_DIM_B ** -0.5 * LOG2E
    return {"r128": t128, "r64": t64, "r128q": [t * qa for t in t128], "r64q": [t * qb for t in t64]}


W_PREP_TR = 256


def _w_in_pieces():
    offs = np.cumsum((0,) + IN_SIZES)
    names = ("qa", "kca", "vca", "ksa", "vsa", "kwa", "vwa", "za", "ga", "qb", "kb", "vb", "zb")
    start = {n: int(offs[i]) for i, n in enumerate(names)}
    width = dict(zip(names, IN_SIZES))
    n_g = 3 * GROUP_A
    halves = lambda n: [(start[n] + i * HEAD_DIM_B, HEAD_DIM_B) for i in (0, 0, 1, 1)]
    pieces = {n: [(start[n], width[n])]
              for n in ("qa", "kca", "ksa", "kwa", "vca", "vsa", "vwa", "za", "zb", "qb", "vb")}
    pad = LANES // N_KV_A - n_g
    pieces["ga"] = [(start["ga"], n_g), (None, pad), (start["ga"] + n_g, n_g), (None, pad)]
    pieces["kb2"] = halves("kb")
    return [p for name, _, _ in _PROJ_LAYOUT for p in pieces[name]]


def _w_prep_kernel(w_ref, o_ref):
    row = 0
    parts, filled = [], 0
    for src, wd in _w_in_pieces():
        done = 0
        while done < wd:
            take = min(wd - done, LANES - filled)
            if src is None:
                parts.append(jnp.zeros((take, W_PREP_TR), F32))
            else:
                parts.append(w_ref[src + done:src + done + take, :])
            done += take
            filled += take
            if filled == LANES:
                blk = parts[0] if len(parts) == 1 else jnp.concatenate(parts, axis=0)
                o_ref[row:row + LANES, :] = blk.astype(o_ref.dtype)
                row += LANES
                parts, filled = [], 0
    assert row == D_PROJ and not parts


def _prep_w_in(w_in):
    wt = w_in[0].T
    d_in = wt.shape[0]
    return pl.pallas_call(
        _w_prep_kernel,
        grid=(D_MODEL // W_PREP_TR,),
        in_specs=[pl.BlockSpec((d_in, W_PREP_TR), lambda i: (0, i))],
        out_specs=pl.BlockSpec((D_PROJ, W_PREP_TR), lambda i: (0, i)),
        out_shape=jax.ShapeDtypeStruct((D_PROJ, D_MODEL), BF16),
        compiler_params=pltpu.CompilerParams(dimension_semantics=("arbitrary",)),
        name="w_prep",
    )(wt)


def _in_proj(x2, g, w_all, tabs, S):
    M = x2.shape[0]
    params = pltpu.CompilerParams(dimension_semantics=("arbitrary",), vmem_limit_bytes=48 * 1024 * 1024)
    out_sds = jax.ShapeDtypeStruct((M, IN_TN), BF16)
    slabs, strided, h = [], {}, None
    for j in range(N_SLABS):
        kinds = _UNIT_KINDS[j * _IN_UPT:(j + 1) * _IN_UPT]
        with_norm = j == 0
        tm = IN_TM_NORM if with_norm else IN_TM
        groups = _slab_tab_groups(kinds)
        act_spec = pl.BlockSpec((tm, D_MODEL), lambda i: (i, 0))
        in_specs = [act_spec, pl.BlockSpec((1, D_MODEL), lambda i: (0, 0))] if with_norm else [act_spec]
        args = [x2, g] if with_norm else [h]
        in_specs.append(pl.BlockSpec((IN_TN, D_MODEL), lambda i, j=j: (j, 0)))
        args.append(w_all)
        if groups:
            tab = jnp.stack([t for grp in groups for t in tabs[grp]])
            in_specs.append(pl.BlockSpec(tab.shape, lambda i: (0, 0, 0)))
            args.append(tab)
        out_specs = [pl.BlockSpec((tm, IN_TN), lambda i: (i, 0))]
        out_shape = [out_sds]
        if with_norm:
            out_specs.append(act_spec)
            out_shape.append(jax.ShapeDtypeStruct((M, D_MODEL), BF16))
        strided_names = [name for name in ("kca", "vca") if _slab(name) == j]
        strided_chunks = tuple(_col(name) // CHUNK for name in strided_names)
        scratch = []
        for _ in strided_chunks:
            out_specs.append(pl.BlockSpec((N_KV_A, tm // CMP_STRIDE, _HALF), lambda i: (0, i, 0)))
            out_shape.append(jax.ShapeDtypeStruct((N_KV_A, M // CMP_STRIDE, _HALF), BF16))
        if strided_chunks:
            scratch.append(pltpu.VMEM((len(strided_chunks) * N_KV_A, IN_SUB, HEAD_DIM_A), F32))
        res = list(pl.pallas_call(
            functools.partial(_in_proj_kernel, kinds, with_norm, S, strided_chunks),
            grid=(M // tm,),
            in_specs=in_specs,
            out_specs=out_specs,
            out_shape=out_shape,
            scratch_shapes=scratch,
            compiler_params=params,
            name=f"in_proj_{j}",
        )(*args))
        slabs.append(res.pop(0))
        if with_norm:
            h = res.pop(0)
        for name in strided_names:
            strided[name] = res.pop(0)
    return slabs, strided


CMP_TM = 256
_HALF = CMP_STRIDE * HEAD_DIM_A


def _compress_kernel(tk_ref, tv_ref, w1k_ref, w1v_ref, w2k_ref, w2v_ref, posk_ref, posv_ref, kc_ref, vc_ref,
                     pb_ref):
    @pl.when(pl.program_id(0) == 0)
    def _():
        pb_ref[0] = _dot(posk_ref[...], w1k_ref[...])
        pb_ref[1] = _dot(posv_ref[...], w1v_ref[...])

    for n, (t_ref, w1_ref, w2_ref, o_ref) in enumerate(((tk_ref, w1k_ref, w2k_ref, kc_ref),
                                                        (tv_ref, w1v_ref, w2v_ref, vc_ref))):
        t = t_ref[...]
        u = _dot(t, w1_ref[0:_HALF, :])
        v = _dot(t, w1_ref[_HALF:2 * _HALF, :])
        pre = u + pltpu.roll(v, CMP_TM - 1, 0) + pb_ref[n, 0:1]
        hid = pre * jax.nn.sigmoid(pre)
        out = _dot(hid.astype(BF16), w2_ref[...])
        row = lax.broadcasted_iota(jnp.int32, out.shape, 0)
        n_cp = LANES
        out = jnp.where((row & (n_cp - 1)) == n_cp - 1, 0.0, out)
        o_ref[...] = out.astype(o_ref.dtype)


def _compress(tk, tv, w1k, w1v, w2k, w2v, posk, posv):
    R = tk.shape[0]
    rows = pl.BlockSpec((CMP_TM, _HALF), lambda i: (i, 0))
    whole = lambda a: pl.BlockSpec(a.shape, lambda i: (0, 0))
    o_spec = pl.BlockSpec((CMP_TM, HEAD_DIM_A), lambda i: (i, 0))
    o_sds = jax.ShapeDtypeStruct((R, HEAD_DIM_A), BF16)
    return pl.pallas_call(
        _compress_kernel,
        grid=(R // CMP_TM,),
        in_specs=[rows, rows, whole(w1k), whole(w1v), whole(w2k), whole(w2v), whole(posk), whole(posv)],
        out_specs=(o_spec, o_spec),
        out_shape=(o_sds, o_sds),
        scratch_shapes=[pltpu.VMEM((2, SUBLANES, CMP_HIDDEN), F32)],
        compiler_params=pltpu.CompilerParams(dimension_semantics=("arbitrary",)),
        name="compress",
    )(tk, tv, w1k, w1v, w2k, w2v, posk, posv)


TQ = 128
ATTN_AHEAD = 8
ONES_ROWS = 16


def _transpose_values(v_ref, vt_ref, n_rows, second_half=None):
    n_blk = vt_ref.shape[0]
    for blk in range(n_blk):
        vt = v_ref[0, blk * LANES:(blk + 1) * LANES, :].astype(F32).T
        if second_half is not None:
            vt = jnp.where(second_half == 0, vt[0:n_rows], vt[n_rows:2 * n_rows])
        vt_ref[blk, 0:n_rows, :] = vt[0:n_rows].astype(vt_ref.dtype)
        vt_ref[blk, n_rows:n_rows + ONES_ROWS, :] = jnp.ones((ONES_ROWS, LANES), vt_ref.dtype)


def _run_skewed(jobs, ahead=1):
    pending = [job[0]() for job in jobs[:ahead]]
    for i, (_, finish) in enumerate(jobs):
        if i + ahead < len(jobs):
            pending.append(jobs[i + ahead][0]())
        finish(pending[i])
        pending[i] = None


def _softmax_step(s, m, acc, vt):
    m_new = jnp.maximum(m, jnp.max(s, axis=0, keepdims=True))
    p = jnp.exp2(s - m_new).astype(BF16)
    return m_new, jnp.exp2(m - m_new) * acc + _dot(vt, p)


SEL_TK = 256
N_SELP = 32
PAIR = 2 * TQ
N_TILE_PAIRS = 2
N_TILES = 2 * N_TILE_PAIRS
_VROWS_A = HEAD_DIM_A + ONES_ROWS
_VROWS_B = HEAD_DIM_B + ONES_ROWS
_WIN_TILES = (256, 256, 128)
_N_SEL_JOBS = 9


def _query_tile(t, i, n_qt):
    low = i if t // 2 == 0 else n_qt // 2 - 1 - i
    return low if t % 2 == 0 else n_qt - 1 - low


def _attn_kernel(*refs):
    refs = list(refs)
    take = lambda n: [refs.pop(0) for _ in range(n)]
    sink_ref, = take(1)
    q_refs = take(N_TILES)
    ksel_ref, vsel_ref, kwin_ref, vwin_ref, kc_ref, vc_ref = take(6)
    g_refs, z_refs = take(N_TILES), take(N_TILES)
    ov_ref, wtab_ref, ctab_ref = take(3)
    qb_refs = take(N_TILES)
    kb_ref, vb_ref = take(2)
    zb_refs = take(N_TILES)
    btab_ref, = take(1)
    y_refs, yb_refs = take(N_TILES), take(N_TILES)
    vselt_ref, vwint_ref, vbt_ref, qs_ref, bias_ref, acc_ref, out_ref = refs

    kv = pl.program_id(1)
    i = pl.program_id(2)
    n_qt = N_TILES * pl.num_programs(2)
    n_pairs = GROUP_A // 2
    tiles = range(N_TILES)
    qts = [_query_tile(t, i, n_qt) for t in tiles]
    t0s = [qt * TQ for qt in qts]

    @pl.when(i == 0)
    def _():
        _transpose_values(vsel_ref, vselt_ref, HEAD_DIM_A)
        _transpose_values(vwin_ref, vwint_ref, HEAD_DIM_A)
        _transpose_values(vb_ref, vbt_ref, HEAD_DIM_B, kv)

    lane = lax.broadcasted_iota(jnp.int32, (1, TQ), 1)
    lane2 = jnp.concatenate([lane, lane], axis=1)
    qpair, gts = [], []
    for t in tiles:
        q = q_refs[t][0]
        pairs = [jnp.concatenate([q[:, h * LANES:(h + 1) * LANES] for h in (2 * p, 2 * p + 1)], axis=0)
                 for p in range(n_pairs)]
        for p in range(n_pairs):
            qs_ref[t, p] = pairs[p]
        qpair.append(pairs)
        gt = g_refs[t][0].astype(F32).T
        gts.append(jnp.where(kv == 0, gt[0:LANES // 2], gt[LANES // 2:LANES]))

    def gate(t, p, branch):
        return jnp.concatenate(
            [gts[t][3 * h + branch:3 * h + branch + 1] for h in (2 * p, 2 * p + 1)], axis=1)

    kc = kc_ref[0]
    vct = vc_ref[0].astype(F32).T.astype(BF16)
    cend = lax.broadcasted_iota(jnp.int32, (LANES, PAIR), 0) * CMP_STRIDE + (CMP_BLOCK - 1)
    psums = [[] for _ in tiles]

    def cmp_scores(t, p):
        return jnp.where(cend <= t0s[t] + lane2, _dot_nt(kc, qpair[t][p]), NEG_INF)

    def cmp_finish(t, p, s):
        m = jnp.max(s, axis=0, keepdims=True)
        e = jnp.where(cend <= t0s[t] + lane2, jnp.exp2(s - m), 0.0)
        pc = e / jnp.maximum(jnp.sum(e, axis=0, keepdims=True), TINY)
        psums[t].append(pc[:, 0:TQ] + pc[:, TQ:PAIR])
        out_ref[t, p] = gate(t, p, 0) * _dot(vct, pc.astype(BF16))

    ksts = [pl.multiple_of(jnp.maximum(t0 - WIN_A, 0), TQ) for t0 in t0s]
    win_m = [[jnp.full((1, PAIR), NEG_INF, F32) for _ in range(n_pairs)] for _ in tiles]
    win_acc = [[jnp.zeros((_VROWS_A, PAIR), F32) for _ in range(n_pairs)] for _ in tiles]

    win_var = [jnp.minimum(qt, WIN_A // TQ) for qt in qts]

    def win_scores(t, p, off, tk):
        kt = kwin_ref[0, pl.ds(pl.multiple_of(ksts[t] + off, LANES), tk), :]
        bias = wtab_ref[win_var[t], off:off + tk, :]
        return _dot_nt(kt, qpair[t][p]) + jnp.concatenate([bias, bias], axis=1)

    def win_finish(t, p, off, tk, s):
        vt = jnp.concatenate(
            [vwint_ref[ksts[t] // LANES + off // LANES + b] for b in range(tk // LANES)], axis=1)
        win_m[t][p], win_acc[t][p] = _softmax_step(s, win_m[t][p], win_acc[t][p], vt)
        if off + tk == sum(_WIN_TILES):
            a = win_acc[t][p]
            out_ref[t, p] = out_ref[t, p] + (
                gate(t, p, 2) / jnp.maximum(a[HEAD_DIM_A:HEAD_DIM_A + 1], TINY)) * a[0:HEAD_DIM_A]

    jobs = [(functools.partial(cmp_scores, t, p), functools.partial(cmp_finish, t, p))
            for t in tiles for p in range(n_pairs)]
    off = 0
    for tk in _WIN_TILES:
        jobs += [(functools.partial(win_scores, t, p, off, tk), functools.partial(win_finish, t, p, off, tk))
                 for t in tiles for p in range(n_pairs)]
        off += tk

    nb_keys = WIN_B + TQ
    left = lax.broadcasted_iota(jnp.int32, (TQ, LANES), 1) < HEAD_DIM_B
    zero = jnp.zeros((TQ, LANES), qb_refs[0].dtype)
    kbs = [pl.multiple_of(jnp.maximum(t0 - WIN_B, 0), TQ) for t0 in t0s]
    b_var = [jnp.minimum(qt, WIN_B // TQ) for qt in qts]

    def b_sink(c):
        return jnp.concatenate(
            [jnp.full((1, TQ), sink_ref[kv * GROUP_B + 2 * c + h] * LOG2E, F32) for h in range(2)], axis=1)

    def b_scores(t, c):
        pair = qb_refs[t][0, :, c * LANES:(c + 1) * LANES]
        qp = jnp.concatenate([jnp.where(left, pair, zero), jnp.where(left, zero, pair)], axis=0)
        bias = btab_ref[b_var[t]]
        return _dot_nt(kb_ref[0, pl.ds(kbs[t], nb_keys), :], qp) + jnp.concatenate([bias, bias], axis=1)

    def b_finish(t, c, s):
        vt = jnp.concatenate([vbt_ref[kbs[t] // LANES + b] for b in range(nb_keys // LANES)], axis=1)
        sink = b_sink(c)
        m = jnp.maximum(jnp.max(s, axis=0, keepdims=True), sink)
        p = jnp.exp2(s - m).astype(BF16)
        a = _dot(vt, p)
        den = jnp.maximum(a[HEAD_DIM_B:HEAD_DIM_B + 1] + jnp.exp2(sink - m), TINY)
        o = a[0:HEAD_DIM_B] / den
        o2 = jnp.concatenate([o[:, 0:TQ], o[:, TQ:PAIR]], axis=0)
        z = zb_refs[t][0, :, c * LANES:(c + 1) * LANES]
        yb_refs[t][0, :, c * LANES:(c + 1) * LANES] = (o2.T * z.astype(F32)).astype(yb_refs[t].dtype)

    b_jobs = [[(functools.partial(b_scores, t, c), functools.partial(b_finish, t, c))
               for c in range(GROUP_B // 2)] for t in tiles]

    def select_blocks(t):
        ov = ov_ref[...]
        jidx = lax.broadcasted_iota(jnp.int32, (N_SELP, TQ), 0)
        psum = psums[t][0] + psums[t][1]
        p_hi = psum.astype(BF16)
        r1 = psum - p_hi.astype(F32)
        p_mid = r1.astype(BF16)
        p_lo = (r1 - p_mid.astype(F32)).astype(BF16)
        psel = _dot(ov, p_hi) + _dot(ov, p_mid) + _dot(ov, p_lo)
        tq = t0s[t] + lane
        cur = tq // SEL_BLOCK
        forced = (jidx == 0) | (jidx == cur) | (jidx == cur - 1)
        score = jnp.where(forced, FORCE_SCORE, jnp.where(jidx * SEL_BLOCK <= tq, psel, -1.0))
        rank = jnp.zeros(psel.shape, F32)
        for r in range(N_SELP):
            row = jnp.broadcast_to(score[r:r + 1, :], score.shape)
            rank = rank + jnp.where(jidx > r, jnp.where(row >= score, 1.0, 0.0), jnp.where(row > score, 1.0, 0.0))
        bias_t = jnp.where(rank < SEL_TOPK, 0.0, NEG_INF)
        for j in range(N_SELP):
            bias_ref[t, j] = jnp.broadcast_to(bias_t[j:j + 1, :], (SUBLANES, TQ))

    blk_per_tile = SEL_TK // SEL_BLOCK
    rep = SEL_BLOCK // SUBLANES
    acc_ref[...] = jnp.zeros(acc_ref.shape, F32)
    ms = [[jnp.full((1, PAIR), NEG_INF, F32) for _ in range(n_pairs)] for _ in tiles]
    ctx = {}

    def sel_scores(g, k, p):
        if p == 0:
            if g == 0 and k == 0:
                for t in tiles:
                    select_blocks(t)
            n_lo = qts[2 * g] // (SEL_TK // TQ) + 1
            hi = k >= n_lo
            t = 2 * g + hi.astype(jnp.int32)
            j = jnp.where(hi, k - n_lo, k)
            ks = pl.multiple_of(j * SEL_TK, SEL_TK)
            bt = jnp.concatenate(
                [bias_ref[t, j * blk_per_tile + b] for b in range(blk_per_tile) for _ in range(rep)], axis=0)
            qt = jnp.where(hi, qts[2 * g + 1], qts[2 * g])
            diag = j == qt // (SEL_TK // TQ)
            bt = bt + ctab_ref[jnp.where(diag, 1 + qt % (SEL_TK // TQ), 0)]
            bt = jnp.concatenate([bt, bt], axis=1)
            ctx[g, k] = (hi, t, j, ks, bt)
        hi, t, j, ks, bt = ctx[g, k]
        return _dot_nt(ksel_ref[0, pl.ds(ks, SEL_TK), :], qs_ref[t, p]) + bt

    def sel_finish(g, k, p, s):
        hi, t, j, ks, bt = ctx[g, k]
        lo_t, hi_t = 2 * g, 2 * g + 1
        vt = jnp.concatenate([vselt_ref[j * (SEL_TK // LANES) + b] for b in range(SEL_TK // LANES)], axis=1)
        m_new, acc_new = _softmax_step(s, jnp.where(hi, ms[hi_t][p], ms[lo_t][p]), acc_ref[t, p], vt)
        acc_ref[t, p] = acc_new
        ms[lo_t][p] = jnp.where(hi, ms[lo_t][p], m_new)
        ms[hi_t][p] = jnp.where(hi, m_new, ms[hi_t][p])

    sel_jobs = [(functools.partial(sel_scores, g, k, p), functools.partial(sel_finish, g, k, p))
                for g in range(N_TILE_PAIRS) for k in range(_N_SEL_JOBS) for p in range(n_pairs)]
    b_before = [job for t in tiles if t % 2 == 0 for job in b_jobs[t]]
    b_after = [job for t in tiles if t % 2 == 1 for job in b_jobs[t]]
    _run_skewed(jobs + b_before + sel_jobs + b_after, ATTN_AHEAD)

    for t in tiles:
        z = z_refs[t][0]
        for p in range(n_pairs):
            a = acc_ref[t, p]
            o = out_ref[t, p] + (gate(t, p, 1) / jnp.maximum(a[HEAD_DIM_A:HEAD_DIM_A + 1], TINY)) * a[0:HEAD_DIM_A]
            for h in (2 * p, 2 * p + 1):
                oh = o[:, (h % 2) * TQ:(h % 2 + 1) * TQ].T
                y_refs[t][0, :, h * LANES:(h + 1) * LANES] = (
                    oh * z[:, h * LANES:(h + 1) * LANES].astype(F32)).astype(y_refs[t].dtype)


def _mask_tables():
    lane = np.arange(TQ)[None, :]
    row = np.arange(WIN_A + TQ)[:, None]
    wtab = []
    for v in range(WIN_A // TQ + 1):
        t0 = v * TQ
        dpos = (t0 + lane) - (max(t0 - WIN_A, 0) + row)
        wtab.append(np.where((dpos >= 0) & (dpos < WIN_A), 0.0, NEG_INF))
    row = np.arange(SEL_TK)[:, None]
    ctab = [np.zeros((SEL_TK, TQ))]
    for odd in range(SEL_TK // TQ):
        ctab.append(np.where(row <= odd * TQ + lane, 0.0, NEG_INF))
    row = np.arange(WIN_B + TQ)[:, None]
    btab = []
    for v in range(WIN_B // TQ + 1):
        t0 = v * TQ
        dpos = (t0 + lane) - (max(t0 - WIN_B, 0) + row)
        btab.append(np.where((dpos >= 0) & (dpos < WIN_B), 0.0, NEG_INF))
    return [jnp.asarray(np.stack(t), F32) for t in (wtab, ctab, btab)]


def _attention(slabs, kc, vc, ov, sinks):
    B, S, _ = slabs[0].shape
    sl = lambda name: slabs[_slab(name)]
    n_cp = S // CMP_STRIDE
    n_qt = S // TQ
    n_steps = n_qt // N_TILES
    assert n_cp == LANES and S // SEL_BLOCK == N_SELP and (n_qt // 2) // (SEL_TK // TQ) * 2 + 1 == _N_SEL_JOBS
    assert N_KV_A == N_KV_B and GROUP_A * HEAD_DIM_A == GROUP_B * HEAD_DIM_B and N_TILE_PAIRS in (1, 2)
    wq = GROUP_A * HEAD_DIM_A
    cb = lambda name: _col(name) // LANES
    kv_spec = lambda name: pl.BlockSpec((1, S, LANES), lambda b, k, i, s, c=cb(name): (b, 0, c + k))
    whole = lambda a: pl.BlockSpec(a.shape, lambda b, k, i, s: (0,) * a.ndim)

    def tile_specs(width, c, per_kv=True):
        return [pl.BlockSpec((1, TQ, width),
                             lambda b, k, i, s, t=t: (b, _query_tile(t, i, n_qt), c + (k if per_kv else 0)))
                for t in range(N_TILES)]

    part = [_query_tile(t, 0, n_qt) // n_steps for t in range(N_TILES)]
    assert sorted(part) == list(range(N_TILES))
    y_specs = [pl.BlockSpec((1, TQ, wq), lambda b, k, i, s, t=t: (b, _query_tile(t, i, n_qt) - part[t] * n_steps, k))
               for t in range(N_TILES)]
    y_sds = jax.ShapeDtypeStruct((B, S // N_TILES, WIDTH_A), BF16)
    wtab, ctab, btab = _mask_tables()
    grid_spec = pltpu.PrefetchScalarGridSpec(
        num_scalar_prefetch=1,
        grid=(B, N_KV_A, n_steps),
        in_specs=[
            *tile_specs(wq, _col("qa") // wq),
            kv_spec("ksa"), kv_spec("vsa"), kv_spec("kwa"), kv_spec("vwa"),
            pl.BlockSpec((1, n_cp, LANES), lambda b, k, i, s: (k * B + b, 0, 0)),
            pl.BlockSpec((1, n_cp, LANES), lambda b, k, i, s: (k * B + b, 0, 0)),
            *tile_specs(LANES, cb("ga"), per_kv=False),
            *tile_specs(wq, _col("za") // wq),
            whole(ov), whole(wtab), whole(ctab),
            *tile_specs(wq, _col("qb") // wq),
            kv_spec("kb2"), pl.BlockSpec((1, S, LANES), lambda b, k, i, s: (b, 0, cb("vb"))),
            *tile_specs(wq, _col("zb") // wq),
            whole(btab),
        ],
        out_specs=y_specs + y_specs,
        scratch_shapes=[
            pltpu.VMEM((S // LANES, _VROWS_A, LANES), BF16),
            pltpu.VMEM((S // LANES, _VROWS_A, LANES), BF16),
            pltpu.VMEM((S // LANES, _VROWS_B, LANES), BF16),
            pltpu.VMEM((N_TILES, GROUP_A // 2, PAIR, HEAD_DIM_A), BF16),
            pltpu.VMEM((N_TILES, N_SELP, SUBLANES, TQ), F32),
            pltpu.VMEM((N_TILES, GROUP_A // 2, _VROWS_A, PAIR), F32),
            pltpu.VMEM((N_TILES, GROUP_A // 2, HEAD_DIM_A, PAIR), F32),
        ],
    )
    per_tile = lambda name: [sl(name)] * N_TILES
    outs = pl.pallas_call(
        _attn_kernel,
        grid_spec=grid_spec,
        out_shape=[y_sds] * (2 * N_TILES),
        compiler_params=pltpu.CompilerParams(
            dimension_semantics=("arbitrary", "arbitrary", "arbitrary"), vmem_limit_bytes=48 * 1024 * 1024),
        name="attn",
    )(sinks, *per_tile("qa"), sl("ksa"), sl("vsa"), sl("kwa"), sl("vwa"), kc, vc,
      *per_tile("ga"), *per_tile("za"), ov, wtab, ctab,
      *per_tile("qb"), sl("kb2"), sl("vb"), *per_tile("zb"), btab)
    in_order = lambda ys: [ys[part.index(n)] for n in range(N_TILES)]
    return in_order(outs[:N_TILES]), in_order(outs[N_TILES:])


def _out_proj_kernel(n_parts, *refs):
    ya_refs, yb_refs = refs[:n_parts], refs[n_parts:2 * n_parts]
    wa_ref, wb_ref, x_ref, g_ref, o_ref = refs[2 * n_parts:]
    part = pl.program_id(0) % n_parts

    for n in range(n_parts):
        @pl.when(part == n)
        def _(n=n):
            r = x_ref[...] + _dot(ya_refs[n][...], wa_ref[...]) + _dot(yb_refs[n][...], wb_ref[...])
            ms = jnp.mean(r * r, axis=-1, keepdims=True)
            o_ref[...] = (r * lax.rsqrt(ms + RMS_EPS)) * g_ref[...]


def _out_proj(ya_parts, yb_parts, w_out, x2, g, S):
    M = x2.shape[0]
    n_parts = len(ya_parts)
    tm = S // n_parts
    n_seq = M // S
    y_spec = lambda w, n: pl.BlockSpec(
        (tm, w), lambda i: (jnp.minimum((i + n_parts - 1 - n) // n_parts, n_seq - 1), 0))
    return pl.pallas_call(
        functools.partial(_out_proj_kernel, n_parts),
        grid=(M // tm,),
        in_specs=[
            *[y_spec(WIDTH_A, n) for n in range(n_parts)], *[y_spec(WIDTH_B, n) for n in range(n_parts)],
            pl.BlockSpec((WIDTH_A, D_MODEL), lambda i: (0, 0)),
            pl.BlockSpec((WIDTH_B, D_MODEL), lambda i: (WIDTH_A // WIDTH_B, 0)),
            pl.BlockSpec((tm, D_MODEL), lambda i: (i, 0)),
            pl.BlockSpec((1, D_MODEL), lambda i: (0, 0)),
        ],
        out_specs=pl.BlockSpec((tm, D_MODEL), lambda i: (i, 0)),
        out_shape=jax.ShapeDtypeStruct((M, D_MODEL), F32),
        compiler_params=pltpu.CompilerParams(
            dimension_semantics=("arbitrary",), vmem_limit_bytes=56 * 1024 * 1024),
        name="out_proj",
    )(*ya_parts, *yb_parts, w_out, w_out, x2, g)


def _overlap_matrix(n_cp):
    c_start = np.arange(n_cp) * CMP_STRIDE
    j_start = np.arange(N_SELP) * SEL_BLOCK
    ov = (c_start[None, :] < j_start[:, None] + SEL_BLOCK) & (c_start[None, :] + CMP_BLOCK > j_start[:, None])
    return jnp.asarray(ov, BF16)


def kernel(x, w_in, cmp_k_w1, cmp_k_w2, cmp_v_w1, cmp_v_w2, cmp_k_pos, cmp_v_pos, sinks, w_out, norm_g, final_g):
    B, S, D = x.shape
    assert D == D_MODEL and w_in.shape[0] == 1
    x2 = x.reshape(B * S, D)

    w_all = _prep_w_in(w_in)
    tabs = _rope_tables(S)
    slabs, strided = _in_proj(x2, norm_g[0].reshape(1, D), w_all, tabs, S)
    slabs = [p.reshape(B, S, IN_TN) for p in slabs]

    n_cp = S // CMP_STRIDE
    tk, tv = [strided[n].reshape(N_KV_A * B * n_cp, _HALF) for n in ("kca", "vca")]
    flat_pos = lambda p: jnp.broadcast_to(p.reshape(1, CMP_BLOCK * HEAD_DIM_A), (8, CMP_BLOCK * HEAD_DIM_A))
    kc, vc = _compress(tk, tv, cmp_k_w1[0].astype(BF16), cmp_v_w1[0].astype(BF16),
                       cmp_k_w2[0].astype(BF16), cmp_v_w2[0].astype(BF16),
                       flat_pos(cmp_k_pos[0]).astype(BF16), flat_pos(cmp_v_pos[0]).astype(BF16))
    kc, vc = [c.reshape(N_KV_A * B, n_cp, HEAD_DIM_A) for c in (kc, vc)]

    ya_parts, yb_parts = _attention(slabs, kc, vc, _overlap_matrix(n_cp), sinks[0])
    flat = lambda ys: [y.reshape(B * y.shape[1], y.shape[2]) for y in ys]
    out = _out_proj(flat(ya_parts), flat(yb_parts), w_out[0].astype(BF16), x2, final_g.reshape(1, D), S)
    return out.reshape(B, S, D)
```

```python
import functools
import math

import numpy as np
import jax
import jax.numpy as jnp
from jax import lax
from jax.experimental import pallas as pl
from jax.experimental.pallas import tpu as pltpu

F32 = jnp.float32
BF16 = jnp.bfloat16

D_MODEL = 2048
ROPE_THETA = 10000.0
RMS_EPS = 1e-6
NEG_INF = -1e30
TINY = 1e-30
LOG2E = math.log2(math.e)

WIDTH_A = 1024
HEAD_DIM_A = 128
N_KV_A = 2
GROUP_A = 4
KV_A = N_KV_A * HEAD_DIM_A
CMP_BLOCK = 32
CMP_STRIDE = 16
CMP_HIDDEN = 256
SEL_BLOCK = 64
SEL_TOPK = 16
WIN_A = 512
FORCE_SCORE = 1e4

WIDTH_B = 1024
HEAD_DIM_B = 64
N_HEADS_B = 16
N_KV_B = 2
GROUP_B = 8
KV_B = N_KV_B * HEAD_DIM_B
WIN_B = 128

IN_SIZES = (WIDTH_A, KV_A, KV_A, KV_A, KV_A, KV_A, KV_A, WIDTH_A, 3 * 8,
            WIDTH_B, KV_B, KV_B, WIDTH_B)

LANES = 128
SUBLANES = 8
CHUNK = 256

EP_NONE, EP_ROPE128, EP_ROPE64, EP_SILU, EP_SIGMOID, EP_ROPE128_Q, EP_ROPE64_Q = range(7)

_PROJ_LAYOUT = (
    ("qa", 8, EP_ROPE128_Q), ("kca", 2, EP_ROPE128), ("ksa", 2, EP_ROPE128), ("kwa", 2, EP_ROPE128),
    ("vca", 2, EP_NONE),
    ("vsa", 2, EP_NONE), ("vwa", 2, EP_NONE), ("za", 8, EP_SILU), ("kb2", 2, EP_ROPE64),
    ("ga", 1, EP_SIGMOID), ("vb", 1, EP_NONE),
    ("zb", 8, EP_SILU), ("qb", 8, EP_ROPE64_Q),
)
_UNIT_KINDS = tuple(k for _, n, k in _PROJ_LAYOUT for _ in range(n))
_UNIT_START = {}
_c = 0
for _name, _n, _k in _PROJ_LAYOUT:
    _UNIT_START[_name] = _c
    _c += _n
D_PROJ = _c * LANES
IN_TN = 2048
_IN_CPT = IN_TN // CHUNK
_IN_UPT = IN_TN // LANES
N_SLABS = D_PROJ // IN_TN


def _slab(name):
    return _UNIT_START[name] // _IN_UPT


def _col(name):
    return (_UNIT_START[name] % _IN_UPT) * LANES


def _dot(a, b):
    return jnp.dot(a, b, preferred_element_type=F32)


def _dot_nt(a, b):
    return lax.dot_general(a, b, (((1,), (1,)), ((), ())), preferred_element_type=F32)


IN_TM_NORM = 512
IN_TM = 1024
IN_SUB = 512

_TAB_GROUP = {EP_ROPE128: "r128", EP_ROPE128_Q: "r128q", EP_ROPE64: "r64", EP_ROPE64_Q: "r64q"}
_TAB_SIZE = {"r128": 2, "r128q": 2, "r64": 3, "r64q": 3}


def _slab_tab_groups(kinds):
    return tuple(dict.fromkeys(_TAB_GROUP[k] for k in kinds if k in _TAB_GROUP))


def _in_proj_kernel(kinds, with_norm, seq_len, strided_chunks, *refs):
    groups = _slab_tab_groups(kinds)
    refs = list(refs)
    if with_norm:
        x_ref, g_ref = refs.pop(0), refs.pop(0)
    else:
        h_ref = refs.pop(0)
    w_ref = refs.pop(0)
    tab_ref = refs.pop(0) if groups else None
    o_ref = refs.pop(0)
    if with_norm:
        h_ref = refs.pop(0)
        x = x_ref[...]
        ms = jnp.mean(x * x, axis=-1, keepdims=True)
        h_ref[...] = ((x * lax.rsqrt(ms + RMS_EPS)) * g_ref[...]).astype(BF16)
    t_refs = {c: refs.pop(0) for c in strided_chunks}
    if strided_chunks:
        stage_ref = refs.pop(0)
    tm = h_ref.shape[0]
    pos0 = (pl.program_id(0) % (seq_len // tm)) * tm
    base, n = {}, 0
    for grp in groups:
        base[grp] = n
        n += _TAB_SIZE[grp]

    def epilogue(kind, r, a):
        if kind == EP_NONE:
            return a
        if kind == EP_SILU:
            return a * jax.nn.sigmoid(a)
        if kind == EP_SIGMOID:
            return jax.nn.sigmoid(a)
        t = base[_TAB_GROUP[kind]]
        tab = lambda k: tab_ref[t + k, pl.ds(pl.multiple_of(pos0 + r * IN_SUB, IN_SUB), IN_SUB), :]
        if kind in (EP_ROPE128, EP_ROPE128_Q):
            return a * tab(0) + pltpu.roll(a, 64, 1) * tab(1)
        return a * tab(0) + pltpu.roll(a, 96, 1) * tab(1) + pltpu.roll(a, 32, 1) * tab(2)

    n_grp = IN_SUB // CMP_STRIDE

    subs = CHUNK // LANES

    def finish(r, c, acc):
        rows = slice(r * IN_SUB, (r + 1) * IN_SUB)
        for sub in range(subs):
            cols = slice(c * CHUNK + sub * LANES, c * CHUNK + (sub + 1) * LANES)
            val = epilogue(kinds[c * subs + sub], r, acc[:, sub * LANES:(sub + 1) * LANES])
            o_ref[rows, cols] = val.astype(o_ref.dtype)
            if c in t_refs:
                slot = strided_chunks.index(c) * subs + sub
                stage_ref[slot] = val
                for k in range(CMP_STRIDE):
                    t_refs[c][sub, r * n_grp:(r + 1) * n_grp, k * LANES:(k + 1) * LANES] = (
                        stage_ref[slot, pl.ds(k, n_grp, stride=CMP_STRIDE), :].astype(t_refs[c].dtype))

    jobs = []
    for r in range(tm // IN_SUB):
        for c in range(len(kinds) // subs):
            jobs.append((lambda r=r, c=c: _dot_nt(h_ref[r * IN_SUB:(r + 1) * IN_SUB, :],
                                                  w_ref[c * CHUNK:(c + 1) * CHUNK, :]),
                         functools.partial(finish, r, c)))
    _run_skewed(jobs)


def _rope_tables(S):
    def cs(d):
        inv = ROPE_THETA ** (-jnp.arange(0, d, 2, dtype=F32) / d)
        ang = jnp.arange(S, dtype=F32)[:, None] * inv[None, :]
        return jnp.cos(ang), jnp.sin(ang)

    c128, s128 = cs(HEAD_DIM_A)
    c64, s64 = cs(HEAD_DIM_B)
    z32 = jnp.zeros_like(s64)
    t128 = [jnp.concatenate([c128, c128], axis=1), jnp.concatenate([-s128, s128], axis=1)]
    t64 = [jnp.concatenate([c64, c64, c64, c64], axis=1),
           jnp.concatenate([-s64, z32, -s64, z32], axis=1),
           jnp.concatenate([z32, s64, z32, s64], axis=1)]
    qa = HEAD_DIM_A ** -0.5 * LOG2E
    qb = HEAD_DIM_B ** -0.5 * LOG2E
    return {"r128": t128, "r64": t64, "r128q": [t * qa for t in t128], "r64q": [t * qb for t in t64]}


W_PREP_TR = 256


def _w_in_pieces():
    offs = np.cumsum((0,) + IN_SIZES)
    names = ("qa", "kca", "vca", "ksa", "vsa", "kwa", "vwa", "za", "ga", "qb", "kb", "vb", "zb")
    start = {n: int(offs[i]) for i, n in enumerate(names)}
    width = dict(zip(names, IN_SIZES))
    n_g = 3 * GROUP_A
    halves = lambda n: [(start[n] + i * HEAD_DIM_B, HEAD_DIM_B) for i in (0, 0, 1, 1)]
    pieces = {n: [(start[n], width[n])]
              for n in ("qa", "kca", "ksa", "kwa", "vca", "vsa", "vwa", "za", "zb", "qb", "vb")}
    pad = LANES // N_KV_A - n_g
    pieces["ga"] = [(start["ga"], n_g), (None, pad), (start["ga"] + n_g, n_g), (None, pad)]
    pieces["kb2"] = halves("kb")
    return [p for name, _, _ in _PROJ_LAYOUT for p in pieces[name]]


def _w_prep_kernel(w_ref, o_ref):
    row = 0
    parts, filled = [], 0
    for src, wd in _w_in_pieces():
        done = 0
        while done < wd:
            take = min(wd - done, LANES - filled)
            if src is None:
                parts.append(jnp.zeros((take, W_PREP_TR), F32))
            else:
                parts.append(w_ref[src + done:src + done + take, :])
            done += take
            filled += take
            if filled == LANES:
                blk = parts[0] if len(parts) == 1 else jnp.concatenate(parts, axis=0)
                o_ref[row:row + LANES, :] = blk.astype(o_ref.dtype)
                row += LANES
                parts, filled = [], 0
    assert row == D_PROJ and not parts


def _prep_w_in(w_in):
    wt = w_in[0].T
    d_in = wt.shape[0]
    return pl.pallas_call(
        _w_prep_kernel,
        grid=(D_MODEL // W_PREP_TR,),
        in_specs=[pl.BlockSpec((d_in, W_PREP_TR), lambda i: (0, i))],
        out_specs=pl.BlockSpec((D_PROJ, W_PREP_TR), lambda i: (0, i)),
        out_shape=jax.ShapeDtypeStruct((D_PROJ, D_MODEL), BF16),
        compiler_params=pltpu.CompilerParams(dimension_semantics=("arbitrary",)),
        name="w_prep",
    )(wt)


def _in_proj(x2, g, w_all, tabs, S):
    M = x2.shape[0]
    params = pltpu.CompilerParams(dimension_semantics=("arbitrary",), vmem_limit_bytes=48 * 1024 * 1024)
    out_sds = jax.ShapeDtypeStruct((M, IN_TN), BF16)
    slabs, strided, h = [], {}, None
    for j in range(N_SLABS):
        kinds = _UNIT_KINDS[j * _IN_UPT:(j + 1) * _IN_UPT]
        with_norm = j == 0
        tm = IN_TM_NORM if with_norm else IN_TM
        groups = _slab_tab_groups(kinds)
        act_spec = pl.BlockSpec((tm, D_MODEL), lambda i: (i, 0))
        in_specs = [act_spec, pl.BlockSpec((1, D_MODEL), lambda i: (0, 0))] if with_norm else [act_spec]
        args = [x2, g] if with_norm else [h]
        in_specs.append(pl.BlockSpec((IN_TN, D_MODEL), lambda i, j=j: (j, 0)))
        args.append(w_all)
        if groups:
            tab = jnp.stack([t for grp in groups for t in tabs[grp]])
            in_specs.append(pl.BlockSpec(tab.shape, lambda i: (0, 0, 0)))
            args.append(tab)
        out_specs = [pl.BlockSpec((tm, IN_TN), lambda i: (i, 0))]
        out_shape = [out_sds]
        if with_norm:
            out_specs.append(act_spec)
            out_shape.append(jax.ShapeDtypeStruct((M, D_MODEL), BF16))
        strided_names = [name for name in ("kca", "vca") if _slab(name) == j]
        strided_chunks = tuple(_col(name) // CHUNK for name in strided_names)
        scratch = []
        for _ in strided_chunks:
            out_specs.append(pl.BlockSpec((N_KV_A, tm // CMP_STRIDE, _HALF), lambda i: (0, i, 0)))
            out_shape.append(jax.ShapeDtypeStruct((N_KV_A, M // CMP_STRIDE, _HALF), BF16))
        if strided_chunks:
            scratch.append(pltpu.VMEM((len(strided_chunks) * N_KV_A, IN_SUB, HEAD_DIM_A), F32))
        res = list(pl.pallas_call(
            functools.partial(_in_proj_kernel, kinds, with_norm, S, strided_chunks),
            grid=(M // tm,),
            in_specs=in_specs,
            out_specs=out_specs,
            out_shape=out_shape,
            scratch_shapes=scratch,
            compiler_params=params,
            name=f"in_proj_{j}",
        )(*args))
        slabs.append(res.pop(0))
        if with_norm:
            h = res.pop(0)
        for name in strided_names:
            strided[name] = res.pop(0)
    return slabs, strided


CMP_TM = 256
_HALF = CMP_STRIDE * HEAD_DIM_A


def _compress_kernel(tk_ref, tv_ref, w1k_ref, w1v_ref, w2k_ref, w2v_ref, posk_ref, posv_ref, kc_ref, vc_ref,
                     pb_ref):
    @pl.when(pl.program_id(0) == 0)
    def _():
        pb_ref[0] = _dot(posk_ref[...], w1k_ref[...])
        pb_ref[1] = _dot(posv_ref[...], w1v_ref[...])

    for n, (t_ref, w1_ref, w2_ref, o_ref) in enumerate(((tk_ref, w1k_ref, w2k_ref, kc_ref),
                                                        (tv_ref, w1v_ref, w2v_ref, vc_ref))):
        t = t_ref[...]
        u = _dot(t, w1_ref[0:_HALF, :])
        v = _dot(t, w1_ref[_HALF:2 * _HALF, :])
        pre = u + pltpu.roll(v, CMP_TM - 1, 0) + pb_ref[n, 0:1]
        hid = pre * jax.nn.sigmoid(pre)
        out = _dot(hid.astype(BF16), w2_ref[...])
        row = lax.broadcasted_iota(jnp.int32, out.shape, 0)
        n_cp = LANES
        out = jnp.where((row & (n_cp - 1)) == n_cp - 1, 0.0, out)
        o_ref[...] = out.astype(o_ref.dtype)


def _compress(tk, tv, w1k, w1v, w2k, w2v, posk, posv):
    R = tk.shape[0]
    rows = pl.BlockSpec((CMP_TM, _HALF), lambda i: (i, 0))
    whole = lambda a: pl.BlockSpec(a.shape, lambda i: (0, 0))
    o_spec = pl.BlockSpec((CMP_TM, HEAD_DIM_A), lambda i: (i, 0))
    o_sds = jax.ShapeDtypeStruct((R, HEAD_DIM_A), BF16)
    return pl.pallas_call(
        _compress_kernel,
        grid=(R // CMP_TM,),
        in_specs=[rows, rows, whole(w1k), whole(w1v), whole(w2k), whole(w2v), whole(posk), whole(posv)],
        out_specs=(o_spec, o_spec),
        out_shape=(o_sds, o_sds),
        scratch_shapes=[pltpu.VMEM((2, SUBLANES, CMP_HIDDEN), F32)],
        compiler_params=pltpu.CompilerParams(dimension_semantics=("arbitrary",)),
        name="compress",
    )(tk, tv, w1k, w1v, w2k, w2v, posk, posv)


TQ = 128
ATTN_AHEAD = 12
ONES_ROWS = 16


def _transpose_values(v_ref, vt_ref, n_rows, second_half=None):
    n_blk = vt_ref.shape[0]
    for blk in range(n_blk):
        vt = v_ref[0, blk * LANES:(blk + 1) * LANES, :].astype(F32).T
        if second_half is not None:
            vt = jnp.where(second_half == 0, vt[0:n_rows], vt[n_rows:2 * n_rows])
        vt_ref[blk, 0:n_rows, :] = vt[0:n_rows].astype(vt_ref.dtype)
        vt_ref[blk, n_rows:n_rows + ONES_ROWS, :] = jnp.ones((ONES_ROWS, LANES), vt_ref.dtype)


def _run_skewed(jobs, ahead=1):
    pending = [job[0]() for job in jobs[:ahead]]
    for i, (_, finish) in enumerate(jobs):
        if i + ahead < len(jobs):
            pending.append(jobs[i + ahead][0]())
        finish(pending[i])
        pending[i] = None


def _softmax_step(s, m, acc, vt):
    m_new = jnp.maximum(m, jnp.max(s, axis=0, keepdims=True))
    p = jnp.exp2(s - m_new).astype(BF16)
    return m_new, jnp.exp2(m - m_new) * acc + _dot(vt, p)


SEL_TK = 256
N_SELP = 32
PAIR = 2 * TQ
N_TILE_PAIRS = 2
N_TILES = 2 * N_TILE_PAIRS
_VROWS_A = HEAD_DIM_A + ONES_ROWS
_VROWS_B = HEAD_DIM_B + ONES_ROWS
_WIN_TILES = (256, 256, 128)
_N_SEL_JOBS = 9


def _query_tile(t, i, n_qt):
    low = i if t // 2 == 0 else n_qt // 2 - 1 - i
    return low if t % 2 == 0 else n_qt - 1 - low


def _attn_kernel(*refs):
    refs = list(refs)
    take = lambda n: [refs.pop(0) for _ in range(n)]
    sink_ref, = take(1)
    q_refs = take(N_TILES)
    ksel_ref, vsel_ref, kwin_ref, vwin_ref, kc_ref, vc_ref = take(6)
    g_refs, z_refs = take(N_TILES), take(N_TILES)
    ov_ref, wtab_ref, ctab_ref = take(3)
    qb_refs = take(N_TILES)
    kb_ref, vb_ref = take(2)
    zb_refs = take(N_TILES)
    btab_ref, = take(1)
    y_refs, yb_refs = take(N_TILES), take(N_TILES)
    vselt_ref, vwint_ref, vbt_ref, qs_ref, bias_ref, acc_ref, out_ref = refs

    kv = pl.program_id(1)
    i = pl.program_id(2)
    n_qt = N_TILES * pl.num_programs(2)
    n_pairs = GROUP_A // 2
    tiles = range(N_TILES)
    qts = [_query_tile(t, i, n_qt) for t in tiles]
    t0s = [qt * TQ for qt in qts]

    @pl.when(i == 0)
    def _():
        _transpose_values(vsel_ref, vselt_ref, HEAD_DIM_A)
        _transpose_values(vwin_ref, vwint_ref, HEAD_DIM_A)
        _transpose_values(vb_ref, vbt_ref, HEAD_DIM_B, kv)

    lane = lax.broadcasted_iota(jnp.int32, (1, TQ), 1)
    lane2 = jnp.concatenate([lane, lane], axis=1)
    qpair, gts = [], []
    for t in tiles:
        q = q_refs[t][0]
        pairs = [jnp.concatenate([q[:, h * LANES:(h + 1) * LANES] for h in (2 * p, 2 * p + 1)], axis=0)
                 for p in range(n_pairs)]
        for p in range(n_pairs):
            qs_ref[t, p] = pairs[p]
        qpair.append(pairs)
        gt = g_refs[t][0].astype(F32).T
        gts.append(jnp.where(kv == 0, gt[0:LANES // 2], gt[LANES // 2:LANES]))

    def gate(t, p, branch):
        return jnp.concatenate(
            [gts[t][3 * h + branch:3 * h + branch + 1] for h in (2 * p, 2 * p + 1)], axis=1)

    kc = kc_ref[0]
    vct = vc_ref[0].astype(F32).T.astype(BF16)
    cend = lax.broadcasted_iota(jnp.int32, (LANES, PAIR), 0) * CMP_STRIDE + (CMP_BLOCK - 1)
    psums = [[] for _ in tiles]

    def cmp_scores(t, p):
        return jnp.where(cend <= t0s[t] + lane2, _dot_nt(kc, qpair[t][p]), NEG_INF)

    def cmp_finish(t, p, s):
        m = jnp.max(s, axis=0, keepdims=True)
        e = jnp.where(cend <= t0s[t] + lane2, jnp.exp2(s - m), 0.0)
        pc = e / jnp.maximum(jnp.sum(e, axis=0, keepdims=True), TINY)
        psums[t].append(pc[:, 0:TQ] + pc[:, TQ:PAIR])
        out_ref[t, p] = gate(t, p, 0) * _dot(vct, pc.astype(BF16))

    ksts = [pl.multiple_of(jnp.maximum(t0 - WIN_A, 0), TQ) for t0 in t0s]
    n_win = sum(_WIN_TILES)

    win_var = [jnp.minimum(qt, WIN_A // TQ) for qt in qts]

    def win_scores(t, p):
        out, off = [], 0
        for tk in _WIN_TILES:
            kt = kwin_ref[0, pl.ds(pl.multiple_of(ksts[t] + off, LANES), tk), :]
            bias = wtab_ref[win_var[t], off:off + tk, :]
            out.append(_dot_nt(kt, qpair[t][p]) + jnp.concatenate([bias, bias], axis=1))
            off += tk
        return out

    def win_finish(t, p, ss):
        m = functools.reduce(jnp.maximum, [jnp.max(s, axis=0, keepdims=True) for s in ss])
        pw = jnp.concatenate([jnp.exp2(s - m).astype(BF16) for s in ss], axis=0)
        vt = jnp.concatenate([vwint_ref[ksts[t] // LANES + b] for b in range(n_win // LANES)], axis=1)
        a = _dot(vt, pw)
        out_ref[t, p] = out_ref[t, p] + (
            gate(t, p, 2) / jnp.maximum(a[HEAD_DIM_A:HEAD_DIM_A + 1], TINY)) * a[0:HEAD_DIM_A]

    jobs = [(functools.partial(cmp_scores, t, p), functools.partial(cmp_finish, t, p))
            for t in tiles for p in range(n_pairs)]
    jobs += [(functools.partial(win_scores, t, p), functools.partial(win_finish, t, p))
             for t in tiles for p in range(n_pairs)]

    nb_keys = WIN_B + TQ
    left = lax.broadcasted_iota(jnp.int32, (TQ, LANES), 1) < HEAD_DIM_B
    zero = jnp.zeros((TQ, LANES), qb_refs[0].dtype)
    kbs = [pl.multiple_of(jnp.maximum(t0 - WIN_B, 0), TQ) for t0 in t0s]
    b_var = [jnp.minimum(qt, WIN_B // TQ) for qt in qts]

    def b_sink(c):
        return jnp.concatenate(
            [jnp.full((1, TQ), sink_ref[kv * GROUP_B + 2 * c + h] * LOG2E, F32) for h in range(2)], axis=1)

    def b_scores(t, c):
        pair = qb_refs[t][0, :, c * LANES:(c + 1) * LANES]
        qp = jnp.concatenate([jnp.where(left, pair, zero), jnp.where(left, zero, pair)], axis=0)
        bias = btab_ref[b_var[t]]
        return _dot_nt(kb_ref[0, pl.ds(kbs[t], nb_keys), :], qp) + jnp.concatenate([bias, bias], axis=1)

    def b_finish(t, c, s):
        vt = jnp.concatenate([vbt_ref[kbs[t] // LANES + b] for b in range(nb_keys // LANES)], axis=1)
        sink = b_sink(c)
        m = jnp.maximum(jnp.max(s, axis=0, keepdims=True), sink)
        p = jnp.exp2(s - m).astype(BF16)
        a = _dot(vt, p)
        den = jnp.maximum(a[HEAD_DIM_B:HEAD_DIM_B + 1] + jnp.exp2(sink - m), TINY)
        o = a[0:HEAD_DIM_B] / den
        o2 = jnp.concatenate([o[:, 0:TQ], o[:, TQ:PAIR]], axis=0)
        z = zb_refs[t][0, :, c * LANES:(c + 1) * LANES]
        yb_refs[t][0, :, c * LANES:(c + 1) * LANES] = (o2.T * z.astype(F32)).astype(yb_refs[t].dtype)

    b_jobs = [[(functools.partial(b_scores, t, c), functools.partial(b_finish, t, c))
               for c in range(GROUP_B // 2)] for t in tiles]

    def select_blocks(t):
        ov = ov_ref[...]
        jidx = lax.broadcasted_iota(jnp.int32, (N_SELP, TQ), 0)
        psum = psums[t][0] + psums[t][1]
        p_hi = psum.astype(BF16)
        r1 = psum - p_hi.astype(F32)
        p_mid = r1.astype(BF16)
        p_lo = (r1 - p_mid.astype(F32)).astype(BF16)
        psel = _dot(ov, p_hi) + _dot(ov, p_mid) + _dot(ov, p_lo)
        tq = t0s[t] + lane
        cur = tq // SEL_BLOCK
        forced = (jidx == 0) | (jidx == cur) | (jidx == cur - 1)
        score = jnp.where(forced, FORCE_SCORE, jnp.where(jidx * SEL_BLOCK <= tq, psel, -1.0))
        rank = jnp.zeros(psel.shape, F32)
        for r in range(N_SELP):
            row = jnp.broadcast_to(score[r:r + 1, :], score.shape)
            rank = rank + jnp.where(jidx > r, jnp.where(row >= score, 1.0, 0.0), jnp.where(row > score, 1.0, 0.0))
        bias_t = jnp.where(rank < SEL_TOPK, 0.0, NEG_INF)
        for j in range(N_SELP):
            bias_ref[t, j] = jnp.broadcast_to(bias_t[j:j + 1, :], (SUBLANES, TQ))

    blk_per_tile = SEL_TK // SEL_BLOCK
    rep = SEL_BLOCK // SUBLANES
    acc_ref[...] = jnp.zeros(acc_ref.shape, F32)
    ms = [[jnp.full((1, PAIR), NEG_INF, F32) for _ in range(n_pairs)] for _ in tiles]
    ctx = {}

    def sel_scores(g, k, p):
        if p == 0:
            if g == 0 and k == 0:
                for t in tiles:
                    select_blocks(t)
            n_lo = qts[2 * g] // (SEL_TK // TQ) + 1
            hi = k >= n_lo
            t = 2 * g + hi.astype(jnp.int32)
            j = jnp.where(hi, k - n_lo, k)
            ks = pl.multiple_of(j * SEL_TK, SEL_TK)
            bt = jnp.concatenate(
                [bias_ref[t, j * blk_per_tile + b] for b in range(blk_per_tile) for _ in range(rep)], axis=0)
            qt = jnp.where(hi, qts[2 * g + 1], qts[2 * g])
            diag = j == qt // (SEL_TK // TQ)
            bt = bt + ctab_ref[jnp.where(diag, 1 + qt % (SEL_TK // TQ), 0)]
            bt = jnp.concatenate([bt, bt], axis=1)
            ctx[g, k] = (hi, t, j, ks, bt)
        hi, t, j, ks, bt = ctx[g, k]
        return _dot_nt(ksel_ref[0, pl.ds(ks, SEL_TK), :], qs_ref[t, p]) + bt

    def sel_finish(g, k, p, s):
        hi, t, j, ks, bt = ctx[g, k]
        lo_t, hi_t = 2 * g, 2 * g + 1
        vt = jnp.concatenate([vselt_ref[j * (SEL_TK // LANES) + b] for b in range(SEL_TK // LANES)], axis=1)
        m_new, acc_new = _softmax_step(s, jnp.where(hi, ms[hi_t][p], ms[lo_t][p]), acc_ref[t, p], vt)
        acc_ref[t, p] = acc_new
        ms[lo_t][p] = jnp.where(hi, ms[lo_t][p], m_new)
        ms[hi_t][p] = jnp.where(hi, m_new, ms[hi_t][p])

    sel_jobs = [(functools.partial(sel_scores, g, k, p), functools.partial(sel_finish, g, k, p))
                for g in range(N_TILE_PAIRS) for k in range(_N_SEL_JOBS) for p in range(n_pairs)]
    b_before = [job for t in tiles if t % 2 == 0 for job in b_jobs[t]]
    b_after = [job for t in tiles if t % 2 == 1 for job in b_jobs[t]]
    _run_skewed(jobs + b_before + sel_jobs + b_after, ATTN_AHEAD)

    for t in tiles:
        z = z_refs[t][0]
        for p in range(n_pairs):
            a = acc_ref[t, p]
            o = out_ref[t, p] + (gate(t, p, 1) / jnp.maximum(a[HEAD_DIM_A:HEAD_DIM_A + 1], TINY)) * a[0:HEAD_DIM_A]
            for h in (2 * p, 2 * p + 1):
                oh = o[:, (h % 2) * TQ:(h % 2 + 1) * TQ].T
                y_refs[t][0, :, h * LANES:(h + 1) * LANES] = (
                    oh * z[:, h * LANES:(h + 1) * LANES].astype(F32)).astype(y_refs[t].dtype)


def _mask_tables():
    lane = np.arange(TQ)[None, :]
    row = np.arange(WIN_A + TQ)[:, None]
    wtab = []
    for v in range(WIN_A // TQ + 1):
        t0 = v * TQ
        dpos = (t0 + lane) - (max(t0 - WIN_A, 0) + row)
        wtab.append(np.where((dpos >= 0) & (dpos < WIN_A), 0.0, NEG_INF))
    row = np.arange(SEL_TK)[:, None]
    ctab = [np.zeros((SEL_TK, TQ))]
    for odd in range(SEL_TK // TQ):
        ctab.append(np.where(row <= odd * TQ + lane, 0.0, NEG_INF))
    row = np.arange(WIN_B + TQ)[:, None]
    btab = []
    for v in range(WIN_B // TQ + 1):
        t0 = v * TQ
        dpos = (t0 + lane) - (max(t0 - WIN_B, 0) + row)
        btab.append(np.where((dpos >= 0) & (dpos < WIN_B), 0.0, NEG_INF))
    return [jnp.asarray(np.stack(t), F32) for t in (wtab, ctab, btab)]


def _attention(slabs, kc, vc, ov, sinks):
    B, S, _ = slabs[0].shape
    sl = lambda name: slabs[_slab(name)]
    n_cp = S // CMP_STRIDE
    n_qt = S // TQ
    n_steps = n_qt // N_TILES
    assert n_cp == LANES and S // SEL_BLOCK == N_SELP and (n_qt // 2) // (SEL_TK // TQ) * 2 + 1 == _N_SEL_JOBS
    assert N_KV_A == N_KV_B and GROUP_A * HEAD_DIM_A == GROUP_B * HEAD_DIM_B and N_TILE_PAIRS in (1, 2)
    wq = GROUP_A * HEAD_DIM_A
    cb = lambda name: _col(name) // LANES
    kv_spec = lambda name: pl.BlockSpec((1, S, LANES), lambda b, k, i, s, c=cb(name): (b, 0, c + k))
    whole = lambda a: pl.BlockSpec(a.shape, lambda b, k, i, s: (0,) * a.ndim)

    def tile_specs(width, c, per_kv=True):
        return [pl.BlockSpec((1, TQ, width),
                             lambda b, k, i, s, t=t: (b, _query_tile(t, i, n_qt), c + (k if per_kv else 0)))
                for t in range(N_TILES)]

    part = [_query_tile(t, 0, n_qt) // n_steps for t in range(N_TILES)]
    assert sorted(part) == list(range(N_TILES))
    y_specs = [pl.BlockSpec((1, TQ, wq), lambda b, k, i, s, t=t: (b, _query_tile(t, i, n_qt) - part[t] * n_steps, k))
               for t in range(N_TILES)]
    y_sds = jax.ShapeDtypeStruct((B, S // N_TILES, WIDTH_A), BF16)
    wtab, ctab, btab = _mask_tables()
    grid_spec = pltpu.PrefetchScalarGridSpec(
        num_scalar_prefetch=1,
        grid=(B, N_KV_A, n_steps),
        in_specs=[
            *tile_specs(wq, _col("qa") // wq),
            kv_spec("ksa"), kv_spec("vsa"), kv_spec("kwa"), kv_spec("vwa"),
            pl.BlockSpec((1, n_cp, LANES), lambda b, k, i, s: (k * B + b, 0, 0)),
            pl.BlockSpec((1, n_cp, LANES), lambda b, k, i, s: (k * B + b, 0, 0)),
            *tile_specs(LANES, cb("ga"), per_kv=False),
            *tile_specs(wq, _col("za") // wq),
            whole(ov), whole(wtab), whole(ctab),
            *tile_specs(wq, _col("qb") // wq),
            kv_spec("kb2"), pl.BlockSpec((1, S, LANES), lambda b, k, i, s: (b, 0, cb("vb"))),
            *tile_specs(wq, _col("zb") // wq),
            whole(btab),
        ],
        out_specs=y_specs + y_specs,
        scratch_shapes=[
            pltpu.VMEM((S // LANES, _VROWS_A, LANES), BF16),
            pltpu.VMEM((S // LANES, _VROWS_A, LANES), BF16),
            pltpu.VMEM((S // LANES, _VROWS_B, LANES), BF16),
            pltpu.VMEM((N_TILES, GROUP_A // 2, PAIR, HEAD_DIM_A), BF16),
            pltpu.VMEM((N_TILES, N_SELP, SUBLANES, TQ), F32),
            pltpu.VMEM((N_TILES, GROUP_A // 2, _VROWS_A, PAIR), F32),
            pltpu.VMEM((N_TILES, GROUP_A // 2, HEAD_DIM_A, PAIR), F32),
        ],
    )
    per_tile = lambda name: [sl(name)] * N_TILES
    outs = pl.pallas_call(
        _attn_kernel,
        grid_spec=grid_spec,
        out_shape=[y_sds] * (2 * N_TILES),
        compiler_params=pltpu.CompilerParams(
            dimension_semantics=("arbitrary", "arbitrary", "arbitrary"), vmem_limit_bytes=48 * 1024 * 1024),
        name="attn",
    )(sinks, *per_tile("qa"), sl("ksa"), sl("vsa"), sl("kwa"), sl("vwa"), kc, vc,
      *per_tile("ga"), *per_tile("za"), ov, wtab, ctab,
      *per_tile("qb"), sl("kb2"), sl("vb"), *per_tile("zb"), btab)
    in_order = lambda ys: [ys[part.index(n)] for n in range(N_TILES)]
    return in_order(outs[:N_TILES]), in_order(outs[N_TILES:])


def _out_proj_kernel(n_parts, *refs):
    ya_refs, yb_refs = refs[:n_parts], refs[n_parts:2 * n_parts]
    wa_ref, wb_ref, x_ref, g_ref, o_ref = refs[2 * n_parts:]
    part = pl.program_id(0) % n_parts

    for n in range(n_parts):
        @pl.when(part == n)
        def _(n=n):
            r = x_ref[...] + _dot(ya_refs[n][...], wa_ref[...]) + _dot(yb_refs[n][...], wb_ref[...])
            ms = jnp.mean(r * r, axis=-1, keepdims=True)
            o_ref[...] = (r * lax.rsqrt(ms + RMS_EPS)) * g_ref[...]


def _out_proj(ya_parts, yb_parts, w_out, x2, g, S):
    M = x2.shape[0]
    n_parts = len(ya_parts)
    tm = S // n_parts
    n_seq = M // S
    y_spec = lambda w, n: pl.BlockSpec(
        (tm, w), lambda i: (jnp.minimum((i + n_parts - 1 - n) // n_parts, n_seq - 1), 0))
    return pl.pallas_call(
        functools.partial(_out_proj_kernel, n_parts),
        grid=(M // tm,),
        in_specs=[
            *[y_spec(WIDTH_A, n) for n in range(n_parts)], *[y_spec(WIDTH_B, n) for n in range(n_parts)],
            pl.BlockSpec((WIDTH_A, D_MODEL), lambda i: (0, 0)),
            pl.BlockSpec((WIDTH_B, D_MODEL), lambda i: (WIDTH_A // WIDTH_B, 0)),
            pl.BlockSpec((tm, D_MODEL), lambda i: (i, 0)),
            pl.BlockSpec((1, D_MODEL), lambda i: (0, 0)),
        ],
        out_specs=pl.BlockSpec((tm, D_MODEL), lambda i: (i, 0)),
        out_shape=jax.ShapeDtypeStruct((M, D_MODEL), F32),
        compiler_params=pltpu.CompilerParams(
            dimension_semantics=("arbitrary",), vmem_limit_bytes=56 * 1024 * 1024),
        name="out_proj",
    )(*ya_parts, *yb_parts, w_out, w_out, x2, g)


def _overlap_matrix(n_cp):
    c_start = np.arange(n_cp) * CMP_STRIDE
    j_start = np.arange(N_SELP) * SEL_BLOCK
    ov = (c_start[None, :] < j_start[:, None] + SEL_BLOCK) & (c_start[None, :] + CMP_BLOCK > j_start[:, None])
    return jnp.asarray(ov, BF16)


def kernel(x, w_in, cmp_k_w1, cmp_k_w2, cmp_v_w1, cmp_v_w2, cmp_k_pos, cmp_v_pos, sinks, w_out, norm_g, final_g):
    B, S, D = x.shape
    assert D == D_MODEL and w_in.shape[0] == 1
    x2 = x.reshape(B * S, D)

    w_all = _prep_w_in(w_in)
    tabs = _rope_tables(S)
    slabs, strided = _in_proj(x2, norm_g[0].reshape(1, D), w_all, tabs, S)
    slabs = [p.reshape(B, S, IN_TN) for p in slabs]

    n_cp = S // CMP_STRIDE
    tk, tv = [strided[n].reshape(N_KV_A * B * n_cp, _HALF) for n in ("kca", "vca")]
    flat_pos = lambda p: jnp.broadcast_to(p.reshape(1, CMP_BLOCK * HEAD_DIM_A), (8, CMP_BLOCK * HEAD_DIM_A))
    kc, vc = _compress(tk, tv, cmp_k_w1[0].astype(BF16), cmp_v_w1[0].astype(BF16),
                       cmp_k_w2[0].astype(BF16), cmp_v_w2[0].astype(BF16),
                       flat_pos(cmp_k_pos[0]).astype(BF16), flat_pos(cmp_v_pos[0]).astype(BF16))
    kc, vc = [c.reshape(N_KV_A * B, n_cp, HEAD_DIM_A) for c in (kc, vc)]

    ya_parts, yb_parts = _attention(slabs, kc, vc, _overlap_matrix(n_cp), sinks[0])
    flat = lambda ys: [y.reshape(B * y.shape[1], y.shape[2]) for y in ys]
    out = _out_proj(flat(ya_parts), flat(yb_parts), w_out[0].astype(BF16), x2, final_g.reshape(1, D), S)
    return out.reshape(B, S, D)
```

```python
import functools
import math

import numpy as np
import jax
import jax.numpy as jnp
from jax import lax
from jax.experimental import pallas as pl
from jax.experimental.pallas import tpu as pltpu

F32 = jnp.float32
BF16 = jnp.bfloat16

D_MODEL = 2048
ROPE_THETA = 10000.0
RMS_EPS = 1e-6
NEG_INF = -1e30
TINY = 1e-30
LOG2E = math.log2(math.e)

WIDTH_A = 1024
HEAD_DIM_A = 128
N_KV_A = 2
GROUP_A = 4
KV_A = N_KV_A * HEAD_DIM_A
CMP_BLOCK = 32
CMP_STRIDE = 16
CMP_HIDDEN = 256
SEL_BLOCK = 64
SEL_TOPK = 16
WIN_A = 512
FORCE_SCORE = 1e4

WIDTH_B = 1024
HEAD_DIM_B = 64
N_KV_B = 2
GROUP_B = 8
KV_B = N_KV_B * HEAD_DIM_B
WIN_B = 128

IN_SIZES = (WIDTH_A, KV_A, KV_A, KV_A, KV_A, KV_A, KV_A, WIDTH_A, 3 * 8,
            WIDTH_B, KV_B, KV_B, WIDTH_B)

LANES = 128
SUBLANES = 8
VMEM_BYTES = 64 * 1024 * 1024
VMEM_LIMIT = VMEM_BYTES * 3 // 4
VMEM_LIMIT_OUT_PROJ = VMEM_BYTES * 7 // 8
CHUNK = 256

EP_NONE, EP_ROPE128, EP_ROPE64, EP_SILU, EP_SIGMOID, EP_ROPE128_Q, EP_ROPE64_Q = range(7)

_PROJ_LAYOUT = (
    ("qa", 8, EP_ROPE128_Q), ("kca", 2, EP_ROPE128), ("ksa", 2, EP_ROPE128), ("kwa", 2, EP_ROPE128),
    ("vca", 2, EP_NONE),
    ("vsa", 2, EP_NONE), ("vwa", 2, EP_NONE), ("za", 8, EP_SILU), ("kb2", 2, EP_ROPE64),
    ("ga", 1, EP_SIGMOID), ("vb", 1, EP_NONE),
    ("zb", 8, EP_SILU), ("qb", 8, EP_ROPE64_Q),
)
_UNIT_KINDS = tuple(k for _, n, k in _PROJ_LAYOUT for _ in range(n))
_UNIT_START = {}
_c = 0
for _name, _n, _k in _PROJ_LAYOUT:
    _UNIT_START[_name] = _c
    _c += _n
D_PROJ = _c * LANES
IN_TN = 2048
_IN_UPT = IN_TN // LANES
N_SLABS = D_PROJ // IN_TN


def _slab(name):
    return _UNIT_START[name] // _IN_UPT


def _col(name):
    return (_UNIT_START[name] % _IN_UPT) * LANES


def _dot(a, b):
    return jnp.dot(a, b, preferred_element_type=F32)


def _dot_nt(a, b):
    return lax.dot_general(a, b, (((1,), (1,)), ((), ())), preferred_element_type=F32)


IN_TM_NORM = 512
IN_TM = 1024
IN_SUB = 512

_TAB_GROUP = {EP_ROPE128: "r128", EP_ROPE128_Q: "r128q", EP_ROPE64: "r64", EP_ROPE64_Q: "r64q"}
_TAB_SIZE = {"r128": 2, "r128q": 2, "r64": 3, "r64q": 3}


def _slab_tab_groups(kinds):
    return tuple(dict.fromkeys(_TAB_GROUP[k] for k in kinds if k in _TAB_GROUP))


def _in_proj_kernel(kinds, with_norm, seq_len, strided_chunks, *refs):
    groups = _slab_tab_groups(kinds)
    refs = list(refs)
    if with_norm:
        x_ref, g_ref = refs.pop(0), refs.pop(0)
    else:
        h_ref = refs.pop(0)
    w_ref = refs.pop(0)
    tab_ref = refs.pop(0) if groups else None
    o_ref = refs.pop(0)
    if with_norm:
        h_ref = refs.pop(0)
        x = x_ref[...]
        ms = jnp.mean(x * x, axis=-1, keepdims=True)
        h_ref[...] = ((x * lax.rsqrt(ms + RMS_EPS)) * g_ref[...]).astype(BF16)
    t_refs = {c: refs.pop(0) for c in strided_chunks}
    if strided_chunks:
        stage_ref = refs.pop(0)
    tm = h_ref.shape[0]
    pos0 = (pl.program_id(0) % (seq_len // tm)) * tm
    base, n = {}, 0
    for grp in groups:
        base[grp] = n
        n += _TAB_SIZE[grp]

    def epilogue(kind, r, a):
        if kind == EP_NONE:
            return a
        if kind == EP_SILU:
            return a * jax.nn.sigmoid(a)
        if kind == EP_SIGMOID:
            return jax.nn.sigmoid(a)
        t = base[_TAB_GROUP[kind]]
        tab = lambda k: tab_ref[t + k, pl.ds(pl.multiple_of(pos0 + r * IN_SUB, IN_SUB), IN_SUB), :]
        if kind in (EP_ROPE128, EP_ROPE128_Q):
            return a * tab(0) + pltpu.roll(a, 64, 1) * tab(1)
        return a * tab(0) + pltpu.roll(a, 96, 1) * tab(1) + pltpu.roll(a, 32, 1) * tab(2)

    n_grp = IN_SUB // CMP_STRIDE

    subs = CHUNK // LANES

    def finish(r, c, acc):
        rows = slice(r * IN_SUB, (r + 1) * IN_SUB)
        for sub in range(subs):
            cols = slice(c * CHUNK + sub * LANES, c * CHUNK + (sub + 1) * LANES)
            val = epilogue(kinds[c * subs + sub], r, acc[:, sub * LANES:(sub + 1) * LANES])
            o_ref[rows, cols] = val.astype(o_ref.dtype)
            if c in t_refs:
                slot = strided_chunks.index(c) * subs + sub
                stage_ref[slot] = val
                for k in range(CMP_STRIDE):
                    t_refs[c][sub, r * n_grp:(r + 1) * n_grp, k * LANES:(k + 1) * LANES] = (
                        stage_ref[slot, pl.ds(k, n_grp, stride=CMP_STRIDE), :].astype(t_refs[c].dtype))

    jobs = []
    for r in range(tm // IN_SUB):
        for c in range(len(kinds) // subs):
            jobs.append((lambda r=r, c=c: _dot_nt(h_ref[r * IN_SUB:(r + 1) * IN_SUB, :],
                                                  w_ref[c * CHUNK:(c + 1) * CHUNK, :]),
                         functools.partial(finish, r, c)))
    _run_skewed(jobs)


def _rope_tables(S):
    def cs(d):
        inv = ROPE_THETA ** (-jnp.arange(0, d, 2, dtype=F32) / d)
        ang = jnp.arange(S, dtype=F32)[:, None] * inv[None, :]
        return jnp.cos(ang), jnp.sin(ang)

    c128, s128 = cs(HEAD_DIM_A)
    c64, s64 = cs(HEAD_DIM_B)
    z32 = jnp.zeros_like(s64)
    t128 = [jnp.concatenate([c128, c128], axis=1), jnp.concatenate([-s128, s128], axis=1)]
    t64 = [jnp.concatenate([c64, c64, c64, c64], axis=1),
           jnp.concatenate([-s64, z32, -s64, z32], axis=1),
           jnp.concatenate([z32, s64, z32, s64], axis=1)]
    qa = HEAD_DIM_A ** -0.5 * LOG2E
    qb = HEAD_DIM_B ** -0.5 * LOG2E
    return {"r128": t128, "r64": t64, "r128q": [t * qa for t in t128], "r64q": [t * qb for t in t64]}


W_PREP_TR = 256


def _w_in_pieces():
    offs = np.cumsum((0,) + IN_SIZES)
    names = ("qa", "kca", "vca", "ksa", "vsa", "kwa", "vwa", "za", "ga", "qb", "kb", "vb", "zb")
    start = {n: int(offs[i]) for i, n in enumerate(names)}
    width = dict(zip(names, IN_SIZES))
    n_g = 3 * GROUP_A
    halves = lambda n: [(start[n] + i * HEAD_DIM_B, HEAD_DIM_B) for i in (0, 0, 1, 1)]
    pieces = {n: [(start[n], width[n])]
              for n in ("qa", "kca", "ksa", "kwa", "vca", "vsa", "vwa", "za", "zb", "qb", "vb")}
    pad = LANES // N_KV_A - n_g
    pieces["ga"] = [(start["ga"], n_g), (None, pad), (start["ga"] + n_g, n_g), (None, pad)]
    pieces["kb2"] = halves("kb")
    return [p for name, _, _ in _PROJ_LAYOUT for p in pieces[name]]


def _w_prep_kernel(w_ref, o_ref):
    row = 0
    parts, filled = [], 0
    for src, wd in _w_in_pieces():
        done = 0
        while done < wd:
            take = min(wd - done, LANES - filled)
            if src is None:
                parts.append(jnp.zeros((take, W_PREP_TR), F32))
            else:
                parts.append(w_ref[src + done:src + done + take, :])
            done += take
            filled += take
            if filled == LANES:
                blk = parts[0] if len(parts) == 1 else jnp.concatenate(parts, axis=0)
                o_ref[row:row + LANES, :] = blk.astype(o_ref.dtype)
                row += LANES
                parts, filled = [], 0
    assert row == D_PROJ and not parts


def _prep_w_in(w_in):
    wt = w_in[0].T
    d_in = wt.shape[0]
    return pl.pallas_call(
        _w_prep_kernel,
        grid=(D_MODEL // W_PREP_TR,),
        in_specs=[pl.BlockSpec((d_in, W_PREP_TR), lambda i: (0, i))],
        out_specs=pl.BlockSpec((D_PROJ, W_PREP_TR), lambda i: (0, i)),
        out_shape=jax.ShapeDtypeStruct((D_PROJ, D_MODEL), BF16),
        compiler_params=pltpu.CompilerParams(dimension_semantics=("arbitrary",)),
        name="w_prep",
    )(wt)


def _in_proj(x2, g, w_all, tabs, S):
    M = x2.shape[0]
    params = pltpu.CompilerParams(dimension_semantics=("arbitrary",), vmem_limit_bytes=VMEM_LIMIT)
    out_sds = jax.ShapeDtypeStruct((M, IN_TN), BF16)
    slabs, strided, h = [], {}, None
    for j in range(N_SLABS):
        kinds = _UNIT_KINDS[j * _IN_UPT:(j + 1) * _IN_UPT]
        with_norm = j == 0
        tm = IN_TM_NORM if with_norm else IN_TM
        groups = _slab_tab_groups(kinds)
        act_spec = pl.BlockSpec((tm, D_MODEL), lambda i: (i, 0))
        in_specs = [act_spec, pl.BlockSpec((1, D_MODEL), lambda i: (0, 0))] if with_norm else [act_spec]
        args = [x2, g] if with_norm else [h]
        in_specs.append(pl.BlockSpec((IN_TN, D_MODEL), lambda i, j=j: (j, 0)))
        args.append(w_all)
        if groups:
            tab = jnp.stack([t for grp in groups for t in tabs[grp]])
            in_specs.append(pl.BlockSpec(tab.shape, lambda i: (0, 0, 0)))
            args.append(tab)
        out_specs = [pl.BlockSpec((tm, IN_TN), lambda i: (i, 0))]
        out_shape = [out_sds]
        if with_norm:
            out_specs.append(act_spec)
            out_shape.append(jax.ShapeDtypeStruct((M, D_MODEL), BF16))
        strided_names = [name for name in ("kca", "vca") if _slab(name) == j]
        strided_chunks = tuple(_col(name) // CHUNK for name in strided_names)
        scratch = []
        for _ in strided_chunks:
            out_specs.append(pl.BlockSpec((N_KV_A, tm // CMP_STRIDE, _HALF), lambda i: (0, i, 0)))
            out_shape.append(jax.ShapeDtypeStruct((N_KV_A, M // CMP_STRIDE, _HALF), BF16))
        if strided_chunks:
            scratch.append(pltpu.VMEM((len(strided_chunks) * N_KV_A, IN_SUB, HEAD_DIM_A), F32))
        res = list(pl.pallas_call(
            functools.partial(_in_proj_kernel, kinds, with_norm, S, strided_chunks),
            grid=(M // tm,),
            in_specs=in_specs,
            out_specs=out_specs,
            out_shape=out_shape,
            scratch_shapes=scratch,
            compiler_params=params,
            name=f"in_proj_{j}",
        )(*args))
        slabs.append(res.pop(0))
        if with_norm:
            h = res.pop(0)
        for name in strided_names:
            strided[name] = res.pop(0)
    return slabs, strided


CMP_TM = 256
_HALF = CMP_STRIDE * HEAD_DIM_A


def _compress_kernel(tk_ref, tv_ref, w1k_ref, w1v_ref, w2k_ref, w2v_ref, posk_ref, posv_ref, kc_ref, vc_ref,
                     pb_ref):
    @pl.when(pl.program_id(0) == 0)
    def _():
        pb_ref[0] = _dot(posk_ref[...], w1k_ref[...])
        pb_ref[1] = _dot(posv_ref[...], w1v_ref[...])

    for n, (t_ref, w1_ref, w2_ref, o_ref) in enumerate(((tk_ref, w1k_ref, w2k_ref, kc_ref),
                                                        (tv_ref, w1v_ref, w2v_ref, vc_ref))):
        t = t_ref[...]
        u = _dot(t, w1_ref[0:_HALF, :])
        v = _dot(t, w1_ref[_HALF:2 * _HALF, :])
        pre = u + pltpu.roll(v, CMP_TM - 1, 0) + pb_ref[n, 0:1]
        hid = pre * jax.nn.sigmoid(pre)
        out = _dot(hid.astype(BF16), w2_ref[...])
        row = lax.broadcasted_iota(jnp.int32, out.shape, 0)
        n_cp = LANES
        out = jnp.where((row & (n_cp - 1)) == n_cp - 1, 0.0, out)
        o_ref[...] = out.astype(o_ref.dtype)


def _compress(tk, tv, w1k, w1v, w2k, w2v, posk, posv):
    R = tk.shape[0]
    rows = pl.BlockSpec((CMP_TM, _HALF), lambda i: (i, 0))
    whole = lambda a: pl.BlockSpec(a.shape, lambda i: (0, 0))
    o_spec = pl.BlockSpec((CMP_TM, HEAD_DIM_A), lambda i: (i, 0))
    o_sds = jax.ShapeDtypeStruct((R, HEAD_DIM_A), BF16)
    return pl.pallas_call(
        _compress_kernel,
        grid=(R // CMP_TM,),
        in_specs=[rows, rows, whole(w1k), whole(w1v), whole(w2k), whole(w2v), whole(posk), whole(posv)],
        out_specs=(o_spec, o_spec),
        out_shape=(o_sds, o_sds),
        scratch_shapes=[pltpu.VMEM((2, SUBLANES, CMP_HIDDEN), F32)],
        compiler_params=pltpu.CompilerParams(dimension_semantics=("arbitrary",)),
        name="compress",
    )(tk, tv, w1k, w1v, w2k, w2v, posk, posv)


TQ = 128
ATTN_AHEAD = 12
ONES_ROWS = 16


def _transpose_values(v_ref, vt_ref, n_rows, second_half=None):
    n_blk = vt_ref.shape[0]
    for blk in range(n_blk):
        vt = v_ref[0, blk * LANES:(blk + 1) * LANES, :].astype(F32).T
        if second_half is not None:
            vt = jnp.where(second_half == 0, vt[0:n_rows], vt[n_rows:2 * n_rows])
        vt_ref[blk, 0:n_rows, :] = vt[0:n_rows].astype(vt_ref.dtype)
        vt_ref[blk, n_rows:n_rows + ONES_ROWS, :] = jnp.ones((ONES_ROWS, LANES), vt_ref.dtype)


def _run_skewed(jobs, ahead=1):
    pending = [job[0]() for job in jobs[:ahead]]
    for i, (_, finish) in enumerate(jobs):
        if i + ahead < len(jobs):
            pending.append(jobs[i + ahead][0]())
        finish(pending[i])
        pending[i] = None


def _softmax_step(s, m, acc, vt):
    m_new = jnp.maximum(m, jnp.max(s, axis=0, keepdims=True))
    p = jnp.exp2(s - m_new).astype(BF16)
    return m_new, jnp.exp2(m - m_new) * acc + _dot(vt, p)


SEL_TK = 256
N_SELP = 32
PAIR = 2 * TQ
N_TILE_PAIRS = 2
N_TILES = 2 * N_TILE_PAIRS
_VROWS_A = HEAD_DIM_A + ONES_ROWS
_VROWS_B = HEAD_DIM_B + ONES_ROWS
_WIN_TILES = (256, 256, 128)
_N_SEL_JOBS = 9


def _query_tile(t, i, n_qt):
    low = i if t // 2 == 0 else n_qt // 2 - 1 - i
    return low if t % 2 == 0 else n_qt - 1 - low


def _attn_kernel(*refs):
    refs = list(refs)
    take = lambda n: [refs.pop(0) for _ in range(n)]
    sink_ref, = take(1)
    q_refs = take(N_TILES)
    ksel_ref, vsel_ref, kwin_ref, vwin_ref, kc_ref, vc_ref = take(6)
    g_refs, z_refs = take(N_TILES), take(N_TILES)
    ov_ref, wtab_ref, ctab_ref = take(3)
    qb_refs = take(N_TILES)
    kb_ref, vb_ref = take(2)
    zb_refs = take(N_TILES)
    btab_ref, = take(1)
    y_refs, yb_refs = take(N_TILES), take(N_TILES)
    vselt_ref, vwint_ref, vbt_ref, qs_ref, bias_ref, acc_ref, out_ref = refs

    kv = pl.program_id(1)
    i = pl.program_id(2)
    n_qt = N_TILES * pl.num_programs(2)
    n_pairs = GROUP_A // 2
    tiles = range(N_TILES)
    qts = [_query_tile(t, i, n_qt) for t in tiles]
    t0s = [qt * TQ for qt in qts]

    @pl.when(i == 0)
    def _():
        _transpose_values(vsel_ref, vselt_ref, HEAD_DIM_A)
        _transpose_values(vwin_ref, vwint_ref, HEAD_DIM_A)
        _transpose_values(vb_ref, vbt_ref, HEAD_DIM_B, kv)

    lane = lax.broadcasted_iota(jnp.int32, (1, TQ), 1)
    lane2 = jnp.concatenate([lane, lane], axis=1)
    qpair, gts = [], []
    for t in tiles:
        q = q_refs[t][0]
        pairs = [jnp.concatenate([q[:, h * LANES:(h + 1) * LANES] for h in (2 * p, 2 * p + 1)], axis=0)
                 for p in range(n_pairs)]
        for p in range(n_pairs):
            qs_ref[t, p] = pairs[p]
        qpair.append(pairs)
        gt = g_refs[t][0].astype(F32).T
        gts.append(jnp.where(kv == 0, gt[0:LANES // 2], gt[LANES // 2:LANES]))

    def gate(t, p, branch):
        return jnp.concatenate(
            [gts[t][3 * h + branch:3 * h + branch + 1] for h in (2 * p, 2 * p + 1)], axis=1)

    kc = kc_ref[0]
    vct = vc_ref[0].astype(F32).T.astype(BF16)
    cend = lax.broadcasted_iota(jnp.int32, (LANES, PAIR), 0) * CMP_STRIDE + (CMP_BLOCK - 1)
    psums = [[] for _ in tiles]

    def cmp_scores(t, p):
        return jnp.where(cend <= t0s[t] + lane2, _dot_nt(kc, qpair[t][p]), NEG_INF)

    def cmp_finish(t, p, s):
        m = jnp.max(s, axis=0, keepdims=True)
        e = jnp.where(cend <= t0s[t] + lane2, jnp.exp2(s - m), 0.0)
        pc = e / jnp.maximum(jnp.sum(e, axis=0, keepdims=True), TINY)
        psums[t].append(pc[:, 0:TQ] + pc[:, TQ:PAIR])
        out_ref[t, p] = gate(t, p, 0) * _dot(vct, pc.astype(BF16))

    ksts = [pl.multiple_of(jnp.maximum(t0 - WIN_A, 0), TQ) for t0 in t0s]
    n_win = sum(_WIN_TILES)

    win_var = [jnp.minimum(qt, WIN_A // TQ) for qt in qts]

    def win_scores(t, p):
        out, off = [], 0
        for tk in _WIN_TILES:
            kt = kwin_ref[0, pl.ds(pl.multiple_of(ksts[t] + off, LANES), tk), :]
            bias = wtab_ref[win_var[t], off:off + tk, :]
            out.append(_dot_nt(kt, qpair[t][p]) + jnp.concatenate([bias, bias], axis=1))
            off += tk
        return out

    def win_finish(t, p, ss):
        m = functools.reduce(jnp.maximum, [jnp.max(s, axis=0, keepdims=True) for s in ss])
        pw = jnp.concatenate([jnp.exp2(s - m).astype(BF16) for s in ss], axis=0)
        vt = jnp.concatenate([vwint_ref[ksts[t] // LANES + b] for b in range(n_win // LANES)], axis=1)
        a = _dot(vt, pw)
        out_ref[t, p] = out_ref[t, p] + (
            gate(t, p, 2) / jnp.maximum(a[HEAD_DIM_A:HEAD_DIM_A + 1], TINY)) * a[0:HEAD_DIM_A]

    jobs = [(functools.partial(cmp_scores, t, p), functools.partial(cmp_finish, t, p))
            for t in tiles for p in range(n_pairs)]
    jobs += [(functools.partial(win_scores, t, p), functools.partial(win_finish, t, p))
             for t in tiles for p in range(n_pairs)]

    nb_keys = WIN_B + TQ
    left = lax.broadcasted_iota(jnp.int32, (TQ, LANES), 1) < HEAD_DIM_B
    zero = jnp.zeros((TQ, LANES), qb_refs[0].dtype)
    kbs = [pl.multiple_of(jnp.maximum(t0 - WIN_B, 0), TQ) for t0 in t0s]
    b_var = [jnp.minimum(qt, WIN_B // TQ) for qt in qts]

    def b_sink(c):
        return jnp.concatenate(
            [jnp.full((1, TQ), sink_ref[kv * GROUP_B + 2 * c + h] * LOG2E, F32) for h in range(2)], axis=1)

    def b_scores(t, c):
        pair = qb_refs[t][0, :, c * LANES:(c + 1) * LANES]
        qp = jnp.concatenate([jnp.where(left, pair, zero), jnp.where(left, zero, pair)], axis=0)
        bias = btab_ref[b_var[t]]
        return _dot_nt(kb_ref[0, pl.ds(kbs[t], nb_keys), :], qp) + jnp.concatenate([bias, bias], axis=1)

    def b_finish(t, c, s):
        vt = jnp.concatenate([vbt_ref[kbs[t] // LANES + b] for b in range(nb_keys // LANES)], axis=1)
        sink = b_sink(c)
        m = jnp.maximum(jnp.max(s, axis=0, keepdims=True), sink)
        p = jnp.exp2(s - m).astype(BF16)
        a = _dot(vt, p)
        den = jnp.maximum(a[HEAD_DIM_B:HEAD_DIM_B + 1] + jnp.exp2(sink - m), TINY)
        o = a[0:HEAD_DIM_B] / den
        o2 = jnp.concatenate([o[:, 0:TQ], o[:, TQ:PAIR]], axis=0)
        z = zb_refs[t][0, :, c * LANES:(c + 1) * LANES]
        yb_refs[t][0, :, c * LANES:(c + 1) * LANES] = (o2.T * z.astype(F32)).astype(yb_refs[t].dtype)

    b_jobs = [[(functools.partial(b_scores, t, c), functools.partial(b_finish, t, c))
               for c in range(GROUP_B // 2)] for t in tiles]

    def select_blocks(t):
        ov = ov_ref[...]
        jidx = lax.broadcasted_iota(jnp.int32, (N_SELP, TQ), 0)
        psum = psums[t][0] + psums[t][1]
        p_hi = psum.astype(BF16)
        r1 = psum - p_hi.astype(F32)
        p_mid = r1.astype(BF16)
        p_lo = (r1 - p_mid.astype(F32)).astype(BF16)
        psel = _dot(ov, p_hi) + _dot(ov, p_mid) + _dot(ov, p_lo)
        tq = t0s[t] + lane
        cur = tq // SEL_BLOCK
        forced = (jidx == 0) | (jidx == cur) | (jidx == cur - 1)
        score = jnp.where(forced, FORCE_SCORE, jnp.where(jidx * SEL_BLOCK <= tq, psel, -1.0))
        rank = jnp.zeros(psel.shape, F32)
        for r in range(N_SELP):
            row = jnp.broadcast_to(score[r:r + 1, :], score.shape)
            rank = rank + jnp.where(jidx > r, jnp.where(row >= score, 1.0, 0.0), jnp.where(row > score, 1.0, 0.0))
        bias_t = jnp.where(rank < SEL_TOPK, 0.0, NEG_INF)
        for j in range(N_SELP):
            bias_ref[t, j] = jnp.broadcast_to(bias_t[j:j + 1, :], (SUBLANES, TQ))

    blk_per_tile = SEL_TK // SEL_BLOCK
    rep = SEL_BLOCK // SUBLANES
    acc_ref[...] = jnp.zeros(acc_ref.shape, F32)
    ms = [[jnp.full((1, PAIR), NEG_INF, F32) for _ in range(n_pairs)] for _ in tiles]
    ctx = {}

    def sel_scores(g, k, p):
        if p == 0:
            if g == 0 and k == 0:
                for t in tiles:
                    select_blocks(t)
            n_lo = qts[2 * g] // (SEL_TK // TQ) + 1
            hi = k >= n_lo
            t = 2 * g + hi.astype(jnp.int32)
            j = jnp.where(hi, k - n_lo, k)
            ks = pl.multiple_of(j * SEL_TK, SEL_TK)
            bt = jnp.concatenate(
                [bias_ref[t, j * blk_per_tile + b] for b in range(blk_per_tile) for _ in range(rep)], axis=0)
            qt = jnp.where(hi, qts[2 * g + 1], qts[2 * g])
            diag = j == qt // (SEL_TK // TQ)
            bt = bt + ctab_ref[jnp.where(diag, 1 + qt % (SEL_TK // TQ), 0)]
            bt = jnp.concatenate([bt, bt], axis=1)
            ctx[g, k] = (hi, t, j, ks, bt)
        hi, t, j, ks, bt = ctx[g, k]
        return _dot_nt(ksel_ref[0, pl.ds(ks, SEL_TK), :], qs_ref[t, p]) + bt

    def sel_finish(g, k, p, s):
        hi, t, j, ks, bt = ctx[g, k]
        lo_t, hi_t = 2 * g, 2 * g + 1
        vt = jnp.concatenate([vselt_ref[j * (SEL_TK // LANES) + b] for b in range(SEL_TK // LANES)], axis=1)
        m_new, acc_new = _softmax_step(s, jnp.where(hi, ms[hi_t][p], ms[lo_t][p]), acc_ref[t, p], vt)
        acc_ref[t, p] = acc_new
        ms[lo_t][p] = jnp.where(hi, ms[lo_t][p], m_new)
        ms[hi_t][p] = jnp.where(hi, m_new, ms[hi_t][p])

    sel_jobs = [(functools.partial(sel_scores, g, k, p), functools.partial(sel_finish, g, k, p))
                for g in range(N_TILE_PAIRS) for k in range(_N_SEL_JOBS) for p in range(n_pairs)]
    b_before = [job for t in tiles if t % 2 == 0 for job in b_jobs[t]]
    b_after = [job for t in tiles if t % 2 == 1 for job in b_jobs[t]]
    _run_skewed(jobs + b_before + sel_jobs + b_after, ATTN_AHEAD)

    for t in tiles:
        z = z_refs[t][0]
        for p in range(n_pairs):
            a = acc_ref[t, p]
            o = out_ref[t, p] + (gate(t, p, 1) / jnp.maximum(a[HEAD_DIM_A:HEAD_DIM_A + 1], TINY)) * a[0:HEAD_DIM_A]
            for h in (2 * p, 2 * p + 1):
                oh = o[:, (h % 2) * TQ:(h % 2 + 1) * TQ].T
                y_refs[t][0, :, h * LANES:(h + 1) * LANES] = (
                    oh * z[:, h * LANES:(h + 1) * LANES].astype(F32)).astype(y_refs[t].dtype)


def _mask_tables():
    lane = np.arange(TQ)[None, :]
    row = np.arange(WIN_A + TQ)[:, None]
    wtab = []
    for v in range(WIN_A // TQ + 1):
        t0 = v * TQ
        dpos = (t0 + lane) - (max(t0 - WIN_A, 0) + row)
        wtab.append(np.where((dpos >= 0) & (dpos < WIN_A), 0.0, NEG_INF))
    row = np.arange(SEL_TK)[:, None]
    ctab = [np.zeros((SEL_TK, TQ))]
    for odd in range(SEL_TK // TQ):
        ctab.append(np.where(row <= odd * TQ + lane, 0.0, NEG_INF))
    row = np.arange(WIN_B + TQ)[:, None]
    btab = []
    for v in range(WIN_B // TQ + 1):
        t0 = v * TQ
        dpos = (t0 + lane) - (max(t0 - WIN_B, 0) + row)
        btab.append(np.where((dpos >= 0) & (dpos < WIN_B), 0.0, NEG_INF))
    return [jnp.asarray(np.stack(t), F32) for t in (wtab, ctab, btab)]


def _attention(slabs, kc, vc, ov, sinks):
    B, S, _ = slabs[0].shape
    sl = lambda name: slabs[_slab(name)]
    n_cp = S // CMP_STRIDE
    n_qt = S // TQ
    n_steps = n_qt // N_TILES
    assert n_cp == LANES and S // SEL_BLOCK == N_SELP and (n_qt // 2) // (SEL_TK // TQ) * 2 + 1 == _N_SEL_JOBS
    assert N_KV_A == N_KV_B and GROUP_A * HEAD_DIM_A == GROUP_B * HEAD_DIM_B and N_TILE_PAIRS in (1, 2)
    wq = GROUP_A * HEAD_DIM_A
    cb = lambda name: _col(name) // LANES
    kv_spec = lambda name: pl.BlockSpec((1, S, LANES), lambda b, k, i, s, c=cb(name): (b, 0, c + k))
    whole = lambda a: pl.BlockSpec(a.shape, lambda b, k, i, s: (0,) * a.ndim)

    def tile_specs(width, c, per_kv=True):
        return [pl.BlockSpec((1, TQ, width),
                             lambda b, k, i, s, t=t: (b, _query_tile(t, i, n_qt), c + (k if per_kv else 0)))
                for t in range(N_TILES)]

    part = [_query_tile(t, 0, n_qt) // n_steps for t in range(N_TILES)]
    assert sorted(part) == list(range(N_TILES))
    y_specs = [pl.BlockSpec((1, TQ, wq), lambda b, k, i, s, t=t: (b, _query_tile(t, i, n_qt) - part[t] * n_steps, k))
               for t in range(N_TILES)]
    y_sds = jax.ShapeDtypeStruct((B, S // N_TILES, WIDTH_A), BF16)
    wtab, ctab, btab = _mask_tables()
    grid_spec = pltpu.PrefetchScalarGridSpec(
        num_scalar_prefetch=1,
        grid=(B, N_KV_A, n_steps),
        in_specs=[
            *tile_specs(wq, _col("qa") // wq),
            kv_spec("ksa"), kv_spec("vsa"), kv_spec("kwa"), kv_spec("vwa"),
            pl.BlockSpec((1, n_cp, LANES), lambda b, k, i, s: (k * B + b, 0, 0)),
            pl.BlockSpec((1, n_cp, LANES), lambda b, k, i, s: (k * B + b, 0, 0)),
            *tile_specs(LANES, cb("ga"), per_kv=False),
            *tile_specs(wq, _col("za") // wq),
            whole(ov), whole(wtab), whole(ctab),
            *tile_specs(wq, _col("qb") // wq),
            kv_spec("kb2"), pl.BlockSpec((1, S, LANES), lambda b, k, i, s: (b, 0, cb("vb"))),
            *tile_specs(wq, _col("zb") // wq),
            whole(btab),
        ],
        out_specs=y_specs + y_specs,
        scratch_shapes=[
            pltpu.VMEM((S // LANES, _VROWS_A, LANES), BF16),
            pltpu.VMEM((S // LANES, _VROWS_A, LANES), BF16),
            pltpu.VMEM((S // LANES, _VROWS_B, LANES), BF16),
            pltpu.VMEM((N_TILES, GROUP_A // 2, PAIR, HEAD_DIM_A), BF16),
            pltpu.VMEM((N_TILES, N_SELP, SUBLANES, TQ), F32),
            pltpu.VMEM((N_TILES, GROUP_A // 2, _VROWS_A, PAIR), F32),
            pltpu.VMEM((N_TILES, GROUP_A // 2, HEAD_DIM_A, PAIR), F32),
        ],
    )
    per_tile = lambda name: [sl(name)] * N_TILES
    outs = pl.pallas_call(
        _attn_kernel,
        grid_spec=grid_spec,
        out_shape=[y_sds] * (2 * N_TILES),
        compiler_params=pltpu.CompilerParams(
            dimension_semantics=("arbitrary", "arbitrary", "arbitrary"), vmem_limit_bytes=VMEM_LIMIT),
        name="attn",
    )(sinks, *per_tile("qa"), sl("ksa"), sl("vsa"), sl("kwa"), sl("vwa"), kc, vc,
      *per_tile("ga"), *per_tile("za"), ov, wtab, ctab,
      *per_tile("qb"), sl("kb2"), sl("vb"), *per_tile("zb"), btab)
    in_order = lambda ys: [ys[part.index(n)] for n in range(N_TILES)]
    return in_order(outs[:N_TILES]), in_order(outs[N_TILES:])


def _out_proj_kernel(n_parts, *refs):
    ya_refs, yb_refs = refs[:n_parts], refs[n_parts:2 * n_parts]
    wa_ref, wb_ref, x_ref, g_ref, o_ref = refs[2 * n_parts:]
    part = pl.program_id(0) % n_parts

    for n in range(n_parts):
        @pl.when(part == n)
        def _(n=n):
            r = x_ref[...] + _dot(ya_refs[n][...], wa_ref[...]) + _dot(yb_refs[n][...], wb_ref[...])
            ms = jnp.mean(r * r, axis=-1, keepdims=True)
            o_ref[...] = (r * lax.rsqrt(ms + RMS_EPS)) * g_ref[...]


def _out_proj(ya_parts, yb_parts, w_out, x2, g, S):
    M = x2.shape[0]
    n_parts = len(ya_parts)
    tm = S // n_parts
    n_seq = M // S
    y_spec = lambda w, n: pl.BlockSpec(
        (tm, w), lambda i: (jnp.minimum((i + n_parts - 1 - n) // n_parts, n_seq - 1), 0))
    return pl.pallas_call(
        functools.partial(_out_proj_kernel, n_parts),
        grid=(M // tm,),
        in_specs=[
            *[y_spec(WIDTH_A, n) for n in range(n_parts)], *[y_spec(WIDTH_B, n) for n in range(n_parts)],
            pl.BlockSpec((WIDTH_A, D_MODEL), lambda i: (0, 0)),
            pl.BlockSpec((WIDTH_B, D_MODEL), lambda i: (WIDTH_A // WIDTH_B, 0)),
            pl.BlockSpec((tm, D_MODEL), lambda i: (i, 0)),
            pl.BlockSpec((1, D_MODEL), lambda i: (0, 0)),
        ],
        out_specs=pl.BlockSpec((tm, D_MODEL), lambda i: (i, 0)),
        out_shape=jax.ShapeDtypeStruct((M, D_MODEL), F32),
        compiler_params=pltpu.CompilerParams(
            dimension_semantics=("arbitrary",), vmem_limit_bytes=VMEM_LIMIT_OUT_PROJ),
        name="out_proj",
    )(*ya_parts, *yb_parts, w_out, w_out, x2, g)


def _overlap_matrix(n_cp):
    c_start = np.arange(n_cp) * CMP_STRIDE
    j_start = np.arange(N_SELP) * SEL_BLOCK
    ov = (c_start[None, :] < j_start[:, None] + SEL_BLOCK) & (c_start[None, :] + CMP_BLOCK > j_start[:, None])
    return jnp.asarray(ov, BF16)


def kernel(x, w_in, cmp_k_w1, cmp_k_w2, cmp_v_w1, cmp_v_w2, cmp_k_pos, cmp_v_pos, sinks, w_out, norm_g, final_g):
    B, S, D = x.shape
    assert D == D_MODEL and w_in.shape[0] == 1
    x2 = x.reshape(B * S, D)

    w_all = _prep_w_in(w_in)
    tabs = _rope_tables(S)
    slabs, strided = _in_proj(x2, norm_g[0].reshape(1, D), w_all, tabs, S)
    slabs = [p.reshape(B, S, IN_TN) for p in slabs]

    n_cp = S // CMP_STRIDE
    tk, tv = [strided[n].reshape(N_KV_A * B * n_cp, _HALF) for n in ("kca", "vca")]
    flat_pos = lambda p: jnp.broadcast_to(p.reshape(1, CMP_BLOCK * HEAD_DIM_A), (8, CMP_BLOCK * HEAD_DIM_A))
    kc, vc = _compress(tk, tv, cmp_k_w1[0].astype(BF16), cmp_v_w1[0].astype(BF16),
                       cmp_k_w2[0].astype(BF16), cmp_v_w2[0].astype(BF16),
                       flat_pos(cmp_k_pos[0]).astype(BF16), flat_pos(cmp_v_pos[0]).astype(BF16))
    kc, vc = [c.reshape(N_KV_A * B, n_cp, HEAD_DIM_A) for c in (kc, vc)]

    ya_parts, yb_parts = _attention(slabs, kc, vc, _overlap_matrix(n_cp), sinks[0])
    flat = lambda ys: [y.reshape(B * y.shape[1], y.shape[2]) for y in ys]
    out = _out_proj(flat(ya_parts), flat(yb_parts), w_out[0].astype(BF16), x2, final_g.reshape(1, D), S)
    return out.reshape(B, S, D)
```

```python
import functools
import math

import numpy as np
import jax
import jax.numpy as jnp
from jax import lax
from jax.experimental import pallas as pl
from jax.experimental.pallas import tpu as pltpu

F32 = jnp.float32
BF16 = jnp.bfloat16

D_MODEL = 2048
ROPE_THETA = 10000.0
RMS_EPS = 1e-6
NEG_INF = -1e30
TINY = 1e-30
LOG2E = math.log2(math.e)

WIDTH_A = 1024
HEAD_DIM_A = 128
N_KV_A = 2
GROUP_A = 4
KV_A = N_KV_A * HEAD_DIM_A
CMP_BLOCK = 32
CMP_STRIDE = 16
CMP_HIDDEN = 256
SEL_BLOCK = 64
SEL_TOPK = 16
WIN_A = 512
FORCE_SCORE = 1e4

WIDTH_B = 1024
HEAD_DIM_B = 64
N_KV_B = 2
GROUP_B = 8
KV_B = N_KV_B * HEAD_DIM_B
WIN_B = 128

IN_SIZES = (WIDTH_A, KV_A, KV_A, KV_A, KV_A, KV_A, KV_A, WIDTH_A, 3 * 8,
            WIDTH_B, KV_B, KV_B, WIDTH_B)

LANES = 128
SUBLANES = 8
VMEM_BYTES = 64 * 1024 * 1024
VMEM_LIMIT = VMEM_BYTES * 3 // 4
VMEM_LIMIT_OUT_PROJ = VMEM_BYTES * 7 // 8
CHUNK = 256

EP_NONE, EP_ROPE128, EP_ROPE64, EP_SILU, EP_SIGMOID, EP_ROPE128_Q, EP_ROPE64_Q = range(7)

_PROJ_LAYOUT = (
    ("qa", 8, EP_ROPE128_Q), ("kca", 2, EP_ROPE128), ("ksa", 2, EP_ROPE128), ("kwa", 2, EP_ROPE128),
    ("vca", 2, EP_NONE),
    ("vsa", 2, EP_NONE), ("vwa", 2, EP_NONE), ("za", 8, EP_SILU), ("kb2", 2, EP_ROPE64),
    ("ga", 1, EP_SIGMOID), ("vb", 1, EP_NONE),
    ("zb", 8, EP_SILU), ("qb", 8, EP_ROPE64_Q),
)
_UNIT_KINDS = tuple(k for _, n, k in _PROJ_LAYOUT for _ in range(n))
_UNIT_START = {}
_c = 0
for _name, _n, _k in _PROJ_LAYOUT:
    _UNIT_START[_name] = _c
    _c += _n
D_PROJ = _c * LANES
IN_TN = 2048
_IN_UPT = IN_TN // LANES
N_SLABS = D_PROJ // IN_TN


def _slab(name):
    return _UNIT_START[name] // _IN_UPT


def _col(name):
    return (_UNIT_START[name] % _IN_UPT) * LANES


def _dot(a, b):
    return jnp.dot(a, b, preferred_element_type=F32)


def _dot_nt(a, b):
    return lax.dot_general(a, b, (((1,), (1,)), ((), ())), preferred_element_type=F32)


IN_TM_NORM = 512
IN_TM = 1024
IN_SUB = 256

_TAB_GROUP = {EP_ROPE128: "r128", EP_ROPE128_Q: "r128q", EP_ROPE64: "r64", EP_ROPE64_Q: "r64q"}
_TAB_SIZE = {"r128": 2, "r128q": 2, "r64": 3, "r64q": 3}


def _slab_tab_groups(kinds):
    return tuple(dict.fromkeys(_TAB_GROUP[k] for k in kinds if k in _TAB_GROUP))


def _in_proj_kernel(kinds, with_norm, seq_len, strided_chunks, *refs):
    groups = _slab_tab_groups(kinds)
    refs = list(refs)
    if with_norm:
        x_ref, g_ref = refs.pop(0), refs.pop(0)
    else:
        h_ref = refs.pop(0)
    w_ref = refs.pop(0)
    tab_ref = refs.pop(0) if groups else None
    o_ref = refs.pop(0)
    if with_norm:
        h_ref = refs.pop(0)
        x = x_ref[...]
        ms = jnp.mean(x * x, axis=-1, keepdims=True)
        h_ref[...] = ((x * lax.rsqrt(ms + RMS_EPS)) * g_ref[...]).astype(BF16)
    t_refs = {c: refs.pop(0) for c in strided_chunks}
    if strided_chunks:
        stage_ref = refs.pop(0)
    tm = h_ref.shape[0]
    sub_rows = min(IN_SUB, tm)
    pos0 = (pl.program_id(0) % (seq_len // tm)) * tm
    base, n = {}, 0
    for grp in groups:
        base[grp] = n
        n += _TAB_SIZE[grp]

    def epilogue(kind, r, a):
        if kind == EP_NONE:
            return a
        if kind == EP_SILU:
            return a * jax.nn.sigmoid(a)
        if kind == EP_SIGMOID:
            return jax.nn.sigmoid(a)
        t = base[_TAB_GROUP[kind]]
        tab = lambda k: tab_ref[t + k, pl.ds(pl.multiple_of(pos0 + r * sub_rows, sub_rows), sub_rows), :]
        if kind in (EP_ROPE128, EP_ROPE128_Q):
            return a * tab(0) + pltpu.roll(a, 64, 1) * tab(1)
        return a * tab(0) + pltpu.roll(a, 96, 1) * tab(1) + pltpu.roll(a, 32, 1) * tab(2)

    n_grp = sub_rows // CMP_STRIDE

    subs = CHUNK // LANES

    def finish(r, c, acc):
        rows = slice(r * sub_rows, (r + 1) * sub_rows)
        for sub in range(subs):
            cols = slice(c * CHUNK + sub * LANES, c * CHUNK + (sub + 1) * LANES)
            val = epilogue(kinds[c * subs + sub], r, acc[:, sub * LANES:(sub + 1) * LANES])
            o_ref[rows, cols] = val.astype(o_ref.dtype)
            if c in t_refs:
                slot = strided_chunks.index(c) * subs + sub
                stage_ref[slot] = val
                for k in range(CMP_STRIDE):
                    t_refs[c][sub, r * n_grp:(r + 1) * n_grp, k * LANES:(k + 1) * LANES] = (
                        stage_ref[slot, pl.ds(k, n_grp, stride=CMP_STRIDE), :].astype(t_refs[c].dtype))

    jobs = []
    for r in range(tm // sub_rows):
        for c in range(len(kinds) // subs):
            jobs.append((lambda r=r, c=c: _dot_nt(h_ref[r * sub_rows:(r + 1) * sub_rows, :],
                                                  w_ref[c * CHUNK:(c + 1) * CHUNK, :]),
                         functools.partial(finish, r, c)))
    _run_skewed(jobs)


def _rope_tables(S):
    def cs(d):
        inv = ROPE_THETA ** (-jnp.arange(0, d, 2, dtype=F32) / d)
        ang = jnp.arange(S, dtype=F32)[:, None] * inv[None, :]
        return jnp.cos(ang), jnp.sin(ang)

    c128, s128 = cs(HEAD_DIM_A)
    c64, s64 = cs(HEAD_DIM_B)
    z32 = jnp.zeros_like(s64)
    t128 = [jnp.concatenate([c128, c128], axis=1), jnp.concatenate([-s128, s128], axis=1)]
    t64 = [jnp.concatenate([c64, c64, c64, c64], axis=1),
           jnp.concatenate([-s64, z32, -s64, z32], axis=1),
           jnp.concatenate([z32, s64, z32, s64], axis=1)]
    qa = HEAD_DIM_A ** -0.5 * LOG2E
    qb = HEAD_DIM_B ** -0.5 * LOG2E
    return {"r128": t128, "r64": t64, "r128q": [t * qa for t in t128], "r64q": [t * qb for t in t64]}


W_PREP_TR = 256


def _w_in_pieces():
    offs = np.cumsum((0,) + IN_SIZES)
    names = ("qa", "kca", "vca", "ksa", "vsa", "kwa", "vwa", "za", "ga", "qb", "kb", "vb", "zb")
    start = {n: int(offs[i]) for i, n in enumerate(names)}
    width = dict(zip(names, IN_SIZES))
    n_g = 3 * GROUP_A
    halves = lambda n: [(start[n] + i * HEAD_DIM_B, HEAD_DIM_B) for i in (0, 0, 1, 1)]
    pieces = {n: [(start[n], width[n])]
              for n in ("qa", "kca", "ksa", "kwa", "vca", "vsa", "vwa", "za", "zb", "qb", "vb")}
    pad = LANES // N_KV_A - n_g
    pieces["ga"] = [(start["ga"], n_g), (None, pad), (start["ga"] + n_g, n_g), (None, pad)]
    pieces["kb2"] = halves("kb")
    return [p for name, _, _ in _PROJ_LAYOUT for p in pieces[name]]


def _w_prep_kernel(w_ref, o_ref):
    row = 0
    parts, filled = [], 0
    for src, wd in _w_in_pieces():
        done = 0
        while done < wd:
            take = min(wd - done, LANES - filled)
            if src is None:
                parts.append(jnp.zeros((take, W_PREP_TR), F32))
            else:
                parts.append(w_ref[src + done:src + done + take, :])
            done += take
            filled += take
            if filled == LANES:
                blk = parts[0] if len(parts) == 1 else jnp.concatenate(parts, axis=0)
                o_ref[row:row + LANES, :] = blk.astype(o_ref.dtype)
                row += LANES
                parts, filled = [], 0
    assert row == D_PROJ and not parts


def _prep_w_in(w_in):
    wt = w_in[0].T
    d_in = wt.shape[0]
    return pl.pallas_call(
        _w_prep_kernel,
        grid=(D_MODEL // W_PREP_TR,),
        in_specs=[pl.BlockSpec((d_in, W_PREP_TR), lambda i: (0, i))],
        out_specs=pl.BlockSpec((D_PROJ, W_PREP_TR), lambda i: (0, i)),
        out_shape=jax.ShapeDtypeStruct((D_PROJ, D_MODEL), BF16),
        compiler_params=pltpu.CompilerParams(dimension_semantics=("arbitrary",)),
        name="w_prep",
    )(wt)


def _in_proj(x2, g, w_all, tabs, S):
    M = x2.shape[0]
    params = pltpu.CompilerParams(dimension_semantics=("arbitrary",), vmem_limit_bytes=VMEM_LIMIT)
    out_sds = jax.ShapeDtypeStruct((M, IN_TN), BF16)
    slabs, strided, h = [], {}, None
    for j in range(N_SLABS):
        kinds = _UNIT_KINDS[j * _IN_UPT:(j + 1) * _IN_UPT]
        with_norm = j == 0
        tm = IN_TM_NORM if with_norm else IN_TM
        groups = _slab_tab_groups(kinds)
        act_spec = pl.BlockSpec((tm, D_MODEL), lambda i: (i, 0))
        in_specs = [act_spec, pl.BlockSpec((1, D_MODEL), lambda i: (0, 0))] if with_norm else [act_spec]
        args = [x2, g] if with_norm else [h]
        in_specs.append(pl.BlockSpec((IN_TN, D_MODEL), lambda i, j=j: (j, 0)))
        args.append(w_all)
        if groups:
            tab = jnp.stack([t for grp in groups for t in tabs[grp]])
            in_specs.append(pl.BlockSpec(tab.shape, lambda i: (0, 0, 0)))
            args.append(tab)
        out_specs = [pl.BlockSpec((tm, IN_TN), lambda i: (i, 0))]
        out_shape = [out_sds]
        if with_norm:
            out_specs.append(act_spec)
            out_shape.append(jax.ShapeDtypeStruct((M, D_MODEL), BF16))
        strided_names = [name for name in ("kca", "vca") if _slab(name) == j]
        strided_chunks = tuple(_col(name) // CHUNK for name in strided_names)
        scratch = []
        for _ in strided_chunks:
            out_specs.append(pl.BlockSpec((N_KV_A, tm // CMP_STRIDE, _HALF), lambda i: (0, i, 0)))
            out_shape.append(jax.ShapeDtypeStruct((N_KV_A, M // CMP_STRIDE, _HALF), BF16))
        if strided_chunks:
            scratch.append(pltpu.VMEM((len(strided_chunks) * N_KV_A, min(IN_SUB, tm), HEAD_DIM_A), F32))
        res = list(pl.pallas_call(
            functools.partial(_in_proj_kernel, kinds, with_norm, S, strided_chunks),
            grid=(M // tm,),
            in_specs=in_specs,
            out_specs=out_specs,
            out_shape=out_shape,
            scratch_shapes=scratch,
            compiler_params=params,
            name=f"in_proj_{j}",
        )(*args))
        slabs.append(res.pop(0))
        if with_norm:
            h = res.pop(0)
        for name in strided_names:
            strided[name] = res.pop(0)
    return slabs, strided


CMP_TM = 256
_HALF = CMP_STRIDE * HEAD_DIM_A


def _compress_kernel(tk_ref, tv_ref, w1k_ref, w1v_ref, w2k_ref, w2v_ref, posk_ref, posv_ref, kc_ref, vc_ref,
                     pb_ref):
    @pl.when(pl.program_id(0) == 0)
    def _():
        pb_ref[0] = _dot(posk_ref[...], w1k_ref[...])
        pb_ref[1] = _dot(posv_ref[...], w1v_ref[...])

    for n, (t_ref, w1_ref, w2_ref, o_ref) in enumerate(((tk_ref, w1k_ref, w2k_ref, kc_ref),
                                                        (tv_ref, w1v_ref, w2v_ref, vc_ref))):
        t = t_ref[...]
        u = _dot(t, w1_ref[0:_HALF, :])
        v = _dot(t, w1_ref[_HALF:2 * _HALF, :])
        pre = u + pltpu.roll(v, CMP_TM - 1, 0) + pb_ref[n, 0:1]
        hid = pre * jax.nn.sigmoid(pre)
        out = _dot(hid.astype(BF16), w2_ref[...])
        row = lax.broadcasted_iota(jnp.int32, out.shape, 0)
        n_cp = LANES
        out = jnp.where((row & (n_cp - 1)) == n_cp - 1, 0.0, out)
        o_ref[...] = out.astype(o_ref.dtype)


def _compress(tk, tv, w1k, w1v, w2k, w2v, posk, posv):
    R = tk.shape[0]
    rows = pl.BlockSpec((CMP_TM, _HALF), lambda i: (i, 0))
    whole = lambda a: pl.BlockSpec(a.shape, lambda i: (0, 0))
    o_spec = pl.BlockSpec((CMP_TM, HEAD_DIM_A), lambda i: (i, 0))
    o_sds = jax.ShapeDtypeStruct((R, HEAD_DIM_A), BF16)
    return pl.pallas_call(
        _compress_kernel,
        grid=(R // CMP_TM,),
        in_specs=[rows, rows, whole(w1k), whole(w1v), whole(w2k), whole(w2v), whole(posk), whole(posv)],
        out_specs=(o_spec, o_spec),
        out_shape=(o_sds, o_sds),
        scratch_shapes=[pltpu.VMEM((2, SUBLANES, CMP_HIDDEN), F32)],
        compiler_params=pltpu.CompilerParams(dimension_semantics=("arbitrary",)),
        name="compress",
    )(tk, tv, w1k, w1v, w2k, w2v, posk, posv)


TQ = 128
ATTN_AHEAD = 12
ONES_ROWS = 16


def _transpose_values(v_ref, vt_ref, n_rows, second_half=None):
    n_blk = vt_ref.shape[0]
    for blk in range(n_blk):
        vt = v_ref[0, blk * LANES:(blk + 1) * LANES, :].astype(F32).T
        if second_half is not None:
            vt = jnp.where(second_half == 0, vt[0:n_rows], vt[n_rows:2 * n_rows])
        vt_ref[blk, 0:n_rows, :] = vt[0:n_rows].astype(vt_ref.dtype)
        vt_ref[blk, n_rows:n_rows + ONES_ROWS, :] = jnp.ones((ONES_ROWS, LANES), vt_ref.dtype)


def _run_skewed(jobs, ahead=1):
    pending = [job[0]() for job in jobs[:ahead]]
    for i, (_, finish) in enumerate(jobs):
        if i + ahead < len(jobs):
            pending.append(jobs[i + ahead][0]())
        finish(pending[i])
        pending[i] = None


def _softmax_step(s, m, acc, vt):
    m_new = jnp.maximum(m, jnp.max(s, axis=0, keepdims=True))
    p = jnp.exp2(s - m_new).astype(BF16)
    return m_new, jnp.exp2(m - m_new) * acc + _dot(vt, p)


SEL_TK = 256
N_SELP = 32
PAIR = 2 * TQ
N_TILE_PAIRS = 2
N_TILES = 2 * N_TILE_PAIRS
_VROWS_A = HEAD_DIM_A + ONES_ROWS
_VROWS_B = HEAD_DIM_B + ONES_ROWS
_WIN_TILES = (256, 256, 128)
_N_SEL_JOBS = 9


def _query_tile(t, i, n_qt):
    low = i if t // 2 == 0 else n_qt // 2 - 1 - i
    return low if t % 2 == 0 else n_qt - 1 - low


def _attn_kernel(*refs):
    refs = list(refs)
    take = lambda n: [refs.pop(0) for _ in range(n)]
    sink_ref, = take(1)
    q_refs = take(N_TILES)
    ksel_ref, vsel_ref, kwin_ref, vwin_ref, kc_ref, vc_ref = take(6)
    g_refs, z_refs = take(N_TILES), take(N_TILES)
    ov_ref, wtab_ref, ctab_ref = take(3)
    qb_refs = take(N_TILES)
    kb_ref, vb_ref = take(2)
    zb_refs = take(N_TILES)
    btab_ref, = take(1)
    y_refs, yb_refs = take(N_TILES), take(N_TILES)
    vselt_ref, vwint_ref, vbt_ref, qs_ref, bias_ref, acc_ref, out_ref = refs

    kv = pl.program_id(1)
    i = pl.program_id(2)
    n_qt = N_TILES * pl.num_programs(2)
    n_pairs = GROUP_A // 2
    tiles = range(N_TILES)
    qts = [_query_tile(t, i, n_qt) for t in tiles]
    t0s = [qt * TQ for qt in qts]

    @pl.when(i == 0)
    def _():
        _transpose_values(vsel_ref, vselt_ref, HEAD_DIM_A)
        _transpose_values(vwin_ref, vwint_ref, HEAD_DIM_A)
        _transpose_values(vb_ref, vbt_ref, HEAD_DIM_B, kv)

    lane = lax.broadcasted_iota(jnp.int32, (1, TQ), 1)
    lane2 = jnp.concatenate([lane, lane], axis=1)
    qpair, gts = [], []
    for t in tiles:
        q = q_refs[t][0]
        pairs = [jnp.concatenate([q[:, h * LANES:(h + 1) * LANES] for h in (2 * p, 2 * p + 1)], axis=0)
                 for p in range(n_pairs)]
        for p in range(n_pairs):
            qs_ref[t, p] = pairs[p]
        qpair.append(pairs)
        gt = g_refs[t][0].astype(F32).T
        gts.append(jnp.where(kv == 0, gt[0:LANES // 2], gt[LANES // 2:LANES]))

    def gate(t, p, branch):
        return jnp.concatenate(
            [gts[t][3 * h + branch:3 * h + branch + 1] for h in (2 * p, 2 * p + 1)], axis=1)

    kc = kc_ref[0]
    vct = vc_ref[0].astype(F32).T.astype(BF16)
    cend = lax.broadcasted_iota(jnp.int32, (LANES, PAIR), 0) * CMP_STRIDE + (CMP_BLOCK - 1)
    psums = [[] for _ in tiles]

    def cmp_scores(t, p):
        return jnp.where(cend <= t0s[t] + lane2, _dot_nt(kc, qpair[t][p]), NEG_INF)

    def cmp_finish(t, p, s):
        m = jnp.max(s, axis=0, keepdims=True)
        e = jnp.where(cend <= t0s[t] + lane2, jnp.exp2(s - m), 0.0)
        pc = e / jnp.maximum(jnp.sum(e, axis=0, keepdims=True), TINY)
        psums[t].append(pc[:, 0:TQ] + pc[:, TQ:PAIR])
        out_ref[t, p] = gate(t, p, 0) * _dot(vct, pc.astype(BF16))

    ksts = [pl.multiple_of(jnp.maximum(t0 - WIN_A, 0), TQ) for t0 in t0s]
    n_win = sum(_WIN_TILES)

    win_var = [jnp.minimum(qt, WIN_A // TQ) for qt in qts]

    def win_scores(t, p):
        out, off = [], 0
        for tk in _WIN_TILES:
            kt = kwin_ref[0, pl.ds(pl.multiple_of(ksts[t] + off, LANES), tk), :]
            bias = wtab_ref[win_var[t], off:off + tk, :]
            out.append(_dot_nt(kt, qpair[t][p]) + jnp.concatenate([bias, bias], axis=1))
            off += tk
        return out

    def win_finish(t, p, ss):
        m = functools.reduce(jnp.maximum, [jnp.max(s, axis=0, keepdims=True) for s in ss])
        pw = jnp.concatenate([jnp.exp2(s - m).astype(BF16) for s in ss], axis=0)
        vt = jnp.concatenate([vwint_ref[ksts[t] // LANES + b] for b in range(n_win // LANES)], axis=1)
        a = _dot(vt, pw)
        out_ref[t, p] = out_ref[t, p] + (
            gate(t, p, 2) / jnp.maximum(a[HEAD_DIM_A:HEAD_DIM_A + 1], TINY)) * a[0:HEAD_DIM_A]

    jobs = [(functools.partial(cmp_scores, t, p), functools.partial(cmp_finish, t, p))
            for t in tiles for p in range(n_pairs)]
    jobs += [(functools.partial(win_scores, t, p), functools.partial(win_finish, t, p))
             for t in tiles for p in range(n_pairs)]

    nb_keys = WIN_B + TQ
    left = lax.broadcasted_iota(jnp.int32, (TQ, LANES), 1) < HEAD_DIM_B
    zero = jnp.zeros((TQ, LANES), qb_refs[0].dtype)
    kbs = [pl.multiple_of(jnp.maximum(t0 - WIN_B, 0), TQ) for t0 in t0s]
    b_var = [jnp.minimum(qt, WIN_B // TQ) for qt in qts]

    def b_sink(c):
        return jnp.concatenate(
            [jnp.full((1, TQ), sink_ref[kv * GROUP_B + 2 * c + h] * LOG2E, F32) for h in range(2)], axis=1)

    def b_scores(t, c):
        pair = qb_refs[t][0, :, c * LANES:(c + 1) * LANES]
        qp = jnp.concatenate([jnp.where(left, pair, zero), jnp.where(left, zero, pair)], axis=0)
        bias = btab_ref[b_var[t]]
        return _dot_nt(kb_ref[0, pl.ds(kbs[t], nb_keys), :], qp) + jnp.concatenate([bias, bias], axis=1)

    def b_finish(t, c, s):
        vt = jnp.concatenate([vbt_ref[kbs[t] // LANES + b] for b in range(nb_keys // LANES)], axis=1)
        sink = b_sink(c)
        m = jnp.maximum(jnp.max(s, axis=0, keepdims=True), sink)
        p = jnp.exp2(s - m).astype(BF16)
        a = _dot(vt, p)
        den = jnp.maximum(a[HEAD_DIM_B:HEAD_DIM_B + 1] + jnp.exp2(sink - m), TINY)
        o = a[0:HEAD_DIM_B] / den
        o2 = jnp.concatenate([o[:, 0:TQ], o[:, TQ:PAIR]], axis=0)
        z = zb_refs[t][0, :, c * LANES:(c + 1) * LANES]
        yb_refs[t][0, :, c * LANES:(c + 1) * LANES] = (o2.T * z.astype(F32)).astype(yb_refs[t].dtype)

    b_jobs = [[(functools.partial(b_scores, t, c), functools.partial(b_finish, t, c))
               for c in range(GROUP_B // 2)] for t in tiles]

    def select_blocks(t):
        ov = ov_ref[...]
        jidx = lax.broadcasted_iota(jnp.int32, (N_SELP, TQ), 0)
        psum = psums[t][0] + psums[t][1]
        p_hi = psum.astype(BF16)
        r1 = psum - p_hi.astype(F32)
        p_mid = r1.astype(BF16)
        p_lo = (r1 - p_mid.astype(F32)).astype(BF16)
        psel = _dot(ov, p_hi) + _dot(ov, p_mid) + _dot(ov, p_lo)
        tq = t0s[t] + lane
        cur = tq // SEL_BLOCK
        forced = (jidx == 0) | (jidx == cur) | (jidx == cur - 1)
        score = jnp.where(forced, FORCE_SCORE, jnp.where(jidx * SEL_BLOCK <= tq, psel, -1.0))
        rank = jnp.zeros(psel.shape, F32)
        for r in range(N_SELP):
            row = jnp.broadcast_to(score[r:r + 1, :], score.shape)
            rank = rank + jnp.where(jidx > r, jnp.where(row >= score, 1.0, 0.0), jnp.where(row > score, 1.0, 0.0))
        bias_t = jnp.where(rank < SEL_TOPK, 0.0, NEG_INF)
        for j in range(N_SELP):
            bias_ref[t, j] = jnp.broadcast_to(bias_t[j:j + 1, :], (SUBLANES, TQ))

    blk_per_tile = SEL_TK // SEL_BLOCK
    rep = SEL_BLOCK // SUBLANES
    acc_ref[...] = jnp.zeros(acc_ref.shape, F32)
    ms = [[jnp.full((1, PAIR), NEG_INF, F32) for _ in range(n_pairs)] for _ in tiles]
    ctx = {}

    def sel_scores(g, k, p):
        if p == 0:
            if g == 0 and k == 0:
                for t in tiles:
                    select_blocks(t)
            n_lo = qts[2 * g] // (SEL_TK // TQ) + 1
            hi = k >= n_lo
            t = 2 * g + hi.astype(jnp.int32)
            j = jnp.where(hi, k - n_lo, k)
            ks = pl.multiple_of(j * SEL_TK, SEL_TK)
            bt = jnp.concatenate(
                [bias_ref[t, j * blk_per_tile + b] for b in range(blk_per_tile) for _ in range(rep)], axis=0)
            qt = jnp.where(hi, qts[2 * g + 1], qts[2 * g])
            diag = j == qt // (SEL_TK // TQ)
            bt = bt + ctab_ref[jnp.where(diag, 1 + qt % (SEL_TK // TQ), 0)]
            bt = jnp.concatenate([bt, bt], axis=1)
            ctx[g, k] = (hi, t, j, ks, bt)
        hi, t, j, ks, bt = ctx[g, k]
        return _dot_nt(ksel_ref[0, pl.ds(ks, SEL_TK), :], qs_ref[t, p]) + bt

    def sel_finish(g, k, p, s):
        hi, t, j, ks, bt = ctx[g, k]
        lo_t, hi_t = 2 * g, 2 * g + 1
        vt = jnp.concatenate([vselt_ref[j * (SEL_TK // LANES) + b] for b in range(SEL_TK // LANES)], axis=1)
        m_new, acc_new = _softmax_step(s, jnp.where(hi, ms[hi_t][p], ms[lo_t][p]), acc_ref[t, p], vt)
        acc_ref[t, p] = acc_new
        ms[lo_t][p] = jnp.where(hi, ms[lo_t][p], m_new)
        ms[hi_t][p] = jnp.where(hi, m_new, ms[hi_t][p])

    sel_jobs = [(functools.partial(sel_scores, g, k, p), functools.partial(sel_finish, g, k, p))
                for g in range(N_TILE_PAIRS) for k in range(_N_SEL_JOBS) for p in range(n_pairs)]
    b_before = [job for t in tiles if t % 2 == 0 for job in b_jobs[t]]
    b_after = [job for t in tiles if t % 2 == 1 for job in b_jobs[t]]
    _run_skewed(jobs + b_before + sel_jobs + b_after, ATTN_AHEAD)

    for t in tiles:
        z = z_refs[t][0]
        for p in range(n_pairs):
            a = acc_ref[t, p]
            o = out_ref[t, p] + (gate(t, p, 1) / jnp.maximum(a[HEAD_DIM_A:HEAD_DIM_A + 1], TINY)) * a[0:HEAD_DIM_A]
            for h in (2 * p, 2 * p + 1):
                oh = o[:, (h % 2) * TQ:(h % 2 + 1) * TQ].T
                y_refs[t][0, :, h * LANES:(h + 1) * LANES] = (
                    oh * z[:, h * LANES:(h + 1) * LANES].astype(F32)).astype(y_refs[t].dtype)


def _mask_tables():
    lane = np.arange(TQ)[None, :]
    row = np.arange(WIN_A + TQ)[:, None]
    wtab = []
    for v in range(WIN_A // TQ + 1):
        t0 = v * TQ
        dpos = (t0 + lane) - (max(t0 - WIN_A, 0) + row)
        wtab.append(np.where((dpos >= 0) & (dpos < WIN_A), 0.0, NEG_INF))
    row = np.arange(SEL_TK)[:, None]
    ctab = [np.zeros((SEL_TK, TQ))]
    for odd in range(SEL_TK // TQ):
        ctab.append(np.where(row <= odd * TQ + lane, 0.0, NEG_INF))
    row = np.arange(WIN_B + TQ)[:, None]
    btab = []
    for v in range(WIN_B // TQ + 1):
        t0 = v * TQ
        dpos = (t0 + lane) - (max(t0 - WIN_B, 0) + row)
        btab.append(np.where((dpos >= 0) & (dpos < WIN_B), 0.0, NEG_INF))
    return [jnp.asarray(np.stack(t), F32) for t in (wtab, ctab, btab)]


def _attention(slabs, kc, vc, ov, sinks):
    B, S, _ = slabs[0].shape
    sl = lambda name: slabs[_slab(name)]
    n_cp = S // CMP_STRIDE
    n_qt = S // TQ
    n_steps = n_qt // N_TILES
    assert n_cp == LANES and S // SEL_BLOCK == N_SELP and (n_qt // 2) // (SEL_TK // TQ) * 2 + 1 == _N_SEL_JOBS
    assert N_KV_A == N_KV_B and GROUP_A * HEAD_DIM_A == GROUP_B * HEAD_DIM_B and N_TILE_PAIRS in (1, 2)
    wq = GROUP_A * HEAD_DIM_A
    cb = lambda name: _col(name) // LANES
    kv_spec = lambda name: pl.BlockSpec((1, S, LANES), lambda b, k, i, s, c=cb(name): (b, 0, c + k))
    whole = lambda a: pl.BlockSpec(a.shape, lambda b, k, i, s: (0,) * a.ndim)

    def tile_specs(width, c, per_kv=True):
        return [pl.BlockSpec((1, TQ, width),
                             lambda b, k, i, s, t=t: (b, _query_tile(t, i, n_qt), c + (k if per_kv else 0)))
                for t in range(N_TILES)]

    part = [_query_tile(t, 0, n_qt) // n_steps for t in range(N_TILES)]
    assert sorted(part) == list(range(N_TILES))
    y_specs = [pl.BlockSpec((1, TQ, wq), lambda b, k, i, s, t=t: (b, _query_tile(t, i, n_qt) - part[t] * n_steps, k))
               for t in range(N_TILES)]
    y_sds = jax.ShapeDtypeStruct((B, S // N_TILES, WIDTH_A), BF16)
    wtab, ctab, btab = _mask_tables()
    grid_spec = pltpu.PrefetchScalarGridSpec(
        num_scalar_prefetch=1,
        grid=(B, N_KV_A, n_steps),
        in_specs=[
            *tile_specs(wq, _col("qa") // wq),
            kv_spec("ksa"), kv_spec("vsa"), kv_spec("kwa"), kv_spec("vwa"),
            pl.BlockSpec((1, n_cp, LANES), lambda b, k, i, s: (k * B + b, 0, 0)),
            pl.BlockSpec((1, n_cp, LANES), lambda b, k, i, s: (k * B + b, 0, 0)),
            *tile_specs(LANES, cb("ga"), per_kv=False),
            *tile_specs(wq, _col("za") // wq),
            whole(ov), whole(wtab), whole(ctab),
            *tile_specs(wq, _col("qb") // wq),
            kv_spec("kb2"), pl.BlockSpec((1, S, LANES), lambda b, k, i, s: (b, 0, cb("vb"))),
            *tile_specs(wq, _col("zb") // wq),
            whole(btab),
        ],
        out_specs=y_specs + y_specs,
        scratch_shapes=[
            pltpu.VMEM((S // LANES, _VROWS_A, LANES), BF16),
            pltpu.VMEM((S // LANES, _VROWS_A, LANES), BF16),
            pltpu.VMEM((S // LANES, _VROWS_B, LANES), BF16),
            pltpu.VMEM((N_TILES, GROUP_A // 2, PAIR, HEAD_DIM_A), BF16),
            pltpu.VMEM((N_TILES, N_SELP, SUBLANES, TQ), F32),
            pltpu.VMEM((N_TILES, GROUP_A // 2, _VROWS_A, PAIR), F32),
            pltpu.VMEM((N_TILES, GROUP_A // 2, HEAD_DIM_A, PAIR), F32),
        ],
    )
    per_tile = lambda name: [sl(name)] * N_TILES
    outs = pl.pallas_call(
        _attn_kernel,
        grid_spec=grid_spec,
        out_shape=[y_sds] * (2 * N_TILES),
        compiler_params=pltpu.CompilerParams(
            dimension_semantics=("arbitrary", "arbitrary", "arbitrary"), vmem_limit_bytes=VMEM_LIMIT),
        name="attn",
    )(sinks, *per_tile("qa"), sl("ksa"), sl("vsa"), sl("kwa"), sl("vwa"), kc, vc,
      *per_tile("ga"), *per_tile("za"), ov, wtab, ctab,
      *per_tile("qb"), sl("kb2"), sl("vb"), *per_tile("zb"), btab)
    in_order = lambda ys: [ys[part.index(n)] for n in range(N_TILES)]
    return in_order(outs[:N_TILES]), in_order(outs[N_TILES:])


def _out_proj_kernel(n_parts, *refs):
    ya_refs, yb_refs = refs[:n_parts], refs[n_parts:2 * n_parts]
    wa_ref, wb_ref, x_ref, g_ref, o_ref = refs[2 * n_parts:]
    part = pl.program_id(0) % n_parts

    for n in range(n_parts):
        @pl.when(part == n)
        def _(n=n):
            r = x_ref[...] + _dot(ya_refs[n][...], wa_ref[...]) + _dot(yb_refs[n][...], wb_ref[...])
            ms = jnp.mean(r * r, axis=-1, keepdims=True)
            o_ref[...] = (r * lax.rsqrt(ms + RMS_EPS)) * g_ref[...]


def _out_proj(ya_parts, yb_parts, w_out, x2, g, S):
    M = x2.shape[0]
    n_parts = len(ya_parts)
    tm = S // n_parts
    n_seq = M // S
    y_spec = lambda w, n: pl.BlockSpec(
        (tm, w), lambda i: (jnp.minimum((i + n_parts - 1 - n) // n_parts, n_seq - 1), 0))
    return pl.pallas_call(
        functools.partial(_out_proj_kernel, n_parts),
        grid=(M // tm,),
        in_specs=[
            *[y_spec(WIDTH_A, n) for n in range(n_parts)], *[y_spec(WIDTH_B, n) for n in range(n_parts)],
            pl.BlockSpec((WIDTH_A, D_MODEL), lambda i: (0, 0)),
            pl.BlockSpec((WIDTH_B, D_MODEL), lambda i: (WIDTH_A // WIDTH_B, 0)),
            pl.BlockSpec((tm, D_MODEL), lambda i: (i, 0)),
            pl.BlockSpec((1, D_MODEL), lambda i: (0, 0)),
        ],
        out_specs=pl.BlockSpec((tm, D_MODEL), lambda i: (i, 0)),
        out_shape=jax.ShapeDtypeStruct((M, D_MODEL), F32),
        compiler_params=pltpu.CompilerParams(
            dimension_semantics=("arbitrary",), vmem_limit_bytes=VMEM_LIMIT_OUT_PROJ),
        name="out_proj",
    )(*ya_parts, *yb_parts, w_out, w_out, x2, g)


def _overlap_matrix(n_cp):
    c_start = np.arange(n_cp) * CMP_STRIDE
    j_start = np.arange(N_SELP) * SEL_BLOCK
    ov = (c_start[None, :] < j_start[:, None] + SEL_BLOCK) & (c_start[None, :] + CMP_BLOCK > j_start[:, None])
    return jnp.asarray(ov, BF16)


def kernel(x, w_in, cmp_k_w1, cmp_k_w2, cmp_v_w1, cmp_v_w2, cmp_k_pos, cmp_v_pos, sinks, w_out, norm_g, final_g):
    B, S, D = x.shape
    assert D == D_MODEL and w_in.shape[0] == 1
    x2 = x.reshape(B * S, D)

    w_all = _prep_w_in(w_in)
    tabs = _rope_tables(S)
    slabs, strided = _in_proj(x2, norm_g[0].reshape(1, D), w_all, tabs, S)
    slabs = [p.reshape(B, S, IN_TN) for p in slabs]

    n_cp = S // CMP_STRIDE
    tk, tv = [strided[n].reshape(N_KV_A * B * n_cp, _HALF) for n in ("kca", "vca")]
    flat_pos = lambda p: jnp.broadcast_to(p.reshape(1, CMP_BLOCK * HEAD_DIM_A), (8, CMP_BLOCK * HEAD_DIM_A))
    kc, vc = _compress(tk, tv, cmp_k_w1[0].astype(BF16), cmp_v_w1[0].astype(BF16),
                       cmp_k_w2[0].astype(BF16), cmp_v_w2[0].astype(BF16),
                       flat_pos(cmp_k_pos[0]).astype(BF16), flat_pos(cmp_v_pos[0]).astype(BF16))
    kc, vc = [c.reshape(N_KV_A * B, n_cp, HEAD_DIM_A) for c in (kc, vc)]

    ya_parts, yb_parts = _attention(slabs, kc, vc, _overlap_matrix(n_cp), sinks[0])
    flat = lambda ys: [y.reshape(B * y.shape[1], y.shape[2]) for y in ys]
    out = _out_proj(flat(ya_parts), flat(yb_parts), w_out[0].astype(BF16), x2, final_g.reshape(1, D), S)
    return out.reshape(B, S, D)
```

```python
import functools
import math

import numpy as np
import jax
import jax.numpy as jnp
from jax import lax
from jax.experimental import pallas as pl
from jax.experimental.pallas import tpu as pltpu

F32 = jnp.float32
BF16 = jnp.bfloat16

D_MODEL = 2048
ROPE_THETA = 10000.0
RMS_EPS = 1e-6
NEG_INF = -1e30
TINY = 1e-30
LOG2E = math.log2(math.e)

WIDTH_A = 1024
HEAD_DIM_A = 128
N_KV_A = 2
GROUP_A = 4
KV_A = N_KV_A * HEAD_DIM_A
CMP_BLOCK = 32
CMP_STRIDE = 16
CMP_HIDDEN = 256
SEL_BLOCK = 64
SEL_TOPK = 16
WIN_A = 512
FORCE_SCORE = 1e4

WIDTH_B = 1024
HEAD_DIM_B = 64
N_KV_B = 2
GROUP_B = 8
KV_B = N_KV_B * HEAD_DIM_B
WIN_B = 128

IN_SIZES = (WIDTH_A, KV_A, KV_A, KV_A, KV_A, KV_A, KV_A, WIDTH_A, 3 * 8,
            WIDTH_B, KV_B, KV_B, WIDTH_B)

LANES = 128
SUBLANES = 8
VMEM_BYTES = 64 * 1024 * 1024
VMEM_LIMIT = VMEM_BYTES * 3 // 4
VMEM_LIMIT_OUT_PROJ = VMEM_BYTES * 7 // 8
CHUNK = 256

EP_NONE, EP_ROPE128, EP_ROPE64, EP_SILU, EP_SIGMOID, EP_ROPE128_Q, EP_ROPE64_Q = range(7)

_PROJ_LAYOUT = (
    ("qa", 8, EP_ROPE128_Q), ("kca", 2, EP_ROPE128), ("ksa", 2, EP_ROPE128), ("kwa", 2, EP_ROPE128),
    ("vca", 2, EP_NONE),
    ("vsa", 2, EP_NONE), ("vwa", 2, EP_NONE), ("za", 8, EP_SILU), ("kb2", 2, EP_ROPE64),
    ("ga", 1, EP_SIGMOID), ("vb", 1, EP_NONE),
    ("zb", 8, EP_SILU), ("qb", 8, EP_ROPE64_Q),
)
_UNIT_KINDS = tuple(k for _, n, k in _PROJ_LAYOUT for _ in range(n))
_UNIT_START = {}
_c = 0
for _name, _n, _k in _PROJ_LAYOUT:
    _UNIT_START[_name] = _c
    _c += _n
D_PROJ = _c * LANES
IN_TN = 2048
_IN_UPT = IN_TN // LANES
N_SLABS = D_PROJ // IN_TN


def _slab(name):
    return _UNIT_START[name] // _IN_UPT


def _col(name):
    return (_UNIT_START[name] % _IN_UPT) * LANES


def _dot(a, b):
    return jnp.dot(a, b, preferred_element_type=F32)


def _dot_nt(a, b):
    return lax.dot_general(a, b, (((1,), (1,)), ((), ())), preferred_element_type=F32)


IN_TM_NORM = 512
IN_TM = 1024
IN_SUB = 256

_TAB_GROUP = {EP_ROPE128: "r128", EP_ROPE128_Q: "r128q", EP_ROPE64: "r64", EP_ROPE64_Q: "r64q"}
_TAB_SIZE = {"r128": 2, "r128q": 2, "r64": 3, "r64q": 3}


def _slab_tab_groups(kinds):
    return tuple(dict.fromkeys(_TAB_GROUP[k] for k in kinds if k in _TAB_GROUP))


def _in_proj_kernel(kinds, with_norm, seq_len, strided_chunks, *refs):
    groups = _slab_tab_groups(kinds)
    refs = list(refs)
    if with_norm:
        x_ref, g_ref = refs.pop(0), refs.pop(0)
    else:
        h_ref = refs.pop(0)
    w_ref = refs.pop(0)
    tab_ref = refs.pop(0) if groups else None
    o_ref = refs.pop(0)
    if with_norm:
        h_ref = refs.pop(0)
        x = x_ref[...]
        ms = jnp.mean(x * x, axis=-1, keepdims=True)
        h_ref[...] = ((x * lax.rsqrt(ms + RMS_EPS)) * g_ref[...]).astype(BF16)
    t_refs = {c: refs.pop(0) for c in strided_chunks}
    if strided_chunks:
        stage_ref = refs.pop(0)
    tm = h_ref.shape[0]
    sub_rows = tm if with_norm else IN_SUB
    pos0 = (pl.program_id(0) % (seq_len // tm)) * tm
    base, n = {}, 0
    for grp in groups:
        base[grp] = n
        n += _TAB_SIZE[grp]

    def epilogue(kind, r, a):
        if kind == EP_NONE:
            return a
        if kind == EP_SILU:
            return a * jax.nn.sigmoid(a)
        if kind == EP_SIGMOID:
            return jax.nn.sigmoid(a)
        t = base[_TAB_GROUP[kind]]
        tab = lambda k: tab_ref[t + k, pl.ds(pl.multiple_of(pos0 + r * sub_rows, sub_rows), sub_rows), :]
        if kind in (EP_ROPE128, EP_ROPE128_Q):
            return a * tab(0) + pltpu.roll(a, 64, 1) * tab(1)
        return a * tab(0) + pltpu.roll(a, 96, 1) * tab(1) + pltpu.roll(a, 32, 1) * tab(2)

    n_grp = sub_rows // CMP_STRIDE

    subs = CHUNK // LANES

    def finish(r, c, acc):
        rows = slice(r * sub_rows, (r + 1) * sub_rows)
        for sub in range(subs):
            cols = slice(c * CHUNK + sub * LANES, c * CHUNK + (sub + 1) * LANES)
            val = epilogue(kinds[c * subs + sub], r, acc[:, sub * LANES:(sub + 1) * LANES])
            o_ref[rows, cols] = val.astype(o_ref.dtype)
            if c in t_refs:
                slot = strided_chunks.index(c) * subs + sub
                stage_ref[slot] = val
                for k in range(CMP_STRIDE):
                    t_refs[c][sub, r * n_grp:(r + 1) * n_grp, k * LANES:(k + 1) * LANES] = (
                        stage_ref[slot, pl.ds(k, n_grp, stride=CMP_STRIDE), :].astype(t_refs[c].dtype))

    jobs = []
    for r in range(tm // sub_rows):
        for c in range(len(kinds) // subs):
            jobs.append((lambda r=r, c=c: _dot_nt(h_ref[r * sub_rows:(r + 1) * sub_rows, :],
                                                  w_ref[c * CHUNK:(c + 1) * CHUNK, :]),
                         functools.partial(finish, r, c)))
    _run_skewed(jobs)


def _rope_tables(S):
    def cs(d):
        inv = ROPE_THETA ** (-jnp.arange(0, d, 2, dtype=F32) / d)
        ang = jnp.arange(S, dtype=F32)[:, None] * inv[None, :]
        return jnp.cos(ang), jnp.sin(ang)

    c128, s128 = cs(HEAD_DIM_A)
    c64, s64 = cs(HEAD_DIM_B)
    z32 = jnp.zeros_like(s64)
    t128 = [jnp.concatenate([c128, c128], axis=1), jnp.concatenate([-s128, s128], axis=1)]
    t64 = [jnp.concatenate([c64, c64, c64, c64], axis=1),
           jnp.concatenate([-s64, z32, -s64, z32], axis=1),
           jnp.concatenate([z32, s64, z32, s64], axis=1)]
    qa = HEAD_DIM_A ** -0.5 * LOG2E
    qb = HEAD_DIM_B ** -0.5 * LOG2E
    return {"r128": t128, "r64": t64, "r128q": [t * qa for t in t128], "r64q": [t * qb for t in t64]}


W_PREP_TR = 256


def _w_in_pieces():
    offs = np.cumsum((0,) + IN_SIZES)
    names = ("qa", "kca", "vca", "ksa", "vsa", "kwa", "vwa", "za", "ga", "qb", "kb", "vb", "zb")
    start = {n: int(offs[i]) for i, n in enumerate(names)}
    width = dict(zip(names, IN_SIZES))
    n_g = 3 * GROUP_A
    halves = lambda n: [(start[n] + i * HEAD_DIM_B, HEAD_DIM_B) for i in (0, 0, 1, 1)]
    pieces = {n: [(start[n], width[n])]
              for n in ("qa", "kca", "ksa", "kwa", "vca", "vsa", "vwa", "za", "zb", "qb", "vb")}
    pad = LANES // N_KV_A - n_g
    pieces["ga"] = [(start["ga"], n_g), (None, pad), (start["ga"] + n_g, n_g), (None, pad)]
    pieces["kb2"] = halves("kb")
    return [p for name, _, _ in _PROJ_LAYOUT for p in pieces[name]]


def _w_prep_kernel(w_ref, o_ref):
    row = 0
    parts, filled = [], 0
    for src, wd in _w_in_pieces():
        done = 0
        while done < wd:
            take = min(wd - done, LANES - filled)
            if src is None:
                parts.append(jnp.zeros((take, W_PREP_TR), F32))
            else:
                parts.append(w_ref[src + done:src + done + take, :])
            done += take
            filled += take
            if filled == LANES:
                blk = parts[0] if len(parts) == 1 else jnp.concatenate(parts, axis=0)
                o_ref[row:row + LANES, :] = blk.astype(o_ref.dtype)
                row += LANES
                parts, filled = [], 0
    assert row == D_PROJ and not parts


def _prep_w_in(w_in):
    wt = w_in[0].T
    d_in = wt.shape[0]
    return pl.pallas_call(
        _w_prep_kernel,
        grid=(D_MODEL // W_PREP_TR,),
        in_specs=[pl.BlockSpec((d_in, W_PREP_TR), lambda i: (0, i))],
        out_specs=pl.BlockSpec((D_PROJ, W_PREP_TR), lambda i: (0, i)),
        out_shape=jax.ShapeDtypeStruct((D_PROJ, D_MODEL), BF16),
        compiler_params=pltpu.CompilerParams(dimension_semantics=("arbitrary",)),
        name="w_prep",
    )(wt)


def _in_proj(x2, g, w_all, tabs, S):
    M = x2.shape[0]
    params = pltpu.CompilerParams(dimension_semantics=("arbitrary",), vmem_limit_bytes=VMEM_LIMIT)
    out_sds = jax.ShapeDtypeStruct((M, IN_TN), BF16)
    slabs, strided, h = [], {}, None
    for j in range(N_SLABS):
        kinds = _UNIT_KINDS[j * _IN_UPT:(j + 1) * _IN_UPT]
        with_norm = j == 0
        tm = IN_TM_NORM if with_norm else IN_TM
        groups = _slab_tab_groups(kinds)
        act_spec = pl.BlockSpec((tm, D_MODEL), lambda i: (i, 0))
        in_specs = [act_spec, pl.BlockSpec((1, D_MODEL), lambda i: (0, 0))] if with_norm else [act_spec]
        args = [x2, g] if with_norm else [h]
        in_specs.append(pl.BlockSpec((IN_TN, D_MODEL), lambda i, j=j: (j, 0)))
        args.append(w_all)
        if groups:
            tab = jnp.stack([t for grp in groups for t in tabs[grp]])
            in_specs.append(pl.BlockSpec(tab.shape, lambda i: (0, 0, 0)))
            args.append(tab)
        out_specs = [pl.BlockSpec((tm, IN_TN), lambda i: (i, 0))]
        out_shape = [out_sds]
        if with_norm:
            out_specs.append(act_spec)
            out_shape.append(jax.ShapeDtypeStruct((M, D_MODEL), BF16))
        strided_names = [name for name in ("kca", "vca") if _slab(name) == j]
        strided_chunks = tuple(_col(name) // CHUNK for name in strided_names)
        scratch = []
        for _ in strided_chunks:
            out_specs.append(pl.BlockSpec((N_KV_A, tm // CMP_STRIDE, _HALF), lambda i: (0, i, 0)))
            out_shape.append(jax.ShapeDtypeStruct((N_KV_A, M // CMP_STRIDE, _HALF), BF16))
        if strided_chunks:
            scratch.append(pltpu.VMEM((len(strided_chunks) * N_KV_A, tm if with_norm else IN_SUB, HEAD_DIM_A), F32))
        res = list(pl.pallas_call(
            functools.partial(_in_proj_kernel, kinds, with_norm, S, strided_chunks),
            grid=(M // tm,),
            in_specs=in_specs,
            out_specs=out_specs,
            out_shape=out_shape,
            scratch_shapes=scratch,
            compiler_params=params,
            name=f"in_proj_{j}",
        )(*args))
        slabs.append(res.pop(0))
        if with_norm:
            h = res.pop(0)
        for name in strided_names:
            strided[name] = res.pop(0)
    return slabs, strided


CMP_TM = 256
_HALF = CMP_STRIDE * HEAD_DIM_A


def _compress_kernel(tk_ref, tv_ref, w1k_ref, w1v_ref, w2k_ref, w2v_ref, posk_ref, posv_ref, kc_ref, vc_ref,
                     pb_ref):
    @pl.when(pl.program_id(0) == 0)
    def _():
        pb_ref[0] = _dot(posk_ref[...], w1k_ref[...])
        pb_ref[1] = _dot(posv_ref[...], w1v_ref[...])

    for n, (t_ref, w1_ref, w2_ref, o_ref) in enumerate(((tk_ref, w1k_ref, w2k_ref, kc_ref),
                                                        (tv_ref, w1v_ref, w2v_ref, vc_ref))):
        t = t_ref[...]
        u = _dot(t, w1_ref[0:_HALF, :])
        v = _dot(t, w1_ref[_HALF:2 * _HALF, :])
        pre = u + pltpu.roll(v, CMP_TM - 1, 0) + pb_ref[n, 0:1]
        hid = pre * jax.nn.sigmoid(pre)
        out = _dot(hid.astype(BF16), w2_ref[...])
        row = lax.broadcasted_iota(jnp.int32, out.shape, 0)
        n_cp = LANES
        out = jnp.where((row & (n_cp - 1)) == n_cp - 1, 0.0, out)
        o_ref[...] = out.astype(o_ref.dtype)


def _compress(tk, tv, w1k, w1v, w2k, w2v, posk, posv):
    R = tk.shape[0]
    rows = pl.BlockSpec((CMP_TM, _HALF), lambda i: (i, 0))
    whole = lambda a: pl.BlockSpec(a.shape, lambda i: (0, 0))
    o_spec = pl.BlockSpec((CMP_TM, HEAD_DIM_A), lambda i: (i, 0))
    o_sds = jax.ShapeDtypeStruct((R, HEAD_DIM_A), BF16)
    return pl.pallas_call(
        _compress_kernel,
        grid=(R // CMP_TM,),
        in_specs=[rows, rows, whole(w1k), whole(w1v), whole(w2k), whole(w2v), whole(posk), whole(posv)],
        out_specs=(o_spec, o_spec),
        out_shape=(o_sds, o_sds),
        scratch_shapes=[pltpu.VMEM((2, SUBLANES, CMP_HIDDEN), F32)],
        compiler_params=pltpu.CompilerParams(dimension_semantics=("arbitrary",)),
        name="compress",
    )(tk, tv, w1k, w1v, w2k, w2v, posk, posv)


TQ = 128
ATTN_AHEAD = 12
ONES_ROWS = 16


def _transpose_values(v_ref, vt_ref, n_rows, second_half=None):
    n_blk = vt_ref.shape[0]
    for blk in range(n_blk):
        vt = v_ref[0, blk * LANES:(blk + 1) * LANES, :].astype(F32).T
        if second_half is not None:
            vt = jnp.where(second_half == 0, vt[0:n_rows], vt[n_rows:2 * n_rows])
        vt_ref[blk, 0:n_rows, :] = vt[0:n_rows].astype(vt_ref.dtype)
        vt_ref[blk, n_rows:n_rows + ONES_ROWS, :] = jnp.ones((ONES_ROWS, LANES), vt_ref.dtype)


def _run_skewed(jobs, ahead=1):
    pending = [job[0]() for job in jobs[:ahead]]
    for i, (_, finish) in enumerate(jobs):
        if i + ahead < len(jobs):
            pending.append(jobs[i + ahead][0]())
        finish(pending[i])
        pending[i] = None


def _softmax_step(s, m, acc, vt):
    m_new = jnp.maximum(m, jnp.max(s, axis=0, keepdims=True))
    p = jnp.exp2(s - m_new).astype(BF16)
    return m_new, jnp.exp2(m - m_new) * acc + _dot(vt, p)


SEL_TK = 256
N_SELP = 32
PAIR = 2 * TQ
N_TILE_PAIRS = 2
N_TILES = 2 * N_TILE_PAIRS
_VROWS_A = HEAD_DIM_A + ONES_ROWS
_VROWS_B = HEAD_DIM_B + ONES_ROWS
_WIN_TILES = (256, 256, 128)
_N_SEL_JOBS = 9


def _query_tile(t, i, n_qt):
    low = i if t // 2 == 0 else n_qt // 2 - 1 - i
    return low if t % 2 == 0 else n_qt - 1 - low


def _attn_kernel(*refs):
    refs = list(refs)
    take = lambda n: [refs.pop(0) for _ in range(n)]
    sink_ref, = take(1)
    q_refs = take(N_TILES)
    ksel_ref, vsel_ref, kwin_ref, vwin_ref, kc_ref, vc_ref = take(6)
    g_refs, z_refs = take(N_TILES), take(N_TILES)
    ov_ref, wtab_ref, ctab_ref = take(3)
    qb_refs = take(N_TILES)
    kb_ref, vb_ref = take(2)
    zb_refs = take(N_TILES)
    btab_ref, = take(1)
    y_refs, yb_refs = take(N_TILES), take(N_TILES)
    vselt_ref, vwint_ref, vbt_ref, qs_ref, bias_ref, acc_ref, out_ref = refs

    kv = pl.program_id(1)
    i = pl.program_id(2)
    n_qt = N_TILES * pl.num_programs(2)
    n_pairs = GROUP_A // 2
    tiles = range(N_TILES)
    qts = [_query_tile(t, i, n_qt) for t in tiles]
    t0s = [qt * TQ for qt in qts]

    @pl.when(i == 0)
    def _():
        _transpose_values(vsel_ref, vselt_ref, HEAD_DIM_A)
        _transpose_values(vwin_ref, vwint_ref, HEAD_DIM_A)
        _transpose_values(vb_ref, vbt_ref, HEAD_DIM_B, kv)

    lane = lax.broadcasted_iota(jnp.int32, (1, TQ), 1)
    lane2 = jnp.concatenate([lane, lane], axis=1)
    qpair, gts = [], []
    for t in tiles:
        q = q_refs[t][0]
        pairs = [jnp.concatenate([q[:, h * LANES:(h + 1) * LANES] for h in (2 * p, 2 * p + 1)], axis=0)
                 for p in range(n_pairs)]
        for p in range(n_pairs):
            qs_ref[t, p] = pairs[p]
        qpair.append(pairs)
        gt = g_refs[t][0].astype(F32).T
        gts.append(jnp.where(kv == 0, gt[0:LANES // 2], gt[LANES // 2:LANES]))

    def gate(t, p, branch):
        return jnp.concatenate(
            [gts[t][3 * h + branch:3 * h + branch + 1] for h in (2 * p, 2 * p + 1)], axis=1)

    kc = kc_ref[0]
    vct = vc_ref[0].astype(F32).T.astype(BF16)
    cend = lax.broadcasted_iota(jnp.int32, (LANES, PAIR), 0) * CMP_STRIDE + (CMP_BLOCK - 1)
    psums = [[] for _ in tiles]

    def cmp_scores(t, p):
        return jnp.where(cend <= t0s[t] + lane2, _dot_nt(kc, qpair[t][p]), NEG_INF)

    def cmp_finish(t, p, s):
        m = jnp.max(s, axis=0, keepdims=True)
        e = jnp.where(cend <= t0s[t] + lane2, jnp.exp2(s - m), 0.0)
        pc = e / jnp.maximum(jnp.sum(e, axis=0, keepdims=True), TINY)
        psums[t].append(pc[:, 0:TQ] + pc[:, TQ:PAIR])
        out_ref[t, p] = gate(t, p, 0) * _dot(vct, pc.astype(BF16))

    ksts = [pl.multiple_of(jnp.maximum(t0 - WIN_A, 0), TQ) for t0 in t0s]
    n_win = sum(_WIN_TILES)

    win_var = [jnp.minimum(qt, WIN_A // TQ) for qt in qts]

    def win_scores(t, p):
        out, off = [], 0
        for tk in _WIN_TILES:
            kt = kwin_ref[0, pl.ds(pl.multiple_of(ksts[t] + off, LANES), tk), :]
            bias = wtab_ref[win_var[t], off:off + tk, :]
            out.append(_dot_nt(kt, qpair[t][p]) + jnp.concatenate([bias, bias], axis=1))
            off += tk
        return out

    def win_finish(t, p, ss):
        m = functools.reduce(jnp.maximum, [jnp.max(s, axis=0, keepdims=True) for s in ss])
        pw = jnp.concatenate([jnp.exp2(s - m).astype(BF16) for s in ss], axis=0)
        vt = jnp.concatenate([vwint_ref[ksts[t] // LANES + b] for b in range(n_win // LANES)], axis=1)
        a = _dot(vt, pw)
        out_ref[t, p] = out_ref[t, p] + (
            gate(t, p, 2) / jnp.maximum(a[HEAD_DIM_A:HEAD_DIM_A + 1], TINY)) * a[0:HEAD_DIM_A]

    jobs = [(functools.partial(cmp_scores, t, p), functools.partial(cmp_finish, t, p))
            for t in tiles for p in range(n_pairs)]
    jobs += [(functools.partial(win_scores, t, p), functools.partial(win_finish, t, p))
             for t in tiles for p in range(n_pairs)]

    nb_keys = WIN_B + TQ
    left = lax.broadcasted_iota(jnp.int32, (TQ, LANES), 1) < HEAD_DIM_B
    zero = jnp.zeros((TQ, LANES), qb_refs[0].dtype)
    kbs = [pl.multiple_of(jnp.maximum(t0 - WIN_B, 0), TQ) for t0 in t0s]
    b_var = [jnp.minimum(qt, WIN_B // TQ) for qt in qts]

    def b_sink(c):
        return jnp.concatenate(
            [jnp.full((1, TQ), sink_ref[kv * GROUP_B + 2 * c + h] * LOG2E, F32) for h in range(2)], axis=1)

    def b_scores(t, c):
        pair = qb_refs[t][0, :, c * LANES:(c + 1) * LANES]
        qp = jnp.concatenate([jnp.where(left, pair, zero), jnp.where(left, zero, pair)], axis=0)
        bias = btab_ref[b_var[t]]
        return _dot_nt(kb_ref[0, pl.ds(kbs[t], nb_keys), :], qp) + jnp.concatenate([bias, bias], axis=1)

    def b_finish(t, c, s):
        vt = jnp.concatenate([vbt_ref[kbs[t] // LANES + b] for b in range(nb_keys // LANES)], axis=1)
        sink = b_sink(c)
        m = jnp.maximum(jnp.max(s, axis=0, keepdims=True), sink)
        p = jnp.exp2(s - m).astype(BF16)
        a = _dot(vt, p)
        den = jnp.maximum(a[HEAD_DIM_B:HEAD_DIM_B + 1] + jnp.exp2(sink - m), TINY)
        o = a[0:HEAD_DIM_B] / den
        o2 = jnp.concatenate([o[:, 0:TQ], o[:, TQ:PAIR]], axis=0)
        z = zb_refs[t][0, :, c * LANES:(c + 1) * LANES]
        yb_refs[t][0, :, c * LANES:(c + 1) * LANES] = (o2.T * z.astype(F32)).astype(yb_refs[t].dtype)

    b_jobs = [[(functools.partial(b_scores, t, c), functools.partial(b_finish, t, c))
               for c in range(GROUP_B // 2)] for t in tiles]

    def select_blocks(t):
        ov = ov_ref[...]
        jidx = lax.broadcasted_iota(jnp.int32, (N_SELP, TQ), 0)
        psum = psums[t][0] + psums[t][1]
        p_hi = psum.astype(BF16)
        r1 = psum - p_hi.astype(F32)
        p_mid = r1.astype(BF16)
        p_lo = (r1 - p_mid.astype(F32)).astype(BF16)
        psel = _dot(ov, p_hi) + _dot(ov, p_mid) + _dot(ov, p_lo)
        tq = t0s[t] + lane
        cur = tq // SEL_BLOCK
        forced = (jidx == 0) | (jidx == cur) | (jidx == cur - 1)
        score = jnp.where(forced, FORCE_SCORE, jnp.where(jidx * SEL_BLOCK <= tq, psel, -1.0))
        rank = jnp.zeros(psel.shape, F32)
        for r in range(N_SELP):
            row = jnp.broadcast_to(score[r:r + 1, :], score.shape)
            rank = rank + jnp.where(jidx > r, jnp.where(row >= score, 1.0, 0.0), jnp.where(row > score, 1.0, 0.0))
        bias_t = jnp.where(rank < SEL_TOPK, 0.0, NEG_INF)
        for j in range(N_SELP):
            bias_ref[t, j] = jnp.broadcast_to(bias_t[j:j + 1, :], (SUBLANES, TQ))

    blk_per_tile = SEL_TK // SEL_BLOCK
    rep = SEL_BLOCK // SUBLANES
    acc_ref[...] = jnp.zeros(acc_ref.shape, F32)
    ms = [[jnp.full((1, PAIR), NEG_INF, F32) for _ in range(n_pairs)] for _ in tiles]
    ctx = {}

    def sel_scores(g, k, p):
        if p == 0:
            if g == 0 and k == 0:
                for t in tiles:
                    select_blocks(t)
            n_lo = qts[2 * g] // (SEL_TK // TQ) + 1
            hi = k >= n_lo
            t = 2 * g + hi.astype(jnp.int32)
            j = jnp.where(hi, k - n_lo, k)
            ks = pl.multiple_of(j * SEL_TK, SEL_TK)
            bt = jnp.concatenate(
                [bias_ref[t, j * blk_per_tile + b] for b in range(blk_per_tile) for _ in range(rep)], axis=0)
            qt = jnp.where(hi, qts[2 * g + 1], qts[2 * g])
            diag = j == qt // (SEL_TK // TQ)
            bt = bt + ctab_ref[jnp.where(diag, 1 + qt % (SEL_TK // TQ), 0)]
            bt = jnp.concatenate([bt, bt], axis=1)
            ctx[g, k] = (hi, t, j, ks, bt)
        hi, t, j, ks, bt = ctx[g, k]
        return _dot_nt(ksel_ref[0, pl.ds(ks, SEL_TK), :], qs_ref[t, p]) + bt

    def sel_finish(g, k, p, s):
        hi, t, j, ks, bt = ctx[g, k]
        lo_t, hi_t = 2 * g, 2 * g + 1
        vt = jnp.concatenate([vselt_ref[j * (SEL_TK // LANES) + b] for b in range(SEL_TK // LANES)], axis=1)
        m_new, acc_new = _softmax_step(s, jnp.where(hi, ms[hi_t][p], ms[lo_t][p]), acc_ref[t, p], vt)
        acc_ref[t, p] = acc_new
        ms[lo_t][p] = jnp.where(hi, ms[lo_t][p], m_new)
        ms[hi_t][p] = jnp.where(hi, m_new, ms[hi_t][p])

    sel_jobs = [(functools.partial(sel_scores, g, k, p), functools.partial(sel_finish, g, k, p))
                for g in range(N_TILE_PAIRS) for k in range(_N_SEL_JOBS) for p in range(n_pairs)]
    b_before = [job for t in tiles if t % 2 == 0 for job in b_jobs[t]]
    b_after = [job for t in tiles if t % 2 == 1 for job in b_jobs[t]]
    _run_skewed(jobs + b_before + sel_jobs + b_after, ATTN_AHEAD)

    for t in tiles:
        z = z_refs[t][0]
        for p in range(n_pairs):
            a = acc_ref[t, p]
            o = out_ref[t, p] + (gate(t, p, 1) / jnp.maximum(a[HEAD_DIM_A:HEAD_DIM_A + 1], TINY)) * a[0:HEAD_DIM_A]
            for h in (2 * p, 2 * p + 1):
                oh = o[:, (h % 2) * TQ:(h % 2 + 1) * TQ].T
                y_refs[t][0, :, h * LANES:(h + 1) * LANES] = (
                    oh * z[:, h * LANES:(h + 1) * LANES].astype(F32)).astype(y_refs[t].dtype)


def _mask_tables():
    lane = np.arange(TQ)[None, :]
    row = np.arange(WIN_A + TQ)[:, None]
    wtab = []
    for v in range(WIN_A // TQ + 1):
        t0 = v * TQ
        dpos = (t0 + lane) - (max(t0 - WIN_A, 0) + row)
        wtab.append(np.where((dpos >= 0) & (dpos < WIN_A), 0.0, NEG_INF))
    row = np.arange(SEL_TK)[:, None]
    ctab = [np.zeros((SEL_TK, TQ))]
    for odd in range(SEL_TK // TQ):
        ctab.append(np.where(row <= odd * TQ + lane, 0.0, NEG_INF))
    row = np.arange(WIN_B + TQ)[:, None]
    btab = []
    for v in range(WIN_B // TQ + 1):
        t0 = v * TQ
        dpos = (t0 + lane) - (max(t0 - WIN_B, 0) + row)
        btab.append(np.where((dpos >= 0) & (dpos < WIN_B), 0.0, NEG_INF))
    return [jnp.asarray(np.stack(t), F32) for t in (wtab, ctab, btab)]


def _attention(slabs, kc, vc, ov, sinks):
    B, S, _ = slabs[0].shape
    sl = lambda name: slabs[_slab(name)]
    n_cp = S // CMP_STRIDE
    n_qt = S // TQ
    n_steps = n_qt // N_TILES
    assert n_cp == LANES and S // SEL_BLOCK == N_SELP and (n_qt // 2) // (SEL_TK // TQ) * 2 + 1 == _N_SEL_JOBS
    assert N_KV_A == N_KV_B and GROUP_A * HEAD_DIM_A == GROUP_B * HEAD_DIM_B and N_TILE_PAIRS in (1, 2)
    wq = GROUP_A * HEAD_DIM_A
    cb = lambda name: _col(name) // LANES
    kv_spec = lambda name: pl.BlockSpec((1, S, LANES), lambda b, k, i, s, c=cb(name): (b, 0, c + k))
    whole = lambda a: pl.BlockSpec(a.shape, lambda b, k, i, s: (0,) * a.ndim)

    def tile_specs(width, c, per_kv=True):
        return [pl.BlockSpec((1, TQ, width),
                             lambda b, k, i, s, t=t: (b, _query_tile(t, i, n_qt), c + (k if per_kv else 0)))
                for t in range(N_TILES)]

    part = [_query_tile(t, 0, n_qt) // n_steps for t in range(N_TILES)]
    assert sorted(part) == list(range(N_TILES))
    y_specs = [pl.BlockSpec((1, TQ, wq), lambda b, k, i, s, t=t: (b, _query_tile(t, i, n_qt) - part[t] * n_steps, k))
               for t in range(N_TILES)]
    y_sds = jax.ShapeDtypeStruct((B, S // N_TILES, WIDTH_A), BF16)
    wtab, ctab, btab = _mask_tables()
    grid_spec = pltpu.PrefetchScalarGridSpec(
        num_scalar_prefetch=1,
        grid=(B, N_KV_A, n_steps),
        in_specs=[
            *tile_specs(wq, _col("qa") // wq),
            kv_spec("ksa"), kv_spec("vsa"), kv_spec("kwa"), kv_spec("vwa"),
            pl.BlockSpec((1, n_cp, LANES), lambda b, k, i, s: (k * B + b, 0, 0)),
            pl.BlockSpec((1, n_cp, LANES), lambda b, k, i, s: (k * B + b, 0, 0)),
            *tile_specs(LANES, cb("ga"), per_kv=False),
            *tile_specs(wq, _col("za") // wq),
            whole(ov), whole(wtab), whole(ctab),
            *tile_specs(wq, _col("qb") // wq),
            kv_spec("kb2"), pl.BlockSpec((1, S, LANES), lambda b, k, i, s: (b, 0, cb("vb"))),
            *tile_specs(wq, _col("zb") // wq),
            whole(btab),
        ],
        out_specs=y_specs + y_specs,
        scratch_shapes=[
            pltpu.VMEM((S // LANES, _VROWS_A, LANES), BF16),
            pltpu.VMEM((S // LANES, _VROWS_A, LANES), BF16),
            pltpu.VMEM((S // LANES, _VROWS_B, LANES), BF16),
            pltpu.VMEM((N_TILES, GROUP_A // 2, PAIR, HEAD_DIM_A), BF16),
            pltpu.VMEM((N_TILES, N_SELP, SUBLANES, TQ), F32),
            pltpu.VMEM((N_TILES, GROUP_A // 2, _VROWS_A, PAIR), F32),
            pltpu.VMEM((N_TILES, GROUP_A // 2, HEAD_DIM_A, PAIR), F32),
        ],
    )
    per_tile = lambda name: [sl(name)] * N_TILES
    outs = pl.pallas_call(
        _attn_kernel,
        grid_spec=grid_spec,
        out_shape=[y_sds] * (2 * N_TILES),
        compiler_params=pltpu.CompilerParams(
            dimension_semantics=("arbitrary", "arbitrary", "arbitrary"), vmem_limit_bytes=VMEM_LIMIT),
        name="attn",
    )(sinks, *per_tile("qa"), sl("ksa"), sl("vsa"), sl("kwa"), sl("vwa"), kc, vc,
      *per_tile("ga"), *per_tile("za"), ov, wtab, ctab,
      *per_tile("qb"), sl("kb2"), sl("vb"), *per_tile("zb"), btab)
    in_order = lambda ys: [ys[part.index(n)] for n in range(N_TILES)]
    return in_order(outs[:N_TILES]), in_order(outs[N_TILES:])


def _out_proj_kernel(n_parts, *refs):
    ya_refs, yb_refs = refs[:n_parts], refs[n_parts:2 * n_parts]
    wa_ref, wb_ref, x_ref, g_ref, o_ref = refs[2 * n_parts:]
    part = pl.program_id(0) % n_parts

    for n in range(n_parts):
        @pl.when(part == n)
        def _(n=n):
            r = x_ref[...] + _dot(ya_refs[n][...], wa_ref[...]) + _dot(yb_refs[n][...], wb_ref[...])
            ms = jnp.mean(r * r, axis=-1, keepdims=True)
            o_ref[...] = (r * lax.rsqrt(ms + RMS_EPS)) * g_ref[...]


def _out_proj(ya_parts, yb_parts, w_out, x2, g, S):
    M = x2.shape[0]
    n_parts = len(ya_parts)
    tm = S // n_parts
    n_seq = M // S
    y_spec = lambda w, n: pl.BlockSpec(
        (tm, w), lambda i: (jnp.minimum((i + n_parts - 1 - n) // n_parts, n_seq - 1), 0))
    return pl.pallas_call(
        functools.partial(_out_proj_kernel, n_parts),
        grid=(M // tm,),
        in_specs=[
            *[y_spec(WIDTH_A, n) for n in range(n_parts)], *[y_spec(WIDTH_B, n) for n in range(n_parts)],
            pl.BlockSpec((WIDTH_A, D_MODEL), lambda i: (0, 0)),
            pl.BlockSpec((WIDTH_B, D_MODEL), lambda i: (WIDTH_A // WIDTH_B, 0)),
            pl.BlockSpec((tm, D_MODEL), lambda i: (i, 0)),
            pl.BlockSpec((1, D_MODEL), lambda i: (0, 0)),
        ],
        out_specs=pl.BlockSpec((tm, D_MODEL), lambda i: (i, 0)),
        out_shape=jax.ShapeDtypeStruct((M, D_MODEL), F32),
        compiler_params=pltpu.CompilerParams(
            dimension_semantics=("arbitrary",), vmem_limit_bytes=VMEM_LIMIT_OUT_PROJ),
        name="out_proj",
    )(*ya_parts, *yb_parts, w_out, w_out, x2, g)


def _overlap_matrix(n_cp):
    c_start = np.arange(n_cp) * CMP_STRIDE
    j_start = np.arange(N_SELP) * SEL_BLOCK
    ov = (c_start[None, :] < j_start[:, None] + SEL_BLOCK) & (c_start[None, :] + CMP_BLOCK > j_start[:, None])
    return jnp.asarray(ov, BF16)


def kernel(x, w_in, cmp_k_w1, cmp_k_w2, cmp_v_w1, cmp_v_w2, cmp_k_pos, cmp_v_pos, sinks, w_out, norm_g, final_g):
    B, S, D = x.shape
    assert D == D_MODEL and w_in.shape[0] == 1
    x2 = x.reshape(B * S, D)

    w_all = _prep_w_in(w_in)
    tabs = _rope_tables(S)
    slabs, strided = _in_proj(x2, norm_g[0].reshape(1, D), w_all, tabs, S)
    slabs = [p.reshape(B, S, IN_TN) for p in slabs]

    n_cp = S // CMP_STRIDE
    tk, tv = [strided[n].reshape(N_KV_A * B * n_cp, _HALF) for n in ("kca", "vca")]
    flat_pos = lambda p: jnp.broadcast_to(p.reshape(1, CMP_BLOCK * HEAD_DIM_A), (8, CMP_BLOCK * HEAD_DIM_A))
    kc, vc = _compress(tk, tv, cmp_k_w1[0].astype(BF16), cmp_v_w1[0].astype(BF16),
                       cmp_k_w2[0].astype(BF16), cmp_v_w2[0].astype(BF16),
                       flat_pos(cmp_k_pos[0]).astype(BF16), flat_pos(cmp_v_pos[0]).astype(BF16))
    kc, vc = [c.reshape(N_KV_A * B, n_cp, HEAD_DIM_A) for c in (kc, vc)]

    ya_parts, yb_parts = _attention(slabs, kc, vc, _overlap_matrix(n_cp), sinks[0])
    flat = lambda ys: [y.reshape(B * y.shape[1], y.shape[2]) for y in ys]
    out = _out_proj(flat(ya_parts), flat(yb_parts), w_out[0].astype(BF16), x2, final_g.reshape(1, D), S)
    return out.reshape(B, S, D)
```

```python
import functools
import math

import numpy as np
import jax
import jax.numpy as jnp
from jax import lax
from jax.experimental import pallas as pl
from jax.experimental.pallas import tpu as pltpu

F32 = jnp.float32
BF16 = jnp.bfloat16

D_MODEL = 2048
ROPE_THETA = 10000.0
RMS_EPS = 1e-6
NEG_INF = -1e30
TINY = 1e-30
LOG2E = math.log2(math.e)

WIDTH_A = 1024
HEAD_DIM_A = 128
N_KV_A = 2
GROUP_A = 4
KV_A = N_KV_A * HEAD_DIM_A
CMP_BLOCK = 32
CMP_STRIDE = 16
CMP_HIDDEN = 256
SEL_BLOCK = 64
SEL_TOPK = 16
WIN_A = 512
FORCE_SCORE = 1e4

WIDTH_B = 1024
HEAD_DIM_B = 64
N_KV_B = 2
GROUP_B = 8
KV_B = N_KV_B * HEAD_DIM_B
WIN_B = 128

IN_SIZES = (WIDTH_A, KV_A, KV_A, KV_A, KV_A, KV_A, KV_A, WIDTH_A, 3 * 8,
            WIDTH_B, KV_B, KV_B, WIDTH_B)

LANES = 128
SUBLANES = 8
VMEM_BYTES = 64 * 1024 * 1024
VMEM_LIMIT = VMEM_BYTES * 3 // 4
VMEM_LIMIT_OUT_PROJ = VMEM_BYTES * 7 // 8
CHUNK = 256

EP_NONE, EP_ROPE128, EP_ROPE64, EP_SILU, EP_SIGMOID, EP_ROPE128_Q, EP_ROPE64_Q = range(7)

_PROJ_LAYOUT = (
    ("qa", 8, EP_ROPE128_Q), ("kca", 2, EP_ROPE128), ("ksa", 2, EP_ROPE128), ("kwa", 2, EP_ROPE128),
    ("vca", 2, EP_NONE),
    ("vsa", 2, EP_NONE), ("vwa", 2, EP_NONE), ("za", 8, EP_SILU), ("kb2", 2, EP_ROPE64),
    ("ga", 1, EP_SIGMOID), ("vb", 1, EP_NONE),
    ("zb", 8, EP_SILU), ("qb", 8, EP_ROPE64_Q),
)
_UNIT_KINDS = tuple(k for _, n, k in _PROJ_LAYOUT for _ in range(n))
_UNIT_START = {}
_c = 0
for _name, _n, _k in _PROJ_LAYOUT:
    _UNIT_START[_name] = _c
    _c += _n
D_PROJ = _c * LANES
IN_TN = 2048
_IN_UPT = IN_TN // LANES
N_SLABS = D_PROJ // IN_TN


def _slab(name):
    return _UNIT_START[name] // _IN_UPT


def _col(name):
    return (_UNIT_START[name] % _IN_UPT) * LANES


def _dot(a, b):
    return jnp.dot(a, b, preferred_element_type=F32)


def _dot_nt(a, b):
    return lax.dot_general(a, b, (((1,), (1,)), ((), ())), preferred_element_type=F32)


IN_TM_NORM = 1024
IN_TM = 1024
IN_SUB = 256
IN_SUB_NORM = 512

_TAB_GROUP = {EP_ROPE128: "r128", EP_ROPE128_Q: "r128q", EP_ROPE64: "r64", EP_ROPE64_Q: "r64q"}
_TAB_SIZE = {"r128": 2, "r128q": 2, "r64": 3, "r64q": 3}


def _slab_tab_groups(kinds):
    return tuple(dict.fromkeys(_TAB_GROUP[k] for k in kinds if k in _TAB_GROUP))


def _in_proj_kernel(kinds, with_norm, seq_len, strided_chunks, *refs):
    groups = _slab_tab_groups(kinds)
    refs = list(refs)
    if with_norm:
        x_ref, g_ref = refs.pop(0), refs.pop(0)
    else:
        h_ref = refs.pop(0)
    w_ref = refs.pop(0)
    tab_ref = refs.pop(0) if groups else None
    o_ref = refs.pop(0)
    if with_norm:
        h_ref = refs.pop(0)
    t_refs = {c: refs.pop(0) for c in strided_chunks}
    if strided_chunks:
        stage_ref = refs.pop(0)
    tm = h_ref.shape[0]
    sub_rows = IN_SUB_NORM if with_norm else IN_SUB

    def normalise(r):
        rows = slice(r * sub_rows, (r + 1) * sub_rows)
        x = x_ref[rows, :]
        ms = jnp.mean(x * x, axis=-1, keepdims=True)
        h_ref[rows, :] = ((x * lax.rsqrt(ms + RMS_EPS)) * g_ref[...]).astype(BF16)
    pos0 = (pl.program_id(0) % (seq_len // tm)) * tm
    base, n = {}, 0
    for grp in groups:
        base[grp] = n
        n += _TAB_SIZE[grp]

    def epilogue(kind, r, a):
        if kind == EP_NONE:
            return a
        if kind == EP_SILU:
            return a * jax.nn.sigmoid(a)
        if kind == EP_SIGMOID:
            return jax.nn.sigmoid(a)
        t = base[_TAB_GROUP[kind]]
        tab = lambda k: tab_ref[t + k, pl.ds(pl.multiple_of(pos0 + r * sub_rows, sub_rows), sub_rows), :]
        if kind in (EP_ROPE128, EP_ROPE128_Q):
            return a * tab(0) + pltpu.roll(a, 64, 1) * tab(1)
        return a * tab(0) + pltpu.roll(a, 96, 1) * tab(1) + pltpu.roll(a, 32, 1) * tab(2)

    n_grp = sub_rows // CMP_STRIDE

    subs = CHUNK // LANES

    def finish(r, c, acc):
        rows = slice(r * sub_rows, (r + 1) * sub_rows)
        for sub in range(subs):
            cols = slice(c * CHUNK + sub * LANES, c * CHUNK + (sub + 1) * LANES)
            val = epilogue(kinds[c * subs + sub], r, acc[:, sub * LANES:(sub + 1) * LANES])
            o_ref[rows, cols] = val.astype(o_ref.dtype)
            if c in t_refs:
                slot = strided_chunks.index(c) * subs + sub
                stage_ref[slot] = val
                for k in range(CMP_STRIDE):
                    t_refs[c][sub, r * n_grp:(r + 1) * n_grp, k * LANES:(k + 1) * LANES] = (
                        stage_ref[slot, pl.ds(k, n_grp, stride=CMP_STRIDE), :].astype(t_refs[c].dtype))

    def project(r, c):
        if with_norm and c == 0:
            normalise(r)
        return _dot_nt(h_ref[r * sub_rows:(r + 1) * sub_rows, :], w_ref[c * CHUNK:(c + 1) * CHUNK, :])

    jobs = [(functools.partial(project, r, c), functools.partial(finish, r, c))
            for r in range(tm // sub_rows) for c in range(len(kinds) // subs)]
    _run_skewed(jobs)


def _rope_tables(S):
    def cs(d):
        inv = ROPE_THETA ** (-np.arange(0, d, 2, dtype=np.float64) / d)
        ang = np.arange(S, dtype=np.float64)[:, None] * inv[None, :]
        return np.cos(ang), np.sin(ang)

    c128, s128 = cs(HEAD_DIM_A)
    c64, s64 = cs(HEAD_DIM_B)
    z32 = np.zeros_like(s64)
    t128 = [np.concatenate([c128, c128], axis=1), np.concatenate([-s128, s128], axis=1)]
    t64 = [np.concatenate([c64, c64, c64, c64], axis=1),
           np.concatenate([-s64, z32, -s64, z32], axis=1),
           np.concatenate([z32, s64, z32, s64], axis=1)]
    qa = HEAD_DIM_A ** -0.5 * LOG2E
    qb = HEAD_DIM_B ** -0.5 * LOG2E
    groups = {"r128": t128, "r64": t64, "r128q": [t * qa for t in t128], "r64q": [t * qb for t in t64]}
    return {name: [t.astype(np.float32) for t in ts] for name, ts in groups.items()}


W_PREP_TR = 256


def _w_in_pieces():
    offs = np.cumsum((0,) + IN_SIZES)
    names = ("qa", "kca", "vca", "ksa", "vsa", "kwa", "vwa", "za", "ga", "qb", "kb", "vb", "zb")
    start = {n: int(offs[i]) for i, n in enumerate(names)}
    width = dict(zip(names, IN_SIZES))
    n_g = 3 * GROUP_A
    halves = lambda n: [(start[n] + i * HEAD_DIM_B, HEAD_DIM_B) for i in (0, 0, 1, 1)]
    pieces = {n: [(start[n], width[n])]
              for n in ("qa", "kca", "ksa", "kwa", "vca", "vsa", "vwa", "za", "zb", "qb", "vb")}
    pad = LANES // N_KV_A - n_g
    pieces["ga"] = [(start["ga"], n_g), (None, pad), (start["ga"] + n_g, n_g), (None, pad)]
    pieces["kb2"] = halves("kb")
    return [p for name, _, _ in _PROJ_LAYOUT for p in pieces[name]]


def _w_prep_kernel(w_ref, o_ref):
    row = 0
    parts, filled = [], 0
    for src, wd in _w_in_pieces():
        done = 0
        while done < wd:
            take = min(wd - done, LANES - filled)
            if src is None:
                parts.append(jnp.zeros((take, W_PREP_TR), F32))
            else:
                parts.append(w_ref[src + done:src + done + take, :])
            done += take
            filled += take
            if filled == LANES:
                blk = parts[0] if len(parts) == 1 else jnp.concatenate(parts, axis=0)
                o_ref[row:row + LANES, :] = blk.astype(o_ref.dtype)
                row += LANES
                parts, filled = [], 0
    assert row == D_PROJ and not parts


def _prep_w_in(w_in):
    wt = w_in[0].T
    d_in = wt.shape[0]
    return pl.pallas_call(
        _w_prep_kernel,
        grid=(D_MODEL // W_PREP_TR,),
        in_specs=[pl.BlockSpec((d_in, W_PREP_TR), lambda i: (0, i))],
        out_specs=pl.BlockSpec((D_PROJ, W_PREP_TR), lambda i: (0, i)),
        out_shape=jax.ShapeDtypeStruct((D_PROJ, D_MODEL), BF16),
        compiler_params=pltpu.CompilerParams(dimension_semantics=("arbitrary",)),
        name="w_prep",
    )(wt)


def _in_proj(x2, g, w_all, tabs, S):
    M = x2.shape[0]
    params = pltpu.CompilerParams(dimension_semantics=("arbitrary",), vmem_limit_bytes=VMEM_LIMIT)
    out_sds = jax.ShapeDtypeStruct((M, IN_TN), BF16)
    slabs, strided, h = [], {}, None
    for j in range(N_SLABS):
        kinds = _UNIT_KINDS[j * _IN_UPT:(j + 1) * _IN_UPT]
        with_norm = j == 0
        tm = IN_TM_NORM if with_norm else IN_TM
        groups = _slab_tab_groups(kinds)
        act_spec = pl.BlockSpec((tm, D_MODEL), lambda i: (i, 0))
        in_specs = [act_spec, pl.BlockSpec((1, D_MODEL), lambda i: (0, 0))] if with_norm else [act_spec]
        args = [x2, g] if with_norm else [h]
        resident = pl.Buffered(1)
        in_specs.append(pl.BlockSpec((IN_TN, D_MODEL), lambda i, j=j: (j, 0), pipeline_mode=resident))
        args.append(w_all)
        if groups:
            tab = jnp.asarray(np.stack([t for grp in groups for t in tabs[grp]]))
            in_specs.append(pl.BlockSpec(tab.shape, lambda i: (0, 0, 0), pipeline_mode=resident))
            args.append(tab)
        out_specs = [pl.BlockSpec((tm, IN_TN), lambda i: (i, 0))]
        out_shape = [out_sds]
        if with_norm:
            out_specs.append(act_spec)
            out_shape.append(jax.ShapeDtypeStruct((M, D_MODEL), BF16))
        strided_names = [name for name in ("kca", "vca") if _slab(name) == j]
        strided_chunks = tuple(_col(name) // CHUNK for name in strided_names)
        scratch = []
        for _ in strided_chunks:
            out_specs.append(pl.BlockSpec((N_KV_A, tm // CMP_STRIDE, _HALF), lambda i: (0, i, 0)))
            out_shape.append(jax.ShapeDtypeStruct((N_KV_A, M // CMP_STRIDE, _HALF), BF16))
        if strided_chunks:
            scratch.append(pltpu.VMEM(
                (len(strided_chunks) * N_KV_A, IN_SUB_NORM if with_norm else IN_SUB, HEAD_DIM_A), F32))
        res = list(pl.pallas_call(
            functools.partial(_in_proj_kernel, kinds, with_norm, S, strided_chunks),
            grid=(M // tm,),
            in_specs=in_specs,
            out_specs=out_specs,
            out_shape=out_shape,
            scratch_shapes=scratch,
            compiler_params=params,
            name=f"in_proj_{j}",
        )(*args))
        slabs.append(res.pop(0))
        if with_norm:
            h = res.pop(0)
        for name in strided_names:
            strided[name] = res.pop(0)
    return slabs, strided


CMP_TM = 256
_HALF = CMP_STRIDE * HEAD_DIM_A


def _compress_kernel(tk_ref, tv_ref, w1k_ref, w1v_ref, w2k_ref, w2v_ref, posk_ref, posv_ref, kc_ref, vc_ref,
                     pb_ref):
    @pl.when(pl.program_id(0) == 0)
    def _():
        pb_ref[0] = _dot(posk_ref[...], w1k_ref[...])
        pb_ref[1] = _dot(posv_ref[...], w1v_ref[...])

    for n, (t_ref, w1_ref, w2_ref, o_ref) in enumerate(((tk_ref, w1k_ref, w2k_ref, kc_ref),
                                                        (tv_ref, w1v_ref, w2v_ref, vc_ref))):
        t = t_ref[...]
        u = _dot(t, w1_ref[0:_HALF, :])
        v = _dot(t, w1_ref[_HALF:2 * _HALF, :])
        pre = u + pltpu.roll(v, CMP_TM - 1, 0) + pb_ref[n, 0:1]
        hid = pre * jax.nn.sigmoid(pre)
        out = _dot(hid.astype(BF16), w2_ref[...])
        row = lax.broadcasted_iota(jnp.int32, out.shape, 0)
        n_cp = LANES
        out = jnp.where((row & (n_cp - 1)) == n_cp - 1, 0.0, out)
        o_ref[...] = out.astype(o_ref.dtype)


def _compress(tk, tv, w1k, w1v, w2k, w2v, posk, posv):
    R = tk.shape[0]
    rows = pl.BlockSpec((CMP_TM, _HALF), lambda i: (i, 0))
    whole = lambda a: pl.BlockSpec(a.shape, lambda i: (0, 0))
    o_spec = pl.BlockSpec((CMP_TM, HEAD_DIM_A), lambda i: (i, 0))
    o_sds = jax.ShapeDtypeStruct((R, HEAD_DIM_A), BF16)
    return pl.pallas_call(
        _compress_kernel,
        grid=(R // CMP_TM,),
        in_specs=[rows, rows, whole(w1k), whole(w1v), whole(w2k), whole(w2v), whole(posk), whole(posv)],
        out_specs=(o_spec, o_spec),
        out_shape=(o_sds, o_sds),
        scratch_shapes=[pltpu.VMEM((2, SUBLANES, CMP_HIDDEN), F32)],
        compiler_params=pltpu.CompilerParams(dimension_semantics=("arbitrary",)),
        name="compress",
    )(tk, tv, w1k, w1v, w2k, w2v, posk, posv)


TQ = 128
ATTN_AHEAD = 12
ONES_ROWS = 16


def _transpose_values(v_ref, vt_ref, n_rows, second_half=None):
    n_blk = vt_ref.shape[0]
    for blk in range(n_blk):
        vt = v_ref[0, blk * LANES:(blk + 1) * LANES, :].astype(F32).T
        if second_half is not None:
            vt = jnp.where(second_half == 0, vt[0:n_rows], vt[n_rows:2 * n_rows])
        vt_ref[blk, 0:n_rows, :] = vt[0:n_rows].astype(vt_ref.dtype)
        vt_ref[blk, n_rows:n_rows + ONES_ROWS, :] = jnp.ones((ONES_ROWS, LANES), vt_ref.dtype)


def _run_skewed(jobs, ahead=1):
    pending = [job[0]() for job in jobs[:ahead]]
    for i, (_, finish) in enumerate(jobs):
        if i + ahead < len(jobs):
            pending.append(jobs[i + ahead][0]())
        finish(pending[i])
        pending[i] = None


def _softmax_step(s, m, acc, vt):
    m_new = jnp.maximum(m, jnp.max(s, axis=0, keepdims=True))
    p = jnp.exp2(s - m_new).astype(BF16)
    return m_new, jnp.exp2(m - m_new) * acc + _dot(vt, p)


SEL_TK = 256
N_SELP = 32
PAIR = 2 * TQ
N_TILE_PAIRS = 2
N_TILES = 2 * N_TILE_PAIRS
_VROWS_A = HEAD_DIM_A + ONES_ROWS
_VROWS_B = HEAD_DIM_B + ONES_ROWS
_WIN_TILES = (256, 256, 128)
_N_SEL_JOBS = 9


def _query_tile(t, i, n_qt):
    low = i if t // 2 == 0 else n_qt // 2 - 1 - i
    return low if t % 2 == 0 else n_qt - 1 - low


def _attn_kernel(*refs):
    refs = list(refs)
    take = lambda n: [refs.pop(0) for _ in range(n)]
    sink_ref, = take(1)
    q_refs = take(N_TILES)
    ksel_ref, vsel_ref, kwin_ref, vwin_ref, kc_ref, vc_ref = take(6)
    g_refs, z_refs = take(N_TILES), take(N_TILES)
    ov_ref, wtab_ref, ctab_ref = take(3)
    qb_refs = take(N_TILES)
    kb_ref, vb_ref = take(2)
    zb_refs = take(N_TILES)
    btab_ref, = take(1)
    y_refs, yb_refs = take(N_TILES), take(N_TILES)
    vselt_ref, vwint_ref, vbt_ref, qs_ref, bias_ref, acc_ref, out_ref = refs

    kv = pl.program_id(1)
    i = pl.program_id(2)
    n_qt = N_TILES * pl.num_programs(2)
    n_pairs = GROUP_A // 2
    tiles = range(N_TILES)
    qts = [_query_tile(t, i, n_qt) for t in tiles]
    t0s = [qt * TQ for qt in qts]

    @pl.when(i == 0)
    def _():
        _transpose_values(vsel_ref, vselt_ref, HEAD_DIM_A)
        _transpose_values(vwin_ref, vwint_ref, HEAD_DIM_A)
        _transpose_values(vb_ref, vbt_ref, HEAD_DIM_B, kv)

    lane = lax.broadcasted_iota(jnp.int32, (1, TQ), 1)
    lane2 = jnp.concatenate([lane, lane], axis=1)
    qpair, gts = [], []
    for t in tiles:
        q = q_refs[t][0]
        pairs = [jnp.concatenate([q[:, h * LANES:(h + 1) * LANES] for h in (2 * p, 2 * p + 1)], axis=0)
                 for p in range(n_pairs)]
        for p in range(n_pairs):
            qs_ref[t, p] = pairs[p]
        qpair.append(pairs)
        gt = g_refs[t][0].astype(F32).T
        gts.append(jnp.where(kv == 0, gt[0:LANES // 2], gt[LANES // 2:LANES]))

    def gate(t, p, branch):
        return jnp.concatenate(
            [gts[t][3 * h + branch:3 * h + branch + 1] for h in (2 * p, 2 * p + 1)], axis=1)

    kc = kc_ref[0]
    vct = vc_ref[0].astype(F32).T.astype(BF16)
    cend = lax.broadcasted_iota(jnp.int32, (LANES, PAIR), 0) * CMP_STRIDE + (CMP_BLOCK - 1)
    psums = [[] for _ in tiles]

    def cmp_scores(t, p):
        return jnp.where(cend <= t0s[t] + lane2, _dot_nt(kc, qpair[t][p]), NEG_INF)

    def cmp_finish(t, p, s):
        m = jnp.max(s, axis=0, keepdims=True)
        e = jnp.where(cend <= t0s[t] + lane2, jnp.exp2(s - m), 0.0)
        pc = e / jnp.maximum(jnp.sum(e, axis=0, keepdims=True), TINY)
        psums[t].append(pc[:, 0:TQ] + pc[:, TQ:PAIR])
        out_ref[t, p] = gate(t, p, 0) * _dot(vct, pc.astype(BF16))

    ksts = [pl.multiple_of(jnp.maximum(t0 - WIN_A, 0), TQ) for t0 in t0s]
    n_win = sum(_WIN_TILES)

    win_var = [jnp.minimum(qt, WIN_A // TQ) for qt in qts]

    def win_scores(t, p):
        out, off = [], 0
        for tk in _WIN_TILES:
            kt = kwin_ref[0, pl.ds(pl.multiple_of(ksts[t] + off, LANES), tk), :]
            bias = wtab_ref[win_var[t], off:off + tk, :]
            out.append(_dot_nt(kt, qpair[t][p]) + jnp.concatenate([bias, bias], axis=1))
            off += tk
        return out

    def win_finish(t, p, ss):
        m = functools.reduce(jnp.maximum, [jnp.max(s, axis=0, keepdims=True) for s in ss])
        pw = jnp.concatenate([jnp.exp2(s - m).astype(BF16) for s in ss], axis=0)
        vt = jnp.concatenate([vwint_ref[ksts[t] // LANES + b] for b in range(n_win // LANES)], axis=1)
        a = _dot(vt, pw)
        out_ref[t, p] = out_ref[t, p] + (
            gate(t, p, 2) / jnp.maximum(a[HEAD_DIM_A:HEAD_DIM_A + 1], TINY)) * a[0:HEAD_DIM_A]

    jobs = [(functools.partial(cmp_scores, t, p), functools.partial(cmp_finish, t, p))
            for t in tiles for p in range(n_pairs)]
    jobs += [(functools.partial(win_scores, t, p), functools.partial(win_finish, t, p))
             for t in tiles for p in range(n_pairs)]

    nb_keys = WIN_B + TQ
    left = lax.broadcasted_iota(jnp.int32, (TQ, LANES), 1) < HEAD_DIM_B
    zero = jnp.zeros((TQ, LANES), qb_refs[0].dtype)
    kbs = [pl.multiple_of(jnp.maximum(t0 - WIN_B, 0), TQ) for t0 in t0s]
    b_var = [jnp.minimum(qt, WIN_B // TQ) for qt in qts]

    def b_sink(c):
        return jnp.concatenate(
            [jnp.full((1, TQ), sink_ref[kv * GROUP_B + 2 * c + h] * LOG2E, F32) for h in range(2)], axis=1)

    def b_scores(t, c):
        pair = qb_refs[t][0, :, c * LANES:(c + 1) * LANES]
        qp = jnp.concatenate([jnp.where(left, pair, zero), jnp.where(left, zero, pair)], axis=0)
        bias = btab_ref[b_var[t]]
        return _dot_nt(kb_ref[0, pl.ds(kbs[t], nb_keys), :], qp) + jnp.concatenate([bias, bias], axis=1)

    def b_finish(t, c, s):
        vt = jnp.concatenate([vbt_ref[kbs[t] // LANES + b] for b in range(nb_keys // LANES)], axis=1)
        sink = b_sink(c)
        m = jnp.maximum(jnp.max(s, axis=0, keepdims=True), sink)
        p = jnp.exp2(s - m).astype(BF16)
        a = _dot(vt, p)
        den = jnp.maximum(a[HEAD_DIM_B:HEAD_DIM_B + 1] + jnp.exp2(sink - m), TINY)
        o = a[0:HEAD_DIM_B] / den
        o2 = jnp.concatenate([o[:, 0:TQ], o[:, TQ:PAIR]], axis=0)
        z = zb_refs[t][0, :, c * LANES:(c + 1) * LANES]
        yb_refs[t][0, :, c * LANES:(c + 1) * LANES] = (o2.T * z.astype(F32)).astype(yb_refs[t].dtype)

    b_jobs = [[(functools.partial(b_scores, t, c), functools.partial(b_finish, t, c))
               for c in range(GROUP_B // 2)] for t in tiles]

    def select_blocks(t):
        ov = ov_ref[...]
        jidx = lax.broadcasted_iota(jnp.int32, (N_SELP, TQ), 0)
        psum = psums[t][0] + psums[t][1]
        p_hi = psum.astype(BF16)
        r1 = psum - p_hi.astype(F32)
        p_mid = r1.astype(BF16)
        p_lo = (r1 - p_mid.astype(F32)).astype(BF16)
        psel = _dot(ov, p_hi) + _dot(ov, p_mid) + _dot(ov, p_lo)
        tq = t0s[t] + lane
        cur = tq // SEL_BLOCK
        forced = (jidx == 0) | (jidx == cur) | (jidx == cur - 1)
        score = jnp.where(forced, FORCE_SCORE, jnp.where(jidx * SEL_BLOCK <= tq, psel, -1.0))
        rank = jnp.zeros(psel.shape, F32)
        for r in range(N_SELP):
            row = jnp.broadcast_to(score[r:r + 1, :], score.shape)
            rank = rank + jnp.where(jidx > r, jnp.where(row >= score, 1.0, 0.0), jnp.where(row > score, 1.0, 0.0))
        bias_t = jnp.where(rank < SEL_TOPK, 0.0, NEG_INF)
        for j in range(N_SELP):
            bias_ref[t, j] = jnp.broadcast_to(bias_t[j:j + 1, :], (SUBLANES, TQ))

    blk_per_tile = SEL_TK // SEL_BLOCK
    rep = SEL_BLOCK // SUBLANES
    acc_ref[...] = jnp.zeros(acc_ref.shape, F32)
    ms = [[jnp.full((1, PAIR), NEG_INF, F32) for _ in range(n_pairs)] for _ in tiles]
    ctx = {}

    def sel_scores(g, k, p):
        if p == 0:
            if g == 0 and k == 0:
                for t in tiles:
                    select_blocks(t)
            n_lo = qts[2 * g] // (SEL_TK // TQ) + 1
            hi = k >= n_lo
            t = 2 * g + hi.astype(jnp.int32)
            j = jnp.where(hi, k - n_lo, k)
            ks = pl.multiple_of(j * SEL_TK, SEL_TK)
            bt = jnp.concatenate(
                [bias_ref[t, j * blk_per_tile + b] for b in range(blk_per_tile) for _ in range(rep)], axis=0)
            qt = jnp.where(hi, qts[2 * g + 1], qts[2 * g])
            diag = j == qt // (SEL_TK // TQ)
            bt = bt + ctab_ref[jnp.where(diag, 1 + qt % (SEL_TK // TQ), 0)]
            bt = jnp.concatenate([bt, bt], axis=1)
            ctx[g, k] = (hi, t, j, ks, bt)
        hi, t, j, ks, bt = ctx[g, k]
        return _dot_nt(ksel_ref[0, pl.ds(ks, SEL_TK), :], qs_ref[t, p]) + bt

    def sel_finish(g, k, p, s):
        hi, t, j, ks, bt = ctx[g, k]
        lo_t, hi_t = 2 * g, 2 * g + 1
        vt = jnp.concatenate([vselt_ref[j * (SEL_TK // LANES) + b] for b in range(SEL_TK // LANES)], axis=1)
        m_new, acc_new = _softmax_step(s, jnp.where(hi, ms[hi_t][p], ms[lo_t][p]), acc_ref[t, p], vt)
        acc_ref[t, p] = acc_new
        ms[lo_t][p] = jnp.where(hi, ms[lo_t][p], m_new)
        ms[hi_t][p] = jnp.where(hi, m_new, ms[hi_t][p])

    sel_jobs = [(functools.partial(sel_scores, g, k, p), functools.partial(sel_finish, g, k, p))
                for g in range(N_TILE_PAIRS) for k in range(_N_SEL_JOBS) for p in range(n_pairs)]
    b_before = [job for t in tiles if t % 2 == 0 for job in b_jobs[t]]
    b_after = [job for t in tiles if t % 2 == 1 for job in b_jobs[t]]
    _run_skewed(jobs + b_before + sel_jobs + b_after, ATTN_AHEAD)

    for t in tiles:
        z = z_refs[t][0]
        for p in range(n_pairs):
            a = acc_ref[t, p]
            o = out_ref[t, p] + (gate(t, p, 1) / jnp.maximum(a[HEAD_DIM_A:HEAD_DIM_A + 1], TINY)) * a[0:HEAD_DIM_A]
            for h in (2 * p, 2 * p + 1):
                oh = o[:, (h % 2) * TQ:(h % 2 + 1) * TQ].T
                y_refs[t][0, :, h * LANES:(h + 1) * LANES] = (
                    oh * z[:, h * LANES:(h + 1) * LANES].astype(F32)).astype(y_refs[t].dtype)


def _mask_tables():
    lane = np.arange(TQ)[None, :]
    row = np.arange(WIN_A + TQ)[:, None]
    wtab = []
    for v in range(WIN_A // TQ + 1):
        t0 = v * TQ
        dpos = (t0 + lane) - (max(t0 - WIN_A, 0) + row)
        wtab.append(np.where((dpos >= 0) & (dpos < WIN_A), 0.0, NEG_INF))
    row = np.arange(SEL_TK)[:, None]
    ctab = [np.zeros((SEL_TK, TQ))]
    for odd in range(SEL_TK // TQ):
        ctab.append(np.where(row <= odd * TQ + lane, 0.0, NEG_INF))
    row = np.arange(WIN_B + TQ)[:, None]
    btab = []
    for v in range(WIN_B // TQ + 1):
        t0 = v * TQ
        dpos = (t0 + lane) - (max(t0 - WIN_B, 0) + row)
        btab.append(np.where((dpos >= 0) & (dpos < WIN_B), 0.0, NEG_INF))
    return [jnp.asarray(np.stack(t), F32) for t in (wtab, ctab, btab)]


def _attention(slabs, kc, vc, ov, sinks):
    B, S, _ = slabs[0].shape
    sl = lambda name: slabs[_slab(name)]
    n_cp = S // CMP_STRIDE
    n_qt = S // TQ
    n_steps = n_qt // N_TILES
    assert n_cp == LANES and S // SEL_BLOCK == N_SELP and (n_qt // 2) // (SEL_TK // TQ) * 2 + 1 == _N_SEL_JOBS
    assert N_KV_A == N_KV_B and GROUP_A * HEAD_DIM_A == GROUP_B * HEAD_DIM_B and N_TILE_PAIRS in (1, 2)
    wq = GROUP_A * HEAD_DIM_A
    cb = lambda name: _col(name) // LANES
    kv_spec = lambda name: pl.BlockSpec((1, S, LANES), lambda b, k, i, s, c=cb(name): (b, 0, c + k))
    whole = lambda a: pl.BlockSpec(a.shape, lambda b, k, i, s: (0,) * a.ndim)

    def tile_specs(width, c, per_kv=True):
        return [pl.BlockSpec((1, TQ, width),
                             lambda b, k, i, s, t=t: (b, _query_tile(t, i, n_qt), c + (k if per_kv else 0)))
                for t in range(N_TILES)]

    part = [_query_tile(t, 0, n_qt) // n_steps for t in range(N_TILES)]
    assert sorted(part) == list(range(N_TILES))
    y_specs = [pl.BlockSpec((1, TQ, wq), lambda b, k, i, s, t=t: (b, _query_tile(t, i, n_qt) - part[t] * n_steps, k))
               for t in range(N_TILES)]
    y_sds = jax.ShapeDtypeStruct((B, S // N_TILES, WIDTH_A), BF16)
    wtab, ctab, btab = _mask_tables()
    grid_spec = pltpu.PrefetchScalarGridSpec(
        num_scalar_prefetch=1,
        grid=(B, N_KV_A, n_steps),
        in_specs=[
            *tile_specs(wq, _col("qa") // wq),
            kv_spec("ksa"), kv_spec("vsa"), kv_spec("kwa"), kv_spec("vwa"),
            pl.BlockSpec((1, n_cp, LANES), lambda b, k, i, s: (k * B + b, 0, 0)),
            pl.BlockSpec((1, n_cp, LANES), lambda b, k, i, s: (k * B + b, 0, 0)),
            *tile_specs(LANES, cb("ga"), per_kv=False),
            *tile_specs(wq, _col("za") // wq),
            whole(ov), whole(wtab), whole(ctab),
            *tile_specs(wq, _col("qb") // wq),
            kv_spec("kb2"), pl.BlockSpec((1, S, LANES), lambda b, k, i, s: (b, 0, cb("vb"))),
            *tile_specs(wq, _col("zb") // wq),
            whole(btab),
        ],
        out_specs=y_specs + y_specs,
        scratch_shapes=[
            pltpu.VMEM((S // LANES, _VROWS_A, LANES), BF16),
            pltpu.VMEM((S // LANES, _VROWS_A, LANES), BF16),
            pltpu.VMEM((S // LANES, _VROWS_B, LANES), BF16),
            pltpu.VMEM((N_TILES, GROUP_A // 2, PAIR, HEAD_DIM_A), BF16),
            pltpu.VMEM((N_TILES, N_SELP, SUBLANES, TQ), F32),
            pltpu.VMEM((N_TILES, GROUP_A // 2, _VROWS_A, PAIR), F32),
            pltpu.VMEM((N_TILES, GROUP_A // 2, HEAD_DIM_A, PAIR), F32),
        ],
    )
    per_tile = lambda name: [sl(name)] * N_TILES
    outs = pl.pallas_call(
        _attn_kernel,
        grid_spec=grid_spec,
        out_shape=[y_sds] * (2 * N_TILES),
        compiler_params=pltpu.CompilerParams(
            dimension_semantics=("arbitrary", "arbitrary", "arbitrary"), vmem_limit_bytes=VMEM_LIMIT),
        name="attn",
    )(sinks, *per_tile("qa"), sl("ksa"), sl("vsa"), sl("kwa"), sl("vwa"), kc, vc,
      *per_tile("ga"), *per_tile("za"), ov, wtab, ctab,
      *per_tile("qb"), sl("kb2"), sl("vb"), *per_tile("zb"), btab)
    in_order = lambda ys: [ys[part.index(n)] for n in range(N_TILES)]
    return in_order(outs[:N_TILES]), in_order(outs[N_TILES:])


def _out_proj_kernel(n_parts, *refs):
    ya_refs, yb_refs = refs[:n_parts], refs[n_parts:2 * n_parts]
    wa_ref, wb_ref, x_ref, g_ref, o_ref = refs[2 * n_parts:]
    part = pl.program_id(0) % n_parts

    for n in range(n_parts):
        @pl.when(part == n)
        def _(n=n):
            r = x_ref[...] + _dot(ya_refs[n][...], wa_ref[...]) + _dot(yb_refs[n][...], wb_ref[...])
            ms = jnp.mean(r * r, axis=-1, keepdims=True)
            o_ref[...] = (r * lax.rsqrt(ms + RMS_EPS)) * g_ref[...]


def _out_proj(ya_parts, yb_parts, w_out, x2, g, S):
    M = x2.shape[0]
    n_parts = len(ya_parts)
    tm = S // n_parts
    n_seq = M // S
    y_spec = lambda w, n: pl.BlockSpec(
        (tm, w), lambda i: (jnp.minimum((i + n_parts - 1 - n) // n_parts, n_seq - 1), 0))
    return pl.pallas_call(
        functools.partial(_out_proj_kernel, n_parts),
        grid=(M // tm,),
        in_specs=[
            *[y_spec(WIDTH_A, n) for n in range(n_parts)], *[y_spec(WIDTH_B, n) for n in range(n_parts)],
            pl.BlockSpec((WIDTH_A, D_MODEL), lambda i: (0, 0)),
            pl.BlockSpec((WIDTH_B, D_MODEL), lambda i: (WIDTH_A // WIDTH_B, 0)),
            pl.BlockSpec((tm, D_MODEL), lambda i: (i, 0)),
            pl.BlockSpec((1, D_MODEL), lambda i: (0, 0)),
        ],
        out_specs=pl.BlockSpec((tm, D_MODEL), lambda i: (i, 0)),
        out_shape=jax.ShapeDtypeStruct((M, D_MODEL), F32),
        compiler_params=pltpu.CompilerParams(
            dimension_semantics=("arbitrary",), vmem_limit_bytes=VMEM_LIMIT_OUT_PROJ),
        name="out_proj",
    )(*ya_parts, *yb_parts, w_out, w_out, x2, g)


def _overlap_matrix(n_cp):
    c_start = np.arange(n_cp) * CMP_STRIDE
    j_start = np.arange(N_SELP) * SEL_BLOCK
    ov = (c_start[None, :] < j_start[:, None] + SEL_BLOCK) & (c_start[None, :] + CMP_BLOCK > j_start[:, None])
    return jnp.asarray(ov, BF16)


def kernel(x, w_in, cmp_k_w1, cmp_k_w2, cmp_v_w1, cmp_v_w2, cmp_k_pos, cmp_v_pos, sinks, w_out, norm_g, final_g):
    B, S, D = x.shape
    assert D == D_MODEL and w_in.shape[0] == 1
    x2 = x.reshape(B * S, D)

    w_all = _prep_w_in(w_in)
    tabs = _rope_tables(S)
    slabs, strided = _in_proj(x2, norm_g[0].reshape(1, D), w_all, tabs, S)
    slabs = [p.reshape(B, S, IN_TN) for p in slabs]

    n_cp = S // CMP_STRIDE
    tk, tv = [strided[n].reshape(N_KV_A * B * n_cp, _HALF) for n in ("kca", "vca")]
    flat_pos = lambda p: jnp.broadcast_to(p.reshape(1, CMP_BLOCK * HEAD_DIM_A), (8, CMP_BLOCK * HEAD_DIM_A))
    kc, vc = _compress(tk, tv, cmp_k_w1[0].astype(BF16), cmp_v_w1[0].astype(BF16),
                       cmp_k_w2[0].astype(BF16), cmp_v_w2[0].astype(BF16),
                       flat_pos(cmp_k_pos[0]).astype(BF16), flat_pos(cmp_v_pos[0]).astype(BF16))
    kc, vc = [c.reshape(N_KV_A * B, n_cp, HEAD_DIM_A) for c in (kc, vc)]

    ya_parts, yb_parts = _attention(slabs, kc, vc, _overlap_matrix(n_cp), sinks[0])
    flat = lambda ys: [y.reshape(B * y.shape[1], y.shape[2]) for y in ys]
    out = _out_proj(flat(ya_parts), flat(yb_parts), w_out[0].astype(BF16), x2, final_g.reshape(1, D), S)
    return out.reshape(B, S, D)
```

```python
import functools
import math

import numpy as np
import jax
import jax.numpy as jnp
from jax import lax
from jax.experimental import pallas as pl
from jax.experimental.pallas import tpu as pltpu

F32 = jnp.float32
BF16 = jnp.bfloat16

D_MODEL = 2048
ROPE_THETA = 10000.0
RMS_EPS = 1e-6
NEG_INF = -1e30
TINY = 1e-30
LOG2E = math.log2(math.e)

WIDTH_A = 1024
HEAD_DIM_A = 128
N_KV_A = 2
GROUP_A = 4
KV_A = N_KV_A * HEAD_DIM_A
CMP_BLOCK = 32
CMP_STRIDE = 16
CMP_HIDDEN = 256
SEL_BLOCK = 64
SEL_TOPK = 16
WIN_A = 512
FORCE_SCORE = 1e4

WIDTH_B = 1024
HEAD_DIM_B = 64
N_KV_B = 2
GROUP_B = 8
KV_B = N_KV_B * HEAD_DIM_B
WIN_B = 128

IN_SIZES = (WIDTH_A, KV_A, KV_A, KV_A, KV_A, KV_A, KV_A, WIDTH_A, 3 * 8,
            WIDTH_B, KV_B, KV_B, WIDTH_B)

LANES = 128
SUBLANES = 8
VMEM_BYTES = 64 * 1024 * 1024
VMEM_LIMIT = VMEM_BYTES * 3 // 4
VMEM_LIMIT_OUT_PROJ = VMEM_BYTES * 7 // 8
CHUNK = 256

EP_NONE, EP_ROPE128, EP_ROPE64, EP_SILU, EP_SIGMOID, EP_ROPE128_Q, EP_ROPE64_Q = range(7)

_PROJ_LAYOUT = (
    ("qa", 8, EP_ROPE128_Q), ("kca", 2, EP_ROPE128), ("ksa", 2, EP_ROPE128), ("kwa", 2, EP_ROPE128),
    ("vca", 2, EP_NONE),
    ("vsa", 2, EP_NONE), ("vwa", 2, EP_NONE), ("za", 8, EP_SILU), ("kb2", 2, EP_ROPE64),
    ("ga", 1, EP_SIGMOID), ("vb", 1, EP_NONE),
    ("zb", 8, EP_SILU), ("qb", 8, EP_ROPE64_Q),
)
_UNIT_KINDS = tuple(k for _, n, k in _PROJ_LAYOUT for _ in range(n))
_UNIT_START = {}
_c = 0
for _name, _n, _k in _PROJ_LAYOUT:
    _UNIT_START[_name] = _c
    _c += _n
D_PROJ = _c * LANES
IN_TN = 2048
_IN_UPT = IN_TN // LANES
N_SLABS = D_PROJ // IN_TN


def _slab(name):
    return _UNIT_START[name] // _IN_UPT


def _col(name):
    return (_UNIT_START[name] % _IN_UPT) * LANES


def _dot(a, b):
    return jnp.dot(a, b, preferred_element_type=F32)


def _dot_nt(a, b):
    return lax.dot_general(a, b, (((1,), (1,)), ((), ())), preferred_element_type=F32)


IN_TM_NORM = 512
IN_TM = 1024
IN_SUB = 256

_TAB_GROUP = {EP_ROPE128: "r128", EP_ROPE128_Q: "r128q", EP_ROPE64: "r64", EP_ROPE64_Q: "r64q"}
_TAB_SIZE = {"r128": 2, "r128q": 2, "r64": 3, "r64q": 3}


def _slab_tab_groups(kinds):
    return tuple(dict.fromkeys(_TAB_GROUP[k] for k in kinds if k in _TAB_GROUP))


def _in_proj_kernel(kinds, with_norm, seq_len, strided_chunks, *refs):
    groups = _slab_tab_groups(kinds)
    refs = list(refs)
    if with_norm:
        x_ref, g_ref = refs.pop(0), refs.pop(0)
    else:
        h_ref = refs.pop(0)
    w_ref = refs.pop(0)
    tab_ref = refs.pop(0) if groups else None
    o_ref = refs.pop(0)
    if with_norm:
        h_ref = refs.pop(0)
        x = x_ref[...]
        ms = jnp.mean(x * x, axis=-1, keepdims=True)
        h_ref[...] = ((x * lax.rsqrt(ms + RMS_EPS)) * g_ref[...]).astype(BF16)
    t_refs = {c: refs.pop(0) for c in strided_chunks}
    if strided_chunks:
        stage_ref = refs.pop(0)
    tm = h_ref.shape[0]
    sub_rows = tm if with_norm else IN_SUB
    pos0 = (pl.program_id(0) % (seq_len // tm)) * tm
    base, n = {}, 0
    for grp in groups:
        base[grp] = n
        n += _TAB_SIZE[grp]

    def epilogue(kind, r, a):
        if kind == EP_NONE:
            return a
        if kind == EP_SILU:
            return a * jax.nn.sigmoid(a)
        if kind == EP_SIGMOID:
            return jax.nn.sigmoid(a)
        t = base[_TAB_GROUP[kind]]
        tab = lambda k: tab_ref[t + k, pl.ds(pl.multiple_of(pos0 + r * sub_rows, sub_rows), sub_rows), :]
        if kind in (EP_ROPE128, EP_ROPE128_Q):
            return a * tab(0) + pltpu.roll(a, 64, 1) * tab(1)
        return a * tab(0) + pltpu.roll(a, 96, 1) * tab(1) + pltpu.roll(a, 32, 1) * tab(2)

    n_grp = sub_rows // CMP_STRIDE

    subs = CHUNK // LANES

    def finish(r, c, acc):
        rows = slice(r * sub_rows, (r + 1) * sub_rows)
        for sub in range(subs):
            cols = slice(c * CHUNK + sub * LANES, c * CHUNK + (sub + 1) * LANES)
            val = epilogue(kinds[c * subs + sub], r, acc[:, sub * LANES:(sub + 1) * LANES])
            o_ref[rows, cols] = val.astype(o_ref.dtype)
            if c in t_refs:
                slot = strided_chunks.index(c) * subs + sub
                stage_ref[slot] = val
                for k in range(CMP_STRIDE):
                    t_refs[c][sub, r * n_grp:(r + 1) * n_grp, k * LANES:(k + 1) * LANES] = (
                        stage_ref[slot, pl.ds(k, n_grp, stride=CMP_STRIDE), :].astype(t_refs[c].dtype))

    jobs = []
    for r in range(tm // sub_rows):
        for c in range(len(kinds) // subs):
            jobs.append((lambda r=r, c=c: _dot_nt(h_ref[r * sub_rows:(r + 1) * sub_rows, :],
                                                  w_ref[c * CHUNK:(c + 1) * CHUNK, :]),
                         functools.partial(finish, r, c)))
    _run_skewed(jobs)


def _rope_tables(S):
    def cs(d):
        inv = ROPE_THETA ** (-np.arange(0, d, 2, dtype=np.float64) / d)
        ang = np.arange(S, dtype=np.float64)[:, None] * inv[None, :]
        return np.cos(ang), np.sin(ang)

    c128, s128 = cs(HEAD_DIM_A)
    c64, s64 = cs(HEAD_DIM_B)
    z32 = np.zeros_like(s64)
    t128 = [np.concatenate([c128, c128], axis=1), np.concatenate([-s128, s128], axis=1)]
    t64 = [np.concatenate([c64, c64, c64, c64], axis=1),
           np.concatenate([-s64, z32, -s64, z32], axis=1),
           np.concatenate([z32, s64, z32, s64], axis=1)]
    qa = HEAD_DIM_A ** -0.5 * LOG2E
    qb = HEAD_DIM_B ** -0.5 * LOG2E
    groups = {"r128": t128, "r64": t64, "r128q": [t * qa for t in t128], "r64q": [t * qb for t in t64]}
    return {name: [t.astype(np.float32) for t in ts] for name, ts in groups.items()}


W_PREP_TR = 256


def _w_in_pieces():
    offs = np.cumsum((0,) + IN_SIZES)
    names = ("qa", "kca", "vca", "ksa", "vsa", "kwa", "vwa", "za", "ga", "qb", "kb", "vb", "zb")
    start = {n: int(offs[i]) for i, n in enumerate(names)}
    width = dict(zip(names, IN_SIZES))
    n_g = 3 * GROUP_A
    halves = lambda n: [(start[n] + i * HEAD_DIM_B, HEAD_DIM_B) for i in (0, 0, 1, 1)]
    pieces = {n: [(start[n], width[n])]
              for n in ("qa", "kca", "ksa", "kwa", "vca", "vsa", "vwa", "za", "zb", "qb", "vb")}
    pad = LANES // N_KV_A - n_g
    pieces["ga"] = [(start["ga"], n_g), (None, pad), (start["ga"] + n_g, n_g), (None, pad)]
    pieces["kb2"] = halves("kb")
    return [p for name, _, _ in _PROJ_LAYOUT for p in pieces[name]]


def _w_prep_kernel(w_ref, o_ref):
    row = 0
    parts, filled = [], 0
    for src, wd in _w_in_pieces():
        done = 0
        while done < wd:
            take = min(wd - done, LANES - filled)
            if src is None:
                parts.append(jnp.zeros((take, W_PREP_TR), F32))
            else:
                parts.append(w_ref[src + done:src + done + take, :])
            done += take
            filled += take
            if filled == LANES:
                blk = parts[0] if len(parts) == 1 else jnp.concatenate(parts, axis=0)
                o_ref[row:row + LANES, :] = blk.astype(o_ref.dtype)
                row += LANES
                parts, filled = [], 0
    assert row == D_PROJ and not parts


def _prep_w_in(w_in):
    wt = w_in[0].T
    d_in = wt.shape[0]
    return pl.pallas_call(
        _w_prep_kernel,
        grid=(D_MODEL // W_PREP_TR,),
        in_specs=[pl.BlockSpec((d_in, W_PREP_TR), lambda i: (0, i))],
        out_specs=pl.BlockSpec((D_PROJ, W_PREP_TR), lambda i: (0, i)),
        out_shape=jax.ShapeDtypeStruct((D_PROJ, D_MODEL), BF16),
        compiler_params=pltpu.CompilerParams(dimension_semantics=("arbitrary",)),
        name="w_prep",
    )(wt)


def _in_proj(x2, g, w_all, tabs, S):
    M = x2.shape[0]
    params = pltpu.CompilerParams(dimension_semantics=("arbitrary",), vmem_limit_bytes=VMEM_LIMIT)
    out_sds = jax.ShapeDtypeStruct((M, IN_TN), BF16)
    slabs, strided, h = [], {}, None
    for j in range(N_SLABS):
        kinds = _UNIT_KINDS[j * _IN_UPT:(j + 1) * _IN_UPT]
        with_norm = j == 0
        tm = IN_TM_NORM if with_norm else IN_TM
        groups = _slab_tab_groups(kinds)
        act_spec = pl.BlockSpec((tm, D_MODEL), lambda i: (i, 0))
        in_specs = [act_spec, pl.BlockSpec((1, D_MODEL), lambda i: (0, 0))] if with_norm else [act_spec]
        args = [x2, g] if with_norm else [h]
        in_specs.append(pl.BlockSpec((IN_TN, D_MODEL), lambda i, j=j: (j, 0)))
        args.append(w_all)
        if groups:
            tab = jnp.asarray(np.stack([t for grp in groups for t in tabs[grp]]))
            in_specs.append(pl.BlockSpec(tab.shape, lambda i: (0, 0, 0)))
            args.append(tab)
        out_specs = [pl.BlockSpec((tm, IN_TN), lambda i: (i, 0))]
        out_shape = [out_sds]
        if with_norm:
            out_specs.append(act_spec)
            out_shape.append(jax.ShapeDtypeStruct((M, D_MODEL), BF16))
        strided_names = [name for name in ("kca", "vca") if _slab(name) == j]
        strided_chunks = tuple(_col(name) // CHUNK for name in strided_names)
        scratch = []
        for _ in strided_chunks:
            out_specs.append(pl.BlockSpec((N_KV_A, tm // CMP_STRIDE, _HALF), lambda i: (0, i, 0)))
            out_shape.append(jax.ShapeDtypeStruct((N_KV_A, M // CMP_STRIDE, _HALF), BF16))
        if strided_chunks:
            scratch.append(pltpu.VMEM((len(strided_chunks) * N_KV_A, tm if with_norm else IN_SUB, HEAD_DIM_A), F32))
        res = list(pl.pallas_call(
            functools.partial(_in_proj_kernel, kinds, with_norm, S, strided_chunks),
            grid=(M // tm,),
            in_specs=in_specs,
            out_specs=out_specs,
            out_shape=out_shape,
            scratch_shapes=scratch,
            compiler_params=params,
            name=f"in_proj_{j}",
        )(*args))
        slabs.append(res.pop(0))
        if with_norm:
            h = res.pop(0)
        for name in strided_names:
            strided[name] = res.pop(0)
    return slabs, strided


CMP_TM = 256
_HALF = CMP_STRIDE * HEAD_DIM_A


def _compress_kernel(tk_ref, tv_ref, w1k_ref, w1v_ref, w2k_ref, w2v_ref, posk_ref, posv_ref, kc_ref, vc_ref,
                     pb_ref):
    @pl.when(pl.program_id(0) == 0)
    def _():
        pb_ref[0] = _dot(posk_ref[...], w1k_ref[...])
        pb_ref[1] = _dot(posv_ref[...], w1v_ref[...])

    for n, (t_ref, w1_ref, w2_ref, o_ref) in enumerate(((tk_ref, w1k_ref, w2k_ref, kc_ref),
                                                        (tv_ref, w1v_ref, w2v_ref, vc_ref))):
        t = t_ref[...]
        u = _dot(t, w1_ref[0:_HALF, :])
        v = _dot(t, w1_ref[_HALF:2 * _HALF, :])
        pre = u + pltpu.roll(v, CMP_TM - 1, 0) + pb_ref[n, 0:1]
        hid = pre * jax.nn.sigmoid(pre)
        out = _dot(hid.astype(BF16), w2_ref[...])
        row = lax.broadcasted_iota(jnp.int32, out.shape, 0)
        n_cp = LANES
        out = jnp.where((row & (n_cp - 1)) == n_cp - 1, 0.0, out)
        o_ref[...] = out.astype(o_ref.dtype)


def _compress(tk, tv, w1k, w1v, w2k, w2v, posk, posv):
    R = tk.shape[0]
    rows = pl.BlockSpec((CMP_TM, _HALF), lambda i: (i, 0))
    whole = lambda a: pl.BlockSpec(a.shape, lambda i: (0, 0))
    o_spec = pl.BlockSpec((CMP_TM, HEAD_DIM_A), lambda i: (i, 0))
    o_sds = jax.ShapeDtypeStruct((R, HEAD_DIM_A), BF16)
    return pl.pallas_call(
        _compress_kernel,
        grid=(R // CMP_TM,),
        in_specs=[rows, rows, whole(w1k), whole(w1v), whole(w2k), whole(w2v), whole(posk), whole(posv)],
        out_specs=(o_spec, o_spec),
        out_shape=(o_sds, o_sds),
        scratch_shapes=[pltpu.VMEM((2, SUBLANES, CMP_HIDDEN), F32)],
        compiler_params=pltpu.CompilerParams(dimension_semantics=("arbitrary",)),
        name="compress",
    )(tk, tv, w1k, w1v, w2k, w2v, posk, posv)


TQ = 128
ATTN_AHEAD = 12
ONES_ROWS = 16


def _transpose_values(v_ref, vt_ref, n_rows, second_half=None):
    n_blk = vt_ref.shape[0]
    for blk in range(n_blk):
        vt = v_ref[0, blk * LANES:(blk + 1) * LANES, :].astype(F32).T
        if second_half is not None:
            vt = jnp.where(second_half == 0, vt[0:n_rows], vt[n_rows:2 * n_rows])
        vt_ref[blk, 0:n_rows, :] = vt[0:n_rows].astype(vt_ref.dtype)
        vt_ref[blk, n_rows:n_rows + ONES_ROWS, :] = jnp.ones((ONES_ROWS, LANES), vt_ref.dtype)


def _run_skewed(jobs, ahead=1):
    pending = [job[0]() for job in jobs[:ahead]]
    for i, (_, finish) in enumerate(jobs):
        if i + ahead < len(jobs):
            pending.append(jobs[i + ahead][0]())
        finish(pending[i])
        pending[i] = None


def _softmax_step(s, m, acc, vt):
    m_new = jnp.maximum(m, jnp.max(s, axis=0, keepdims=True))
    p = jnp.exp2(s - m_new).astype(BF16)
    return m_new, jnp.exp2(m - m_new) * acc + _dot(vt, p)


SEL_TK = 256
N_SELP = 32
PAIR = 2 * TQ
N_TILE_PAIRS = 2
N_TILES = 2 * N_TILE_PAIRS
_VROWS_A = HEAD_DIM_A + ONES_ROWS
_VROWS_B = HEAD_DIM_B + ONES_ROWS
_WIN_TILES = (256, 256, 128)
_N_SEL_JOBS = 9


def _query_tile(t, i, n_qt):
    low = i if t // 2 == 0 else n_qt // 2 - 1 - i
    return low if t % 2 == 0 else n_qt - 1 - low


def _attn_kernel(*refs):
    refs = list(refs)
    take = lambda n: [refs.pop(0) for _ in range(n)]
    sink_ref, = take(1)
    q_refs = take(N_TILES)
    ksel_ref, vsel_ref, kwin_ref, vwin_ref, kc_ref, vc_ref = take(6)
    g_refs, z_refs = take(N_TILES), take(N_TILES)
    ov_ref, wtab_ref, ctab_ref = take(3)
    qb_refs = take(N_TILES)
    kb_ref, vb_ref = take(2)
    zb_refs = take(N_TILES)
    btab_ref, = take(1)
    y_refs, yb_refs = take(N_TILES), take(N_TILES)
    vselt_ref, vwint_ref, vbt_ref, qs_ref, bias_ref, acc_ref, out_ref = refs

    kv = pl.program_id(1)
    i = pl.program_id(2)
    n_qt = N_TILES * pl.num_programs(2)
    n_pairs = GROUP_A // 2
    tiles = range(N_TILES)
    qts = [_query_tile(t, i, n_qt) for t in tiles]
    t0s = [qt * TQ for qt in qts]

    @pl.when(i == 0)
    def _():
        _transpose_values(vsel_ref, vselt_ref, HEAD_DIM_A)
        _transpose_values(vwin_ref, vwint_ref, HEAD_DIM_A)
        _transpose_values(vb_ref, vbt_ref, HEAD_DIM_B, kv)

    lane = lax.broadcasted_iota(jnp.int32, (1, TQ), 1)
    lane2 = jnp.concatenate([lane, lane], axis=1)
    qpair, gts = [], []
    for t in tiles:
        q = q_refs[t][0]
        pairs = [jnp.concatenate([q[:, h * LANES:(h + 1) * LANES] for h in (2 * p, 2 * p + 1)], axis=0)
                 for p in range(n_pairs)]
        for p in range(n_pairs):
            qs_ref[t, p] = pairs[p]
        qpair.append(pairs)
        gt = g_refs[t][0].astype(F32).T
        gts.append(jnp.where(kv == 0, gt[0:LANES // 2], gt[LANES // 2:LANES]))

    def gate(t, p, branch):
        return jnp.concatenate(
            [gts[t][3 * h + branch:3 * h + branch + 1] for h in (2 * p, 2 * p + 1)], axis=1)

    kc = kc_ref[0]
    vct = vc_ref[0].astype(F32).T.astype(BF16)
    cend = lax.broadcasted_iota(jnp.int32, (LANES, PAIR), 0) * CMP_STRIDE + (CMP_BLOCK - 1)
    psums = [[] for _ in tiles]

    def cmp_scores(t, p):
        return jnp.where(cend <= t0s[t] + lane2, _dot_nt(kc, qpair[t][p]), NEG_INF)

    def cmp_finish(t, p, s):
        m = jnp.max(s, axis=0, keepdims=True)
        e = jnp.where(cend <= t0s[t] + lane2, jnp.exp2(s - m), 0.0)
        pc = e / jnp.maximum(jnp.sum(e, axis=0, keepdims=True), TINY)
        psums[t].append(pc[:, 0:TQ] + pc[:, TQ:PAIR])
        out_ref[t, p] = gate(t, p, 0) * _dot(vct, pc.astype(BF16))

    ksts = [pl.multiple_of(jnp.maximum(t0 - WIN_A, 0), TQ) for t0 in t0s]
    n_win = sum(_WIN_TILES)

    win_var = [jnp.minimum(qt, WIN_A // TQ) for qt in qts]

    def win_scores(t, p):
        out, off = [], 0
        for tk in _WIN_TILES:
            kt = kwin_ref[0, pl.ds(pl.multiple_of(ksts[t] + off, LANES), tk), :]
            bias = wtab_ref[win_var[t], off:off + tk, :]
            out.append(_dot_nt(kt, qpair[t][p]) + jnp.concatenate([bias, bias], axis=1))
            off += tk
        return out

    def win_finish(t, p, ss):
        m = functools.reduce(jnp.maximum, [jnp.max(s, axis=0, keepdims=True) for s in ss])
        pw = jnp.concatenate([jnp.exp2(s - m).astype(BF16) for s in ss], axis=0)
        vt = jnp.concatenate([vwint_ref[ksts[t] // LANES + b] for b in range(n_win // LANES)], axis=1)
        a = _dot(vt, pw)
        out_ref[t, p] = out_ref[t, p] + (
            gate(t, p, 2) / jnp.maximum(a[HEAD_DIM_A:HEAD_DIM_A + 1], TINY)) * a[0:HEAD_DIM_A]

    jobs = [(functools.partial(cmp_scores, t, p), functools.partial(cmp_finish, t, p))
            for t in tiles for p in range(n_pairs)]
    jobs += [(functools.partial(win_scores, t, p), functools.partial(win_finish, t, p))
             for t in tiles for p in range(n_pairs)]

    nb_keys = WIN_B + TQ
    left = lax.broadcasted_iota(jnp.int32, (TQ, LANES), 1) < HEAD_DIM_B
    zero = jnp.zeros((TQ, LANES), qb_refs[0].dtype)
    kbs = [pl.multiple_of(jnp.maximum(t0 - WIN_B, 0), TQ) for t0 in t0s]
    b_var = [jnp.minimum(qt, WIN_B // TQ) for qt in qts]

    def b_sink(c):
        return jnp.concatenate(
            [jnp.full((1, TQ), sink_ref[kv * GROUP_B + 2 * c + h] * LOG2E, F32) for h in range(2)], axis=1)

    def b_scores(t, c):
        pair = qb_refs[t][0, :, c * LANES:(c + 1) * LANES]
        qp = jnp.concatenate([jnp.where(left, pair, zero), jnp.where(left, zero, pair)], axis=0)
        bias = btab_ref[b_var[t]]
        return _dot_nt(kb_ref[0, pl.ds(kbs[t], nb_keys), :], qp) + jnp.concatenate([bias, bias], axis=1)

    def b_finish(t, c, s):
        vt = jnp.concatenate([vbt_ref[kbs[t] // LANES + b] for b in range(nb_keys // LANES)], axis=1)
        sink = b_sink(c)
        m = jnp.maximum(jnp.max(s, axis=0, keepdims=True), sink)
        p = jnp.exp2(s - m).astype(BF16)
        a = _dot(vt, p)
        den = jnp.maximum(a[HEAD_DIM_B:HEAD_DIM_B + 1] + jnp.exp2(sink - m), TINY)
        o = a[0:HEAD_DIM_B] / den
        o2 = jnp.concatenate([o[:, 0:TQ], o[:, TQ:PAIR]], axis=0)
        z = zb_refs[t][0, :, c * LANES:(c + 1) * LANES]
        yb_refs[t][0, :, c * LANES:(c + 1) * LANES] = (o2.T * z.astype(F32)).astype(yb_refs[t].dtype)

    b_jobs = [[(functools.partial(b_scores, t, c), functools.partial(b_finish, t, c))
               for c in range(GROUP_B // 2)] for t in tiles]

    def select_blocks(t):
        ov = ov_ref[...]
        jidx = lax.broadcasted_iota(jnp.int32, (N_SELP, TQ), 0)
        psum = psums[t][0] + psums[t][1]
        p_hi = psum.astype(BF16)
        r1 = psum - p_hi.astype(F32)
        p_mid = r1.astype(BF16)
        p_lo = (r1 - p_mid.astype(F32)).astype(BF16)
        psel = _dot(ov, p_hi) + _dot(ov, p_mid) + _dot(ov, p_lo)
        tq = t0s[t] + lane
        cur = tq // SEL_BLOCK
        forced = (jidx == 0) | (jidx == cur) | (jidx == cur - 1)
        score = jnp.where(forced, FORCE_SCORE, jnp.where(jidx * SEL_BLOCK <= tq, psel, -1.0))
        rank = jnp.zeros(psel.shape, F32)
        for r in range(N_SELP):
            row = jnp.broadcast_to(score[r:r + 1, :], score.shape)
            rank = rank + jnp.where(jidx > r, jnp.where(row >= score, 1.0, 0.0), jnp.where(row > score, 1.0, 0.0))
        bias_t = jnp.where(rank < SEL_TOPK, 0.0, NEG_INF)
        for j in range(N_SELP):
            bias_ref[t, j] = jnp.broadcast_to(bias_t[j:j + 1, :], (SUBLANES, TQ))

    blk_per_tile = SEL_TK // SEL_BLOCK
    rep = SEL_BLOCK // SUBLANES
    acc_ref[...] = jnp.zeros(acc_ref.shape, F32)
    ms = [[jnp.full((1, PAIR), NEG_INF, F32) for _ in range(n_pairs)] for _ in tiles]
    ctx = {}

    def sel_scores(g, k, p):
        if p == 0:
            if g == 0 and k == 0:
                for t in tiles:
                    select_blocks(t)
            n_lo = qts[2 * g] // (SEL_TK // TQ) + 1
            hi = k >= n_lo
            t = 2 * g + hi.astype(jnp.int32)
            j = jnp.where(hi, k - n_lo, k)
            ks = pl.multiple_of(j * SEL_TK, SEL_TK)
            bt = jnp.concatenate(
                [bias_ref[t, j * blk_per_tile + b] for b in range(blk_per_tile) for _ in range(rep)], axis=0)
            qt = jnp.where(hi, qts[2 * g + 1], qts[2 * g])
            diag = j == qt // (SEL_TK // TQ)
            bt = bt + ctab_ref[jnp.where(diag, 1 + qt % (SEL_TK // TQ), 0)]
            bt = jnp.concatenate([bt, bt], axis=1)
            ctx[g, k] = (hi, t, j, ks, bt)
        hi, t, j, ks, bt = ctx[g, k]
        return _dot_nt(ksel_ref[0, pl.ds(ks, SEL_TK), :], qs_ref[t, p]) + bt

    def sel_finish(g, k, p, s):
        hi, t, j, ks, bt = ctx[g, k]
        lo_t, hi_t = 2 * g, 2 * g + 1
        vt = jnp.concatenate([vselt_ref[j * (SEL_TK // LANES) + b] for b in range(SEL_TK // LANES)], axis=1)
        m_new, acc_new = _softmax_step(s, jnp.where(hi, ms[hi_t][p], ms[lo_t][p]), acc_ref[t, p], vt)
        acc_ref[t, p] = acc_new
        ms[lo_t][p] = jnp.where(hi, ms[lo_t][p], m_new)
        ms[hi_t][p] = jnp.where(hi, m_new, ms[hi_t][p])

    sel_jobs = [(functools.partial(sel_scores, g, k, p), functools.partial(sel_finish, g, k, p))
                for k in range(_N_SEL_JOBS) for g in range(N_TILE_PAIRS) for p in range(n_pairs)]
    b_before = [job for t in tiles if t % 2 == 0 for job in b_jobs[t]]
    b_after = [job for t in tiles if t % 2 == 1 for job in b_jobs[t]]
    _run_skewed(jobs + b_before + sel_jobs + b_after, ATTN_AHEAD)

    for t in tiles:
        z = z_refs[t][0]
        for p in range(n_pairs):
            a = acc_ref[t, p]
            o = out_ref[t, p] + (gate(t, p, 1) / jnp.maximum(a[HEAD_DIM_A:HEAD_DIM_A + 1], TINY)) * a[0:HEAD_DIM_A]
            for h in (2 * p, 2 * p + 1):
                oh = o[:, (h % 2) * TQ:(h % 2 + 1) * TQ].T
                y_refs[t][0, :, h * LANES:(h + 1) * LANES] = (
                    oh * z[:, h * LANES:(h + 1) * LANES].astype(F32)).astype(y_refs[t].dtype)


def _mask_tables():
    lane = np.arange(TQ)[None, :]
    row = np.arange(WIN_A + TQ)[:, None]
    wtab = []
    for v in range(WIN_A // TQ + 1):
        t0 = v * TQ
        dpos = (t0 + lane) - (max(t0 - WIN_A, 0) + row)
        wtab.append(np.where((dpos >= 0) & (dpos < WIN_A), 0.0, NEG_INF))
    row = np.arange(SEL_TK)[:, None]
    ctab = [np.zeros((SEL_TK, TQ))]
    for odd in range(SEL_TK // TQ):
        ctab.append(np.where(row <= odd * TQ + lane, 0.0, NEG_INF))
    row = np.arange(WIN_B + TQ)[:, None]
    btab = []
    for v in range(WIN_B // TQ + 1):
        t0 = v * TQ
        dpos = (t0 + lane) - (max(t0 - WIN_B, 0) + row)
        btab.append(np.where((dpos >= 0) & (dpos < WIN_B), 0.0, NEG_INF))
    return [jnp.asarray(np.stack(t), F32) for t in (wtab, ctab, btab)]


def _attention(slabs, kc, vc, ov, sinks):
    B, S, _ = slabs[0].shape
    sl = lambda name: slabs[_slab(name)]
    n_cp = S // CMP_STRIDE
    n_qt = S // TQ
    n_steps = n_qt // N_TILES
    assert n_cp == LANES and S // SEL_BLOCK == N_SELP and (n_qt // 2) // (SEL_TK // TQ) * 2 + 1 == _N_SEL_JOBS
    assert N_KV_A == N_KV_B and GROUP_A * HEAD_DIM_A == GROUP_B * HEAD_DIM_B and N_TILE_PAIRS in (1, 2)
    wq = GROUP_A * HEAD_DIM_A
    cb = lambda name: _col(name) // LANES
    kv_spec = lambda name: pl.BlockSpec((1, S, LANES), lambda b, k, i, s, c=cb(name): (b, 0, c + k))
    whole = lambda a: pl.BlockSpec(a.shape, lambda b, k, i, s: (0,) * a.ndim)

    def tile_specs(width, c, per_kv=True):
        return [pl.BlockSpec((1, TQ, width),
                             lambda b, k, i, s, t=t: (b, _query_tile(t, i, n_qt), c + (k if per_kv else 0)))
                for t in range(N_TILES)]

    part = [_query_tile(t, 0, n_qt) // n_steps for t in range(N_TILES)]
    assert sorted(part) == list(range(N_TILES))
    y_specs = [pl.BlockSpec((1, TQ, wq), lambda b, k, i, s, t=t: (b, _query_tile(t, i, n_qt) - part[t] * n_steps, k))
               for t in range(N_TILES)]
    y_sds = jax.ShapeDtypeStruct((B, S // N_TILES, WIDTH_A), BF16)
    wtab, ctab, btab = _mask_tables()
    grid_spec = pltpu.PrefetchScalarGridSpec(
        num_scalar_prefetch=1,
        grid=(B, N_KV_A, n_steps),
        in_specs=[
            *tile_specs(wq, _col("qa") // wq),
            kv_spec("ksa"), kv_spec("vsa"), kv_spec("kwa"), kv_spec("vwa"),
            pl.BlockSpec((1, n_cp, LANES), lambda b, k, i, s: (k * B + b, 0, 0)),
            pl.BlockSpec((1, n_cp, LANES), lambda b, k, i, s: (k * B + b, 0, 0)),
            *tile_specs(LANES, cb("ga"), per_kv=False),
            *tile_specs(wq, _col("za") // wq),
            whole(ov), whole(wtab), whole(ctab),
            *tile_specs(wq, _col("qb") // wq),
            kv_spec("kb2"), pl.BlockSpec((1, S, LANES), lambda b, k, i, s: (b, 0, cb("vb"))),
            *tile_specs(wq, _col("zb") // wq),
            whole(btab),
        ],
        out_specs=y_specs + y_specs,
        scratch_shapes=[
            pltpu.VMEM((S // LANES, _VROWS_A, LANES), BF16),
            pltpu.VMEM((S // LANES, _VROWS_A, LANES), BF16),
            pltpu.VMEM((S // LANES, _VROWS_B, LANES), BF16),
            pltpu.VMEM((N_TILES, GROUP_A // 2, PAIR, HEAD_DIM_A), BF16),
            pltpu.VMEM((N_TILES, N_SELP, SUBLANES, TQ), F32),
            pltpu.VMEM((N_TILES, GROUP_A // 2, _VROWS_A, PAIR), F32),
            pltpu.VMEM((N_TILES, GROUP_A // 2, HEAD_DIM_A, PAIR), F32),
        ],
    )
    per_tile = lambda name: [sl(name)] * N_TILES
    outs = pl.pallas_call(
        _attn_kernel,
        grid_spec=grid_spec,
        out_shape=[y_sds] * (2 * N_TILES),
        compiler_params=pltpu.CompilerParams(
            dimension_semantics=("arbitrary", "arbitrary", "arbitrary"), vmem_limit_bytes=VMEM_LIMIT),
        name="attn",
    )(sinks, *per_tile("qa"), sl("ksa"), sl("vsa"), sl("kwa"), sl("vwa"), kc, vc,
      *per_tile("ga"), *per_tile("za"), ov, wtab, ctab,
      *per_tile("qb"), sl("kb2"), sl("vb"), *per_tile("zb"), btab)
    in_order = lambda ys: [ys[part.index(n)] for n in range(N_TILES)]
    return in_order(outs[:N_TILES]), in_order(outs[N_TILES:])


def _out_proj_kernel(n_parts, *refs):
    ya_refs, yb_refs = refs[:n_parts], refs[n_parts:2 * n_parts]
    wa_ref, wb_ref, x_ref, g_ref, o_ref = refs[2 * n_parts:]
    part = pl.program_id(0) % n_parts

    for n in range(n_parts):
        @pl.when(part == n)
        def _(n=n):
            r = x_ref[...] + _dot(ya_refs[n][...], wa_ref[...]) + _dot(yb_refs[n][...], wb_ref[...])
            ms = jnp.mean(r * r, axis=-1, keepdims=True)
            o_ref[...] = (r * lax.rsqrt(ms + RMS_EPS)) * g_ref[...]


def _out_proj(ya_parts, yb_parts, w_out, x2, g, S):
    M = x2.shape[0]
    n_parts = len(ya_parts)
    tm = S // n_parts
    n_seq = M // S
    y_spec = lambda w, n: pl.BlockSpec(
        (tm, w), lambda i: (jnp.minimum((i + n_parts - 1 - n) // n_parts, n_seq - 1), 0))
    return pl.pallas_call(
        functools.partial(_out_proj_kernel, n_parts),
        grid=(M // tm,),
        in_specs=[
            *[y_spec(WIDTH_A, n) for n in range(n_parts)], *[y_spec(WIDTH_B, n) for n in range(n_parts)],
            pl.BlockSpec((WIDTH_A, D_MODEL), lambda i: (0, 0)),
            pl.BlockSpec((WIDTH_B, D_MODEL), lambda i: (WIDTH_A // WIDTH_B, 0)),
            pl.BlockSpec((tm, D_MODEL), lambda i: (i, 0)),
            pl.BlockSpec((1, D_MODEL), lambda i: (0, 0)),
        ],
        out_specs=pl.BlockSpec((tm, D_MODEL), lambda i: (i, 0)),
        out_shape=jax.ShapeDtypeStruct((M, D_MODEL), F32),
        compiler_params=pltpu.CompilerParams(
            dimension_semantics=("arbitrary",), vmem_limit_bytes=VMEM_LIMIT_OUT_PROJ),
        name="out_proj",
    )(*ya_parts, *yb_parts, w_out, w_out, x2, g)


def _overlap_matrix(n_cp):
    c_start = np.arange(n_cp) * CMP_STRIDE
    j_start = np.arange(N_SELP) * SEL_BLOCK
    ov = (c_start[None, :] < j_start[:, None] + SEL_BLOCK) & (c_start[None, :] + CMP_BLOCK > j_start[:, None])
    return jnp.asarray(ov, BF16)


def kernel(x, w_in, cmp_k_w1, cmp_k_w2, cmp_v_w1, cmp_v_w2, cmp_k_pos, cmp_v_pos, sinks, w_out, norm_g, final_g):
    B, S, D = x.shape
    assert D == D_MODEL and w_in.shape[0] == 1
    x2 = x.reshape(B * S, D)

    w_all = _prep_w_in(w_in)
    tabs = _rope_tables(S)
    slabs, strided = _in_proj(x2, norm_g[0].reshape(1, D), w_all, tabs, S)
    slabs = [p.reshape(B, S, IN_TN) for p in slabs]

    n_cp = S // CMP_STRIDE
    tk, tv = [strided[n].reshape(N_KV_A * B * n_cp, _HALF) for n in ("kca", "vca")]
    flat_pos = lambda p: jnp.broadcast_to(p.reshape(1, CMP_BLOCK * HEAD_DIM_A), (8, CMP_BLOCK * HEAD_DIM_A))
    kc, vc = _compress(tk, tv, cmp_k_w1[0].astype(BF16), cmp_v_w1[0].astype(BF16),
                       cmp_k_w2[0].astype(BF16), cmp_v_w2[0].astype(BF16),
                       flat_pos(cmp_k_pos[0]).astype(BF16), flat_pos(cmp_v_pos[0]).astype(BF16))
    kc, vc = [c.reshape(N_KV_A * B, n_cp, HEAD_DIM_A) for c in (kc, vc)]

    ya_parts, yb_parts = _attention(slabs, kc, vc, _overlap_matrix(n_cp), sinks[0])
    flat = lambda ys: [y.reshape(B * y.shape[1], y.shape[2]) for y in ys]
    out = _out_proj(flat(ya_parts), flat(yb_parts), w_out[0].astype(BF16), x2, final_g.reshape(1, D), S)
    return out.reshape(B, S, D)
```

```python
import functools
import math

import numpy as np
import jax
import jax.numpy as jnp
from jax import lax
from jax.experimental import pallas as pl
from jax.experimental.pallas import tpu as pltpu

F32 = jnp.float32
BF16 = jnp.bfloat16

D_MODEL = 2048
ROPE_THETA = 10000.0
RMS_EPS = 1e-6
NEG_INF = -1e30
TINY = 1e-30
LOG2E = math.log2(math.e)

WIDTH_A = 1024
HEAD_DIM_A = 128
N_KV_A = 2
GROUP_A = 4
KV_A = N_KV_A * HEAD_DIM_A
CMP_BLOCK = 32
CMP_STRIDE = 16
CMP_HIDDEN = 256
SEL_BLOCK = 64
SEL_TOPK = 16
WIN_A = 512
FORCE_SCORE = 1e4

WIDTH_B = 1024
HEAD_DIM_B = 64
N_KV_B = 2
GROUP_B = 8
KV_B = N_KV_B * HEAD_DIM_B
WIN_B = 128

IN_SIZES = (WIDTH_A, KV_A, KV_A, KV_A, KV_A, KV_A, KV_A, WIDTH_A, 3 * 8,
            WIDTH_B, KV_B, KV_B, WIDTH_B)

LANES = 128
SUBLANES = 8
VMEM_BYTES = 64 * 1024 * 1024
VMEM_LIMIT = VMEM_BYTES * 3 // 4
VMEM_LIMIT_OUT_PROJ = VMEM_BYTES * 7 // 8
CHUNK = 256

EP_NONE, EP_ROPE128, EP_ROPE64, EP_SILU, EP_SIGMOID, EP_ROPE128_Q, EP_ROPE64_Q = range(7)

_PROJ_LAYOUT = (
    ("qa", 8, EP_ROPE128_Q), ("kca", 2, EP_ROPE128), ("ksa", 2, EP_ROPE128), ("kwa", 2, EP_ROPE128),
    ("vca", 2, EP_NONE),
    ("vsa", 2, EP_NONE), ("vwa", 2, EP_NONE), ("za", 8, EP_SILU), ("kb2", 2, EP_ROPE64),
    ("ga", 1, EP_SIGMOID), ("vb", 1, EP_NONE),
    ("zb", 8, EP_SILU), ("qb", 8, EP_ROPE64_Q),
)
_UNIT_KINDS = tuple(k for _, n, k in _PROJ_LAYOUT for _ in range(n))
_UNIT_START = {}
_c = 0
for _name, _n, _k in _PROJ_LAYOUT:
    _UNIT_START[_name] = _c
    _c += _n
D_PROJ = _c * LANES
IN_TN = 2048
_IN_UPT = IN_TN // LANES
N_SLABS = D_PROJ // IN_TN


def _slab(name):
    return _UNIT_START[name] // _IN_UPT


def _col(name):
    return (_UNIT_START[name] % _IN_UPT) * LANES


def _dot(a, b):
    return jnp.dot(a, b, preferred_element_type=F32)


def _dot_nt(a, b):
    return lax.dot_general(a, b, (((1,), (1,)), ((), ())), preferred_element_type=F32)


IN_TM_NORM = 512
IN_TM = 1024
IN_SUB = 256

_TAB_GROUP = {EP_ROPE128: "r128", EP_ROPE128_Q: "r128q", EP_ROPE64: "r64", EP_ROPE64_Q: "r64q"}
_TAB_SIZE = {"r128": 2, "r128q": 2, "r64": 3, "r64q": 3}


def _slab_tab_groups(kinds):
    return tuple(dict.fromkeys(_TAB_GROUP[k] for k in kinds if k in _TAB_GROUP))


def _in_proj_kernel(kinds, with_norm, seq_len, strided_chunks, *refs):
    groups = _slab_tab_groups(kinds)
    refs = list(refs)
    if with_norm:
        x_ref, g_ref = refs.pop(0), refs.pop(0)
    else:
        h_ref = refs.pop(0)
    w_ref = refs.pop(0)
    tab_ref = refs.pop(0) if groups else None
    o_ref = refs.pop(0)
    if with_norm:
        h_ref = refs.pop(0)
        x = x_ref[...]
        ms = jnp.mean(x * x, axis=-1, keepdims=True)
        h_ref[...] = ((x * lax.rsqrt(ms + RMS_EPS)) * g_ref[...]).astype(BF16)
    t_refs = {c: refs.pop(0) for c in strided_chunks}
    if strided_chunks:
        stage_ref = refs.pop(0)
    tm = h_ref.shape[0]
    sub_rows = tm if with_norm else IN_SUB
    pos0 = (pl.program_id(0) % (seq_len // tm)) * tm
    base, n = {}, 0
    for grp in groups:
        base[grp] = n
        n += _TAB_SIZE[grp]

    def epilogue(kind, r, a):
        if kind == EP_NONE:
            return a
        if kind == EP_SILU:
            return a * jax.nn.sigmoid(a)
        if kind == EP_SIGMOID:
            return jax.nn.sigmoid(a)
        t = base[_TAB_GROUP[kind]]
        tab = lambda k: tab_ref[t + k, pl.ds(pl.multiple_of(pos0 + r * sub_rows, sub_rows), sub_rows), :]
        if kind in (EP_ROPE128, EP_ROPE128_Q):
            return a * tab(0) + pltpu.roll(a, 64, 1) * tab(1)
        return a * tab(0) + pltpu.roll(a, 96, 1) * tab(1) + pltpu.roll(a, 32, 1) * tab(2)

    n_grp = sub_rows // CMP_STRIDE

    subs = CHUNK // LANES

    def finish(r, c, acc):
        rows = slice(r * sub_rows, (r + 1) * sub_rows)
        for sub in range(subs):
            cols = slice(c * CHUNK + sub * LANES, c * CHUNK + (sub + 1) * LANES)
            val = epilogue(kinds[c * subs + sub], r, acc[:, sub * LANES:(sub + 1) * LANES])
            o_ref[rows, cols] = val.astype(o_ref.dtype)
            if c in t_refs:
                slot = strided_chunks.index(c) * subs + sub
                stage_ref[slot] = val
                for k in range(CMP_STRIDE):
                    t_refs[c][sub, r * n_grp:(r + 1) * n_grp, k * LANES:(k + 1) * LANES] = (
                        stage_ref[slot, pl.ds(k, n_grp, stride=CMP_STRIDE), :].astype(t_refs[c].dtype))

    jobs = []
    for r in range(tm // sub_rows):
        for c in range(len(kinds) // subs):
            jobs.append((lambda r=r, c=c: _dot_nt(h_ref[r * sub_rows:(r + 1) * sub_rows, :],
                                                  w_ref[c * CHUNK:(c + 1) * CHUNK, :]),
                         functools.partial(finish, r, c)))
    _run_skewed(jobs)


def _rope_tables(S):
    def cs(d):
        inv = ROPE_THETA ** (-np.arange(0, d, 2, dtype=np.float64) / d)
        ang = np.arange(S, dtype=np.float64)[:, None] * inv[None, :]
        return np.cos(ang), np.sin(ang)

    c128, s128 = cs(HEAD_DIM_A)
    c64, s64 = cs(HEAD_DIM_B)
    z32 = np.zeros_like(s64)
    t128 = [np.concatenate([c128, c128], axis=1), np.concatenate([-s128, s128], axis=1)]
    t64 = [np.concatenate([c64, c64, c64, c64], axis=1),
           np.concatenate([-s64, z32, -s64, z32], axis=1),
           np.concatenate([z32, s64, z32, s64], axis=1)]
    qa = HEAD_DIM_A ** -0.5 * LOG2E
    qb = HEAD_DIM_B ** -0.5 * LOG2E
    groups = {"r128": t128, "r64": t64, "r128q": [t * qa for t in t128], "r64q": [t * qb for t in t64]}
    return {name: [t.astype(np.float32) for t in ts] for name, ts in groups.items()}


W_PREP_TR = 256


def _w_in_pieces():
    offs = np.cumsum((0,) + IN_SIZES)
    names = ("qa", "kca", "vca", "ksa", "vsa", "kwa", "vwa", "za", "ga", "qb", "kb", "vb", "zb")
    start = {n: int(offs[i]) for i, n in enumerate(names)}
    width = dict(zip(names, IN_SIZES))
    n_g = 3 * GROUP_A
    halves = lambda n: [(start[n] + i * HEAD_DIM_B, HEAD_DIM_B) for i in (0, 0, 1, 1)]
    pieces = {n: [(start[n], width[n])]
              for n in ("qa", "kca", "ksa", "kwa", "vca", "vsa", "vwa", "za", "zb", "qb", "vb")}
    pad = LANES // N_KV_A - n_g
    pieces["ga"] = [(start["ga"], n_g), (None, pad), (start["ga"] + n_g, n_g), (None, pad)]
    pieces["kb2"] = halves("kb")
    return [p for name, _, _ in _PROJ_LAYOUT for p in pieces[name]]


def _w_prep_kernel(w_ref, o_ref):
    row = 0
    parts, filled = [], 0
    for src, wd in _w_in_pieces():
        done = 0
        while done < wd:
            take = min(wd - done, LANES - filled)
            if src is None:
                parts.append(jnp.zeros((take, W_PREP_TR), F32))
            else:
                parts.append(w_ref[src + done:src + done + take, :])
            done += take
            filled += take
            if filled == LANES:
                blk = parts[0] if len(parts) == 1 else jnp.concatenate(parts, axis=0)
                o_ref[row:row + LANES, :] = blk.astype(o_ref.dtype)
                row += LANES
                parts, filled = [], 0
    assert row == D_PROJ and not parts


def _prep_w_in(w_in):
    wt = w_in[0].T
    d_in = wt.shape[0]
    return pl.pallas_call(
        _w_prep_kernel,
        grid=(D_MODEL // W_PREP_TR,),
        in_specs=[pl.BlockSpec((d_in, W_PREP_TR), lambda i: (0, i))],
        out_specs=pl.BlockSpec((D_PROJ, W_PREP_TR), lambda i: (0, i)),
        out_shape=jax.ShapeDtypeStruct((D_PROJ, D_MODEL), BF16),
        compiler_params=pltpu.CompilerParams(dimension_semantics=("arbitrary",)),
        name="w_prep",
    )(wt)


def _in_proj(x2, g, w_all, tabs, S):
    M = x2.shape[0]
    params = pltpu.CompilerParams(dimension_semantics=("arbitrary",), vmem_limit_bytes=VMEM_LIMIT)
    out_sds = jax.ShapeDtypeStruct((M, IN_TN), BF16)
    slabs, strided, h = [], {}, None
    for j in range(N_SLABS):
        kinds = _UNIT_KINDS[j * _IN_UPT:(j + 1) * _IN_UPT]
        with_norm = j == 0
        tm = IN_TM_NORM if with_norm else IN_TM
        groups = _slab_tab_groups(kinds)
        act_spec = pl.BlockSpec((tm, D_MODEL), lambda i: (i, 0))
        in_specs = [act_spec, pl.BlockSpec((1, D_MODEL), lambda i: (0, 0))] if with_norm else [act_spec]
        args = [x2, g] if with_norm else [h]
        in_specs.append(pl.BlockSpec((IN_TN, D_MODEL), lambda i, j=j: (j, 0)))
        args.append(w_all)
        if groups:
            tab = jnp.asarray(np.stack([t for grp in groups for t in tabs[grp]]))
            in_specs.append(pl.BlockSpec(tab.shape, lambda i: (0, 0, 0)))
            args.append(tab)
        out_specs = [pl.BlockSpec((tm, IN_TN), lambda i: (i, 0))]
        out_shape = [out_sds]
        if with_norm:
            out_specs.append(act_spec)
            out_shape.append(jax.ShapeDtypeStruct((M, D_MODEL), BF16))
        strided_names = [name for name in ("kca", "vca") if _slab(name) == j]
        strided_chunks = tuple(_col(name) // CHUNK for name in strided_names)
        scratch = []
        for _ in strided_chunks:
            out_specs.append(pl.BlockSpec((N_KV_A, tm // CMP_STRIDE, _HALF), lambda i: (0, i, 0)))
            out_shape.append(jax.ShapeDtypeStruct((N_KV_A, M // CMP_STRIDE, _HALF), BF16))
        if strided_chunks:
            scratch.append(pltpu.VMEM((len(strided_chunks) * N_KV_A, tm if with_norm else IN_SUB, HEAD_DIM_A), F32))
        res = list(pl.pallas_call(
            functools.partial(_in_proj_kernel, kinds, with_norm, S, strided_chunks),
            grid=(M // tm,),
            in_specs=in_specs,
            out_specs=out_specs,
            out_shape=out_shape,
            scratch_shapes=scratch,
            compiler_params=params,
            name=f"in_proj_{j}",
        )(*args))
        slabs.append(res.pop(0))
        if with_norm:
            h = res.pop(0)
        for name in strided_names:
            strided[name] = res.pop(0)
    return slabs, strided


CMP_TM = 256
_HALF = CMP_STRIDE * HEAD_DIM_A


def _compress_kernel(tk_ref, tv_ref, w1k_ref, w1v_ref, w2k_ref, w2v_ref, posk_ref, posv_ref, kc_ref, vc_ref,
                     pb_ref):
    @pl.when(pl.program_id(0) == 0)
    def _():
        pb_ref[0] = _dot(posk_ref[...], w1k_ref[...])
        pb_ref[1] = _dot(posv_ref[...], w1v_ref[...])

    for n, (t_ref, w1_ref, w2_ref, o_ref) in enumerate(((tk_ref, w1k_ref, w2k_ref, kc_ref),
                                                        (tv_ref, w1v_ref, w2v_ref, vc_ref))):
        t = t_ref[...]
        u = _dot(t, w1_ref[0:_HALF, :])
        v = _dot(t, w1_ref[_HALF:2 * _HALF, :])
        pre = u + pltpu.roll(v, CMP_TM - 1, 0) + pb_ref[n, 0:1]
        hid = pre * jax.nn.sigmoid(pre)
        out = _dot(hid.astype(BF16), w2_ref[...])
        row = lax.broadcasted_iota(jnp.int32, out.shape, 0)
        n_cp = LANES
        out = jnp.where((row & (n_cp - 1)) == n_cp - 1, 0.0, out)
        o_ref[...] = out.astype(o_ref.dtype)


def _compress(tk, tv, w1k, w1v, w2k, w2v, posk, posv):
    R = tk.shape[0]
    rows = pl.BlockSpec((CMP_TM, _HALF), lambda i: (i, 0))
    whole = lambda a: pl.BlockSpec(a.shape, lambda i: (0, 0))
    o_spec = pl.BlockSpec((CMP_TM, HEAD_DIM_A), lambda i: (i, 0))
    o_sds = jax.ShapeDtypeStruct((R, HEAD_DIM_A), BF16)
    return pl.pallas_call(
        _compress_kernel,
        grid=(R // CMP_TM,),
        in_specs=[rows, rows, whole(w1k), whole(w1v), whole(w2k), whole(w2v), whole(posk), whole(posv)],
        out_specs=(o_spec, o_spec),
        out_shape=(o_sds, o_sds),
        scratch_shapes=[pltpu.VMEM((2, SUBLANES, CMP_HIDDEN), F32)],
        compiler_params=pltpu.CompilerParams(dimension_semantics=("arbitrary",)),
        name="compress",
    )(tk, tv, w1k, w1v, w2k, w2v, posk, posv)


TQ = 128
ATTN_AHEAD = 8
ONES_ROWS = 16


def _transpose_values(v_ref, vt_ref, n_rows, second_half=None):
    n_blk = vt_ref.shape[0]
    for blk in range(n_blk):
        vt = v_ref[0, blk * LANES:(blk + 1) * LANES, :].astype(F32).T
        if second_half is not None:
            vt = jnp.where(second_half == 0, vt[0:n_rows], vt[n_rows:2 * n_rows])
        vt_ref[blk, 0:n_rows, :] = vt[0:n_rows].astype(vt_ref.dtype)
        vt_ref[blk, n_rows:n_rows + ONES_ROWS, :] = jnp.ones((ONES_ROWS, LANES), vt_ref.dtype)


def _run_skewed(jobs, ahead=1):
    pending = [job[0]() for job in jobs[:ahead]]
    for i, (_, finish) in enumerate(jobs):
        if i + ahead < len(jobs):
            pending.append(jobs[i + ahead][0]())
        finish(pending[i])
        pending[i] = None


def _softmax_step(s, m, acc, vt):
    m_new = jnp.maximum(m, jnp.max(s, axis=0, keepdims=True))
    p = jnp.exp2(s - m_new).astype(BF16)
    return m_new, jnp.exp2(m - m_new) * acc + _dot(vt, p)


SEL_TK = 256
N_SELP = 32
PAIR = 2 * TQ
N_TILE_PAIRS = 2
N_TILES = 2 * N_TILE_PAIRS
_VROWS_A = HEAD_DIM_A + ONES_ROWS
_VROWS_B = HEAD_DIM_B + ONES_ROWS
_WIN_TILES = (256, 256, 128)
_N_SEL_JOBS = 9


def _query_tile(t, i, n_qt):
    low = i if t // 2 == 0 else n_qt // 2 - 1 - i
    return low if t % 2 == 0 else n_qt - 1 - low


def _attn_kernel(*refs):
    refs = list(refs)
    take = lambda n: [refs.pop(0) for _ in range(n)]
    sink_ref, = take(1)
    q_refs = take(N_TILES)
    ksel_ref, vsel_ref, kwin_ref, vwin_ref, kc_ref, vc_ref = take(6)
    g_refs, z_refs = take(N_TILES), take(N_TILES)
    ov_ref, wtab_ref, ctab_ref = take(3)
    qb_refs = take(N_TILES)
    kb_ref, vb_ref = take(2)
    zb_refs = take(N_TILES)
    btab_ref, = take(1)
    y_refs, yb_refs = take(N_TILES), take(N_TILES)
    vselt_ref, vwint_ref, vbt_ref, qs_ref, bias_ref, acc_ref, out_ref = refs

    kv = pl.program_id(1)
    i = pl.program_id(2)
    n_qt = N_TILES * pl.num_programs(2)
    n_pairs = GROUP_A // 2
    tiles = range(N_TILES)
    qts = [_query_tile(t, i, n_qt) for t in tiles]
    t0s = [qt * TQ for qt in qts]

    @pl.when(i == 0)
    def _():
        _transpose_values(vsel_ref, vselt_ref, HEAD_DIM_A)
        _transpose_values(vwin_ref, vwint_ref, HEAD_DIM_A)
        _transpose_values(vb_ref, vbt_ref, HEAD_DIM_B, kv)

    lane = lax.broadcasted_iota(jnp.int32, (1, TQ), 1)
    lane2 = jnp.concatenate([lane, lane], axis=1)
    qpair, gts = [], []
    for t in tiles:
        q = q_refs[t][0]
        pairs = [jnp.concatenate([q[:, h * LANES:(h + 1) * LANES] for h in (2 * p, 2 * p + 1)], axis=0)
                 for p in range(n_pairs)]
        for p in range(n_pairs):
            qs_ref[t, p] = pairs[p]
        qpair.append(pairs)
        gt = g_refs[t][0].astype(F32).T
        gts.append(jnp.where(kv == 0, gt[0:LANES // 2], gt[LANES // 2:LANES]))

    def gate(t, p, branch):
        return jnp.concatenate(
            [gts[t][3 * h + branch:3 * h + branch + 1] for h in (2 * p, 2 * p + 1)], axis=1)

    kc = kc_ref[0]
    vct = vc_ref[0].astype(F32).T.astype(BF16)
    cend = lax.broadcasted_iota(jnp.int32, (LANES, PAIR), 0) * CMP_STRIDE + (CMP_BLOCK - 1)
    psums = [[] for _ in tiles]

    def cmp_scores(t, p):
        return jnp.where(cend <= t0s[t] + lane2, _dot_nt(kc, qpair[t][p]), NEG_INF)

    def cmp_finish(t, p, s):
        m = jnp.max(s, axis=0, keepdims=True)
        e = jnp.where(cend <= t0s[t] + lane2, jnp.exp2(s - m), 0.0)
        pc = e / jnp.maximum(jnp.sum(e, axis=0, keepdims=True), TINY)
        psums[t].append(pc[:, 0:TQ] + pc[:, TQ:PAIR])
        out_ref[t, p] = gate(t, p, 0) * _dot(vct, pc.astype(BF16))

    ksts = [pl.multiple_of(jnp.maximum(t0 - WIN_A, 0), TQ) for t0 in t0s]
    n_win = sum(_WIN_TILES)

    win_var = [jnp.minimum(qt, WIN_A // TQ) for qt in qts]

    def win_scores(t, p):
        out, off = [], 0
        for tk in _WIN_TILES:
            kt = kwin_ref[0, pl.ds(pl.multiple_of(ksts[t] + off, LANES), tk), :]
            bias = wtab_ref[win_var[t], off:off + tk, :]
            out.append(_dot_nt(kt, qpair[t][p]) + jnp.concatenate([bias, bias], axis=1))
            off += tk
        return out

    def win_finish(t, p, ss):
        m = functools.reduce(jnp.maximum, [jnp.max(s, axis=0, keepdims=True) for s in ss])
        pw = jnp.concatenate([jnp.exp2(s - m).astype(BF16) for s in ss], axis=0)
        vt = jnp.concatenate([vwint_ref[ksts[t] // LANES + b] for b in range(n_win // LANES)], axis=1)
        a = _dot(vt, pw)
        out_ref[t, p] = out_ref[t, p] + (
            gate(t, p, 2) / jnp.maximum(a[HEAD_DIM_A:HEAD_DIM_A + 1], TINY)) * a[0:HEAD_DIM_A]

    jobs = [(functools.partial(cmp_scores, t, p), functools.partial(cmp_finish, t, p))
            for t in tiles for p in range(n_pairs)]
    jobs += [(functools.partial(win_scores, t, p), functools.partial(win_finish, t, p))
             for t in tiles for p in range(n_pairs)]

    nb_keys = WIN_B + TQ
    left = lax.broadcasted_iota(jnp.int32, (TQ, LANES), 1) < HEAD_DIM_B
    zero = jnp.zeros((TQ, LANES), qb_refs[0].dtype)
    kbs = [pl.multiple_of(jnp.maximum(t0 - WIN_B, 0), TQ) for t0 in t0s]
    b_var = [jnp.minimum(qt, WIN_B // TQ) for qt in qts]

    def b_sink(c):
        return jnp.concatenate(
            [jnp.full((1, TQ), sink_ref[kv * GROUP_B + 2 * c + h] * LOG2E, F32) for h in range(2)], axis=1)

    def b_scores(t, c):
        pair = qb_refs[t][0, :, c * LANES:(c + 1) * LANES]
        qp = jnp.concatenate([jnp.where(left, pair, zero), jnp.where(left, zero, pair)], axis=0)
        bias = btab_ref[b_var[t]]
        return _dot_nt(kb_ref[0, pl.ds(kbs[t], nb_keys), :], qp) + jnp.concatenate([bias, bias], axis=1)

    def b_finish(t, c, s):
        vt = jnp.concatenate([vbt_ref[kbs[t] // LANES + b] for b in range(nb_keys // LANES)], axis=1)
        sink = b_sink(c)
        m = jnp.maximum(jnp.max(s, axis=0, keepdims=True), sink)
        p = jnp.exp2(s - m).astype(BF16)
        a = _dot(vt, p)
        den = jnp.maximum(a[HEAD_DIM_B:HEAD_DIM_B + 1] + jnp.exp2(sink - m), TINY)
        o = a[0:HEAD_DIM_B] / den
        o2 = jnp.concatenate([o[:, 0:TQ], o[:, TQ:PAIR]], axis=0)
        z = zb_refs[t][0, :, c * LANES:(c + 1) * LANES]
        yb_refs[t][0, :, c * LANES:(c + 1) * LANES] = (o2.T * z.astype(F32)).astype(yb_refs[t].dtype)

    b_jobs = [[(functools.partial(b_scores, t, c), functools.partial(b_finish, t, c))
               for c in range(GROUP_B // 2)] for t in tiles]

    def select_blocks(t):
        ov = ov_ref[...]
        jidx = lax.broadcasted_iota(jnp.int32, (N_SELP, TQ), 0)
        psum = psums[t][0] + psums[t][1]
        p_hi = psum.astype(BF16)
        r1 = psum - p_hi.astype(F32)
        p_mid = r1.astype(BF16)
        p_lo = (r1 - p_mid.astype(F32)).astype(BF16)
        psel = _dot(ov, p_hi) + _dot(ov, p_mid) + _dot(ov, p_lo)
        tq = t0s[t] + lane
        cur = tq // SEL_BLOCK
        forced = (jidx == 0) | (jidx == cur) | (jidx == cur - 1)
        score = jnp.where(forced, FORCE_SCORE, jnp.where(jidx * SEL_BLOCK <= tq, psel, -1.0))
        rank = jnp.zeros(psel.shape, F32)
        for r in range(N_SELP):
            row = jnp.broadcast_to(score[r:r + 1, :], score.shape)
            rank = rank + jnp.where(jidx > r, jnp.where(row >= score, 1.0, 0.0), jnp.where(row > score, 1.0, 0.0))
        bias_t = jnp.where(rank < SEL_TOPK, 0.0, NEG_INF)
        for j in range(N_SELP):
            bias_ref[t, j] = jnp.broadcast_to(bias_t[j:j + 1, :], (SUBLANES, TQ))

    blk_per_tile = SEL_TK // SEL_BLOCK
    rep = SEL_BLOCK // SUBLANES
    acc_ref[...] = jnp.zeros(acc_ref.shape, F32)
    ms = [[jnp.full((1, PAIR), NEG_INF, F32) for _ in range(n_pairs)] for _ in tiles]
    ctx = {}

    def sel_scores(g, k, p):
        if p == 0:
            if g == 0 and k == 0:
                for t in tiles:
                    select_blocks(t)
            n_lo = qts[2 * g] // (SEL_TK // TQ) + 1
            hi = k >= n_lo
            t = 2 * g + hi.astype(jnp.int32)
            j = jnp.where(hi, k - n_lo, k)
            ks = pl.multiple_of(j * SEL_TK, SEL_TK)
            bt = jnp.concatenate(
                [bias_ref[t, j * blk_per_tile + b] for b in range(blk_per_tile) for _ in range(rep)], axis=0)
            qt = jnp.where(hi, qts[2 * g + 1], qts[2 * g])
            diag = j == qt // (SEL_TK // TQ)
            bt = bt + ctab_ref[jnp.where(diag, 1 + qt % (SEL_TK // TQ), 0)]
            bt = jnp.concatenate([bt, bt], axis=1)
            ctx[g, k] = (hi, t, j, ks, bt)
        hi, t, j, ks, bt = ctx[g, k]
        return _dot_nt(ksel_ref[0, pl.ds(ks, SEL_TK), :], qs_ref[t, p]) + bt

    def sel_finish(g, k, p, s):
        hi, t, j, ks, bt = ctx[g, k]
        lo_t, hi_t = 2 * g, 2 * g + 1
        vt = jnp.concatenate([vselt_ref[j * (SEL_TK // LANES) + b] for b in range(SEL_TK // LANES)], axis=1)
        m_new, acc_new = _softmax_step(s, jnp.where(hi, ms[hi_t][p], ms[lo_t][p]), acc_ref[t, p], vt)
        acc_ref[t, p] = acc_new
        ms[lo_t][p] = jnp.where(hi, ms[lo_t][p], m_new)
        ms[hi_t][p] = jnp.where(hi, m_new, ms[hi_t][p])

    sel_jobs = [(functools.partial(sel_scores, g, k, p), functools.partial(sel_finish, g, k, p))
                for k in range(_N_SEL_JOBS) for g in range(N_TILE_PAIRS) for p in range(n_pairs)]
    b_before = [job for t in tiles if t % 2 == 0 for job in b_jobs[t]]
    b_after = [job for t in tiles if t % 2 == 1 for job in b_jobs[t]]
    _run_skewed(jobs + b_before + sel_jobs + b_after, ATTN_AHEAD)

    for t in tiles:
        z = z_refs[t][0]
        for p in range(n_pairs):
            a = acc_ref[t, p]
            o = out_ref[t, p] + (gate(t, p, 1) / jnp.maximum(a[HEAD_DIM_A:HEAD_DIM_A + 1], TINY)) * a[0:HEAD_DIM_A]
            for h in (2 * p, 2 * p + 1):
                oh = o[:, (h % 2) * TQ:(h % 2 + 1) * TQ].T
                y_refs[t][0, :, h * LANES:(h + 1) * LANES] = (
                    oh * z[:, h * LANES:(h + 1) * LANES].astype(F32)).astype(y_refs[t].dtype)


def _mask_tables():
    lane = np.arange(TQ)[None, :]
    row = np.arange(WIN_A + TQ)[:, None]
    wtab = []
    for v in range(WIN_A // TQ + 1):
        t0 = v * TQ
        dpos = (t0 + lane) - (max(t0 - WIN_A, 0) + row)
        wtab.append(np.where((dpos >= 0) & (dpos < WIN_A), 0.0, NEG_INF))
    row = np.arange(SEL_TK)[:, None]
    ctab = [np.zeros((SEL_TK, TQ))]
    for odd in range(SEL_TK // TQ):
        ctab.append(np.where(row <= odd * TQ + lane, 0.0, NEG_INF))
    row = np.arange(WIN_B + TQ)[:, None]
    btab = []
    for v in range(WIN_B // TQ + 1):
        t0 = v * TQ
        dpos = (t0 + lane) - (max(t0 - WIN_B, 0) + row)
        btab.append(np.where((dpos >= 0) & (dpos < WIN_B), 0.0, NEG_INF))
    return [jnp.asarray(np.stack(t), F32) for t in (wtab, ctab, btab)]


def _attention(slabs, kc, vc, ov, sinks):
    B, S, _ = slabs[0].shape
    sl = lambda name: slabs[_slab(name)]
    n_cp = S // CMP_STRIDE
    n_qt = S // TQ
    n_steps = n_qt // N_TILES
    assert n_cp == LANES and S // SEL_BLOCK == N_SELP and (n_qt // 2) // (SEL_TK // TQ) * 2 + 1 == _N_SEL_JOBS
    assert N_KV_A == N_KV_B and GROUP_A * HEAD_DIM_A == GROUP_B * HEAD_DIM_B and N_TILE_PAIRS in (1, 2)
    wq = GROUP_A * HEAD_DIM_A
    cb = lambda name: _col(name) // LANES
    kv_spec = lambda name: pl.BlockSpec((1, S, LANES), lambda b, k, i, s, c=cb(name): (b, 0, c + k))
    whole = lambda a: pl.BlockSpec(a.shape, lambda b, k, i, s: (0,) * a.ndim)

    def tile_specs(width, c, per_kv=True):
        return [pl.BlockSpec((1, TQ, width),
                             lambda b, k, i, s, t=t: (b, _query_tile(t, i, n_qt), c + (k if per_kv else 0)))
                for t in range(N_TILES)]

    part = [_query_tile(t, 0, n_qt) // n_steps for t in range(N_TILES)]
    assert sorted(part) == list(range(N_TILES))
    y_specs = [pl.BlockSpec((1, TQ, wq), lambda b, k, i, s, t=t: (b, _query_tile(t, i, n_qt) - part[t] * n_steps, k))
               for t in range(N_TILES)]
    y_sds = jax.ShapeDtypeStruct((B, S // N_TILES, WIDTH_A), BF16)
    wtab, ctab, btab = _mask_tables()
    grid_spec = pltpu.PrefetchScalarGridSpec(
        num_scalar_prefetch=1,
        grid=(B, N_KV_A, n_steps),
        in_specs=[
            *tile_specs(wq, _col("qa") // wq),
            kv_spec("ksa"), kv_spec("vsa"), kv_spec("kwa"), kv_spec("vwa"),
            pl.BlockSpec((1, n_cp, LANES), lambda b, k, i, s: (k * B + b, 0, 0)),
            pl.BlockSpec((1, n_cp, LANES), lambda b, k, i, s: (k * B + b, 0, 0)),
            *tile_specs(LANES, cb("ga"), per_kv=False),
            *tile_specs(wq, _col("za") // wq),
            whole(ov), whole(wtab), whole(ctab),
            *tile_specs(wq, _col("qb") // wq),
            kv_spec("kb2"), pl.BlockSpec((1, S, LANES), lambda b, k, i, s: (b, 0, cb("vb"))),
            *tile_specs(wq, _col("zb") // wq),
            whole(btab),
        ],
        out_specs=y_specs + y_specs,
        scratch_shapes=[
            pltpu.VMEM((S // LANES, _VROWS_A, LANES), BF16),
            pltpu.VMEM((S // LANES, _VROWS_A, LANES), BF16),
            pltpu.VMEM((S // LANES, _VROWS_B, LANES), BF16),
            pltpu.VMEM((N_TILES, GROUP_A // 2, PAIR, HEAD_DIM_A), BF16),
            pltpu.VMEM((N_TILES, N_SELP, SUBLANES, TQ), F32),
            pltpu.VMEM((N_TILES, GROUP_A // 2, _VROWS_A, PAIR), F32),
            pltpu.VMEM((N_TILES, GROUP_A // 2, HEAD_DIM_A, PAIR), F32),
        ],
    )
    per_tile = lambda name: [sl(name)] * N_TILES
    outs = pl.pallas_call(
        _attn_kernel,
        grid_spec=grid_spec,
        out_shape=[y_sds] * (2 * N_TILES),
        compiler_params=pltpu.CompilerParams(
            dimension_semantics=("arbitrary", "arbitrary", "arbitrary"), vmem_limit_bytes=VMEM_LIMIT),
        name="attn",
    )(sinks, *per_tile("qa"), sl("ksa"), sl("vsa"), sl("kwa"), sl("vwa"), kc, vc,
      *per_tile("ga"), *per_tile("za"), ov, wtab, ctab,
      *per_tile("qb"), sl("kb2"), sl("vb"), *per_tile("zb"), btab)
    in_order = lambda ys: [ys[part.index(n)] for n in range(N_TILES)]
    return in_order(outs[:N_TILES]), in_order(outs[N_TILES:])


def _out_proj_kernel(n_parts, *refs):
    ya_refs, yb_refs = refs[:n_parts], refs[n_parts:2 * n_parts]
    wa_ref, wb_ref, x_ref, g_ref, o_ref = refs[2 * n_parts:]
    part = pl.program_id(0) % n_parts

    for n in range(n_parts):
        @pl.when(part == n)
        def _(n=n):
            r = x_ref[...] + _dot(ya_refs[n][...], wa_ref[...]) + _dot(yb_refs[n][...], wb_ref[...])
            ms = jnp.mean(r * r, axis=-1, keepdims=True)
            o_ref[...] = (r * lax.rsqrt(ms + RMS_EPS)) * g_ref[...]


def _out_proj(ya_parts, yb_parts, w_out, x2, g, S):
    M = x2.shape[0]
    n_parts = len(ya_parts)
    tm = S // n_parts
    n_seq = M // S
    y_spec = lambda w, n: pl.BlockSpec(
        (tm, w), lambda i: (jnp.minimum((i + n_parts - 1 - n) // n_parts, n_seq - 1), 0))
    return pl.pallas_call(
        functools.partial(_out_proj_kernel, n_parts),
        grid=(M // tm,),
        in_specs=[
            *[y_spec(WIDTH_A, n) for n in range(n_parts)], *[y_spec(WIDTH_B, n) for n in range(n_parts)],
            pl.BlockSpec((WIDTH_A, D_MODEL), lambda i: (0, 0)),
            pl.BlockSpec((WIDTH_B, D_MODEL), lambda i: (WIDTH_A // WIDTH_B, 0)),
            pl.BlockSpec((tm, D_MODEL), lambda i: (i, 0)),
            pl.BlockSpec((1, D_MODEL), lambda i: (0, 0)),
        ],
        out_specs=pl.BlockSpec((tm, D_MODEL), lambda i: (i, 0)),
        out_shape=jax.ShapeDtypeStruct((M, D_MODEL), F32),
        compiler_params=pltpu.CompilerParams(
            dimension_semantics=("arbitrary",), vmem_limit_bytes=VMEM_LIMIT_OUT_PROJ),
        name="out_proj",
    )(*ya_parts, *yb_parts, w_out, w_out, x2, g)


def _overlap_matrix(n_cp):
    c_start = np.arange(n_cp) * CMP_STRIDE
    j_start = np.arange(N_SELP) * SEL_BLOCK
    ov = (c_start[None, :] < j_start[:, None] + SEL_BLOCK) & (c_start[None, :] + CMP_BLOCK > j_start[:, None])
    return jnp.asarray(ov, BF16)


def kernel(x, w_in, cmp_k_w1, cmp_k_w2, cmp_v_w1, cmp_v_w2, cmp_k_pos, cmp_v_pos, sinks, w_out, norm_g, final_g):
    B, S, D = x.shape
    assert D == D_MODEL and w_in.shape[0] == 1
    x2 = x.reshape(B * S, D)

    w_all = _prep_w_in(w_in)
    tabs = _rope_tables(S)
    slabs, strided = _in_proj(x2, norm_g[0].reshape(1, D), w_all, tabs, S)
    slabs = [p.reshape(B, S, IN_TN) for p in slabs]

    n_cp = S // CMP_STRIDE
    tk, tv = [strided[n].reshape(N_KV_A * B * n_cp, _HALF) for n in ("kca", "vca")]
    flat_pos = lambda p: jnp.broadcast_to(p.reshape(1, CMP_BLOCK * HEAD_DIM_A), (8, CMP_BLOCK * HEAD_DIM_A))
    kc, vc = _compress(tk, tv, cmp_k_w1[0].astype(BF16), cmp_v_w1[0].astype(BF16),
                       cmp_k_w2[0].astype(BF16), cmp_v_w2[0].astype(BF16),
                       flat_pos(cmp_k_pos[0]).astype(BF16), flat_pos(cmp_v_pos[0]).astype(BF16))
    kc, vc = [c.reshape(N_KV_A * B, n_cp, HEAD_DIM_A) for c in (kc, vc)]

    ya_parts, yb_parts = _attention(slabs, kc, vc, _overlap_matrix(n_cp), sinks[0])
    flat = lambda ys: [y.reshape(B * y.shape[1], y.shape[2]) for y in ys]
    out = _out_proj(flat(ya_parts), flat(yb_parts), w_out[0].astype(BF16), x2, final_g.reshape(1, D), S)
    return out.reshape(B, S, D)
```
